```python
import jax
import jax.numpy as jnp
from jax import lax
import numpy as np

D_MODEL = 1024
BATCH = 16
SEQ = 2048
DEPTH = 2
DEC_BATCH = 16
DEC_SEQ = 16
PAST_LEN = 2048

CHUNK = 64
Q_BLOCK = 128
N_BRANCH = 4
BRANCH_W = D_MODEL // 4
H_A = 4
DN_A = 64
DR_A = 32
DV_A = BRANCH_W // H_A
Q_RANK = (3 * D_MODEL) // 16
KV_RANK = D_MODEL // 8
H_R = 4
DK_R = BRANCH_W // H_R
DV_R = BRANCH_W // H_R
H_C = 4
DH_C = BRANCH_W // H_C
H_D = 4
DH_D = BRANCH_W // H_D
PREV_CHUNKS = 8
REL_CLIP = 128
ROPE_BASE = 10000.0
N_EXPERTS = 16
N_GROUPS = 4
EXPERTS_PER_GROUP = N_EXPERTS // N_GROUPS
TOP_K = 2
D_EXPERT = D_MODEL // 4
ALPHA = (2.0 * DEPTH) ** 0.25
BETA = (8.0 * DEPTH) ** -0.25
EPS = 1e-5
NEG_INF = -1e30
IN_SIZES = (Q_RANK, KV_RANK, DR_A,
            H_R * DK_R, H_R * DK_R, H_R * DV_R, H_R * DV_R,
            H_C * DH_C, H_C * DH_C, H_C * DH_C,
            H_D * DH_D, H_D * DH_D, H_D * DH_D,
            N_BRANCH * D_MODEL)
IN_COLS = sum(IN_SIZES)

kernel_name = "hybrid_streaming_encoder_step"


def split_columns(z):
    offsets = []
    acc = 0
    for s in IN_SIZES[:-1]:
        acc += s
        offsets.append(acc)
    return jnp.split(z, offsets, axis=-1)


def rms_norm(x, g):
    xf = x.astype(jnp.float32)
    y = xf * lax.rsqrt(jnp.mean(xf * xf, axis=-1, keepdims=True) + EPS)
    return (y * g.astype(jnp.float32)).astype(x.dtype)


def layer_norm(x, g, b):
    xf = x.astype(jnp.float32)
    mu = jnp.mean(xf, axis=-1, keepdims=True)
    var = jnp.mean(jnp.square(xf - mu), axis=-1, keepdims=True)
    y = (xf - mu) * lax.rsqrt(var + EPS)
    return (y * g.astype(jnp.float32) + b.astype(jnp.float32)).astype(x.dtype)


def head_group_norm(o, g, b):
    of = o.astype(jnp.float32)
    mu = jnp.mean(of, axis=-1, keepdims=True)
    var = jnp.mean(jnp.square(of - mu), axis=-1, keepdims=True)
    y = ((of - mu) * lax.rsqrt(var + EPS)).reshape(o.shape[0], o.shape[1], -1)
    return y * g.astype(jnp.float32) + b.astype(jnp.float32)


def rope(x, pos):
    half = x.shape[-1] // 2
    inv = jnp.power(ROPE_BASE, -jnp.arange(half, dtype=jnp.float32) / half)
    ang = pos.astype(jnp.float32)[:, None] * inv[None, :]
    cos = jnp.cos(ang)[None, :, None, :]
    sin = jnp.sin(ang)[None, :, None, :]
    xf = x.astype(jnp.float32)
    x1, x2 = xf[..., :half], xf[..., half:]
    return jnp.concatenate([x1 * cos - x2 * sin, x1 * sin + x2 * cos], axis=-1).astype(x.dtype)


def sweep_query_blocks(fn, q, q_pos):
    b, s, h, d = q.shape
    nb = s // Q_BLOCK
    qb = q.reshape(b, nb, Q_BLOCK, h, d).transpose(1, 0, 2, 3, 4)
    pb = q_pos.reshape(nb, Q_BLOCK)
    out = lax.map(lambda blk: fn(blk[0], blk[1]), (qb, pb))
    return out.transpose(1, 0, 2, 3, 4).reshape(b, s, h, out.shape[-1])


def chunk_causal_attn(q, k, v, q_pos, k_pos):
    s = jnp.einsum("bqhd,bkhd->bhqk", q, k).astype(jnp.float32) * (q.shape[-1] ** -0.5)
    mask = (k_pos[None, :] // CHUNK) <= (q_pos[:, None] // CHUNK)
    p = jax.nn.softmax(jnp.where(mask, s, NEG_INF), axis=-1)
    return jnp.einsum("bhqk,bkhd->bqhd", p.astype(v.dtype), v)


def mla_expand(ckv, kpe, w_ukv):
    kv = jnp.einsum("bkr,rhe->bkhe", ckv, w_ukv)
    k_nope, v = kv[..., :DN_A], kv[..., DN_A:]
    k_pe = jnp.broadcast_to(kpe[:, :, None, :], k_nope.shape[:3] + (DR_A,))
    return jnp.concatenate([k_nope, k_pe.astype(k_nope.dtype)], axis=-1), v


def stick_breaking_attn(q, k, v, q_pos, k_pos):
    z = jnp.einsum("bqhd,bkhd->bhqk", q, k).astype(jnp.float32) * (q.shape[-1] ** -0.5)
    mask = k_pos[None, :] < q_pos[:, None]
    log_stay = jnp.where(mask, jax.nn.log_sigmoid(-z), 0.0)
    later = lax.cumsum(log_stay, axis=3, reverse=True) - log_stay
    w = jnp.where(mask, jnp.exp(jax.nn.log_sigmoid(z) + later), 0.0)
    return jnp.einsum("bhqk,bkhd->bqhd", w.astype(v.dtype), v)


def retention_log_decay():
    return jnp.log1p(-jnp.exp2(-5.0 - jnp.arange(H_R, dtype=jnp.float32)))


def retention_block(q, k, v, state):
    q, k, v = q.astype(jnp.float32), k.astype(jnp.float32), v.astype(jnp.float32)
    state = state.astype(jnp.float32)
    length = q.shape[1]
    log_g = retention_log_decay()
    i = jnp.arange(length, dtype=jnp.float32)
    diff = i[:, None] - i[None, :]
    decay = jnp.where(diff >= 0, jnp.exp(jnp.maximum(diff, 0.0)[None] * log_g[:, None, None]), 0.0)
    scores = jnp.einsum("bqhd,bkhd->bhqk", q, k) * decay[None]
    out = jnp.einsum("bhqk,bkhe->bqhe", scores, v)
    q_decay = jnp.exp((i[:, None] + 1.0) * log_g[None, :])
    out = out + jnp.einsum("bqhd,bhde->bqhe", q, state) * q_decay[None, :, :, None]
    k_decay = jnp.exp((length - 1.0 - i)[:, None] * log_g[None, :])
    new_state = (jnp.exp(length * log_g)[None, :, None, None] * state
                 + jnp.einsum("bkhd,bkhe->bhde", k * k_decay[None, :, :, None], v))
    return out, new_state


def retention_prompt(q, k, v):
    b, s, h, dk = q.shape
    nc = s // CHUNK
    to_chunks = lambda t: t.reshape(b, nc, CHUNK, h, -1).transpose(1, 0, 2, 3, 4)
    s0 = jnp.zeros((b, h, dk, v.shape[-1]), jnp.float32)

    def step(state, blk):
        o, state = retention_block(blk[0], blk[1], blk[2], state)
        return state, o

    s_fin, o = lax.scan(step, s0, (to_chunks(q), to_chunks(k), to_chunks(v)))
    return o.transpose(1, 0, 2, 3, 4).reshape(b, s, h, -1), s_fin


def band_attn(q, k, v, q_pos, k_pos, rel_bias):
    s = jnp.einsum("bnqhd,bnkhd->bnhqk", q, k).astype(jnp.float32) * (q.shape[-1] ** -0.5)
    rel = jnp.clip(k_pos[:, None, :] - q_pos[:, :, None], -REL_CLIP, REL_CLIP) + REL_CLIP
    bias = jnp.moveaxis(rel_bias[:, rel], 0, 1).astype(jnp.float32)
    qc = q_pos[:, :, None] // CHUNK
    kc = k_pos[:, None, :] // CHUNK
    mask = (k_pos[:, None, :] >= 0) & (kc <= qc) & (kc >= qc - PREV_CHUNKS)
    p = jax.nn.softmax(jnp.where(mask[None, :, None], s + bias[None], NEG_INF), axis=-1)
    return jnp.einsum("bnhqk,bnkhd->bnqhd", p.astype(v.dtype), v)


def band_prompt(q, k, v, pos, rel_bias):
    b, s, h, d = q.shape
    nc = s // CHUNK

    def gather(t):
        tc = t.reshape(b, nc, CHUNK, h, -1)
        tp = jnp.pad(tc, ((0, 0), (PREV_CHUNKS, 0), (0, 0), (0, 0), (0, 0)))
        return jnp.concatenate([tp[:, i:i + nc] for i in range(PREV_CHUNKS + 1)], axis=2)

    c = jnp.arange(nc)
    slot = jnp.arange((PREV_CHUNKS + 1) * CHUNK)
    k_pos = (c[:, None] - PREV_CHUNKS) * CHUNK + slot[None, :]
    out = band_attn(q.reshape(b, nc, CHUNK, h, d), gather(k), gather(v), pos.reshape(nc, CHUNK), k_pos, rel_bias)
    return out.reshape(b, s, h, -1)


def token_mixers(x, pos, past, w_in, q_norm, w_uq, kv_norm, w_ukv, gn_g, gn_b, rel_bias, w_branch, w_o):
    b, length, _ = x.shape
    (c_q, c_kv, k_pe, rq, rk, rv, rg, sq, sk, sv, bq, bk, bv, gate_logits) = split_columns(x @ w_in)
    heads = lambda t, h: t.reshape(b, length, h, -1)
    q_a = jnp.einsum("bsr,rhe->bshe", rms_norm(c_q, q_norm), w_uq)
    q_a = jnp.concatenate([q_a[..., :DN_A], rope(q_a[..., DN_A:], pos)], axis=-1)
    ckv = rms_norm(c_kv, kv_norm)
    kpe = rope(k_pe[:, :, None, :], pos)[:, :, 0, :]
    rq = rope(heads(rq, H_R), pos)
    rk = rope(heads(rk, H_R), pos) * (DK_R ** -0.5)
    rv = heads(rv, H_R)
    sq, sk, sv = heads(sq, H_C), heads(sk, H_C), heads(sv, H_C)
    bq, bk, bv = heads(bq, H_D), heads(bk, H_D), heads(bv, H_D)
    if past is None:
        k_a, v_a = mla_expand(ckv, kpe, w_ukv)
        o_a = sweep_query_blocks(lambda qb, pb: chunk_causal_attn(qb, k_a, v_a, pb, pos), q_a, pos)
        o_r, s_ret = retention_prompt(rq, rk, rv)
        o_c = sweep_query_blocks(lambda qb, pb: stick_breaking_attn(qb, sk, sv, pb, pos), sq, pos)
        o_d = band_prompt(bq, bk, bv, pos, rel_bias)
        keep = min(PREV_CHUNKS * CHUNK, length)
        new_state = (ckv, kpe, s_ret, sk, sv, bk[:, length - keep:], bv[:, length - keep:])
    else:
        c_ckv, c_kpe, s_prev, c_sk, c_sv, c_bk, c_bv = past
        n_past, n_band = c_ckv.shape[1], c_bk.shape[1]
        k_pos = jnp.arange(n_past + length)
        k_a, v_a = mla_expand(jnp.concatenate([c_ckv, ckv], axis=1), jnp.concatenate([c_kpe, kpe], axis=1), w_ukv)
        o_a = chunk_causal_attn(q_a, k_a, v_a, pos, k_pos)
        o_r, s_ret = retention_block(rq, rk, rv, s_prev)
        o_c = stick_breaking_attn(sq, jnp.concatenate([c_sk, sk], axis=1), jnp.concatenate([c_sv, sv], axis=1), pos, k_pos)
        band_pos = jnp.concatenate([n_past - n_band + jnp.arange(n_band), pos])
        o_d = band_attn(bq[:, None], jnp.concatenate([c_bk, bk], axis=1)[:, None],
                        jnp.concatenate([c_bv, bv], axis=1)[:, None], pos[None], band_pos[None], rel_bias)[:, 0]
        new_state = (ckv, kpe, s_ret, sk, sv, bk, bv)
    o_r = head_group_norm(o_r, gn_g, gn_b) * jax.nn.silu(rg.astype(jnp.float32))
    branches = jnp.stack([o_a.reshape(b, length, -1).astype(x.dtype), o_r.astype(x.dtype),
                          o_c.reshape(b, length, -1).astype(x.dtype), o_d.reshape(b, length, -1).astype(x.dtype)],
                         axis=2)
    gates = jax.nn.sigmoid(gate_logits.reshape(b, length, N_BRANCH, D_MODEL))
    merged = jnp.sum(jnp.einsum("bsnc,ncd->bsnd", branches, w_branch) * gates, axis=2)
    return merged @ w_o, new_state


def routed_moe(x, w_router, b_router, w_gate, w_up, w_down):
    aff = jax.nn.sigmoid((x @ w_router).astype(jnp.float32))
    sel = aff + b_router.astype(jnp.float32)
    grouped = sel.reshape(sel.shape[:-1] + (N_GROUPS, EXPERTS_PER_GROUP))
    group_score = jnp.sum(lax.top_k(grouped, TOP_K)[0], axis=-1)
    best = jnp.argmax(group_score, axis=-1)
    in_group = (jnp.arange(N_EXPERTS) // EXPERTS_PER_GROUP) == best[..., None]
    _, idx = lax.top_k(jnp.where(in_group, sel, NEG_INF), TOP_K)
    w = jnp.take_along_axis(aff, idx, axis=-1)
    w = w / jnp.sum(w, axis=-1, keepdims=True)
    gate = jnp.sum(jax.nn.one_hot(idx, N_EXPERTS, dtype=jnp.float32) * w[..., None], axis=-2).astype(x.dtype)
    y = jnp.zeros_like(x)
    for e in range(N_EXPERTS):
        h = jax.nn.silu(x @ w_gate[e]) * (x @ w_up[e])
        y = y + gate[..., e:e + 1] * (h @ w_down[e])
    return y


def setup_inputs(seed: int = 0) -> dict:
    key = jax.random.key(seed)
    keys = iter(jax.random.split(key, 32))

    def nrm(shape, scale=1.0):
        return jax.random.normal(next(keys), shape, jnp.float32) * scale

    def gain(shape):
        return 1.0 + nrm(shape, 0.01)

    band_rows = min(PREV_CHUNKS * CHUNK, PAST_LEN)
    return {
        "x_prompt": nrm((BATCH, SEQ, D_MODEL)),
        "x_sample": nrm((DEC_BATCH, DEC_SEQ, D_MODEL)),
        "cache_mla_ckv": nrm((DEPTH, DEC_BATCH, PAST_LEN, KV_RANK)),
        "cache_mla_kpe": nrm((DEPTH, DEC_BATCH, PAST_LEN, DR_A)),
        "state_ret": nrm((DEPTH, DEC_BATCH, H_R, DK_R, DV_R), 0.5),
        "cache_sb_k": nrm((DEPTH, DEC_BATCH, PAST_LEN, H_C, DH_C)),
        "cache_sb_v": nrm((DEPTH, DEC_BATCH, PAST_LEN, H_C, DH_C)),
        "cache_band_k": nrm((DEPTH, DEC_BATCH, band_rows, H_D, DH_D)),
        "cache_band_v": nrm((DEPTH, DEC_BATCH, band_rows, H_D, DH_D)),
        "w_in": nrm((DEPTH, D_MODEL, IN_COLS), D_MODEL ** -0.5),
        "mla_q_norm": gain((DEPTH, Q_RANK)),
        "mla_w_uq": nrm((DEPTH, Q_RANK, H_A, DN_A + DR_A), Q_RANK ** -0.5),
        "mla_kv_norm": gain((DEPTH, KV_RANK)),
        "mla_w_ukv": nrm((DEPTH, KV_RANK, H_A, DN_A + DV_A), KV_RANK ** -0.5),
        "ret_gn_g": gain((DEPTH, H_R * DV_R)),
        "ret_gn_b": nrm((DEPTH, H_R * DV_R), 0.02),
        "band_rel_bias": nrm((DEPTH, H_D, 2 * REL_CLIP + 1), 0.5),
        "w_branch": nrm((DEPTH, N_BRANCH, BRANCH_W, D_MODEL), BRANCH_W ** -0.5),
        "w_o": nrm((DEPTH, D_MODEL, D_MODEL), BETA * D_MODEL ** -0.5),
        "ln1_g": gain((DEPTH, D_MODEL)),
        "ln1_b": nrm((DEPTH, D_MODEL), 0.02),
        "w_router": nrm((D_MODEL, N_EXPERTS), D_MODEL ** -0.5),
        "b_router": nrm((N_EXPERTS,), 0.01),
        "w_exp_gate": nrm((DEPTH, N_EXPERTS, D_MODEL, D_EXPERT), D_MODEL ** -0.5),
        "w_exp_up": nrm((DEPTH, N_EXPERTS, D_MODEL, D_EXPERT), D_MODEL ** -0.5),
        "w_exp_down": nrm((DEPTH, N_EXPERTS, D_EXPERT, D_MODEL), BETA * D_EXPERT ** -0.5),
        "ln2_g": gain((DEPTH, D_MODEL)),
        "ln2_b": nrm((DEPTH, D_MODEL), 0.02),
    }


def reference(x_prompt, x_sample, cache_mla_ckv, cache_mla_kpe, state_ret, cache_sb_k, cache_sb_v,
              cache_band_k, cache_band_v, w_in, mla_q_norm, mla_w_uq, mla_kv_norm, mla_w_ukv,
              ret_gn_g, ret_gn_b, band_rel_bias, w_branch, w_o, ln1_g, ln1_b, w_router, b_router,
              w_exp_gate, w_exp_up, w_exp_down, ln2_g, ln2_b):
    pos_p = jnp.arange(x_prompt.shape[1])
    pos_s = cache_mla_ckv.shape[2] + jnp.arange(x_sample.shape[1])
    xp, xs = x_prompt, x_sample
    st_p, st_s = [], []
    for l in range(DEPTH):
        lw = (w_in[l], mla_q_norm[l], mla_w_uq[l], mla_kv_norm[l], mla_w_ukv[l],
              ret_gn_g[l], ret_gn_b[l], band_rel_bias[l], w_branch[l], w_o[l])
        past = (cache_mla_ckv[l], cache_mla_kpe[l], state_ret[l], cache_sb_k[l], cache_sb_v[l],
                cache_band_k[l], cache_band_v[l])
        mix_p, new_p = token_mixers(xp, pos_p, None, *lw)
        mix_s, new_s = token_mixers(xs, pos_s, past, *lw)
        xp = layer_norm(ALPHA * xp + mix_p, ln1_g[l], ln1_b[l])
        xs = layer_norm(ALPHA * xs + mix_s, ln1_g[l], ln1_b[l])
        moe_w = (w_router, b_router, w_exp_gate[l], w_exp_up[l], w_exp_down[l])
        xp = layer_norm(ALPHA * xp + routed_moe(xp, *moe_w), ln2_g[l], ln2_b[l])
        xs = layer_norm(ALPHA * xs + routed_moe(xs, *moe_w), ln2_g[l], ln2_b[l])
        st_p.append(new_p)
        st_s.append(new_s)

    def stack(states, i):
        return jnp.stack([s[i] for s in states], axis=0)

    return (xp, xs,
            stack(st_p, 0), stack(st_p, 1), stack(st_p, 2), stack(st_p, 3), stack(st_p, 4), stack(st_p, 5), stack(st_p, 6),
            stack(st_s, 0), stack(st_s, 1), stack(st_s, 2), stack(st_s, 3), stack(st_s, 4), stack(st_s, 5), stack(st_s, 6))
```

```python
import functools

import jax
import jax.numpy as jnp
import numpy as np
from jax import lax
from jax.experimental import pallas as pl
from jax.experimental.pallas import tpu as pltpu

D_MODEL = 1024
DEPTH = 2
CHUNK = 64
N_BRANCH = 4
BRANCH_W = D_MODEL // 4
H_A = 4
DN_A = 64
DR_A = 32
DV_A = BRANCH_W // H_A
DQK_A = DN_A + DR_A
Q_RANK = (3 * D_MODEL) // 16
KV_RANK = D_MODEL // 8
H_R = 4
DK_R = BRANCH_W // H_R
DV_R = BRANCH_W // H_R
H_C = 4
DH_C = BRANCH_W // H_C
H_D = 4
DH_D = BRANCH_W // H_D
PREV_CHUNKS = 8
REL_CLIP = 128
ROPE_BASE = 10000.0
N_EXPERTS = 16
N_GROUPS = 4
EXPERTS_PER_GROUP = N_EXPERTS // N_GROUPS
TOP_K = 2
D_EXPERT = D_MODEL // 4
ALPHA = (2.0 * DEPTH) ** 0.25
EPS = 1e-5
NEG_INF = -1e30

F32 = jnp.float32
BF16 = jnp.bfloat16

V7X_VMEM_LIMIT = 56 * 1024 * 1024
LANE = 128

_IN_NAMES = ("c_q", "c_kv", "k_pe", "rq", "rk", "rv", "rg", "sq", "sk", "sv", "bq", "bk", "bv")
_IN_WIDTH = (Q_RANK, KV_RANK, DR_A) + (BRANCH_W,) * 10


def _round_up(n, m):
    return (n + m - 1) // m * m


_IN_OFF = {}
_off = 0
for _n, _w in zip(_IN_NAMES, _IN_WIDTH):
    _IN_OFF[_n] = (_off, _w)
    _off += _round_up(_w, LANE)
IN_PACKED = _off
GATE_COL0 = sum(_IN_WIDTH)


def _params(sem):
    return pltpu.CompilerParams(dimension_semantics=sem, vmem_limit_bytes=V7X_VMEM_LIMIT)


def _nt_dot(a, b):
    return lax.dot_general(a, b, (((1,), (1,)), ((), ())), preferred_element_type=F32)


def _layer_norm(v, g, b):
    mu = jnp.mean(v, axis=-1, keepdims=True)
    d = v - mu
    var = jnp.mean(d * d, axis=-1, keepdims=True)
    return d * lax.rsqrt(var + EPS) * g + b


def _sigmoid(v):
    return 0.5 * jnp.tanh(0.5 * v) + 0.5


def _mm_kernel(x_ref, w_ref, o_ref):
    o_ref[...] = jnp.dot(x_ref[...].astype(BF16), w_ref[...],
                         preferred_element_type=F32).astype(o_ref.dtype)


def _matmul(x, w, out_dtype, tm):
    m, k = x.shape
    n = w.shape[1]
    tm = min(tm, m)
    assert m % tm == 0
    return pl.pallas_call(
        _mm_kernel,
        grid=(m // tm,),
        in_specs=[pl.BlockSpec((tm, k), lambda i: (i, 0)),
                  pl.BlockSpec((k, n), lambda i: (0, 0))],
        out_specs=pl.BlockSpec((tm, n), lambda i: (i, 0)),
        out_shape=jax.ShapeDtypeStruct((m, n), out_dtype),
        compiler_params=_params(("parallel",)),
        name="matmul",
    )(x, w)


def _mla_kernel(q_ref, k_ref, v_ref, o_ref, *, tq, tk, q_pos0, n_keys):
    qi = pl.program_id(2)
    q = q_ref[...]
    row0 = q_pos0 + qi * tq
    qpos = row0 + lax.broadcasted_iota(jnp.int32, (tq, tk), 0)
    limit = jnp.minimum((qpos // CHUNK + 1) * CHUNK, n_keys)
    col = lax.broadcasted_iota(jnp.int32, (tq, tk), 1)
    n_vis = jnp.minimum(((row0 + tq - 1) // CHUNK + 1) * CHUNK, n_keys)
    nt = (n_vis + tk - 1) // tk

    def body(j, carry):
        m, l, acc = carry
        start = pl.multiple_of(j * tk, tk)
        k = k_ref[pl.ds(start, tk), :]
        v = v_ref[pl.ds(start, tk), :]
        s = _nt_dot(q, k)
        s = jnp.where(col + start < limit, s, NEG_INF)
        m_new = jnp.maximum(m, jnp.max(s, axis=1, keepdims=True))
        p = jnp.exp(s - m_new)
        a = jnp.exp(m - m_new)
        l = a * l + jnp.sum(p, axis=1, keepdims=True)
        acc = a * acc + jnp.dot(p.astype(BF16), v, preferred_element_type=F32)
        return m_new, l, acc

    m0 = jnp.full((tq, 1), NEG_INF, F32)
    l0 = jnp.zeros((tq, 1), F32)
    a0 = jnp.zeros((tq, v_ref.shape[-1]), F32)
    _, l, acc = lax.fori_loop(0, nt, body, (m0, l0, a0))
    o_ref[...] = (acc / l).astype(o_ref.dtype)


def _mla_attn(q, k, v, *, q_pos0, n_keys, tq, tk):
    b, h, lq, dqk = q.shape
    lk, dv = k.shape[2], v.shape[3]
    assert lq % tq == 0 and lk % tk == 0
    return pl.pallas_call(
        functools.partial(_mla_kernel, tq=tq, tk=tk, q_pos0=q_pos0, n_keys=n_keys),
        grid=(b, h, lq // tq),
        in_specs=[pl.BlockSpec((None, None, tq, dqk), lambda i, j, t: (i, j, t, 0)),
                  pl.BlockSpec((None, None, lk, dqk), lambda i, j, t: (i, j, 0, 0)),
                  pl.BlockSpec((None, None, lk, dv), lambda i, j, t: (i, j, 0, 0))],
        out_specs=pl.BlockSpec((None, None, tq, dv), lambda i, j, t: (i, j, t, 0)),
        out_shape=jax.ShapeDtypeStruct((b, h, lq, dv), BF16),
        compiler_params=_params(("parallel", "parallel", "arbitrary")),
        name="mla_attn",
    )(q, k, v)


def _sb_kernel(q_ref, k_ref, v_ref, o_ref, *, tq, tk, q_pos0, n_keys):
    qi = pl.program_id(2)
    q = q_ref[...]
    row0 = q_pos0 + qi * tq
    qpos = row0 + lax.broadcasted_iota(jnp.int32, (tq, tk), 0)
    limit = jnp.minimum(qpos, n_keys)
    col = lax.broadcasted_iota(jnp.int32, (tq, tk), 1)
    n_vis = jnp.minimum(row0 + tq - 1, n_keys)
    nt = (n_vis + tk - 1) // tk
    tri = jnp.where(lax.broadcasted_iota(jnp.int32, (tk, tk), 0)
                    > lax.broadcasted_iota(jnp.int32, (tk, tk), 1), 1.0, 0.0).astype(BF16)

    def body(jj, carry):
        run, acc = carry
        start = pl.multiple_of((nt - 1 - jj) * tk, tk)
        k = k_ref[pl.ds(start, tk), :]
        v = v_ref[pl.ds(start, tk), :]
        z = _nt_dot(q, k)
        mask = col + start < limit
        t = jnp.log1p(jnp.exp(-jnp.abs(z)))
        log_beta = jnp.minimum(z, 0.0) - t
        log_stay = jnp.where(mask, jnp.minimum(-z, 0.0) - t, 0.0)
        hi = log_stay.astype(BF16)
        lo = (log_stay - hi.astype(F32)).astype(BF16)
        later = (jnp.dot(hi, tri, preferred_element_type=F32)
                 + jnp.dot(lo, tri, preferred_element_type=F32) + run)
        w = jnp.where(mask, jnp.exp(log_beta + later), 0.0)
        acc = acc + jnp.dot(w.astype(BF16), v, preferred_element_type=F32)
        run = run + jnp.sum(log_stay, axis=1, keepdims=True)
        return run, acc

    run0 = jnp.zeros((tq, 1), F32)
    a0 = jnp.zeros((tq, v_ref.shape[-1]), F32)
    _, acc = lax.fori_loop(0, nt, body, (run0, a0))
    o_ref[...] = acc.astype(o_ref.dtype)


def _sb_attn(q, k, v, *, q_pos0, n_keys, tq, tk):
    b, h, lq, d = q.shape
    lk = k.shape[2]
    assert lq % tq == 0 and lk % tk == 0
    return pl.pallas_call(
        functools.partial(_sb_kernel, tq=tq, tk=tk, q_pos0=q_pos0, n_keys=n_keys),
        grid=(b, h, lq // tq),
        in_specs=[pl.BlockSpec((None, None, tq, d), lambda i, j, t: (i, j, t, 0)),
                  pl.BlockSpec((None, None, lk, d), lambda i, j, t: (i, j, 0, 0)),
                  pl.BlockSpec((None, None, lk, d), lambda i, j, t: (i, j, 0, 0))],
        out_specs=pl.BlockSpec((None, None, tq, d), lambda i, j, t: (i, j, t, 0)),
        out_shape=jax.ShapeDtypeStruct((b, h, lq, d), BF16),
        compiler_params=_params(("parallel", "parallel", "arbitrary")),
        name="sb_attn",
    )(q, k, v)


def _ret_kernel(q_ref, k_ref, kdt_ref, v_ref, rg_ref, s0_ref, dec_ref, qdec_ref, gl_ref,
                gng_ref, gnb_ref, o_ref, sout_ref, state_ref):
    c = pl.program_id(2)

    @pl.when(c == 0)
    def _():
        state_ref[...] = s0_ref[...]

    q = q_ref[...]
    v = v_ref[...]
    state = state_ref[...]
    scores = _nt_dot(q, k_ref[...]) * dec_ref[...]
    o = jnp.dot(scores.astype(BF16), v, preferred_element_type=F32)
    o = o + jnp.dot(q, state.astype(BF16), preferred_element_type=F32) * qdec_ref[...]
    new_state = gl_ref[...] * state + jnp.dot(kdt_ref[...], v, preferred_element_type=F32)
    state_ref[...] = new_state

    mu = jnp.mean(o, axis=-1, keepdims=True)
    d = o - mu
    var = jnp.mean(d * d, axis=-1, keepdims=True)
    y = d * lax.rsqrt(var + EPS) * gng_ref[...] + gnb_ref[...]
    rg = rg_ref[...]
    o_ref[...] = (y * (rg * _sigmoid(rg))).astype(o_ref.dtype)

    @pl.when(c == pl.num_programs(2) - 1)
    def _():
        sout_ref[...] = new_state


def _retention(q, k, kdt, v, rg, s0, dec, qdec, gl, gng, gnb, *, lc):
    b, h, length, dk = q.shape
    dv = v.shape[3]
    assert length % lc == 0
    seq = lambda i, j, t: (i, j, t, 0)
    per_head = lambda i, j, t: (j, 0, 0)
    return pl.pallas_call(
        _ret_kernel,
        grid=(b, h, length // lc),
        in_specs=[pl.BlockSpec((None, None, lc, dk), seq),
                  pl.BlockSpec((None, None, lc, dk), seq),
                  pl.BlockSpec((None, None, dk, lc), lambda i, j, t: (i, j, 0, t)),
                  pl.BlockSpec((None, None, lc, dv), seq),
                  pl.BlockSpec((None, None, lc, dv), seq),
                  pl.BlockSpec((None, None, dk, dv), lambda i, j, t: (i, j, 0, 0)),
                  pl.BlockSpec((None, lc, lc), per_head),
                  pl.BlockSpec((None, lc, dv), per_head),
                  pl.BlockSpec((None, dk, dv), per_head),
                  pl.BlockSpec((None, 1, dv), per_head),
                  pl.BlockSpec((None, 1, dv), per_head)],
        out_specs=[pl.BlockSpec((None, None, lc, dv), seq),
                   pl.BlockSpec((None, None, dk, dv), lambda i, j, t: (i, j, 0, 0))],
        out_shape=[jax.ShapeDtypeStruct((b, h, length, dv), BF16),
                   jax.ShapeDtypeStruct((b, h, dk, dv), F32)],
        scratch_shapes=[pltpu.VMEM((dk, dv), F32)],
        compiler_params=_params(("parallel", "parallel", "arbitrary")),
        name="retention",
    )(q, k, kdt, v, rg, s0, dec, qdec, gl, gng, gnb)


def _band_kernel(q_ref, k_ref, v_ref, bias_ref, o_ref, *, tq, win, front_pad):
    qi = pl.program_id(2)
    start = pl.multiple_of(qi * tq, tq)
    k = k_ref[pl.ds(start, win), :]
    v = v_ref[pl.ds(start, win), :]
    s = _nt_dot(q_ref[...], k) + bias_ref[...]
    kpos = start - front_pad + lax.broadcasted_iota(jnp.int32, (tq, win), 1)
    s = jnp.where(kpos >= 0, s, NEG_INF)
    m = jnp.max(s, axis=1, keepdims=True)
    p = jnp.exp(s - m)
    l = jnp.sum(p, axis=1, keepdims=True)
    o = jnp.dot(p.astype(BF16), v, preferred_element_type=F32)
    o_ref[...] = (o / l).astype(o_ref.dtype)


def _band_attn(q, k, v, bias, *, tq, win, front_pad):
    b, h, lq, d = q.shape
    lk = k.shape[2]
    assert lq % tq == 0 and lk == lq - tq + win
    return pl.pallas_call(
        functools.partial(_band_kernel, tq=tq, win=win, front_pad=front_pad),
        grid=(b, h, lq // tq),
        in_specs=[pl.BlockSpec((None, None, tq, d), lambda i, j, t: (i, j, t, 0)),
                  pl.BlockSpec((None, None, lk, d), lambda i, j, t: (i, j, 0, 0)),
                  pl.BlockSpec((None, None, lk, d), lambda i, j, t: (i, j, 0, 0)),
                  pl.BlockSpec((None, tq, win), lambda i, j, t: (j, 0, 0))],
        out_specs=pl.BlockSpec((None, None, tq, d), lambda i, j, t: (i, j, t, 0)),
        out_shape=jax.ShapeDtypeStruct((b, h, lq, d), BF16),
        compiler_params=_params(("parallel", "parallel", "arbitrary")),
        name="band_attn",
    )(q, k, v, bias)


def _merge_kernel(x_ref, br_ref, wg_ref, wb_ref, wo_ref, g_ref, b_ref, o_ref):
    x = x_ref[...]
    xb = x.astype(BF16)
    merged = None
    for n in range(N_BRANCH):
        logits = jnp.dot(xb, wg_ref[:, n * D_MODEL:(n + 1) * D_MODEL], preferred_element_type=F32)
        proj = jnp.dot(br_ref[:, n * BRANCH_W:(n + 1) * BRANCH_W], wb_ref[n],
                       preferred_element_type=F32)
        term = proj * _sigmoid(logits)
        merged = term if merged is None else merged + term
    mix = jnp.dot(merged.astype(BF16), wo_ref[...], preferred_element_type=F32)
    o_ref[...] = _layer_norm(ALPHA * x + mix, g_ref[...], b_ref[...])


def _merge(x, br, wg, wb, wo, g, b, *, tm):
    m = x.shape[0]
    tm = min(tm, m)
    assert m % tm == 0
    const2 = lambda i: (0, 0)
    return pl.pallas_call(
        _merge_kernel,
        grid=(m // tm,),
        in_specs=[pl.BlockSpec((tm, D_MODEL), lambda i: (i, 0)),
                  pl.BlockSpec((tm, D_MODEL), lambda i: (i, 0)),
                  pl.BlockSpec((D_MODEL, N_BRANCH * D_MODEL), const2),
                  pl.BlockSpec((N_BRANCH, BRANCH_W, D_MODEL), lambda i: (0, 0, 0)),
                  pl.BlockSpec((D_MODEL, D_MODEL), const2),
                  pl.BlockSpec((1, D_MODEL), const2),
                  pl.BlockSpec((1, D_MODEL), const2)],
        out_specs=pl.BlockSpec((tm, D_MODEL), lambda i: (i, 0)),
        out_shape=jax.ShapeDtypeStruct((m, D_MODEL), F32),
        compiler_params=_params(("parallel",)),
        name="merge",
    )(x, br, wg, wb, wo, g, b)


def _route(aff_t, sel_t):
    def top2_sum(a, b, c, d):
        hi1, lo1 = jnp.maximum(a, b), jnp.minimum(a, b)
        hi2, lo2 = jnp.maximum(c, d), jnp.minimum(c, d)
        return jnp.maximum(hi1, hi2) + jnp.maximum(jnp.minimum(hi1, hi2), jnp.maximum(lo1, lo2))

    score = [top2_sum(*sel_t[g * EXPERTS_PER_GROUP:(g + 1) * EXPERTS_PER_GROUP])
             for g in range(N_GROUPS)]
    best_here = []
    for g in range(N_GROUPS):
        ok = None
        for o in range(N_GROUPS):
            if o == g:
                continue
            c = (score[g] > score[o]) if o < g else (score[g] >= score[o])
            ok = c if ok is None else jnp.logical_and(ok, c)
        best_here.append(ok)
    picked = []
    for e in range(N_EXPERTS):
        g = e // EXPERTS_PER_GROUP
        rank = jnp.zeros_like(sel_t[e])
        for o in range(g * EXPERTS_PER_GROUP, (g + 1) * EXPERTS_PER_GROUP):
            if o == e:
                continue
            ahead = (sel_t[o] >= sel_t[e]) if o < e else (sel_t[o] > sel_t[e])
            rank = rank + jnp.where(ahead, 1.0, 0.0)
        picked.append(jnp.where(jnp.logical_and(best_here[g], rank < TOP_K), aff_t[e], 0.0))
    total = picked[0]
    for e in range(1, N_EXPERTS):
        total = total + picked[e]
    return [p / total for p in picked]


def _moe_kernel(x_ref, wrh_ref, wrl_ref, br_ref, wgu_ref, wd_ref, g_ref, b_ref, o_ref, acc_ref):
    x = x_ref[...]
    xh = x.astype(BF16)
    xl = (x - xh.astype(F32)).astype(BF16)
    wrh = wrh_ref[...]
    logits = (jnp.dot(xh, wrh, preferred_element_type=F32)
              + jnp.dot(xl, wrh, preferred_element_type=F32)
              + jnp.dot(xh, wrl_ref[...], preferred_element_type=F32))
    aff = _sigmoid(logits).T
    bias = br_ref[...]
    aff_t = [aff[e:e + 1, :] for e in range(N_EXPERTS)]
    sel_t = [aff_t[e] + bias[e:e + 1, :] for e in range(N_EXPERTS)]
    gate_rows = _route(aff_t, sel_t)
    tm = x.shape[0]
    gate_t = jnp.concatenate(gate_rows + [jnp.zeros((LANE - N_EXPERTS, tm), F32)], axis=0)
    gate = gate_t.T

    for e in range(N_EXPERTS):
        gu = jnp.dot(xh, wgu_ref[e], preferred_element_type=F32)
        gt, up = gu[:, :D_EXPERT], gu[:, D_EXPERT:]
        hmid = gt * _sigmoid(gt) * up
        y = jnp.dot(hmid.astype(BF16), wd_ref[e], preferred_element_type=F32) * gate[:, e:e + 1]
        if e == 0:
            acc_ref[...] = y
        else:
            acc_ref[...] += y
    o_ref[...] = _layer_norm(ALPHA * x + acc_ref[...], g_ref[...], b_ref[...])


def _moe(x, wrh, wrl, br, wgu, wd, g, b, *, tm):
    m = x.shape[0]
    tm = min(tm, m)
    assert m % tm == 0
    const2 = lambda i: (0, 0)
    const3 = lambda i: (0, 0, 0)
    return pl.pallas_call(
        _moe_kernel,
        grid=(m // tm,),
        in_specs=[pl.BlockSpec((tm, D_MODEL), lambda i: (i, 0)),
                  pl.BlockSpec((D_MODEL, LANE), const2),
                  pl.BlockSpec((D_MODEL, LANE), const2),
                  pl.BlockSpec((LANE, 1), const2),
                  pl.BlockSpec((N_EXPERTS, D_MODEL, 2 * D_EXPERT), const3, pipeline_mode=pl.Buffered(1)),
                  pl.BlockSpec((N_EXPERTS, D_EXPERT, D_MODEL), const3, pipeline_mode=pl.Buffered(1)),
                  pl.BlockSpec((1, D_MODEL), const2),
                  pl.BlockSpec((1, D_MODEL), const2)],
        out_specs=pl.BlockSpec((tm, D_MODEL), lambda i: (i, 0)),
        out_shape=jax.ShapeDtypeStruct((m, D_MODEL), F32),
        scratch_shapes=[pltpu.VMEM((tm, D_MODEL), F32)],
        compiler_params=_params(("parallel",)),
        name="moe",
    )(x, wrh, wrl, br, wgu, wd, g, b)


def _rms_norm(x, g):
    return x * lax.rsqrt(jnp.mean(x * x, axis=-1, keepdims=True) + EPS) * g


def _rope(x, pos):
    half = x.shape[-1] // 2
    inv = jnp.power(ROPE_BASE, -jnp.arange(half, dtype=F32) / half)
    ang = pos.astype(F32)[:, None] * inv[None, :]
    cos = jnp.cos(ang)[None, :, None, :]
    sin = jnp.sin(ang)[None, :, None, :]
    x1, x2 = x[..., :half], x[..., half:]
    return jnp.concatenate([x1 * cos - x2 * sin, x1 * sin + x2 * cos], axis=-1)


def _to_heads(t, h):
    b, length, _ = t.shape
    return t.reshape(b, length, h, -1).transpose(0, 2, 1, 3)


def _from_heads(t):
    b, h, length, d = t.shape
    return t.transpose(0, 2, 1, 3).reshape(b, length, h * d)


def _pad_keys(t, lk):
    return jnp.pad(t, ((0, 0), (0, 0), (0, lk - t.shape[2]), (0, 0)))


def _retention_tables(lc):
    log_g = jnp.log1p(-jnp.exp2(-5.0 - jnp.arange(H_R, dtype=F32)))
    i = jnp.arange(lc, dtype=F32)
    diff = i[:, None] - i[None, :]
    dec = jnp.where(diff >= 0, jnp.exp(jnp.maximum(diff, 0.0)[None] * log_g[:, None, None]), 0.0)
    qdec = jnp.exp((i[None, :] + 1.0) * log_g[:, None])
    kdec = jnp.exp((lc - 1.0 - i)[None, :] * log_g[:, None])
    gl = jnp.exp(lc * log_g)
    qdec = jnp.broadcast_to(qdec[:, :, None], (H_R, lc, DV_R))
    gl = jnp.broadcast_to(gl[:, None, None], (H_R, DK_R, DV_R))
    return dec, qdec, kdec, gl


def _band_bias_tile(rel_bias, q_local, k_local, valid):
    rel = jnp.clip(k_local[None, :] - q_local[:, None], -REL_CLIP, REL_CLIP) + REL_CLIP
    bias = rel_bias[:, rel].astype(F32)
    return jnp.where(valid[None], bias, NEG_INF)


def _token_mixers(x, pos0, past, lw, *, prompt):
    (w_in_p, q_norm, w_uq, kv_norm, w_ukv, gn_g, gn_b, rel_bias) = lw
    b, length, _ = x.shape
    pos = pos0 + jnp.arange(length)
    xf = x.reshape(b * length, D_MODEL)
    z = _matmul(xf, w_in_p, F32, 512).reshape(b, length, IN_PACKED)
    col = lambda name: z[..., _IN_OFF[name][0]:_IN_OFF[name][0] + _IN_OFF[name][1]]

    cq = _rms_norm(col("c_q"), q_norm)
    cq = jnp.pad(cq, ((0, 0), (0, 0), (0, 2 * LANE - Q_RANK))).astype(BF16)
    q_a = _matmul(cq.reshape(b * length, 2 * LANE), w_uq, F32, 512).reshape(b, length, H_A, DQK_A)
    q_a = jnp.concatenate([q_a[..., :DN_A], _rope(q_a[..., DN_A:], pos)], axis=-1)
    q_a = (q_a * (DQK_A ** -0.5)).astype(BF16).transpose(0, 2, 1, 3)
    ckv = _rms_norm(col("c_kv"), kv_norm)
    kpe = _rope(col("k_pe")[:, :, None, :], pos)[:, :, 0, :]
    if prompt:
        ckv_all, kpe_all, n_keys = ckv, kpe, length
        tq_a, tk_a = 256, 256
    else:
        ckv_all = jnp.concatenate([past[0], ckv], axis=1)
        kpe_all = jnp.concatenate([past[1], kpe], axis=1)
        n_keys = ckv_all.shape[1]
        tq_a, tk_a = length, 256
    lk = _round_up(n_keys, tk_a)
    ckv_all = jnp.pad(ckv_all, ((0, 0), (0, lk - n_keys), (0, 0)))
    kpe_all = jnp.pad(kpe_all, ((0, 0), (0, lk - n_keys), (0, 0)))
    kv = _matmul(ckv_all.reshape(b * lk, KV_RANK), w_ukv, BF16, 512).reshape(b, lk, H_A, DN_A + DV_A)
    k_a = jnp.concatenate(
        [kv[..., :DN_A], jnp.broadcast_to(kpe_all[:, :, None, :].astype(BF16), (b, lk, H_A, DR_A))], axis=-1)
    o_a = _mla_attn(q_a, k_a.transpose(0, 2, 1, 3), kv[..., DN_A:].transpose(0, 2, 1, 3),
                    q_pos0=pos0, n_keys=n_keys, tq=tq_a, tk=tk_a)

    lc = 256 if prompt else length
    dec, qdec, kdec, gl = _retention_tables(lc)
    rq = _rope(col("rq").reshape(b, length, H_R, DK_R), pos)
    rk = _rope(col("rk").reshape(b, length, H_R, DK_R), pos) * (DK_R ** -0.5)
    kdec_full = jnp.tile(kdec, (1, length // lc))
    rkd = rk * kdec_full.T[None, :, :, None]
    s0 = jnp.zeros((b, H_R, DK_R, DV_R), F32) if prompt else past[2]
    o_r, s_ret = _retention(
        rq.astype(BF16).transpose(0, 2, 1, 3), rk.astype(BF16).transpose(0, 2, 1, 3),
        rkd.astype(BF16).transpose(0, 2, 3, 1), _to_heads(col("rv"), H_R).astype(BF16),
        _to_heads(col("rg"), H_R), s0, dec, qdec, gl,
        gn_g.reshape(H_R, 1, DV_R), gn_b.reshape(H_R, 1, DV_R), lc=lc)

    sk, sv = col("sk"), col("sv")
    sq_h = _to_heads(col("sq") * (DH_C ** -0.5), H_C).astype(BF16)
    if prompt:
        sk_all, sv_all = sk, sv
        tq_c, tk_c = 128, 128
    else:
        sk_all = jnp.concatenate([past[3].reshape(b, -1, BRANCH_W), sk], axis=1)
        sv_all = jnp.concatenate([past[4].reshape(b, -1, BRANCH_W), sv], axis=1)
        tq_c, tk_c = length, 128
    n_keys_c = sk_all.shape[1]
    lk_c = _round_up(n_keys_c, tk_c)
    o_c = _sb_attn(sq_h, _pad_keys(_to_heads(sk_all, H_C).astype(BF16), lk_c),
                   _pad_keys(_to_heads(sv_all, H_C).astype(BF16), lk_c),
                   q_pos0=pos0, n_keys=n_keys_c, tq=tq_c, tk=tk_c)

    bk, bv = col("bk"), col("bv")
    bq_h = _to_heads(col("bq") * (DH_D ** -0.5), H_D).astype(BF16)
    if prompt:
        tq_d = 256
        front = PREV_CHUNKS * CHUNK
        win = front + tq_d
        q_local = jnp.arange(tq_d)
        k_local = jnp.arange(win) - front
        qc, kc = q_local[:, None] // CHUNK, jnp.floor_divide(k_local[None, :], CHUNK)
        valid = (kc <= qc) & (kc >= qc - PREV_CHUNKS)
        bias = _band_bias_tile(rel_bias, q_local, k_local, valid)
        pad_front = lambda t: jnp.pad(t, ((0, 0), (0, 0), (front, 0), (0, 0)))
        k_d = pad_front(_to_heads(bk, H_D).astype(BF16))
        v_d = pad_front(_to_heads(bv, H_D).astype(BF16))
    else:
        tq_d, front = length, 0
        n_band = past[5].shape[1]
        bk_all = jnp.concatenate([past[5].reshape(b, n_band, BRANCH_W), bk], axis=1)
        bv_all = jnp.concatenate([past[6].reshape(b, n_band, BRANCH_W), bv], axis=1)
        n_keys_d = n_band + length
        win = _round_up(n_keys_d, LANE)
        n_past = pos0
        k_pos = jnp.concatenate([n_past - n_band + jnp.arange(n_band), pos,
                                 jnp.zeros((win - n_keys_d,), pos.dtype)])
        real = jnp.arange(win) < n_keys_d
        qc, kc = pos[:, None] // CHUNK, k_pos[None, :] // CHUNK
        valid = real[None, :] & (k_pos[None, :] >= 0) & (kc <= qc) & (kc >= qc - PREV_CHUNKS)
        bias = _band_bias_tile(rel_bias, pos, k_pos, valid)
        k_d = _pad_keys(_to_heads(bk_all, H_D).astype(BF16), win)
        v_d = _pad_keys(_to_heads(bv_all, H_D).astype(BF16), win)
    o_d = _band_attn(bq_h, k_d, v_d, bias, tq=tq_d, win=win, front_pad=front)

    branches = jnp.concatenate([_from_heads(o_a), _from_heads(o_r), _from_heads(o_c), _from_heads(o_d)],
                               axis=-1).reshape(b * length, D_MODEL)
    heads4 = lambda t: t.reshape(b, length, 4, -1)
    if prompt:
        keep = min(PREV_CHUNKS * CHUNK, length)
        new_state = (ckv, kpe, s_ret, heads4(sk), heads4(sv),
                     heads4(bk)[:, length - keep:], heads4(bv)[:, length - keep:])
    else:
        new_state = (ckv, kpe, s_ret, heads4(sk), heads4(sv), heads4(bk), heads4(bv))
    return branches, new_state


def _pack_w_in(w_in_l):
    cols = []
    src = 0
    for w in _IN_WIDTH:
        seg = w_in_l[:, src:src + w]
        cols.append(jnp.pad(seg, ((0, 0), (0, _round_up(w, LANE) - w))))
        src += w
    return jnp.concatenate(cols, axis=1).astype(BF16)


def kernel(x_prompt, x_sample, cache_mla_ckv, cache_mla_kpe, state_ret, cache_sb_k, cache_sb_v, cache_band_k, cache_band_v, w_in, mla_q_norm, mla_w_uq, mla_kv_norm, mla_w_ukv, ret_gn_g, ret_gn_b, band_rel_bias, w_branch, w_o, ln1_g, ln1_b, w_router, b_router, w_exp_gate, w_exp_up, w_exp_down, ln2_g, ln2_b):
    bp, lp, _ = x_prompt.shape
    bs, ls, _ = x_sample.shape
    past_len = cache_mla_ckv.shape[2]
    xp = x_prompt.reshape(bp * lp, D_MODEL)
    xs = x_sample.reshape(bs * ls, D_MODEL)

    wr = jnp.pad(w_router, ((0, 0), (0, LANE - N_EXPERTS)))
    wrh = wr.astype(BF16)
    wrl = (wr - wrh.astype(F32)).astype(BF16)
    br = jnp.pad(b_router, (0, LANE - N_EXPERTS)).reshape(LANE, 1)

    st_p, st_s = [], []
    for l in range(DEPTH):
        w_uq = jnp.pad(mla_w_uq[l].reshape(Q_RANK, H_A * DQK_A), ((0, 2 * LANE - Q_RANK), (0, 0))).astype(BF16)
        w_ukv = mla_w_ukv[l].reshape(KV_RANK, H_A * (DN_A + DV_A)).astype(BF16)
        lw = (_pack_w_in(w_in[l]), mla_q_norm[l], w_uq, mla_kv_norm[l], w_ukv,
              ret_gn_g[l], ret_gn_b[l], band_rel_bias[l])
        wg = w_in[l][:, GATE_COL0:].astype(BF16)
        wb = w_branch[l].astype(BF16)
        wo = w_o[l].astype(BF16)
        g1, b1 = ln1_g[l].reshape(1, D_MODEL), ln1_b[l].reshape(1, D_MODEL)
        g2, b2 = ln2_g[l].reshape(1, D_MODEL), ln2_b[l].reshape(1, D_MODEL)
        wgu = jnp.concatenate([w_exp_gate[l], w_exp_up[l]], axis=-1).astype(BF16)
        wd = w_exp_down[l].astype(BF16)
        past = (cache_mla_ckv[l], cache_mla_kpe[l], state_ret[l], cache_sb_k[l], cache_sb_v[l],
                cache_band_k[l], cache_band_v[l])

        br_p, new_p = _token_mixers(xp.reshape(bp, lp, D_MODEL), 0, None, lw, prompt=True)
        br_s, new_s = _token_mixers(xs.reshape(bs, ls, D_MODEL), past_len, past, lw, prompt=False)
        xp = _merge(xp, br_p, wg, wb, wo, g1, b1, tm=256)
        xs = _merge(xs, br_s, wg, wb, wo, g1, b1, tm=256)
        xp = _moe(xp, wrh, wrl, br, wgu, wd, g2, b2, tm=256)
        xs = _moe(xs, wrh, wrl, br, wgu, wd, g2, b2, tm=256)
        st_p.append(new_p)
        st_s.append(new_s)

    stack = lambda states, i: jnp.stack([s[i] for s in states], axis=0)
    return ((xp.reshape(bp, lp, D_MODEL), xs.reshape(bs, ls, D_MODEL))
            + tuple(stack(st_p, i) for i in range(7))
            + tuple(stack(st_s, i) for i in range(7)))
```

```python
import functools

import jax
import jax.numpy as jnp
import numpy as np
from jax import lax
from jax.experimental import pallas as pl
from jax.experimental.pallas import tpu as pltpu

D_MODEL = 1024
DEPTH = 2
CHUNK = 64
N_BRANCH = 4
BRANCH_W = D_MODEL // 4
N_HEADS = 4
D_HEAD = BRANCH_W // N_HEADS
DN_A = 64
DR_A = 32
DQK_A = DN_A + DR_A
Q_RANK = (3 * D_MODEL) // 16
KV_RANK = D_MODEL // 8
PREV_CHUNKS = 8
REL_CLIP = 128
ROPE_BASE = 10000.0
N_EXPERTS = 16
N_GROUPS = 4
EXPERTS_PER_GROUP = N_EXPERTS // N_GROUPS
TOP_K = 2
D_EXPERT = D_MODEL // 4
ALPHA = (2.0 * DEPTH) ** 0.25
EPS = 1e-5
NEG_INF = -1e30
LOG2E = 1.4426950408889634

F32 = jnp.float32
BF16 = jnp.bfloat16

V7X_VMEM_LIMIT = 56 * 1024 * 1024
LANE = 128

_IN_NAMES = ("c_q", "c_kv", "k_pe", "rq", "rk", "rv", "rg", "sq", "sk", "sv", "bq", "bk", "bv")
_IN_WIDTH = (Q_RANK, KV_RANK, DR_A) + (BRANCH_W,) * 10
QA_W = N_HEADS * DN_A + N_HEADS * DR_A


def _round_up(n, m):
    return (n + m - 1) // m * m


_IN_OFF = {}
_off = 0
for _n, _w in zip(_IN_NAMES, _IN_WIDTH):
    _IN_OFF[_n] = (_off, _round_up(_w, LANE))
    _off += _round_up(_w, LANE)
IN_PACKED = _off
GATE_COL0 = sum(_IN_WIDTH)


def _params(sem):
    return pltpu.CompilerParams(dimension_semantics=sem, vmem_limit_bytes=V7X_VMEM_LIMIT)


def _nt_dot(a, b):
    return lax.dot_general(a, b, (((1,), (1,)), ((), ())), preferred_element_type=F32)


def _tn_dot(a, b):
    return lax.dot_general(a, b, (((0,), (0,)), ((), ())), preferred_element_type=F32)


def _dot(a, b):
    return jnp.dot(a, b, preferred_element_type=F32)


def _layer_norm(v, g, b):
    mu = jnp.mean(v, axis=-1, keepdims=True)
    d = v - mu
    var = jnp.mean(d * d, axis=-1, keepdims=True)
    return d * lax.rsqrt(var + EPS) * g + b


def _sigmoid(v):
    return 0.5 * jnp.tanh(0.5 * v) + 0.5


def _head_slice(h):
    return slice(h * D_HEAD, (h + 1) * D_HEAD)


def _stack_heads(q, lane_sets):
    lane = lax.broadcasted_iota(jnp.int32, q.shape, 1)
    zero = jnp.zeros_like(q)
    parts = []
    for h in range(N_HEADS):
        keep = None
        for lo, hi in lane_sets(h):
            m = jnp.logical_and(lane >= lo, lane < hi)
            keep = m if keep is None else jnp.logical_or(keep, m)
        parts.append(jnp.where(keep, q, zero))
    return jnp.concatenate(parts, axis=0)


def _own_lanes(h):
    return ((h * D_HEAD, (h + 1) * D_HEAD),)


def _mla_lanes(h):
    base = N_HEADS * DN_A
    return ((h * DN_A, (h + 1) * DN_A), (base + h * DR_A, base + (h + 1) * DR_A))


def _store_heads(o_ref, parts):
    for h, p in enumerate(parts):
        o_ref[:, _head_slice(h)] = p.astype(o_ref.dtype)


def _expand_kernel(ckv_ref, kpe_ref, w_ref, kf_ref, v_ref):
    kvx = _dot(ckv_ref[...].astype(BF16), w_ref[...])
    kp = kpe_ref[...]
    kpt = kp + pltpu.roll(kp, DR_A, 1) + pltpu.roll(kp, 2 * DR_A, 1) + pltpu.roll(kp, 3 * DR_A, 1)
    kf_ref[:, :N_HEADS * DN_A] = kvx[:, :N_HEADS * DN_A].astype(BF16)
    kf_ref[:, N_HEADS * DN_A:] = kpt.astype(BF16)
    v_ref[...] = kvx[:, N_HEADS * DN_A:].astype(BF16)


def _expand_latent(ckv, kpe_padded, w_ukv, tm):
    m = ckv.shape[0]
    assert m % tm == 0
    return pl.pallas_call(
        _expand_kernel,
        grid=(m // tm,),
        in_specs=[pl.BlockSpec((tm, KV_RANK), lambda i: (i, 0)),
                  pl.BlockSpec((tm, LANE), lambda i: (i, 0)),
                  pl.BlockSpec((KV_RANK, 2 * BRANCH_W), lambda i: (0, 0))],
        out_specs=[pl.BlockSpec((tm, QA_W), lambda i: (i, 0)),
                   pl.BlockSpec((tm, BRANCH_W), lambda i: (i, 0))],
        out_shape=[jax.ShapeDtypeStruct((m, QA_W), BF16),
                   jax.ShapeDtypeStruct((m, BRANCH_W), BF16)],
        compiler_params=_params(("parallel",)),
        name="expand_latent",
    )(ckv, kpe_padded, w_ukv)


def _rope_block(x, cos, sin_a, sin_b, half):
    return x * cos + pltpu.roll(x, LANE - half, 1) * sin_a + pltpu.roll(x, half, 1) * sin_b


def _inproj_kernel(x_ref, w_ref, qn_ref, wuq_ref, kvn_ref, wukv_ref,
                   c32_ref, a32_ref, b32_ref, c64_ref, a64_ref, b64_ref, kdec_ref,
                   qa_ref, ckv_ref, kpe_ref, kf_ref, va_ref,
                   rq_ref, rk_ref, rkd_ref, rv_ref, rg_ref,
                   sq_ref, sk_ref, sv_ref, bq_ref, bk_ref, bv_ref):
    z = _dot(x_ref[...].astype(BF16), w_ref[...])

    def seg(name):
        o, w = _IN_OFF[name]
        return z[:, o:o + w]

    cq = seg("c_q")
    cqn = cq * lax.rsqrt(jnp.sum(cq * cq, axis=1, keepdims=True) * (1.0 / Q_RANK) + EPS) * qn_ref[...]
    qa = _dot(cqn.astype(BF16), wuq_ref[...])
    scale_a = DQK_A ** -0.5 * LOG2E
    nope_w = N_HEADS * DN_A
    qa_ref[:, :nope_w] = (qa[:, :nope_w] * scale_a).astype(BF16)
    q_pe = _rope_block(qa[:, nope_w:], c32_ref[...], a32_ref[...], b32_ref[...], DR_A // 2)
    qa_ref[:, nope_w:] = (q_pe * scale_a).astype(BF16)

    ckv_raw = seg("c_kv")
    ckv = ckv_raw * lax.rsqrt(jnp.mean(ckv_raw * ckv_raw, axis=1, keepdims=True) + EPS) * kvn_ref[...]
    ckv_ref[...] = ckv
    kvx = _dot(ckv.astype(BF16), wukv_ref[...])
    kp = _rope_block(seg("k_pe"), c32_ref[...], a32_ref[...], b32_ref[...], DR_A // 2)
    lane = lax.broadcasted_iota(jnp.int32, kp.shape, 1)
    kp = jnp.where(lane < DR_A, kp, 0.0)
    kpe_ref[...] = kp[:, :DR_A]
    kpt = kp + pltpu.roll(kp, DR_A, 1) + pltpu.roll(kp, 2 * DR_A, 1) + pltpu.roll(kp, 3 * DR_A, 1)
    kf_ref[:, :nope_w] = kvx[:, :nope_w].astype(BF16)
    kf_ref[:, nope_w:] = kpt.astype(BF16)
    va_ref[...] = kvx[:, nope_w:].astype(BF16)

    rq, rk = seg("rq"), seg("rk")
    kdec = kdec_ref[...]
    for blk in range(BRANCH_W // LANE):
        cols = slice(blk * LANE, (blk + 1) * LANE)
        rq_ref[:, cols] = _rope_block(rq[:, cols], c64_ref[...], a64_ref[...], b64_ref[...],
                                      D_HEAD // 2).astype(BF16)
        rkb = _rope_block(rk[:, cols], c64_ref[...], a64_ref[...], b64_ref[...], D_HEAD // 2) * (D_HEAD ** -0.5)
        rk_ref[:, cols] = rkb.astype(BF16)
        rkd_ref[:, cols] = (rkb * kdec[:, cols]).astype(BF16)
    rv_ref[...] = seg("rv").astype(BF16)
    rg_ref[...] = seg("rg")

    scale_h = D_HEAD ** -0.5 * LOG2E
    sq_ref[...] = (seg("sq") * scale_h).astype(BF16)
    sk_ref[...] = seg("sk")
    sv_ref[...] = seg("sv")
    bq_ref[...] = (seg("bq") * scale_h).astype(BF16)
    bk_ref[...] = seg("bk")
    bv_ref[...] = seg("bv")


_INPROJ_OUT = (
    (QA_W, BF16), (KV_RANK, F32), (DR_A, F32), (QA_W, BF16), (BRANCH_W, BF16),
    (BRANCH_W, BF16), (BRANCH_W, BF16), (BRANCH_W, BF16), (BRANCH_W, BF16), (BRANCH_W, F32),
    (BRANCH_W, BF16), (BRANCH_W, F32), (BRANCH_W, F32), (BRANCH_W, BF16), (BRANCH_W, F32), (BRANCH_W, F32))


def _inproj(x, w_in_p, qn, wuq, kvn, wukv, tables, *, tm):
    m = x.shape[0]
    tm = min(tm, m)
    assert m % tm == 0
    n_pos_tiles = tables[0].shape[0] // tm
    assert tables[0].shape[0] % tm == 0
    row = lambda i: (i, 0)
    const = lambda i: (0, 0)
    pos = lambda i: (i % n_pos_tiles, 0)
    in_specs = [pl.BlockSpec((tm, D_MODEL), row),
                pl.BlockSpec(w_in_p.shape, const),
                pl.BlockSpec(qn.shape, const),
                pl.BlockSpec(wuq.shape, const),
                pl.BlockSpec(kvn.shape, const),
                pl.BlockSpec(wukv.shape, const)]
    in_specs += [pl.BlockSpec((tm, t.shape[1]), pos) for t in tables]
    return pl.pallas_call(
        _inproj_kernel,
        grid=(m // tm,),
        in_specs=in_specs,
        out_specs=[pl.BlockSpec((tm, w), row) for w, _ in _INPROJ_OUT],
        out_shape=[jax.ShapeDtypeStruct((m, w), dt) for w, dt in _INPROJ_OUT],
        compiler_params=_params(("parallel",)),
        name="inproj",
    )(x, w_in_p, qn, wuq, kvn, wukv, *tables)


def _mla_kernel(q_ref, kfo_ref, vo_ref, kfp_ref, vp_ref, o_ref, *, tq, tk, n_own, n_past):
    qi = pl.program_id(1)
    qm = _stack_heads(q_ref[...], _mla_lanes)
    rows = N_HEADS * tq
    tko = kfo_ref.shape[0]

    def tile(k, v, carry, mask):
        m, l, accs = carry
        s = _nt_dot(qm, k)
        if mask is not None:
            s = jnp.where(mask, s, NEG_INF)
        m_new = jnp.maximum(m, jnp.max(s, axis=1, keepdims=True))
        p = jnp.exp2(s - m_new)
        a = jnp.exp2(m - m_new)
        l = a * l + jnp.sum(p, axis=1, keepdims=True)
        pb = p.astype(BF16)
        accs = tuple(a[h * tq:(h + 1) * tq] * accs[h]
                     + _dot(pb[h * tq:(h + 1) * tq], v[:, _head_slice(h)]) for h in range(N_HEADS))
        return m_new, l, accs

    row_q = lax.broadcasted_iota(jnp.int32, (rows, tko), 0) % tq
    col = lax.broadcasted_iota(jnp.int32, (rows, tko), 1)
    own_mask = jnp.logical_and(col < n_own, col // CHUNK <= row_q // CHUNK)
    carry = (jnp.full((rows, 1), NEG_INF, F32), jnp.zeros((rows, 1), F32),
             tuple(jnp.zeros((tq, D_HEAD), F32) for _ in range(N_HEADS)))
    carry = tile(kfo_ref[...], vo_ref[...], carry, own_mask)

    def body(j, c):
        start = pl.multiple_of(j * tk, tk)
        return tile(kfp_ref[pl.ds(start, tk), :], vp_ref[pl.ds(start, tk), :], c, None)

    _, l, accs = lax.fori_loop(0, qi if n_past is None else n_past, body, carry)
    _store_heads(o_ref, [accs[h] / l[h * tq:(h + 1) * tq] for h in range(N_HEADS)])


def _mla_attn(q, kf_own, v_own, kf_past, v_past, *, tq, tk, n_own, causal_tiles):
    b, lq, _ = q.shape
    nqt = lq // tq
    tko = kf_own.shape[1] // nqt
    lp = kf_past.shape[1]
    assert lq % tq == 0 and lp % tk == 0
    return pl.pallas_call(
        functools.partial(_mla_kernel, tq=tq, tk=tk, n_own=n_own,
                          n_past=None if causal_tiles else lp // tk),
        grid=(b, nqt),
        in_specs=[pl.BlockSpec((None, tq, QA_W), lambda i, t: (i, t, 0)),
                  pl.BlockSpec((None, tko, QA_W), lambda i, t: (i, t, 0)),
                  pl.BlockSpec((None, tko, BRANCH_W), lambda i, t: (i, t, 0)),
                  pl.BlockSpec((None, lp, QA_W), lambda i, t: (i, 0, 0)),
                  pl.BlockSpec((None, lp, BRANCH_W), lambda i, t: (i, 0, 0))],
        out_specs=pl.BlockSpec((None, tq, BRANCH_W), lambda i, t: (i, t, 0)),
        out_shape=jax.ShapeDtypeStruct((b, lq, BRANCH_W), BF16),
        compiler_params=_params(("parallel", "arbitrary")),
        name="mla_attn",
    )(q, kf_own, v_own, kf_past, v_past)


def _sb_kernel(q_ref, ko_ref, vo_ref, kp_ref, vp_ref, o_ref, *, tq, tk, n_own, n_past):
    qi = pl.program_id(1)
    qm = _stack_heads(q_ref[...], _own_lanes)
    rows = N_HEADS * tq
    tko = ko_ref.shape[0]

    def tri2(n):
        r = lax.broadcasted_iota(jnp.int32, (2 * n, n), 0) % n
        c = lax.broadcasted_iota(jnp.int32, (2 * n, n), 1)
        return jnp.where(r > c, 1.0, 0.0).astype(BF16)

    def tile(k, v, carry, mask, tri):
        run, accs = carry
        z = _nt_dot(qm, k.astype(BF16))
        nz = -z
        t = jnp.log2(1.0 + jnp.exp2(jnp.minimum(z, nz)))
        log_beta = jnp.minimum(z, 0.0) - t
        log_stay = jnp.minimum(nz, 0.0) - t
        if mask is not None:
            log_stay = jnp.where(mask, log_stay, 0.0)
        hi = log_stay.astype(BF16)
        lo = (log_stay - hi.astype(F32)).astype(BF16)
        later = _dot(jnp.concatenate([hi, lo], axis=1), tri) + run
        w = jnp.exp2(log_beta + later)
        if mask is not None:
            w = jnp.where(mask, w, 0.0)
        wb = w.astype(BF16)
        vb = v.astype(BF16)
        accs = tuple(accs[h] + _dot(wb[h * tq:(h + 1) * tq], vb[:, _head_slice(h)])
                     for h in range(N_HEADS))
        return run + jnp.sum(log_stay, axis=1, keepdims=True), accs

    row_q = lax.broadcasted_iota(jnp.int32, (rows, tko), 0) % tq
    col = lax.broadcasted_iota(jnp.int32, (rows, tko), 1)
    own_mask = jnp.logical_and(col < n_own, col < row_q)
    carry = (jnp.zeros((rows, 1), F32), tuple(jnp.zeros((tq, D_HEAD), F32) for _ in range(N_HEADS)))
    carry = tile(ko_ref[...], vo_ref[...], carry, own_mask, tri2(tko))
    tri_past = tri2(tk)
    n_loop = qi if n_past is None else n_past

    def body(jj, c):
        start = pl.multiple_of((n_loop - 1 - jj) * tk, tk)
        return tile(kp_ref[pl.ds(start, tk), :], vp_ref[pl.ds(start, tk), :], c, None, tri_past)

    _, accs = lax.fori_loop(0, n_loop, body, carry)
    _store_heads(o_ref, accs)


def _sb_attn(q, k_own, v_own, k_past, v_past, *, tq, tk, n_own, causal_tiles):
    b, lq, _ = q.shape
    nqt = lq // tq
    tko = k_own.shape[1] // nqt
    lp = k_past.shape[1]
    assert lq % tq == 0 and lp % tk == 0
    return pl.pallas_call(
        functools.partial(_sb_kernel, tq=tq, tk=tk, n_own=n_own,
                          n_past=None if causal_tiles else lp // tk),
        grid=(b, nqt),
        in_specs=[pl.BlockSpec((None, tq, BRANCH_W), lambda i, t: (i, t, 0)),
                  pl.BlockSpec((None, tko, BRANCH_W), lambda i, t: (i, t, 0)),
                  pl.BlockSpec((None, tko, BRANCH_W), lambda i, t: (i, t, 0)),
                  pl.BlockSpec((None, lp, BRANCH_W), lambda i, t: (i, 0, 0)),
                  pl.BlockSpec((None, lp, BRANCH_W), lambda i, t: (i, 0, 0))],
        out_specs=pl.BlockSpec((None, tq, BRANCH_W), lambda i, t: (i, t, 0)),
        out_shape=jax.ShapeDtypeStruct((b, lq, BRANCH_W), BF16),
        compiler_params=_params(("parallel", "arbitrary")),
        name="sb_attn",
    )(q, k_own, v_own, k_past, v_past)


def _ret_kernel(q_ref, k_ref, kd_ref, v_ref, rg_ref, s0_ref, dec_ref, qdec_ref, gl_ref,
                gng_ref, gnb_ref, o_ref, sout_ref, state_ref, *, lc):
    c = pl.program_id(1)

    @pl.when(c == 0)
    def _():
        state_ref[...] = s0_ref[...]

    qm = _stack_heads(q_ref[...], _own_lanes)
    v = v_ref[...]
    state = state_ref[...]
    scores = (_nt_dot(qm, k_ref[...]) * dec_ref[...]).astype(BF16)
    cross = _dot(qm, state.astype(BF16)) * qdec_ref[...]
    kv_full = _tn_dot(kd_ref[...], v)
    new_state = gl_ref[...] * state + jnp.concatenate(
        [kv_full[_head_slice(h), _head_slice(h)] for h in range(N_HEADS)], axis=0)
    state_ref[...] = new_state

    rg = rg_ref[...]
    outs = []
    for h in range(N_HEADS):
        o = _dot(scores[h * lc:(h + 1) * lc], v[:, _head_slice(h)]) + cross[h * lc:(h + 1) * lc]
        mu = jnp.mean(o, axis=-1, keepdims=True)
        d = o - mu
        var = jnp.mean(d * d, axis=-1, keepdims=True)
        y = d * lax.rsqrt(var + EPS) * gng_ref[h] + gnb_ref[h]
        g = rg[:, _head_slice(h)]
        outs.append(y * (g * _sigmoid(g)))
    _store_heads(o_ref, outs)

    @pl.when(c == pl.num_programs(1) - 1)
    def _():
        sout_ref[...] = new_state


def _retention(q, k, kd, v, rg, s0, dec, qdec, gl, gng, gnb, *, lc):
    b, length, _ = q.shape
    assert length % lc == 0
    seq = lambda i, t: (i, t, 0)
    st = lambda i, t: (i, 0, 0)
    c2 = lambda i, t: (0, 0)
    c3 = lambda i, t: (0, 0, 0)
    return pl.pallas_call(
        functools.partial(_ret_kernel, lc=lc),
        grid=(b, length // lc),
        in_specs=[pl.BlockSpec((None, lc, BRANCH_W), seq)] * 5
        + [pl.BlockSpec((None, BRANCH_W, D_HEAD), st),
           pl.BlockSpec(dec.shape, c2), pl.BlockSpec(qdec.shape, c2), pl.BlockSpec(gl.shape, c2),
           pl.BlockSpec(gng.shape, c3), pl.BlockSpec(gnb.shape, c3)],
        out_specs=[pl.BlockSpec((None, lc, BRANCH_W), seq),
                   pl.BlockSpec((None, BRANCH_W, D_HEAD), st)],
        out_shape=[jax.ShapeDtypeStruct((b, length, BRANCH_W), BF16),
                   jax.ShapeDtypeStruct((b, BRANCH_W, D_HEAD), F32)],
        scratch_shapes=[pltpu.VMEM((BRANCH_W, D_HEAD), F32)],
        compiler_params=_params(("parallel", "arbitrary")),
        name="retention",
    )(q, k, kd, v, rg, s0, dec, qdec, gl, gng, gnb)


def _band_kernel(q_ref, k_ref, v_ref, bias_ref, o_ref, *, tq, win, back):
    qi = pl.program_id(1)
    start = pl.multiple_of(jnp.maximum(qi - back, 0) * tq, tq)
    k = k_ref[pl.ds(start, win), :].astype(BF16)
    v = v_ref[pl.ds(start, win), :].astype(BF16)
    qm = _stack_heads(q_ref[...], _own_lanes)
    s = _nt_dot(qm, k) + bias_ref[...]
    m = jnp.max(s, axis=1, keepdims=True)
    p = jnp.exp2(s - m)
    l = jnp.sum(p, axis=1, keepdims=True)
    pb = p.astype(BF16)
    _store_heads(o_ref, [_dot(pb[h * tq:(h + 1) * tq], v[:, _head_slice(h)]) / l[h * tq:(h + 1) * tq]
                         for h in range(N_HEADS)])


def _band_attn(q, k, v, bias, *, tq, win, back):
    b, lq, _ = q.shape
    lk = k.shape[1]
    nvar = bias.shape[0]
    assert lq % tq == 0
    return pl.pallas_call(
        functools.partial(_band_kernel, tq=tq, win=win, back=back),
        grid=(b, lq // tq),
        in_specs=[pl.BlockSpec((None, tq, BRANCH_W), lambda i, t: (i, t, 0)),
                  pl.BlockSpec((None, lk, BRANCH_W), lambda i, t: (i, 0, 0)),
                  pl.BlockSpec((None, lk, BRANCH_W), lambda i, t: (i, 0, 0)),
                  pl.BlockSpec((None, N_HEADS * tq, win), lambda i, t: (jnp.minimum(t, nvar - 1), 0, 0))],
        out_specs=pl.BlockSpec((None, tq, BRANCH_W), lambda i, t: (i, t, 0)),
        out_shape=jax.ShapeDtypeStruct((b, lq, BRANCH_W), BF16),
        compiler_params=_params(("parallel", "arbitrary")),
        name="band_attn",
    )(q, k, v, bias)


def _merge_kernel(x_ref, ba_ref, bb_ref, bc_ref, bd_ref, wg_ref, wb_ref, wo_ref, g_ref, b_ref, o_ref):
    x = x_ref[...]
    xb = x.astype(BF16)
    merged = None
    for n, br_ref in enumerate((ba_ref, bb_ref, bc_ref, bd_ref)):
        logits = _dot(xb, wg_ref[:, n * D_MODEL:(n + 1) * D_MODEL])
        term = _dot(br_ref[...], wb_ref[n]) * _sigmoid(logits)
        merged = term if merged is None else merged + term
    mix = _dot(merged.astype(BF16), wo_ref[...])
    o_ref[...] = _layer_norm(ALPHA * x + mix, g_ref[...], b_ref[...])


def _merge(x, branches, wg, wb, wo, g, b, *, tm):
    m = x.shape[0]
    tm = min(tm, m)
    assert m % tm == 0
    const2 = lambda i: (0, 0)
    row = lambda i: (i, 0)
    return pl.pallas_call(
        _merge_kernel,
        grid=(m // tm,),
        in_specs=[pl.BlockSpec((tm, D_MODEL), row)]
        + [pl.BlockSpec((tm, BRANCH_W), row)] * N_BRANCH
        + [pl.BlockSpec((D_MODEL, N_BRANCH * D_MODEL), const2),
           pl.BlockSpec((N_BRANCH, BRANCH_W, D_MODEL), lambda i: (0, 0, 0)),
           pl.BlockSpec((D_MODEL, D_MODEL), const2),
           pl.BlockSpec((1, D_MODEL), const2),
           pl.BlockSpec((1, D_MODEL), const2)],
        out_specs=pl.BlockSpec((tm, D_MODEL), row),
        out_shape=jax.ShapeDtypeStruct((m, D_MODEL), F32),
        compiler_params=_params(("parallel",)),
        name="merge",
    )(x, *branches, wg, wb, wo, g, b)


def _route(aff_t, sel_t):
    def top2_sum(a, b, c, d):
        hi1, lo1 = jnp.maximum(a, b), jnp.minimum(a, b)
        hi2, lo2 = jnp.maximum(c, d), jnp.minimum(c, d)
        return jnp.maximum(hi1, hi2) + jnp.maximum(jnp.minimum(hi1, hi2), jnp.maximum(lo1, lo2))

    score = [top2_sum(*sel_t[g * EXPERTS_PER_GROUP:(g + 1) * EXPERTS_PER_GROUP])
             for g in range(N_GROUPS)]
    best_here = []
    for g in range(N_GROUPS):
        ok = None
        for o in range(N_GROUPS):
            if o == g:
                continue
            c = (score[g] > score[o]) if o < g else (score[g] >= score[o])
            ok = c if ok is None else jnp.logical_and(ok, c)
        best_here.append(ok)
    picked = []
    for e in range(N_EXPERTS):
        g = e // EXPERTS_PER_GROUP
        rank = jnp.zeros_like(sel_t[e])
        for o in range(g * EXPERTS_PER_GROUP, (g + 1) * EXPERTS_PER_GROUP):
            if o == e:
                continue
            ahead = (sel_t[o] >= sel_t[e]) if o < e else (sel_t[o] > sel_t[e])
            rank = rank + jnp.where(ahead, 1.0, 0.0)
        picked.append(jnp.where(jnp.logical_and(best_here[g], rank < TOP_K), aff_t[e], 0.0))
    total = picked[0]
    for e in range(1, N_EXPERTS):
        total = total + picked[e]
    return [p / total for p in picked]


def _moe_kernel(x_ref, wrh_ref, wrl_ref, br_ref, wgu_ref, wd_ref, g_ref, b_ref, o_ref, acc_ref):
    x = x_ref[...]
    xh = x.astype(BF16)
    xl = (x - xh.astype(F32)).astype(BF16)
    wrh = wrh_ref[...]
    logits = _dot(xh, wrh) + _dot(xl, wrh) + _dot(xh, wrl_ref[...])
    aff = _sigmoid(logits).T
    bias = br_ref[...]
    aff_t = [aff[e:e + 1, :] for e in range(N_EXPERTS)]
    sel_t = [aff_t[e] + bias[e:e + 1, :] for e in range(N_EXPERTS)]
    gate_rows = _route(aff_t, sel_t)
    tm = x.shape[0]
    gate_t = jnp.concatenate(gate_rows + [jnp.zeros((LANE - N_EXPERTS, tm), F32)], axis=0)
    gate = gate_t.T

    for e in range(N_EXPERTS):
        gu = _dot(xh, wgu_ref[e])
        gt, up = gu[:, :D_EXPERT], gu[:, D_EXPERT:]
        hmid = gt * _sigmoid(gt) * up
        y = _dot(hmid.astype(BF16), wd_ref[e]) * gate[:, e:e + 1]
        if e == 0:
            acc_ref[...] = y
        else:
            acc_ref[...] += y
    o_ref[...] = _layer_norm(ALPHA * x + acc_ref[...], g_ref[...], b_ref[...])


def _moe(x, wrh, wrl, br, wgu, wd, g, b, *, tm):
    m = x.shape[0]
    tm = min(tm, m)
    assert m % tm == 0
    const2 = lambda i: (0, 0)
    const3 = lambda i: (0, 0, 0)
    return pl.pallas_call(
        _moe_kernel,
        grid=(m // tm,),
        in_specs=[pl.BlockSpec((tm, D_MODEL), lambda i: (i, 0)),
                  pl.BlockSpec((D_MODEL, LANE), const2),
                  pl.BlockSpec((D_MODEL, LANE), const2),
                  pl.BlockSpec((LANE, 1), const2),
                  pl.BlockSpec((N_EXPERTS, D_MODEL, 2 * D_EXPERT), const3, pipeline_mode=pl.Buffered(1)),
                  pl.BlockSpec((N_EXPERTS, D_EXPERT, D_MODEL), const3, pipeline_mode=pl.Buffered(1)),
                  pl.BlockSpec((1, D_MODEL), const2),
                  pl.BlockSpec((1, D_MODEL), const2)],
        out_specs=pl.BlockSpec((tm, D_MODEL), lambda i: (i, 0)),
        out_shape=jax.ShapeDtypeStruct((m, D_MODEL), F32),
        scratch_shapes=[pltpu.VMEM((tm, D_MODEL), F32)],
        compiler_params=_params(("parallel",)),
        name="moe",
    )(x, wrh, wrl, br, wgu, wd, g, b)


def _rope_tables(pos, d):
    half = d // 2
    inv = jnp.power(ROPE_BASE, -jnp.arange(half, dtype=F32) / half)
    ang = pos.astype(F32)[:, None] * inv[None, :]
    cos, sin = jnp.cos(ang), jnp.sin(ang)
    zero = jnp.zeros_like(sin)
    rep = LANE // d
    cos_t = jnp.tile(jnp.concatenate([cos, cos], axis=1), (1, rep))
    sin_a = jnp.tile(jnp.concatenate([-sin, zero], axis=1), (1, rep))
    sin_b = jnp.tile(jnp.concatenate([zero, sin], axis=1), (1, rep))
    return cos_t, sin_a, sin_b


def _retention_tables(lc):
    log_g = jnp.log1p(-jnp.exp2(-5.0 - jnp.arange(N_HEADS, dtype=F32)))
    i = jnp.arange(lc, dtype=F32)
    diff = i[:, None] - i[None, :]
    dec = jnp.where(diff >= 0, jnp.exp(jnp.maximum(diff, 0.0)[None] * log_g[:, None, None]), 0.0)
    qdec = jnp.exp((i[None, :] + 1.0) * log_g[:, None])
    kdec = jnp.exp((lc - 1.0 - i)[None, :] * log_g[:, None])
    gl = jnp.exp(lc * log_g)
    dec = dec.reshape(N_HEADS * lc, lc)
    qdec = jnp.broadcast_to(qdec[:, :, None], (N_HEADS, lc, D_HEAD)).reshape(N_HEADS * lc, D_HEAD)
    gl = jnp.broadcast_to(gl[:, None, None], (N_HEADS, D_HEAD, D_HEAD)).reshape(BRANCH_W, D_HEAD)
    kdec = jnp.repeat(kdec.T, D_HEAD, axis=1)
    return dec, qdec, kdec, gl


def _band_bias(rel_bias, tq, win, q_minus_k0, valid):
    length = tq + win - 1
    d = np.arange(length) - (tq - 1) - q_minus_k0
    idx = np.clip(d, -REL_CLIP, REL_CLIP) + REL_CLIP
    g = rel_bias[:, idx].astype(F32) * LOG2E
    gp = jnp.concatenate([g, jnp.zeros((N_HEADS, 1), F32)], axis=1)
    m = jnp.tile(gp, (1, tq))[:, :tq * length].reshape(N_HEADS, tq, length)
    tile = m[:, :, tq - 1:tq - 1 + win]
    return jnp.where(valid[None], tile, NEG_INF).reshape(N_HEADS * tq, win)


def _pack_w_in(w_in_l):
    cols = []
    src = 0
    for w in _IN_WIDTH:
        seg = w_in_l[:, src:src + w]
        cols.append(jnp.pad(seg, ((0, 0), (0, _round_up(w, LANE) - w))))
        src += w
    return jnp.concatenate(cols, axis=1).astype(BF16)


def _pad_rows(t, n):
    return jnp.pad(t, ((0, 0), (0, n - t.shape[1]), (0, 0)))


def _token_mixers(x, pos0, past, lw, *, prompt):
    (w_in_p, qn, wuq, kvn, wukv, gn_g, gn_b, rel_bias, b, length) = lw
    assert pos0 % CHUNK == 0
    pos = pos0 + jnp.arange(length)
    lc = 256 if prompt else length
    dec, qdec, kdec, gl = _retention_tables(lc)
    tables = list(_rope_tables(pos, DR_A)) + list(_rope_tables(pos, D_HEAD)) + [jnp.tile(kdec, (length // lc, 1))]
    if not prompt:
        tables = [jnp.tile(t, (b, 1)) for t in tables]
    outs = _inproj(x, w_in_p, qn, wuq, kvn, wukv, tables, tm=512 if prompt else b * length)
    (q_a, ckv, kpe, kf, v_a, rq, rk, rkd, rv, rg, sq, sk, sv, bq, bk, bv) = [
        o.reshape(b, length, o.shape[-1]) for o in outs]

    if prompt:
        tq = 256
        o_a = _mla_attn(q_a, kf, v_a, kf, v_a, tq=tq, tk=tq, n_own=tq, causal_tiles=True)
        o_c = _sb_attn(sq, sk, sv, sk, sv, tq=tq, tk=tq, n_own=tq, causal_tiles=True)
        s0 = jnp.zeros((b, BRANCH_W, D_HEAD), F32)
        win = 3 * tq
        i = np.arange(tq)[:, None]
        c = np.arange(win)[None, :]
        variants = []
        for t in range(3):
            qc, kc = i // CHUNK + t * (tq // CHUNK), c // CHUNK
            variants.append(_band_bias(rel_bias, tq, win, t * tq, (kc <= qc) & (kc >= qc - PREV_CHUNKS)))
        bias = jnp.stack(variants)
        o_d = _band_attn(bq, bk, bv, bias, tq=tq, win=win, back=2)
    else:
        c_ckv, c_kpe, s_prev, c_sk, c_sv, c_bk, c_bv = past
        n_past = c_ckv.shape[1]
        tko = LANE
        kf_c, v_c = _expand_latent(c_ckv.reshape(b * n_past, KV_RANK),
                                   jnp.pad(c_kpe.reshape(b * n_past, DR_A), ((0, 0), (0, LANE - DR_A))),
                                   wukv, 512)
        o_a = _mla_attn(q_a, _pad_rows(kf, tko), _pad_rows(v_a, tko),
                        kf_c.reshape(b, n_past, QA_W), v_c.reshape(b, n_past, BRANCH_W),
                        tq=length, tk=256, n_own=length, causal_tiles=False)
        o_c = _sb_attn(sq, _pad_rows(sk, tko), _pad_rows(sv, tko),
                       c_sk.reshape(b, n_past, BRANCH_W), c_sv.reshape(b, n_past, BRANCH_W),
                       tq=length, tk=256, n_own=length, causal_tiles=False)
        s0 = s_prev.reshape(b, BRANCH_W, D_HEAD)
        n_band = c_bk.shape[1]
        n_keys = n_band + length
        win = _round_up(n_keys, LANE)
        bk_all = _pad_rows(jnp.concatenate([c_bk.reshape(b, n_band, BRANCH_W), bk], axis=1), win)
        bv_all = _pad_rows(jnp.concatenate([c_bv.reshape(b, n_band, BRANCH_W), bv], axis=1), win)
        k_pos = pos0 - n_band + np.arange(win)
        q_pos = pos0 + np.arange(length)
        qc, kc = q_pos[:, None] // CHUNK, k_pos[None, :] // CHUNK
        valid = (np.arange(win)[None, :] < n_keys) & (k_pos[None, :] >= 0) & (kc <= qc) & (kc >= qc - PREV_CHUNKS)
        bias = _band_bias(rel_bias, length, win, n_band, valid)[None]
        o_d = _band_attn(bq, bk_all, bv_all, bias, tq=length, win=win, back=0)

    o_r, s_ret = _retention(rq, rk, rkd, rv, rg, s0, dec, qdec, gl,
                            gn_g.reshape(N_HEADS, 1, D_HEAD), gn_b.reshape(N_HEADS, 1, D_HEAD), lc=lc)
    s_ret = s_ret.reshape(b, N_HEADS, D_HEAD, D_HEAD)

    flat = lambda t: t.reshape(b * length, BRANCH_W)
    heads4 = lambda t: t.reshape(b, length, N_HEADS, D_HEAD)
    if prompt:
        keep = min(PREV_CHUNKS * CHUNK, length)
        new_state = (ckv, kpe, s_ret, heads4(sk), heads4(sv),
                     heads4(bk)[:, length - keep:], heads4(bv)[:, length - keep:])
    else:
        new_state = (ckv, kpe, s_ret, heads4(sk), heads4(sv), heads4(bk), heads4(bv))
    return (flat(o_a), flat(o_r), flat(o_c), flat(o_d)), new_state


def kernel(x_prompt, x_sample, cache_mla_ckv, cache_mla_kpe, state_ret, cache_sb_k, cache_sb_v, cache_band_k, cache_band_v, w_in, mla_q_norm, mla_w_uq, mla_kv_norm, mla_w_ukv, ret_gn_g, ret_gn_b, band_rel_bias, w_branch, w_o, ln1_g, ln1_b, w_router, b_router, w_exp_gate, w_exp_up, w_exp_down, ln2_g, ln2_b):
    bp, lp, _ = x_prompt.shape
    bs, ls, _ = x_sample.shape
    past_len = cache_mla_ckv.shape[2]
    xp = x_prompt.reshape(bp * lp, D_MODEL)
    xs = x_sample.reshape(bs * ls, D_MODEL)

    wr = jnp.pad(w_router, ((0, 0), (0, LANE - N_EXPERTS)))
    wrh = wr.astype(BF16)
    wrl = (wr - wrh.astype(F32)).astype(BF16)
    br = jnp.pad(b_router, (0, LANE - N_EXPERTS)).reshape(LANE, 1)

    st_p, st_s = [], []
    for l in range(DEPTH):
        wuq = mla_w_uq[l]
        wuq = jnp.concatenate([wuq[:, :, :DN_A].reshape(Q_RANK, -1), wuq[:, :, DN_A:].reshape(Q_RANK, -1)], axis=1)
        wuq = jnp.pad(wuq, ((0, _round_up(Q_RANK, LANE) - Q_RANK), (0, 0))).astype(BF16)
        wukv = mla_w_ukv[l]
        wukv = jnp.concatenate([wukv[:, :, :DN_A].reshape(KV_RANK, -1), wukv[:, :, DN_A:].reshape(KV_RANK, -1)],
                               axis=1).astype(BF16)
        qn = jnp.pad(mla_q_norm[l], (0, _round_up(Q_RANK, LANE) - Q_RANK)).reshape(1, -1)
        kvn = mla_kv_norm[l].reshape(1, KV_RANK)
        w_in_p = _pack_w_in(w_in[l])
        wg = w_in[l][:, GATE_COL0:].astype(BF16)
        wb = w_branch[l].astype(BF16)
        wo = w_o[l].astype(BF16)
        g1, b1 = ln1_g[l].reshape(1, D_MODEL), ln1_b[l].reshape(1, D_MODEL)
        g2, b2 = ln2_g[l].reshape(1, D_MODEL), ln2_b[l].reshape(1, D_MODEL)
        wgu = jnp.concatenate([w_exp_gate[l], w_exp_up[l]], axis=-1).astype(BF16)
        wd = w_exp_down[l].astype(BF16)
        past = (cache_mla_ckv[l], cache_mla_kpe[l], state_ret[l], cache_sb_k[l], cache_sb_v[l],
                cache_band_k[l], cache_band_v[l])
        lw = (w_in_p, qn, wuq, kvn, wukv, ret_gn_g[l], ret_gn_b[l], band_rel_bias[l])

        br_p, new_p = _token_mixers(xp, 0, None, lw + (bp, lp), prompt=True)
        br_s, new_s = _token_mixers(xs, past_len, past, lw + (bs, ls), prompt=False)
        xp = _merge(xp, br_p, wg, wb, wo, g1, b1, tm=256)
        xs = _merge(xs, br_s, wg, wb, wo, g1, b1, tm=256)
        xp = _moe(xp, wrh, wrl, br, wgu, wd, g2, b2, tm=256)
        xs = _moe(xs, wrh, wrl, br, wgu, wd, g2, b2, tm=256)
        st_p.append(new_p)
        st_s.append(new_s)

    stack = lambda states, i: jnp.stack([s[i] for s in states], axis=0)
    return ((xp.reshape(bp, lp, D_MODEL), xs.reshape(bs, ls, D_MODEL))
            + tuple(stack(st_p, i) for i in range(7))
            + tuple(stack(st_s, i) for i in range(7)))
```

```python
import functools

import jax
import jax.numpy as jnp
import numpy as np
from jax import lax
from jax.experimental import pallas as pl
from jax.experimental.pallas import tpu as pltpu

D_MODEL = 1024
DEPTH = 2
CHUNK = 64
N_BRANCH = 4
BRANCH_W = D_MODEL // 4
N_HEADS = 4
D_HEAD = BRANCH_W // N_HEADS
DN_A = 64
DR_A = 32
DQK_A = DN_A + DR_A
Q_RANK = (3 * D_MODEL) // 16
KV_RANK = D_MODEL // 8
PREV_CHUNKS = 8
REL_CLIP = 128
ROPE_BASE = 10000.0
N_EXPERTS = 16
N_GROUPS = 4
EXPERTS_PER_GROUP = N_EXPERTS // N_GROUPS
TOP_K = 2
D_EXPERT = D_MODEL // 4
ALPHA = (2.0 * DEPTH) ** 0.25
EPS = 1e-5
NEG_INF = -1e30
LOG2E = 1.4426950408889634
SB_DEAD_LOG2 = -150.0

F32 = jnp.float32
BF16 = jnp.bfloat16

V7X_VMEM_LIMIT = 56 * 1024 * 1024
LANE = 128
TOKEN_TILE = 512

_IN_NAMES = ("c_q", "c_kv", "k_pe", "rq", "rk", "rv", "rg", "sq", "sk", "sv", "bq", "bk", "bv")
_IN_WIDTH = (Q_RANK, KV_RANK, DR_A) + (BRANCH_W,) * 10
QA_W = N_HEADS * DN_A + N_HEADS * DR_A


def _round_up(n, m):
    return (n + m - 1) // m * m


_IN_OFF = {}
_off = 0
for _n, _w in zip(_IN_NAMES, _IN_WIDTH):
    _IN_OFF[_n] = (_off, _round_up(_w, LANE))
    _off += _round_up(_w, LANE)
IN_PACKED = _off
GATE_COL0 = sum(_IN_WIDTH)


def _params(sem):
    return pltpu.CompilerParams(dimension_semantics=sem, vmem_limit_bytes=V7X_VMEM_LIMIT)


def _nt_dot(a, b):
    return lax.dot_general(a, b, (((1,), (1,)), ((), ())), preferred_element_type=F32)


def _tn_dot(a, b):
    return lax.dot_general(a, b, (((0,), (0,)), ((), ())), preferred_element_type=F32)


def _dot(a, b):
    return jnp.dot(a, b, preferred_element_type=F32)


def _layer_norm(v, g, b):
    mu = jnp.mean(v, axis=-1, keepdims=True)
    d = v - mu
    var = jnp.mean(d * d, axis=-1, keepdims=True)
    return d * lax.rsqrt(var + EPS) * g + b


def _sigmoid(v):
    return 0.5 * jnp.tanh(0.5 * v) + 0.5


def _head_slice(h):
    return slice(h * D_HEAD, (h + 1) * D_HEAD)


def _stack_heads(q, lane_sets):
    lane = lax.broadcasted_iota(jnp.int32, q.shape, 1)
    zero = jnp.zeros_like(q)
    parts = []
    for h in range(N_HEADS):
        keep = None
        for lo, hi in lane_sets(h):
            m = jnp.logical_and(lane >= lo, lane < hi)
            keep = m if keep is None else jnp.logical_or(keep, m)
        parts.append(jnp.where(keep, q, zero))
    return jnp.concatenate(parts, axis=0)


def _own_lanes(h):
    return ((h * D_HEAD, (h + 1) * D_HEAD),)


def _mla_lanes(h):
    base = N_HEADS * DN_A
    return ((h * DN_A, (h + 1) * DN_A), (base + h * DR_A, base + (h + 1) * DR_A))


def _store_heads(o_ref, parts):
    for h, p in enumerate(parts):
        o_ref[:, _head_slice(h)] = p.astype(o_ref.dtype)


def _expand_kernel(ckv_ref, kpe_ref, w_ref, kf_ref, v_ref):
    kvx = _dot(ckv_ref[...].astype(BF16), w_ref[...])
    kp = kpe_ref[...]
    kpt = kp + pltpu.roll(kp, DR_A, 1) + pltpu.roll(kp, 2 * DR_A, 1) + pltpu.roll(kp, 3 * DR_A, 1)
    kf_ref[:, :N_HEADS * DN_A] = kvx[:, :N_HEADS * DN_A].astype(BF16)
    kf_ref[:, N_HEADS * DN_A:] = kpt.astype(BF16)
    v_ref[...] = kvx[:, N_HEADS * DN_A:].astype(BF16)


def _expand_latent(ckv, kpe_padded, w_ukv, tm):
    m = ckv.shape[0]
    assert m % tm == 0
    return pl.pallas_call(
        _expand_kernel,
        grid=(m // tm,),
        in_specs=[pl.BlockSpec((tm, KV_RANK), lambda i: (i, 0)),
                  pl.BlockSpec((tm, LANE), lambda i: (i, 0)),
                  pl.BlockSpec((KV_RANK, 2 * BRANCH_W), lambda i: (0, 0))],
        out_specs=[pl.BlockSpec((tm, QA_W), lambda i: (i, 0)),
                   pl.BlockSpec((tm, BRANCH_W), lambda i: (i, 0))],
        out_shape=[jax.ShapeDtypeStruct((m, QA_W), BF16),
                   jax.ShapeDtypeStruct((m, BRANCH_W), BF16)],
        compiler_params=_params(("parallel",)),
        name="expand_latent",
    )(ckv, kpe_padded, w_ukv)


def _rope_block(x, cos, sin_a, sin_b, half):
    return x * cos + pltpu.roll(x, LANE - half, 1) * sin_a + pltpu.roll(x, half, 1) * sin_b


def _inproj_kernel(x_ref, w_ref, qn_ref, wuq_ref, kvn_ref, wukv_ref,
                   c32_ref, a32_ref, b32_ref, c64_ref, a64_ref, b64_ref, kdec_ref,
                   qa_ref, ckv_ref, kpe_ref, kf_ref, va_ref,
                   rq_ref, rk_ref, rkd_ref, rv_ref, rg_ref,
                   sq_ref, sk_ref, sv_ref, bq_ref, bk_ref, bv_ref):
    z = _dot(x_ref[...].astype(BF16), w_ref[...])

    def seg(name):
        o, w = _IN_OFF[name]
        return z[:, o:o + w]

    cq = seg("c_q")
    cqn = cq * lax.rsqrt(jnp.sum(cq * cq, axis=1, keepdims=True) * (1.0 / Q_RANK) + EPS) * qn_ref[...]
    qa = _dot(cqn.astype(BF16), wuq_ref[...])
    scale_a = DQK_A ** -0.5 * LOG2E
    nope_w = N_HEADS * DN_A
    qa_ref[:, :nope_w] = (qa[:, :nope_w] * scale_a).astype(BF16)
    q_pe = _rope_block(qa[:, nope_w:], c32_ref[...], a32_ref[...], b32_ref[...], DR_A // 2)
    qa_ref[:, nope_w:] = (q_pe * scale_a).astype(BF16)

    ckv_raw = seg("c_kv")
    ckv = ckv_raw * lax.rsqrt(jnp.mean(ckv_raw * ckv_raw, axis=1, keepdims=True) + EPS) * kvn_ref[...]
    ckv_ref[...] = ckv
    kvx = _dot(ckv.astype(BF16), wukv_ref[...])
    kp = _rope_block(seg("k_pe"), c32_ref[...], a32_ref[...], b32_ref[...], DR_A // 2)
    kpe_ref[...] = kp[:, :DR_A]
    kpt = kp + pltpu.roll(kp, DR_A, 1) + pltpu.roll(kp, 2 * DR_A, 1) + pltpu.roll(kp, 3 * DR_A, 1)
    kf_ref[:, :nope_w] = kvx[:, :nope_w].astype(BF16)
    kf_ref[:, nope_w:] = kpt.astype(BF16)
    va_ref[...] = kvx[:, nope_w:].astype(BF16)

    rq, rk = seg("rq"), seg("rk")
    kdec = kdec_ref[...]
    for blk in range(BRANCH_W // LANE):
        cols = slice(blk * LANE, (blk + 1) * LANE)
        rq_ref[:, cols] = _rope_block(rq[:, cols], c64_ref[...], a64_ref[...], b64_ref[...],
                                      D_HEAD // 2).astype(BF16)
        rkb = _rope_block(rk[:, cols], c64_ref[...], a64_ref[...], b64_ref[...], D_HEAD // 2) * (D_HEAD ** -0.5)
        rk_ref[:, cols] = rkb.astype(BF16)
        rkd_ref[:, cols] = (rkb * kdec[:, cols]).astype(BF16)
    rv_ref[...] = seg("rv").astype(BF16)
    rg_ref[...] = seg("rg")

    scale_h = D_HEAD ** -0.5 * LOG2E
    sq_ref[...] = (seg("sq") * scale_h).astype(BF16)
    sk_ref[...] = seg("sk")
    sv_ref[...] = seg("sv")
    bq_ref[...] = (seg("bq") * scale_h).astype(BF16)
    bk_ref[...] = seg("bk")
    bv_ref[...] = seg("bv")


_INPROJ_OUT = (
    (QA_W, BF16), (KV_RANK, F32), (DR_A, F32), (QA_W, BF16), (BRANCH_W, BF16),
    (BRANCH_W, BF16), (BRANCH_W, BF16), (BRANCH_W, BF16), (BRANCH_W, BF16), (BRANCH_W, F32),
    (BRANCH_W, BF16), (BRANCH_W, F32), (BRANCH_W, F32), (BRANCH_W, BF16), (BRANCH_W, F32), (BRANCH_W, F32))


def _inproj(x, w_in_p, qn, wuq, kvn, wukv, tables, *, tm):
    m = x.shape[0]
    tm = min(tm, m)
    assert m % tm == 0
    n_pos_tiles = tables[0].shape[0] // tm
    assert tables[0].shape[0] % tm == 0
    row = lambda i: (i, 0)
    const = lambda i: (0, 0)
    pos = lambda i: (i % n_pos_tiles, 0)
    in_specs = [pl.BlockSpec((tm, D_MODEL), row),
                pl.BlockSpec(w_in_p.shape, const),
                pl.BlockSpec(qn.shape, const),
                pl.BlockSpec(wuq.shape, const),
                pl.BlockSpec(kvn.shape, const),
                pl.BlockSpec(wukv.shape, const)]
    in_specs += [pl.BlockSpec((tm, t.shape[1]), pos) for t in tables]
    return pl.pallas_call(
        _inproj_kernel,
        grid=(m // tm,),
        in_specs=in_specs,
        out_specs=[pl.BlockSpec((tm, w), row) for w, _ in _INPROJ_OUT],
        out_shape=[jax.ShapeDtypeStruct((m, w), dt) for w, dt in _INPROJ_OUT],
        compiler_params=_params(("parallel",)),
        name="inproj",
    )(x, w_in_p, qn, wuq, kvn, wukv, *tables)


def _mla_kernel(q_ref, kfo_ref, vo_ref, kfp_ref, vp_ref, o_ref, *, tq, tk, n_own, n_past):
    qi = pl.program_id(1)
    qm = _stack_heads(q_ref[...], _mla_lanes)
    rows = N_HEADS * tq
    tko = kfo_ref.shape[0]
    n_loop = qi if n_past is None else n_past

    def softmax_pv(s, v, carry):
        m, l, accs = carry
        m_new = jnp.maximum(m, jnp.max(s, axis=1, keepdims=True))
        p = jnp.exp2(s - m_new)
        a = jnp.exp2(m - m_new)
        l = a * l + jnp.sum(p, axis=1, keepdims=True)
        pb = p.astype(BF16)
        accs = tuple(a[h * tq:(h + 1) * tq] * accs[h]
                     + _dot(pb[h * tq:(h + 1) * tq], v[:, _head_slice(h)]) for h in range(N_HEADS))
        return m_new, l, accs

    def past_scores(j):
        start = pl.multiple_of(jnp.minimum(j, jnp.maximum(n_loop - 1, 0)) * tk, tk)
        return _nt_dot(qm, kfp_ref[pl.ds(start, tk), :])

    row_q = lax.broadcasted_iota(jnp.int32, (rows, tko), 0) % tq
    col = lax.broadcasted_iota(jnp.int32, (rows, tko), 1)
    own_mask = jnp.logical_and(col < n_own, col // CHUNK <= row_q // CHUNK)
    carry = (jnp.full((rows, 1), NEG_INF, F32), jnp.zeros((rows, 1), F32),
             tuple(jnp.zeros((tq, D_HEAD), F32) for _ in range(N_HEADS)))
    s_next = past_scores(0)
    carry = softmax_pv(jnp.where(own_mask, _nt_dot(qm, kfo_ref[...]), NEG_INF), vo_ref[...], carry)

    def body(j, c):
        s_cur, rest = c
        s_after = past_scores(j + 1)
        start = pl.multiple_of(j * tk, tk)
        return s_after, softmax_pv(s_cur, vp_ref[pl.ds(start, tk), :], rest)

    _, (_, l, accs) = lax.fori_loop(0, n_loop, body, (s_next, carry))
    _store_heads(o_ref, [accs[h] / l[h * tq:(h + 1) * tq] for h in range(N_HEADS)])


def _mla_attn(q, kf_own, v_own, kf_past, v_past, *, tq, tk, n_own, causal_tiles):
    b, lq, _ = q.shape
    nqt = lq // tq
    tko = kf_own.shape[1] // nqt
    lp = kf_past.shape[1]
    assert lq % tq == 0 and lp % tk == 0
    return pl.pallas_call(
        functools.partial(_mla_kernel, tq=tq, tk=tk, n_own=n_own,
                          n_past=None if causal_tiles else lp // tk),
        grid=(b, nqt),
        in_specs=[pl.BlockSpec((None, tq, QA_W), lambda i, t: (i, t, 0)),
                  pl.BlockSpec((None, tko, QA_W), lambda i, t: (i, t, 0)),
                  pl.BlockSpec((None, tko, BRANCH_W), lambda i, t: (i, t, 0)),
                  pl.BlockSpec((None, lp, QA_W), lambda i, t: (i, 0, 0)),
                  pl.BlockSpec((None, lp, BRANCH_W), lambda i, t: (i, 0, 0))],
        out_specs=pl.BlockSpec((None, tq, BRANCH_W), lambda i, t: (i, t, 0)),
        out_shape=jax.ShapeDtypeStruct((b, lq, BRANCH_W), BF16),
        compiler_params=_params(("parallel", "arbitrary")),
        name="mla_attn",
    )(q, kf_own, v_own, kf_past, v_past)


def _sb_kernel(q_ref, ko_ref, vo_ref, kp_ref, vp_ref, o_ref, *, tq, tk, n_own, n_past):
    qi = pl.program_id(1)
    qm = _stack_heads(q_ref[...], _own_lanes)
    rows = N_HEADS * tq
    tko = ko_ref.shape[0]
    n_loop = qi if n_past is None else n_past

    def tri2(n):
        r = lax.broadcasted_iota(jnp.int32, (2 * n, n), 0) % n
        c = lax.broadcasted_iota(jnp.int32, (2 * n, n), 1)
        return jnp.where(r > c, 1.0, 0.0).astype(BF16)

    def weigh(z, v, carry, mask, tri):
        run, accs = carry
        neg_abs = lax.bitcast_convert_type(
            lax.bitcast_convert_type(z, jnp.uint32) | jnp.uint32(0x80000000), F32)
        t = jnp.log2(1.0 + jnp.exp2(neg_abs))
        log_beta = jnp.minimum(z, 0.0) - t
        log_stay = log_beta - z
        if mask is not None:
            log_stay = jnp.where(mask, log_stay, 0.0)
        hi = log_stay.astype(BF16)
        lo = (log_stay - hi.astype(F32)).astype(BF16)
        later = _dot(jnp.concatenate([hi, lo], axis=1), tri) + run
        w = jnp.exp2(log_beta + later)
        if mask is not None:
            w = jnp.where(mask, w, 0.0)
        wb = w.astype(BF16)
        vb = v.astype(BF16)
        accs = tuple(accs[h] + _dot(wb[h * tq:(h + 1) * tq], vb[:, _head_slice(h)])
                     for h in range(N_HEADS))
        return run + jnp.sum(log_stay, axis=1, keepdims=True), accs

    row_q = lax.broadcasted_iota(jnp.int32, (rows, tko), 0) % tq
    col = lax.broadcasted_iota(jnp.int32, (rows, tko), 1)
    own_mask = jnp.logical_and(col < n_own, col < row_q)

    def past_start(jj):
        return pl.multiple_of(jnp.clip(n_loop - 1 - jj, 0, kp_ref.shape[0] // tk - 1) * tk, tk)

    def past_scores(jj):
        return _nt_dot(qm, kp_ref[pl.ds(past_start(jj), tk), :].astype(BF16))

    def alive(run):
        return (jnp.max(run) > SB_DEAD_LOG2).astype(jnp.int32)

    z_next = past_scores(0)
    carry = (jnp.zeros((rows, 1), F32), tuple(jnp.zeros((tq, D_HEAD), F32) for _ in range(N_HEADS)))
    run, accs = weigh(_nt_dot(qm, ko_ref[...].astype(BF16)), vo_ref[...], carry, own_mask, tri2(tko))
    tri_past = tri2(tk)

    def cond(c):
        return jnp.logical_and(c[0] < n_loop, c[1] > 0)

    def body(c):
        jj, _, z_cur, run, accs = c
        z_after = past_scores(jj + 1)
        run, accs = weigh(z_cur, vp_ref[pl.ds(past_start(jj), tk), :], (run, accs), None, tri_past)
        return jj + 1, alive(run), z_after, run, accs

    out = lax.while_loop(cond, body, (jnp.int32(0), alive(run), z_next, run, accs))
    _store_heads(o_ref, out[4])


def _sb_attn(q, k_own, v_own, k_past, v_past, *, tq, tk, n_own, causal_tiles):
    b, lq, _ = q.shape
    nqt = lq // tq
    tko = k_own.shape[1] // nqt
    lp = k_past.shape[1]
    assert lq % tq == 0 and lp % tk == 0
    return pl.pallas_call(
        functools.partial(_sb_kernel, tq=tq, tk=tk, n_own=n_own,
                          n_past=None if causal_tiles else lp // tk),
        grid=(b, nqt),
        in_specs=[pl.BlockSpec((None, tq, BRANCH_W), lambda i, t: (i, t, 0)),
                  pl.BlockSpec((None, tko, BRANCH_W), lambda i, t: (i, t, 0)),
                  pl.BlockSpec((None, tko, BRANCH_W), lambda i, t: (i, t, 0)),
                  pl.BlockSpec((None, lp, BRANCH_W), lambda i, t: (i, 0, 0)),
                  pl.BlockSpec((None, lp, BRANCH_W), lambda i, t: (i, 0, 0))],
        out_specs=pl.BlockSpec((None, tq, BRANCH_W), lambda i, t: (i, t, 0)),
        out_shape=jax.ShapeDtypeStruct((b, lq, BRANCH_W), BF16),
        compiler_params=_params(("parallel", "arbitrary")),
        name="sb_attn",
    )(q, k_own, v_own, k_past, v_past)


def _ret_kernel(q_ref, k_ref, kd_ref, v_ref, rg_ref, s0_ref, dec_ref, qdec_ref, gl_ref,
                gng_ref, gnb_ref, o_ref, sout_ref, state_ref, *, lc):
    c = pl.program_id(1)

    @pl.when(c == 0)
    def _():
        state_ref[...] = s0_ref[...]

    qm = _stack_heads(q_ref[...], _own_lanes)
    v = v_ref[...]
    state = state_ref[...]
    scores = (_nt_dot(qm, k_ref[...]) * dec_ref[...]).astype(BF16)
    cross = _dot(qm, state.astype(BF16)) * qdec_ref[...]
    kv_full = _tn_dot(kd_ref[...], v)
    new_state = gl_ref[...] * state + jnp.concatenate(
        [kv_full[_head_slice(h), _head_slice(h)] for h in range(N_HEADS)], axis=0)
    state_ref[...] = new_state

    rg = rg_ref[...]
    outs = []
    for h in range(N_HEADS):
        o = _dot(scores[h * lc:(h + 1) * lc], v[:, _head_slice(h)]) + cross[h * lc:(h + 1) * lc]
        mu = jnp.mean(o, axis=-1, keepdims=True)
        d = o - mu
        var = jnp.mean(d * d, axis=-1, keepdims=True)
        y = d * lax.rsqrt(var + EPS) * gng_ref[h] + gnb_ref[h]
        g = rg[:, _head_slice(h)]
        outs.append(y * (g * _sigmoid(g)))
    _store_heads(o_ref, outs)

    @pl.when(c == pl.num_programs(1) - 1)
    def _():
        sout_ref[...] = new_state


def _retention(q, k, kd, v, rg, s0, dec, qdec, gl, gng, gnb, *, lc):
    b, length, _ = q.shape
    assert length % lc == 0
    seq = lambda i, t: (i, t, 0)
    st = lambda i, t: (i, 0, 0)
    c2 = lambda i, t: (0, 0)
    c3 = lambda i, t: (0, 0, 0)
    return pl.pallas_call(
        functools.partial(_ret_kernel, lc=lc),
        grid=(b, length // lc),
        in_specs=[pl.BlockSpec((None, lc, BRANCH_W), seq)] * 5
        + [pl.BlockSpec((None, BRANCH_W, D_HEAD), st),
           pl.BlockSpec(dec.shape, c2), pl.BlockSpec(qdec.shape, c2), pl.BlockSpec(gl.shape, c2),
           pl.BlockSpec(gng.shape, c3), pl.BlockSpec(gnb.shape, c3)],
        out_specs=[pl.BlockSpec((None, lc, BRANCH_W), seq),
                   pl.BlockSpec((None, BRANCH_W, D_HEAD), st)],
        out_shape=[jax.ShapeDtypeStruct((b, length, BRANCH_W), BF16),
                   jax.ShapeDtypeStruct((b, BRANCH_W, D_HEAD), F32)],
        scratch_shapes=[pltpu.VMEM((BRANCH_W, D_HEAD), F32)],
        compiler_params=_params(("parallel", "arbitrary")),
        name="retention",
    )(q, k, kd, v, rg, s0, dec, qdec, gl, gng, gnb)


def _band_kernel(q_ref, k_ref, v_ref, bias_ref, o_ref, *, tq, win, back):
    qi = pl.program_id(1)
    start = pl.multiple_of(jnp.maximum(qi - back, 0) * tq, tq)
    k = k_ref[pl.ds(start, win), :].astype(BF16)
    v = v_ref[pl.ds(start, win), :].astype(BF16)
    qm = _stack_heads(q_ref[...], _own_lanes)
    s = _nt_dot(qm, k) + bias_ref[...]
    m = jnp.max(s, axis=1, keepdims=True)
    p = jnp.exp2(s - m)
    l = jnp.sum(p, axis=1, keepdims=True)
    pb = p.astype(BF16)
    _store_heads(o_ref, [_dot(pb[h * tq:(h + 1) * tq], v[:, _head_slice(h)]) / l[h * tq:(h + 1) * tq]
                         for h in range(N_HEADS)])


def _band_attn(q, k, v, bias, *, tq, win, back):
    b, lq, _ = q.shape
    lk = k.shape[1]
    nvar = bias.shape[0]
    assert lq % tq == 0
    return pl.pallas_call(
        functools.partial(_band_kernel, tq=tq, win=win, back=back),
        grid=(b, lq // tq),
        in_specs=[pl.BlockSpec((None, tq, BRANCH_W), lambda i, t: (i, t, 0)),
                  pl.BlockSpec((None, lk, BRANCH_W), lambda i, t: (i, 0, 0)),
                  pl.BlockSpec((None, lk, BRANCH_W), lambda i, t: (i, 0, 0)),
                  pl.BlockSpec((None, N_HEADS * tq, win), lambda i, t: (jnp.minimum(t, nvar - 1), 0, 0))],
        out_specs=pl.BlockSpec((None, tq, BRANCH_W), lambda i, t: (i, t, 0)),
        out_shape=jax.ShapeDtypeStruct((b, lq, BRANCH_W), BF16),
        compiler_params=_params(("parallel", "arbitrary")),
        name="band_attn",
    )(q, k, v, bias)


def _merge_kernel(x_ref, ba_ref, bb_ref, bc_ref, bd_ref, wg_ref, wb_ref, wo_ref, g_ref, b_ref, o_ref):
    x = x_ref[...]
    xb = x.astype(BF16)
    merged = None
    for n, br_ref in enumerate((ba_ref, bb_ref, bc_ref, bd_ref)):
        logits = _dot(xb, wg_ref[:, n * D_MODEL:(n + 1) * D_MODEL])
        term = _dot(br_ref[...], wb_ref[n]) * _sigmoid(logits)
        merged = term if merged is None else merged + term
    mix = _dot(merged.astype(BF16), wo_ref[...])
    o_ref[...] = _layer_norm(ALPHA * x + mix, g_ref[...], b_ref[...])


def _merge(x, branches, wg, wb, wo, g, b, *, tm):
    m = x.shape[0]
    tm = min(tm, m)
    assert m % tm == 0
    const2 = lambda i: (0, 0)
    row = lambda i: (i, 0)
    return pl.pallas_call(
        _merge_kernel,
        grid=(m // tm,),
        in_specs=[pl.BlockSpec((tm, D_MODEL), row)]
        + [pl.BlockSpec((tm, BRANCH_W), row)] * N_BRANCH
        + [pl.BlockSpec((D_MODEL, N_BRANCH * D_MODEL), const2),
           pl.BlockSpec((N_BRANCH, BRANCH_W, D_MODEL), lambda i: (0, 0, 0)),
           pl.BlockSpec((D_MODEL, D_MODEL), const2),
           pl.BlockSpec((1, D_MODEL), const2),
           pl.BlockSpec((1, D_MODEL), const2)],
        out_specs=pl.BlockSpec((tm, D_MODEL), row),
        out_shape=jax.ShapeDtypeStruct((m, D_MODEL), F32),
        compiler_params=_params(("parallel",)),
        name="merge",
    )(x, *branches, wg, wb, wo, g, b)


def _route(aff_t, sel_t):
    def top2_sum(a, b, c, d):
        hi1, lo1 = jnp.maximum(a, b), jnp.minimum(a, b)
        hi2, lo2 = jnp.maximum(c, d), jnp.minimum(c, d)
        return jnp.maximum(hi1, hi2) + jnp.maximum(jnp.minimum(hi1, hi2), jnp.maximum(lo1, lo2))

    score = [top2_sum(*sel_t[g * EXPERTS_PER_GROUP:(g + 1) * EXPERTS_PER_GROUP])
             for g in range(N_GROUPS)]
    best_here = []
    for g in range(N_GROUPS):
        ok = None
        for o in range(N_GROUPS):
            if o == g:
                continue
            c = (score[g] > score[o]) if o < g else (score[g] >= score[o])
            ok = c if ok is None else jnp.logical_and(ok, c)
        best_here.append(ok)
    picked = []
    for e in range(N_EXPERTS):
        g = e // EXPERTS_PER_GROUP
        rank = jnp.zeros_like(sel_t[e])
        for o in range(g * EXPERTS_PER_GROUP, (g + 1) * EXPERTS_PER_GROUP):
            if o == e:
                continue
            ahead = (sel_t[o] >= sel_t[e]) if o < e else (sel_t[o] > sel_t[e])
            rank = rank + jnp.where(ahead, 1.0, 0.0)
        picked.append(jnp.where(jnp.logical_and(best_here[g], rank < TOP_K), aff_t[e], 0.0))
    total = picked[0]
    for e in range(1, N_EXPERTS):
        total = total + picked[e]
    return [p / total for p in picked]


def _moe_kernel(x_ref, wrh_ref, wrl_ref, br_ref, wgu_ref, wd_ref, g_ref, b_ref, o_ref, acc_ref):
    x = x_ref[...]
    xh = x.astype(BF16)
    xl = (x - xh.astype(F32)).astype(BF16)
    wrh = wrh_ref[...]
    logits = _dot(xh, wrh) + _dot(xl, wrh) + _dot(xh, wrl_ref[...])
    aff = _sigmoid(logits).T
    bias = br_ref[...]
    aff_t = [aff[e:e + 1, :] for e in range(N_EXPERTS)]
    sel_t = [aff_t[e] + bias[e:e + 1, :] for e in range(N_EXPERTS)]
    gate_rows = _route(aff_t, sel_t)
    tm = x.shape[0]
    gate_t = jnp.concatenate(gate_rows + [jnp.zeros((LANE - N_EXPERTS, tm), F32)], axis=0)
    gate = gate_t.T

    for e in range(N_EXPERTS):
        gu = _dot(xh, wgu_ref[e])
        gt, up = gu[:, :D_EXPERT], gu[:, D_EXPERT:]
        hmid = gt * _sigmoid(gt) * up
        y = _dot(hmid.astype(BF16), wd_ref[e]) * gate[:, e:e + 1]
        if e == 0:
            acc_ref[...] = y
        else:
            acc_ref[...] += y
    o_ref[...] = _layer_norm(ALPHA * x + acc_ref[...], g_ref[...], b_ref[...])


def _moe(x, wrh, wrl, br, wgu, wd, g, b, *, tm):
    m = x.shape[0]
    tm = min(tm, m)
    assert m % tm == 0
    const2 = lambda i: (0, 0)
    const3 = lambda i: (0, 0, 0)
    return pl.pallas_call(
        _moe_kernel,
        grid=(m // tm,),
        in_specs=[pl.BlockSpec((tm, D_MODEL), lambda i: (i, 0)),
                  pl.BlockSpec((D_MODEL, LANE), const2),
                  pl.BlockSpec((D_MODEL, LANE), const2),
                  pl.BlockSpec((LANE, 1), const2),
                  pl.BlockSpec((N_EXPERTS, D_MODEL, 2 * D_EXPERT), const3, pipeline_mode=pl.Buffered(1)),
                  pl.BlockSpec((N_EXPERTS, D_EXPERT, D_MODEL), const3, pipeline_mode=pl.Buffered(1)),
                  pl.BlockSpec((1, D_MODEL), const2),
                  pl.BlockSpec((1, D_MODEL), const2)],
        out_specs=pl.BlockSpec((tm, D_MODEL), lambda i: (i, 0)),
        out_shape=jax.ShapeDtypeStruct((m, D_MODEL), F32),
        scratch_shapes=[pltpu.VMEM((tm, D_MODEL), F32)],
        compiler_params=_params(("parallel",)),
        name="moe",
    )(x, wrh, wrl, br, wgu, wd, g, b)


def _rope_tables(pos, d):
    half = d // 2
    inv = jnp.power(ROPE_BASE, -jnp.arange(half, dtype=F32) / half)
    ang = pos.astype(F32)[:, None] * inv[None, :]
    cos, sin = jnp.cos(ang), jnp.sin(ang)
    zero = jnp.zeros_like(sin)
    rep = LANE // d
    cos_t = jnp.tile(jnp.concatenate([cos, cos], axis=1), (1, rep))
    sin_a = jnp.tile(jnp.concatenate([-sin, zero], axis=1), (1, rep))
    sin_b = jnp.tile(jnp.concatenate([zero, sin], axis=1), (1, rep))
    return cos_t, sin_a, sin_b


def _retention_tables(lc):
    log_g = jnp.log1p(-jnp.exp2(-5.0 - jnp.arange(N_HEADS, dtype=F32)))
    i = jnp.arange(lc, dtype=F32)
    diff = i[:, None] - i[None, :]
    dec = jnp.where(diff >= 0, jnp.exp(jnp.maximum(diff, 0.0)[None] * log_g[:, None, None]), 0.0)
    qdec = jnp.exp((i[None, :] + 1.0) * log_g[:, None])
    kdec = jnp.exp((lc - 1.0 - i)[None, :] * log_g[:, None])
    gl = jnp.exp(lc * log_g)
    dec = dec.reshape(N_HEADS * lc, lc)
    qdec = jnp.broadcast_to(qdec[:, :, None], (N_HEADS, lc, D_HEAD)).reshape(N_HEADS * lc, D_HEAD)
    gl = jnp.broadcast_to(gl[:, None, None], (N_HEADS, D_HEAD, D_HEAD)).reshape(BRANCH_W, D_HEAD)
    kdec = jnp.repeat(kdec.T, D_HEAD, axis=1)
    return dec, qdec, kdec, gl


def _band_bias(rel_bias, tq, win, q_minus_k0, valid):
    length = tq + win - 1
    d = np.arange(length) - (tq - 1) - q_minus_k0
    idx = np.clip(d, -REL_CLIP, REL_CLIP) + REL_CLIP
    g = rel_bias[:, idx].astype(F32) * LOG2E
    gp = jnp.concatenate([g, jnp.zeros((N_HEADS, 1), F32)], axis=1)
    m = jnp.tile(gp, (1, tq))[:, :tq * length].reshape(N_HEADS, tq, length)
    tile = m[:, :, tq - 1:tq - 1 + win]
    return jnp.where(valid[None], tile, NEG_INF).reshape(N_HEADS * tq, win)


def _pack_w_in(w_in_l):
    cols = []
    src = 0
    for w in _IN_WIDTH:
        seg = w_in_l[:, src:src + w]
        cols.append(jnp.pad(seg, ((0, 0), (0, _round_up(w, LANE) - w))))
        src += w
    return jnp.concatenate(cols, axis=1).astype(BF16)


def _pad_rows(t, n):
    return jnp.pad(t, ((0, 0), (0, n - t.shape[1]), (0, 0)))


def _token_mixers(x, pos0, past, lw, *, prompt):
    (w_in_p, qn, wuq, kvn, wukv, gn_g, gn_b, rel_bias, b, length) = lw
    assert pos0 % CHUNK == 0
    pos = pos0 + jnp.arange(length)
    lc = 256 if prompt else length
    dec, qdec, kdec, gl = _retention_tables(lc)
    tables = list(_rope_tables(pos, DR_A)) + list(_rope_tables(pos, D_HEAD)) + [jnp.tile(kdec, (length // lc, 1))]
    if not prompt:
        tables = [jnp.tile(t, (b, 1)) for t in tables]
    outs = _inproj(x, w_in_p, qn, wuq, kvn, wukv, tables, tm=512 if prompt else b * length)
    (q_a, ckv, kpe, kf, v_a, rq, rk, rkd, rv, rg, sq, sk, sv, bq, bk, bv) = [
        o.reshape(b, length, o.shape[-1]) for o in outs]

    if prompt:
        tq = 256
        o_a = _mla_attn(q_a, kf, v_a, kf, v_a, tq=tq, tk=tq, n_own=tq, causal_tiles=True)
        o_c = _sb_attn(sq, sk, sv, sk, sv, tq=tq, tk=tq, n_own=tq, causal_tiles=True)
        s0 = jnp.zeros((b, BRANCH_W, D_HEAD), F32)
        win = 3 * tq
        i = np.arange(tq)[:, None]
        c = np.arange(win)[None, :]
        variants = []
        for t in range(3):
            qc, kc = i // CHUNK + t * (tq // CHUNK), c // CHUNK
            variants.append(_band_bias(rel_bias, tq, win, t * tq, (kc <= qc) & (kc >= qc - PREV_CHUNKS)))
        bias = jnp.stack(variants)
        o_d = _band_attn(bq, bk, bv, bias, tq=tq, win=win, back=2)
    else:
        c_ckv, c_kpe, s_prev, c_sk, c_sv, c_bk, c_bv = past
        n_past = c_ckv.shape[1]
        tko = LANE
        kf_c, v_c = _expand_latent(c_ckv.reshape(b * n_past, KV_RANK),
                                   jnp.pad(c_kpe.reshape(b * n_past, DR_A), ((0, 0), (0, LANE - DR_A))),
                                   wukv, 512)
        o_a = _mla_attn(q_a, _pad_rows(kf, tko), _pad_rows(v_a, tko),
                        kf_c.reshape(b, n_past, QA_W), v_c.reshape(b, n_past, BRANCH_W),
                        tq=length, tk=256, n_own=length, causal_tiles=False)
        o_c = _sb_attn(sq, _pad_rows(sk, tko), _pad_rows(sv, tko),
                       c_sk.reshape(b, n_past, BRANCH_W), c_sv.reshape(b, n_past, BRANCH_W),
                       tq=length, tk=256, n_own=length, causal_tiles=False)
        s0 = s_prev.reshape(b, BRANCH_W, D_HEAD)
        n_band = c_bk.shape[1]
        n_keys = n_band + length
        win = _round_up(n_keys, LANE)
        bk_all = _pad_rows(jnp.concatenate([c_bk.reshape(b, n_band, BRANCH_W), bk], axis=1), win)
        bv_all = _pad_rows(jnp.concatenate([c_bv.reshape(b, n_band, BRANCH_W), bv], axis=1), win)
        k_pos = pos0 - n_band + np.arange(win)
        q_pos = pos0 + np.arange(length)
        qc, kc = q_pos[:, None] // CHUNK, k_pos[None, :] // CHUNK
        valid = (np.arange(win)[None, :] < n_keys) & (k_pos[None, :] >= 0) & (kc <= qc) & (kc >= qc - PREV_CHUNKS)
        bias = _band_bias(rel_bias, length, win, n_band, valid)[None]
        o_d = _band_attn(bq, bk_all, bv_all, bias, tq=length, win=win, back=0)

    o_r, s_ret = _retention(rq, rk, rkd, rv, rg, s0, dec, qdec, gl,
                            gn_g.reshape(N_HEADS, 1, D_HEAD), gn_b.reshape(N_HEADS, 1, D_HEAD), lc=lc)
    s_ret = s_ret.reshape(b, N_HEADS, D_HEAD, D_HEAD)

    flat = lambda t: t.reshape(b * length, BRANCH_W)
    heads4 = lambda t: t.reshape(b, length, N_HEADS, D_HEAD)
    if prompt:
        keep = min(PREV_CHUNKS * CHUNK, length)
        new_state = (ckv, kpe, s_ret, heads4(sk), heads4(sv),
                     heads4(bk)[:, length - keep:], heads4(bv)[:, length - keep:])
    else:
        new_state = (ckv, kpe, s_ret, heads4(sk), heads4(sv), heads4(bk), heads4(bv))
    return (flat(o_a), flat(o_r), flat(o_c), flat(o_d)), new_state


def kernel(x_prompt, x_sample, cache_mla_ckv, cache_mla_kpe, state_ret, cache_sb_k, cache_sb_v, cache_band_k, cache_band_v, w_in, mla_q_norm, mla_w_uq, mla_kv_norm, mla_w_ukv, ret_gn_g, ret_gn_b, band_rel_bias, w_branch, w_o, ln1_g, ln1_b, w_router, b_router, w_exp_gate, w_exp_up, w_exp_down, ln2_g, ln2_b):
    bp, lp, _ = x_prompt.shape
    bs, ls, _ = x_sample.shape
    past_len = cache_mla_ckv.shape[2]
    xp = x_prompt.reshape(bp * lp, D_MODEL)
    xs = x_sample.reshape(bs * ls, D_MODEL)

    wr = jnp.pad(w_router, ((0, 0), (0, LANE - N_EXPERTS)))
    wrh = wr.astype(BF16)
    wrl = (wr - wrh.astype(F32)).astype(BF16)
    br = jnp.pad(b_router, (0, LANE - N_EXPERTS)).reshape(LANE, 1)

    st_p, st_s = [], []
    for l in range(DEPTH):
        wuq = mla_w_uq[l]
        wuq = jnp.concatenate([wuq[:, :, :DN_A].reshape(Q_RANK, -1), wuq[:, :, DN_A:].reshape(Q_RANK, -1)], axis=1)
        wuq = jnp.pad(wuq, ((0, _round_up(Q_RANK, LANE) - Q_RANK), (0, 0))).astype(BF16)
        wukv = mla_w_ukv[l]
        wukv = jnp.concatenate([wukv[:, :, :DN_A].reshape(KV_RANK, -1), wukv[:, :, DN_A:].reshape(KV_RANK, -1)],
                               axis=1).astype(BF16)
        qn = jnp.pad(mla_q_norm[l], (0, _round_up(Q_RANK, LANE) - Q_RANK)).reshape(1, -1)
        kvn = mla_kv_norm[l].reshape(1, KV_RANK)
        w_in_p = _pack_w_in(w_in[l])
        wg = w_in[l][:, GATE_COL0:].astype(BF16)
        wb = w_branch[l].astype(BF16)
        wo = w_o[l].astype(BF16)
        g1, b1 = ln1_g[l].reshape(1, D_MODEL), ln1_b[l].reshape(1, D_MODEL)
        g2, b2 = ln2_g[l].reshape(1, D_MODEL), ln2_b[l].reshape(1, D_MODEL)
        wgu = jnp.concatenate([w_exp_gate[l], w_exp_up[l]], axis=-1).astype(BF16)
        wd = w_exp_down[l].astype(BF16)
        past = (cache_mla_ckv[l], cache_mla_kpe[l], state_ret[l], cache_sb_k[l], cache_sb_v[l],
                cache_band_k[l], cache_band_v[l])
        lw = (w_in_p, qn, wuq, kvn, wukv, ret_gn_g[l], ret_gn_b[l], band_rel_bias[l])

        br_p, new_p = _token_mixers(xp, 0, None, lw + (bp, lp), prompt=True)
        br_s, new_s = _token_mixers(xs, past_len, past, lw + (bs, ls), prompt=False)
        xp = _merge(xp, br_p, wg, wb, wo, g1, b1, tm=TOKEN_TILE)
        xs = _merge(xs, br_s, wg, wb, wo, g1, b1, tm=TOKEN_TILE)
        xp = _moe(xp, wrh, wrl, br, wgu, wd, g2, b2, tm=TOKEN_TILE)
        xs = _moe(xs, wrh, wrl, br, wgu, wd, g2, b2, tm=TOKEN_TILE)
        st_p.append(new_p)
        st_s.append(new_s)

    stack = lambda states, i: jnp.stack([s[i] for s in states], axis=0)
    return ((xp.reshape(bp, lp, D_MODEL), xs.reshape(bs, ls, D_MODEL))
            + tuple(stack(st_p, i) for i in range(7))
            + tuple(stack(st_s, i) for i in range(7)))
```

```python
import functools

import jax
import jax.numpy as jnp
import numpy as np
from jax import lax
from jax.experimental import pallas as pl
from jax.experimental.pallas import tpu as pltpu

D_MODEL = 1024
DEPTH = 2
CHUNK = 64
N_BRANCH = 4
BRANCH_W = D_MODEL // 4
N_HEADS = 4
D_HEAD = BRANCH_W // N_HEADS
DN_A = 64
DR_A = 32
DQK_A = DN_A + DR_A
Q_RANK = (3 * D_MODEL) // 16
KV_RANK = D_MODEL // 8
PREV_CHUNKS = 8
REL_CLIP = 128
ROPE_BASE = 10000.0
N_EXPERTS = 16
N_GROUPS = 4
EXPERTS_PER_GROUP = N_EXPERTS // N_GROUPS
TOP_K = 2
D_EXPERT = D_MODEL // 4
ALPHA = (2.0 * DEPTH) ** 0.25
EPS = 1e-5
NEG_INF = -1e30
LOG2E = 1.4426950408889634
SB_DEAD_LOG2 = -150.0

F32 = jnp.float32
BF16 = jnp.bfloat16

V7X_VMEM_LIMIT = 56 * 1024 * 1024
LANE = 128
TOKEN_TILE = 512

_IN_NAMES = ("c_q", "c_kv", "k_pe", "rq", "rk", "rv", "rg", "sq", "sk", "sv", "bq", "bk", "bv")
_IN_WIDTH = (Q_RANK, KV_RANK, DR_A) + (BRANCH_W,) * 10
QA_W = N_HEADS * DN_A + N_HEADS * DR_A


def _round_up(n, m):
    return (n + m - 1) // m * m


_IN_OFF = {}
_off = 0
for _n, _w in zip(_IN_NAMES, _IN_WIDTH):
    _IN_OFF[_n] = (_off, _round_up(_w, LANE))
    _off += _round_up(_w, LANE)
IN_PACKED = _off
GATE_COL0 = sum(_IN_WIDTH)


def _params(sem):
    return pltpu.CompilerParams(dimension_semantics=sem, vmem_limit_bytes=V7X_VMEM_LIMIT)


def _nt_dot(a, b):
    return lax.dot_general(a, b, (((1,), (1,)), ((), ())), preferred_element_type=F32)


def _tn_dot(a, b):
    return lax.dot_general(a, b, (((0,), (0,)), ((), ())), preferred_element_type=F32)


def _dot(a, b):
    return jnp.dot(a, b, preferred_element_type=F32)


def _layer_norm(v, g, b):
    mu = jnp.mean(v, axis=-1, keepdims=True)
    d = v - mu
    var = jnp.mean(d * d, axis=-1, keepdims=True)
    return d * lax.rsqrt(var + EPS) * g + b


def _sigmoid(v):
    return 0.5 * jnp.tanh(0.5 * v) + 0.5


def _head_slice(h):
    return slice(h * D_HEAD, (h + 1) * D_HEAD)


def _stack_heads(q, lane_sets):
    lane = lax.broadcasted_iota(jnp.int32, q.shape, 1)
    zero = jnp.zeros_like(q)
    parts = []
    for h in range(N_HEADS):
        keep = None
        for lo, hi in lane_sets(h):
            m = jnp.logical_and(lane >= lo, lane < hi)
            keep = m if keep is None else jnp.logical_or(keep, m)
        parts.append(jnp.where(keep, q, zero))
    return jnp.concatenate(parts, axis=0)


def _own_lanes(h):
    return ((h * D_HEAD, (h + 1) * D_HEAD),)


def _mla_lanes(h):
    base = N_HEADS * DN_A
    return ((h * DN_A, (h + 1) * DN_A), (base + h * DR_A, base + (h + 1) * DR_A))


def _store_heads(o_ref, parts):
    for h, p in enumerate(parts):
        o_ref[:, _head_slice(h)] = p.astype(o_ref.dtype)


def _expand_kernel(ckv_ref, kpe_ref, w_ref, kf_ref, v_ref):
    kvx = _dot(ckv_ref[...].astype(BF16), w_ref[...])
    kp = kpe_ref[...]
    kpt = kp + pltpu.roll(kp, DR_A, 1) + pltpu.roll(kp, 2 * DR_A, 1) + pltpu.roll(kp, 3 * DR_A, 1)
    kf_ref[:, :N_HEADS * DN_A] = kvx[:, :N_HEADS * DN_A].astype(BF16)
    kf_ref[:, N_HEADS * DN_A:] = kpt.astype(BF16)
    v_ref[...] = kvx[:, N_HEADS * DN_A:].astype(BF16)


def _expand_latent(ckv, kpe_padded, w_ukv, tm):
    m = ckv.shape[0]
    assert m % tm == 0
    return pl.pallas_call(
        _expand_kernel,
        grid=(m // tm,),
        in_specs=[pl.BlockSpec((tm, KV_RANK), lambda i: (i, 0)),
                  pl.BlockSpec((tm, LANE), lambda i: (i, 0)),
                  pl.BlockSpec((KV_RANK, 2 * BRANCH_W), lambda i: (0, 0))],
        out_specs=[pl.BlockSpec((tm, QA_W), lambda i: (i, 0)),
                   pl.BlockSpec((tm, BRANCH_W), lambda i: (i, 0))],
        out_shape=[jax.ShapeDtypeStruct((m, QA_W), BF16),
                   jax.ShapeDtypeStruct((m, BRANCH_W), BF16)],
        compiler_params=_params(("parallel",)),
        name="expand_latent",
    )(ckv, kpe_padded, w_ukv)


def _rope_block(x, cos, sin_a, sin_b, half):
    return x * cos + pltpu.roll(x, LANE - half, 1) * sin_a + pltpu.roll(x, half, 1) * sin_b


def _inproj_kernel(x_ref, w_ref, qn_ref, wuq_ref, kvn_ref, wukv_ref,
                   c32_ref, a32_ref, b32_ref, c64_ref, a64_ref, b64_ref, kdec_ref,
                   qa_ref, ckv_ref, kpe_ref, kf_ref, va_ref,
                   rq_ref, rk_ref, rkd_ref, rv_ref, rg_ref,
                   sq_ref, sk_ref, sv_ref, bq_ref, bk_ref, bv_ref):
    z = _dot(x_ref[...].astype(BF16), w_ref[...])

    def seg(name):
        o, w = _IN_OFF[name]
        return z[:, o:o + w]

    cq = seg("c_q")
    cqn = cq * lax.rsqrt(jnp.sum(cq * cq, axis=1, keepdims=True) * (1.0 / Q_RANK) + EPS) * qn_ref[...]
    qa = _dot(cqn.astype(BF16), wuq_ref[...])
    scale_a = DQK_A ** -0.5 * LOG2E
    nope_w = N_HEADS * DN_A
    qa_ref[:, :nope_w] = (qa[:, :nope_w] * scale_a).astype(BF16)
    q_pe = _rope_block(qa[:, nope_w:], c32_ref[...], a32_ref[...], b32_ref[...], DR_A // 2)
    qa_ref[:, nope_w:] = (q_pe * scale_a).astype(BF16)

    ckv_raw = seg("c_kv")
    ckv = ckv_raw * lax.rsqrt(jnp.mean(ckv_raw * ckv_raw, axis=1, keepdims=True) + EPS) * kvn_ref[...]
    ckv_ref[...] = ckv
    kvx = _dot(ckv.astype(BF16), wukv_ref[...])
    kp = _rope_block(seg("k_pe"), c32_ref[...], a32_ref[...], b32_ref[...], DR_A // 2)
    kpe_ref[...] = kp[:, :DR_A]
    kpt = kp + pltpu.roll(kp, DR_A, 1) + pltpu.roll(kp, 2 * DR_A, 1) + pltpu.roll(kp, 3 * DR_A, 1)
    kf_ref[:, :nope_w] = kvx[:, :nope_w].astype(BF16)
    kf_ref[:, nope_w:] = kpt.astype(BF16)
    va_ref[...] = kvx[:, nope_w:].astype(BF16)

    rq, rk = seg("rq"), seg("rk")
    kdec = kdec_ref[...]
    for blk in range(BRANCH_W // LANE):
        cols = slice(blk * LANE, (blk + 1) * LANE)
        rq_ref[:, cols] = _rope_block(rq[:, cols], c64_ref[...], a64_ref[...], b64_ref[...],
                                      D_HEAD // 2).astype(BF16)
        rkb = _rope_block(rk[:, cols], c64_ref[...], a64_ref[...], b64_ref[...], D_HEAD // 2) * (D_HEAD ** -0.5)
        rk_ref[:, cols] = rkb.astype(BF16)
        rkd_ref[:, cols] = (rkb * kdec[:, cols]).astype(BF16)
    rv_ref[...] = seg("rv").astype(BF16)
    rg_ref[...] = seg("rg")

    scale_h = D_HEAD ** -0.5 * LOG2E
    sq_ref[...] = (seg("sq") * scale_h).astype(BF16)
    sk_ref[...] = seg("sk")
    sv_ref[...] = seg("sv")
    bq_ref[...] = (seg("bq") * scale_h).astype(BF16)
    bk_ref[...] = seg("bk")
    bv_ref[...] = seg("bv")


_INPROJ_OUT = (
    (QA_W, BF16), (KV_RANK, F32), (DR_A, F32), (QA_W, BF16), (BRANCH_W, BF16),
    (BRANCH_W, BF16), (BRANCH_W, BF16), (BRANCH_W, BF16), (BRANCH_W, BF16), (BRANCH_W, F32),
    (BRANCH_W, BF16), (BRANCH_W, F32), (BRANCH_W, F32), (BRANCH_W, BF16), (BRANCH_W, F32), (BRANCH_W, F32))


_STATE_OUTS = (1, 2, 11, 12, 14, 15)


def _inproj_entry(*refs, n_in, n_alias):
    _inproj_kernel(*refs[:n_in], *refs[n_in + n_alias:])


def _inproj(x, w_in_p, qn, wuq, kvn, wukv, tables, *, tm, layer, stacked):
    m = x.shape[0]
    tm = min(tm, m)
    assert m % tm == 0
    n_pos_tiles = tables[0].shape[0] // tm
    assert tables[0].shape[0] % tm == 0
    row = lambda i: (i, 0)
    const = lambda i: (0, 0)
    pos = lambda i: (i % n_pos_tiles, 0)
    in_specs = [pl.BlockSpec((tm, D_MODEL), row),
                pl.BlockSpec(w_in_p.shape, const),
                pl.BlockSpec(qn.shape, const),
                pl.BlockSpec(wuq.shape, const),
                pl.BlockSpec(kvn.shape, const),
                pl.BlockSpec(wukv.shape, const)]
    in_specs += [pl.BlockSpec((tm, t.shape[1]), pos) for t in tables]
    n_in = len(in_specs)
    out_specs, out_shape = [], []
    for k, (w, dt) in enumerate(_INPROJ_OUT):
        if k in _STATE_OUTS:
            out_specs.append(pl.BlockSpec((None, tm, w), lambda i: (layer, i, 0)))
            out_shape.append(jax.ShapeDtypeStruct((DEPTH, m, w), dt))
        else:
            out_specs.append(pl.BlockSpec((tm, w), row))
            out_shape.append(jax.ShapeDtypeStruct((m, w), dt))
    aliases = {}
    extra = ()
    if stacked is not None:
        extra = tuple(stacked)
        in_specs += [pl.BlockSpec(memory_space=pl.ANY)] * len(extra)
        aliases = {n_in + k: o for k, o in enumerate(_STATE_OUTS)}
    return pl.pallas_call(
        functools.partial(_inproj_entry, n_in=n_in, n_alias=len(extra)),
        grid=(m // tm,),
        in_specs=in_specs,
        out_specs=out_specs,
        out_shape=out_shape,
        input_output_aliases=aliases,
        compiler_params=_params(("parallel",)),
        name="inproj",
    )(x, w_in_p, qn, wuq, kvn, wukv, *tables, *extra)


def _tile_spec(src, rows):
    arr, layer = src
    return pl.BlockSpec((None, None, rows, arr.shape[3]), lambda i, t: (layer, i, t, 0))


def _full_spec(src):
    arr, layer = src
    return pl.BlockSpec((None, None, arr.shape[2], arr.shape[3]), lambda i, t: (layer, i, 0, 0))


def _mla_kernel(q_ref, kfo_ref, vo_ref, kfp_ref, vp_ref, o_ref, *, tq, tk, n_own, n_past):
    qi = pl.program_id(1)
    qm = _stack_heads(q_ref[...], _mla_lanes)
    rows = N_HEADS * tq
    tko = kfo_ref.shape[0]
    n_loop = qi if n_past is None else n_past

    def softmax_pv(s, v, carry):
        m, l, accs = carry
        m_new = jnp.maximum(m, jnp.max(s, axis=1, keepdims=True))
        p = jnp.exp2(s - m_new)
        a = jnp.exp2(m - m_new)
        l = a * l + jnp.sum(p, axis=1, keepdims=True)
        pb = p.astype(BF16)
        accs = tuple(a[h * tq:(h + 1) * tq] * accs[h]
                     + _dot(pb[h * tq:(h + 1) * tq], v[:, _head_slice(h)]) for h in range(N_HEADS))
        return m_new, l, accs

    def past_scores(j):
        start = pl.multiple_of(jnp.minimum(j, jnp.maximum(n_loop - 1, 0)) * tk, tk)
        return _nt_dot(qm, kfp_ref[pl.ds(start, tk), :])

    row_q = lax.broadcasted_iota(jnp.int32, (rows, tko), 0) % tq
    col = lax.broadcasted_iota(jnp.int32, (rows, tko), 1)
    own_mask = jnp.logical_and(col < n_own, col // CHUNK <= row_q // CHUNK)
    carry = (jnp.full((rows, 1), NEG_INF, F32), jnp.zeros((rows, 1), F32),
             tuple(jnp.zeros((tq, D_HEAD), F32) for _ in range(N_HEADS)))
    s_next = past_scores(0)
    carry = softmax_pv(jnp.where(own_mask, _nt_dot(qm, kfo_ref[...]), NEG_INF), vo_ref[...], carry)

    def body(j, c):
        s_cur, rest = c
        s_after = past_scores(j + 1)
        start = pl.multiple_of(j * tk, tk)
        return s_after, softmax_pv(s_cur, vp_ref[pl.ds(start, tk), :], rest)

    _, (_, l, accs) = lax.fori_loop(0, n_loop, body, (s_next, carry))
    _store_heads(o_ref, [accs[h] / l[h * tq:(h + 1) * tq] for h in range(N_HEADS)])


def _mla_attn(q, kf_own, v_own, kf_past, v_past, *, tq, tk, n_own, causal_tiles):
    b, lq, _ = q.shape
    nqt = lq // tq
    tko = kf_own[0].shape[2] // nqt
    lp = kf_past[0].shape[2]
    assert lq % tq == 0 and lp % tk == 0
    return pl.pallas_call(
        functools.partial(_mla_kernel, tq=tq, tk=tk, n_own=n_own,
                          n_past=None if causal_tiles else lp // tk),
        grid=(b, nqt),
        in_specs=[pl.BlockSpec((None, tq, QA_W), lambda i, t: (i, t, 0)),
                  _tile_spec(kf_own, tko), _tile_spec(v_own, tko),
                  _full_spec(kf_past), _full_spec(v_past)],
        out_specs=pl.BlockSpec((None, tq, BRANCH_W), lambda i, t: (i, t, 0)),
        out_shape=jax.ShapeDtypeStruct((b, lq, BRANCH_W), BF16),
        compiler_params=_params(("parallel", "arbitrary")),
        name="mla_attn",
    )(q, kf_own[0], v_own[0], kf_past[0], v_past[0])


def _sb_kernel(q_ref, ko_ref, vo_ref, kp_ref, vp_ref, o_ref, *, tq, tk, n_own, n_past):
    qi = pl.program_id(1)
    qm = _stack_heads(q_ref[...], _own_lanes)
    rows = N_HEADS * tq
    tko = ko_ref.shape[0]
    n_loop = qi if n_past is None else n_past

    def tri2(n):
        r = lax.broadcasted_iota(jnp.int32, (2 * n, n), 0) % n
        c = lax.broadcasted_iota(jnp.int32, (2 * n, n), 1)
        return jnp.where(r > c, 1.0, 0.0).astype(BF16)

    def weigh(z, v, carry, mask, tri):
        run, accs = carry
        neg_abs = lax.bitcast_convert_type(
            lax.bitcast_convert_type(z, jnp.uint32) | jnp.uint32(0x80000000), F32)
        t = jnp.log2(1.0 + jnp.exp2(neg_abs))
        log_beta = jnp.minimum(z, 0.0) - t
        log_stay = log_beta - z
        if mask is not None:
            log_stay = jnp.where(mask, log_stay, 0.0)
        hi = log_stay.astype(BF16)
        lo = (log_stay - hi.astype(F32)).astype(BF16)
        later = _dot(jnp.concatenate([hi, lo], axis=1), tri) + run
        w = jnp.exp2(log_beta + later)
        if mask is not None:
            w = jnp.where(mask, w, 0.0)
        wb = w.astype(BF16)
        vb = v.astype(BF16)
        accs = tuple(accs[h] + _dot(wb[h * tq:(h + 1) * tq], vb[:, _head_slice(h)])
                     for h in range(N_HEADS))
        return run + jnp.sum(log_stay, axis=1, keepdims=True), accs

    row_q = lax.broadcasted_iota(jnp.int32, (rows, tko), 0) % tq
    col = lax.broadcasted_iota(jnp.int32, (rows, tko), 1)
    own_mask = jnp.logical_and(col < n_own, col < row_q)

    def past_start(jj):
        return pl.multiple_of(jnp.clip(n_loop - 1 - jj, 0, kp_ref.shape[0] // tk - 1) * tk, tk)

    def past_scores(jj):
        return _nt_dot(qm, kp_ref[pl.ds(past_start(jj), tk), :].astype(BF16))

    def alive(run):
        return (jnp.max(run) > SB_DEAD_LOG2).astype(jnp.int32)

    z_next = past_scores(0)
    carry = (jnp.zeros((rows, 1), F32), tuple(jnp.zeros((tq, D_HEAD), F32) for _ in range(N_HEADS)))
    run, accs = weigh(_nt_dot(qm, ko_ref[...].astype(BF16)), vo_ref[...], carry, own_mask, tri2(tko))
    tri_past = tri2(tk)

    def cond(c):
        return jnp.logical_and(c[0] < n_loop, c[1] > 0)

    def body(c):
        jj, _, z_cur, run, accs = c
        z_after = past_scores(jj + 1)
        run, accs = weigh(z_cur, vp_ref[pl.ds(past_start(jj), tk), :], (run, accs), None, tri_past)
        return jj + 1, alive(run), z_after, run, accs

    out = lax.while_loop(cond, body, (jnp.int32(0), alive(run), z_next, run, accs))
    _store_heads(o_ref, out[4])


def _sb_attn(q, k_own, v_own, k_past, v_past, *, tq, tk, n_own, causal_tiles):
    b, lq, _ = q.shape
    nqt = lq // tq
    tko = k_own[0].shape[2] // nqt
    lp = k_past[0].shape[2]
    assert lq % tq == 0 and lp % tk == 0
    return pl.pallas_call(
        functools.partial(_sb_kernel, tq=tq, tk=tk, n_own=n_own,
                          n_past=None if causal_tiles else lp // tk),
        grid=(b, nqt),
        in_specs=[pl.BlockSpec((None, tq, BRANCH_W), lambda i, t: (i, t, 0)),
                  _tile_spec(k_own, tko), _tile_spec(v_own, tko),
                  _full_spec(k_past), _full_spec(v_past)],
        out_specs=pl.BlockSpec((None, tq, BRANCH_W), lambda i, t: (i, t, 0)),
        out_shape=jax.ShapeDtypeStruct((b, lq, BRANCH_W), BF16),
        compiler_params=_params(("parallel", "arbitrary")),
        name="sb_attn",
    )(q, k_own[0], v_own[0], k_past[0], v_past[0])


def _ret_kernel(q_ref, k_ref, kd_ref, v_ref, rg_ref, s0_ref, dec_ref, qdec_ref, gl_ref,
                gng_ref, gnb_ref, o_ref, sout_ref, state_ref, *, lc):
    c = pl.program_id(1)

    @pl.when(c == 0)
    def _():
        state_ref[...] = s0_ref[...]

    qm = _stack_heads(q_ref[...], _own_lanes)
    v = v_ref[...]
    state = state_ref[...]
    scores = (_nt_dot(qm, k_ref[...]) * dec_ref[...]).astype(BF16)
    cross = _dot(qm, state.astype(BF16)) * qdec_ref[...]
    kv_full = _tn_dot(kd_ref[...], v)
    new_state = gl_ref[...] * state + jnp.concatenate(
        [kv_full[_head_slice(h), _head_slice(h)] for h in range(N_HEADS)], axis=0)
    state_ref[...] = new_state

    rg = rg_ref[...]
    outs = []
    for h in range(N_HEADS):
        o = _dot(scores[h * lc:(h + 1) * lc], v[:, _head_slice(h)]) + cross[h * lc:(h + 1) * lc]
        mu = jnp.mean(o, axis=-1, keepdims=True)
        d = o - mu
        var = jnp.mean(d * d, axis=-1, keepdims=True)
        y = d * lax.rsqrt(var + EPS) * gng_ref[h] + gnb_ref[h]
        g = rg[:, _head_slice(h)]
        outs.append(y * (g * _sigmoid(g)))
    _store_heads(o_ref, outs)

    @pl.when(c == pl.num_programs(1) - 1)
    def _():
        sout_ref[...] = new_state


def _retention(q, k, kd, v, rg, s0, dec, qdec, gl, gng, gnb, *, lc):
    b, length, _ = q.shape
    assert length % lc == 0
    seq = lambda i, t: (i, t, 0)
    st = lambda i, t: (i, 0, 0)
    c2 = lambda i, t: (0, 0)
    c3 = lambda i, t: (0, 0, 0)
    return pl.pallas_call(
        functools.partial(_ret_kernel, lc=lc),
        grid=(b, length // lc),
        in_specs=[pl.BlockSpec((None, lc, BRANCH_W), seq)] * 5
        + [pl.BlockSpec((None, BRANCH_W, D_HEAD), st),
           pl.BlockSpec(dec.shape, c2), pl.BlockSpec(qdec.shape, c2), pl.BlockSpec(gl.shape, c2),
           pl.BlockSpec(gng.shape, c3), pl.BlockSpec(gnb.shape, c3)],
        out_specs=[pl.BlockSpec((None, lc, BRANCH_W), seq),
                   pl.BlockSpec((None, BRANCH_W, D_HEAD), st)],
        out_shape=[jax.ShapeDtypeStruct((b, length, BRANCH_W), BF16),
                   jax.ShapeDtypeStruct((b, BRANCH_W, D_HEAD), F32)],
        scratch_shapes=[pltpu.VMEM((BRANCH_W, D_HEAD), F32)],
        compiler_params=_params(("parallel", "arbitrary")),
        name="retention",
    )(q, k, kd, v, rg, s0, dec, qdec, gl, gng, gnb)


def _band_kernel(q_ref, k_ref, v_ref, bias_ref, o_ref, *, tq, win, back):
    qi = pl.program_id(1)
    start = pl.multiple_of(jnp.maximum(qi - back, 0) * tq, tq)
    k = k_ref[pl.ds(start, win), :].astype(BF16)
    v = v_ref[pl.ds(start, win), :].astype(BF16)
    qm = _stack_heads(q_ref[...], _own_lanes)
    s = _nt_dot(qm, k) + bias_ref[...]
    m = jnp.max(s, axis=1, keepdims=True)
    p = jnp.exp2(s - m)
    l = jnp.sum(p, axis=1, keepdims=True)
    pb = p.astype(BF16)
    _store_heads(o_ref, [_dot(pb[h * tq:(h + 1) * tq], v[:, _head_slice(h)]) / l[h * tq:(h + 1) * tq]
                         for h in range(N_HEADS)])


def _band_attn(q, k, v, bias, *, tq, win, back):
    b, lq, _ = q.shape
    nvar = bias.shape[0]
    assert lq % tq == 0
    return pl.pallas_call(
        functools.partial(_band_kernel, tq=tq, win=win, back=back),
        grid=(b, lq // tq),
        in_specs=[pl.BlockSpec((None, tq, BRANCH_W), lambda i, t: (i, t, 0)),
                  _full_spec(k), _full_spec(v),
                  pl.BlockSpec((None, N_HEADS * tq, win), lambda i, t: (jnp.minimum(t, nvar - 1), 0, 0))],
        out_specs=pl.BlockSpec((None, tq, BRANCH_W), lambda i, t: (i, t, 0)),
        out_shape=jax.ShapeDtypeStruct((b, lq, BRANCH_W), BF16),
        compiler_params=_params(("parallel", "arbitrary")),
        name="band_attn",
    )(q, k[0], v[0], bias)


def _merge_kernel(x_ref, ba_ref, bb_ref, bc_ref, bd_ref, wg_ref, wb_ref, wo_ref, g_ref, b_ref, o_ref):
    x = x_ref[...]
    xb = x.astype(BF16)
    merged = None
    for n, br_ref in enumerate((ba_ref, bb_ref, bc_ref, bd_ref)):
        logits = _dot(xb, wg_ref[:, n * D_MODEL:(n + 1) * D_MODEL])
        term = _dot(br_ref[...], wb_ref[n]) * _sigmoid(logits)
        merged = term if merged is None else merged + term
    mix = _dot(merged.astype(BF16), wo_ref[...])
    o_ref[...] = _layer_norm(ALPHA * x + mix, g_ref[...], b_ref[...])


def _merge(x, branches, wg, wb, wo, g, b, *, tm):
    m = x.shape[0]
    tm = min(tm, m)
    assert m % tm == 0
    const2 = lambda i: (0, 0)
    row = lambda i: (i, 0)
    return pl.pallas_call(
        _merge_kernel,
        grid=(m // tm,),
        in_specs=[pl.BlockSpec((tm, D_MODEL), row)]
        + [pl.BlockSpec((tm, BRANCH_W), row)] * N_BRANCH
        + [pl.BlockSpec((D_MODEL, N_BRANCH * D_MODEL), const2),
           pl.BlockSpec((N_BRANCH, BRANCH_W, D_MODEL), lambda i: (0, 0, 0)),
           pl.BlockSpec((D_MODEL, D_MODEL), const2),
           pl.BlockSpec((1, D_MODEL), const2),
           pl.BlockSpec((1, D_MODEL), const2)],
        out_specs=pl.BlockSpec((tm, D_MODEL), row),
        out_shape=jax.ShapeDtypeStruct((m, D_MODEL), F32),
        compiler_params=_params(("parallel",)),
        name="merge",
    )(x, *branches, wg, wb, wo, g, b)


def _route(aff_t, sel_t):
    def top2_sum(a, b, c, d):
        hi1, lo1 = jnp.maximum(a, b), jnp.minimum(a, b)
        hi2, lo2 = jnp.maximum(c, d), jnp.minimum(c, d)
        return jnp.maximum(hi1, hi2) + jnp.maximum(jnp.minimum(hi1, hi2), jnp.maximum(lo1, lo2))

    score = [top2_sum(*sel_t[g * EXPERTS_PER_GROUP:(g + 1) * EXPERTS_PER_GROUP])
             for g in range(N_GROUPS)]
    best_here = []
    for g in range(N_GROUPS):
        ok = None
        for o in range(N_GROUPS):
            if o == g:
                continue
            c = (score[g] > score[o]) if o < g else (score[g] >= score[o])
            ok = c if ok is None else jnp.logical_and(ok, c)
        best_here.append(ok)
    picked = []
    for e in range(N_EXPERTS):
        g = e // EXPERTS_PER_GROUP
        rank = jnp.zeros_like(sel_t[e])
        for o in range(g * EXPERTS_PER_GROUP, (g + 1) * EXPERTS_PER_GROUP):
            if o == e:
                continue
            ahead = (sel_t[o] >= sel_t[e]) if o < e else (sel_t[o] > sel_t[e])
            rank = rank + jnp.where(ahead, 1.0, 0.0)
        picked.append(jnp.where(jnp.logical_and(best_here[g], rank < TOP_K), aff_t[e], 0.0))
    total = picked[0]
    for e in range(1, N_EXPERTS):
        total = total + picked[e]
    return [p / total for p in picked]


def _moe_kernel(x_ref, wrh_ref, wrl_ref, br_ref, wgu_ref, wd_ref, g_ref, b_ref, o_ref, acc_ref):
    x = x_ref[...]
    xh = x.astype(BF16)
    xl = (x - xh.astype(F32)).astype(BF16)
    wrh = wrh_ref[...]
    logits = _dot(xh, wrh) + _dot(xl, wrh) + _dot(xh, wrl_ref[...])
    aff = _sigmoid(logits).T
    bias = br_ref[...]
    aff_t = [aff[e:e + 1, :] for e in range(N_EXPERTS)]
    sel_t = [aff_t[e] + bias[e:e + 1, :] for e in range(N_EXPERTS)]
    gate_rows = _route(aff_t, sel_t)
    tm = x.shape[0]
    gate_t = jnp.concatenate(gate_rows + [jnp.zeros((LANE - N_EXPERTS, tm), F32)], axis=0)
    gate = gate_t.T

    for e in range(N_EXPERTS):
        gu = _dot(xh, wgu_ref[e])
        gt, up = gu[:, :D_EXPERT], gu[:, D_EXPERT:]
        hmid = gt * _sigmoid(gt) * up
        y = _dot(hmid.astype(BF16), wd_ref[e]) * gate[:, e:e + 1]
        if e == 0:
            acc_ref[...] = y
        else:
            acc_ref[...] += y
    o_ref[...] = _layer_norm(ALPHA * x + acc_ref[...], g_ref[...], b_ref[...])


def _moe(x, wrh, wrl, br, wgu, wd, g, b, *, tm):
    m = x.shape[0]
    tm = min(tm, m)
    assert m % tm == 0
    const2 = lambda i: (0, 0)
    const3 = lambda i: (0, 0, 0)
    return pl.pallas_call(
        _moe_kernel,
        grid=(m // tm,),
        in_specs=[pl.BlockSpec((tm, D_MODEL), lambda i: (i, 0)),
                  pl.BlockSpec((D_MODEL, LANE), const2),
                  pl.BlockSpec((D_MODEL, LANE), const2),
                  pl.BlockSpec((LANE, 1), const2),
                  pl.BlockSpec((N_EXPERTS, D_MODEL, 2 * D_EXPERT), const3, pipeline_mode=pl.Buffered(1)),
                  pl.BlockSpec((N_EXPERTS, D_EXPERT, D_MODEL), const3, pipeline_mode=pl.Buffered(1)),
                  pl.BlockSpec((1, D_MODEL), const2),
                  pl.BlockSpec((1, D_MODEL), const2)],
        out_specs=pl.BlockSpec((tm, D_MODEL), lambda i: (i, 0)),
        out_shape=jax.ShapeDtypeStruct((m, D_MODEL), F32),
        scratch_shapes=[pltpu.VMEM((tm, D_MODEL), F32)],
        compiler_params=_params(("parallel",)),
        name="moe",
    )(x, wrh, wrl, br, wgu, wd, g, b)


def _rope_tables(pos, d):
    half = d // 2
    inv = jnp.power(ROPE_BASE, -jnp.arange(half, dtype=F32) / half)
    ang = pos.astype(F32)[:, None] * inv[None, :]
    cos, sin = jnp.cos(ang), jnp.sin(ang)
    zero = jnp.zeros_like(sin)
    rep = LANE // d
    cos_t = jnp.tile(jnp.concatenate([cos, cos], axis=1), (1, rep))
    sin_a = jnp.tile(jnp.concatenate([-sin, zero], axis=1), (1, rep))
    sin_b = jnp.tile(jnp.concatenate([zero, sin], axis=1), (1, rep))
    return cos_t, sin_a, sin_b


def _retention_tables(lc):
    log_g = jnp.log1p(-jnp.exp2(-5.0 - jnp.arange(N_HEADS, dtype=F32)))
    i = jnp.arange(lc, dtype=F32)
    diff = i[:, None] - i[None, :]
    dec = jnp.where(diff >= 0, jnp.exp(jnp.maximum(diff, 0.0)[None] * log_g[:, None, None]), 0.0)
    qdec = jnp.exp((i[None, :] + 1.0) * log_g[:, None])
    kdec = jnp.exp((lc - 1.0 - i)[None, :] * log_g[:, None])
    gl = jnp.exp(lc * log_g)
    dec = dec.reshape(N_HEADS * lc, lc)
    qdec = jnp.broadcast_to(qdec[:, :, None], (N_HEADS, lc, D_HEAD)).reshape(N_HEADS * lc, D_HEAD)
    gl = jnp.broadcast_to(gl[:, None, None], (N_HEADS, D_HEAD, D_HEAD)).reshape(BRANCH_W, D_HEAD)
    kdec = jnp.repeat(kdec.T, D_HEAD, axis=1)
    return dec, qdec, kdec, gl


def _band_bias(rel_bias, tq, win, q_minus_k0, valid):
    length = tq + win - 1
    d = np.arange(length) - (tq - 1) - q_minus_k0
    idx = np.clip(d, -REL_CLIP, REL_CLIP) + REL_CLIP
    g = rel_bias[:, idx].astype(F32) * LOG2E
    gp = jnp.concatenate([g, jnp.zeros((N_HEADS, 1), F32)], axis=1)
    m = jnp.tile(gp, (1, tq))[:, :tq * length].reshape(N_HEADS, tq, length)
    tile = m[:, :, tq - 1:tq - 1 + win]
    return jnp.where(valid[None], tile, NEG_INF).reshape(N_HEADS * tq, win)


def _pack_w_in(w_in_l):
    cols = []
    src = 0
    for w in _IN_WIDTH:
        seg = w_in_l[:, src:src + w]
        cols.append(jnp.pad(seg, ((0, 0), (0, _round_up(w, LANE) - w))))
        src += w
    return jnp.concatenate(cols, axis=1).astype(BF16)


def _pad_rows(t, n):
    return jnp.pad(t, ((0, 0), (0, n - t.shape[1]), (0, 0)))


def _token_mixers(x, pos0, past, lw, *, prompt, layer, stacked):
    (w_in_p, qn, wuq, kvn, wukv, gn_g, gn_b, rel_bias, b, length) = lw
    assert pos0 % CHUNK == 0
    pos = pos0 + jnp.arange(length)
    lc = 256 if prompt else length
    dec, qdec, kdec, gl = _retention_tables(lc)
    tables = list(_rope_tables(pos, DR_A)) + list(_rope_tables(pos, D_HEAD)) + [jnp.tile(kdec, (length // lc, 1))]
    if not prompt:
        tables = [jnp.tile(t, (b, 1)) for t in tables]
    outs = _inproj(x, w_in_p, qn, wuq, kvn, wukv, tables, tm=TOKEN_TILE if prompt else b * length,
                   layer=layer, stacked=stacked)
    stacked = tuple(outs[k] for k in _STATE_OUTS)
    per_batch = lambda o: o.reshape(o.shape[:-2] + (b, length, o.shape[-1]))
    (q_a, ckv, kpe, kf, v_a, rq, rk, rkd, rv, rg, sq, sk, sv, bq, bk, bv) = [per_batch(o) for o in outs]
    here = lambda t: (t, layer)
    only = lambda t: (t[None], 0)

    if prompt:
        tq = 256
        o_a = _mla_attn(q_a, only(kf), only(v_a), only(kf), only(v_a), tq=tq, tk=tq, n_own=tq, causal_tiles=True)
        o_c = _sb_attn(sq, here(sk), here(sv), here(sk), here(sv), tq=tq, tk=tq, n_own=tq, causal_tiles=True)
        s0 = jnp.zeros((b, BRANCH_W, D_HEAD), F32)
        win = 3 * tq
        i = np.arange(tq)[:, None]
        c = np.arange(win)[None, :]
        variants = []
        for t in range(3):
            qc, kc = i // CHUNK + t * (tq // CHUNK), c // CHUNK
            variants.append(_band_bias(rel_bias, tq, win, t * tq, (kc <= qc) & (kc >= qc - PREV_CHUNKS)))
        bias = jnp.stack(variants)
        o_d = _band_attn(bq, here(bk), here(bv), bias, tq=tq, win=win, back=2)
    else:
        c_ckv, c_kpe, s_prev, c_sk, c_sv, c_bk, c_bv = past
        n_past = c_ckv.shape[2]
        tko = LANE
        kf_c, v_c = _expand_latent(c_ckv[layer].reshape(b * n_past, KV_RANK),
                                   jnp.pad(c_kpe[layer].reshape(b * n_past, DR_A), ((0, 0), (0, LANE - DR_A))),
                                   wukv, 512)
        o_a = _mla_attn(q_a, only(_pad_rows(kf, tko)), only(_pad_rows(v_a, tko)),
                        only(kf_c.reshape(b, n_past, QA_W)), only(v_c.reshape(b, n_past, BRANCH_W)),
                        tq=length, tk=256, n_own=length, causal_tiles=False)
        o_c = _sb_attn(sq, only(_pad_rows(sk[layer], tko)), only(_pad_rows(sv[layer], tko)),
                       (c_sk.reshape(DEPTH, b, n_past, BRANCH_W), layer),
                       (c_sv.reshape(DEPTH, b, n_past, BRANCH_W), layer),
                       tq=length, tk=256, n_own=length, causal_tiles=False)
        s0 = s_prev[layer].reshape(b, BRANCH_W, D_HEAD)
        n_band = c_bk.shape[2]
        n_keys = n_band + length
        win = _round_up(n_keys, LANE)
        bk_all = _pad_rows(jnp.concatenate([c_bk[layer].reshape(b, n_band, BRANCH_W), bk[layer]], axis=1), win)
        bv_all = _pad_rows(jnp.concatenate([c_bv[layer].reshape(b, n_band, BRANCH_W), bv[layer]], axis=1), win)
        k_pos = pos0 - n_band + np.arange(win)
        q_pos = pos0 + np.arange(length)
        qc, kc = q_pos[:, None] // CHUNK, k_pos[None, :] // CHUNK
        valid = (np.arange(win)[None, :] < n_keys) & (k_pos[None, :] >= 0) & (kc <= qc) & (kc >= qc - PREV_CHUNKS)
        bias = _band_bias(rel_bias, length, win, n_band, valid)[None]
        o_d = _band_attn(bq, only(bk_all), only(bv_all), bias, tq=length, win=win, back=0)

    o_r, s_ret = _retention(rq, rk, rkd, rv, rg, s0, dec, qdec, gl,
                            gn_g.reshape(N_HEADS, 1, D_HEAD), gn_b.reshape(N_HEADS, 1, D_HEAD), lc=lc)
    s_ret = s_ret.reshape(b, N_HEADS, D_HEAD, D_HEAD)
    flat = lambda t: t.reshape(b * length, BRANCH_W)
    return (flat(o_a), flat(o_r), flat(o_c), flat(o_d)), s_ret, stacked


def _state_outputs(stacked, s_ret, b, length, keep):
    ckv, kpe, sk, sv, bk, bv = [t.reshape(DEPTH, b, length, t.shape[-1]) for t in stacked]
    heads4 = lambda t: t.reshape(DEPTH, b, length, N_HEADS, D_HEAD)
    return (ckv, kpe, jnp.stack(s_ret, axis=0), heads4(sk), heads4(sv),
            heads4(bk)[:, :, length - keep:], heads4(bv)[:, :, length - keep:])


def kernel(x_prompt, x_sample, cache_mla_ckv, cache_mla_kpe, state_ret, cache_sb_k, cache_sb_v, cache_band_k, cache_band_v, w_in, mla_q_norm, mla_w_uq, mla_kv_norm, mla_w_ukv, ret_gn_g, ret_gn_b, band_rel_bias, w_branch, w_o, ln1_g, ln1_b, w_router, b_router, w_exp_gate, w_exp_up, w_exp_down, ln2_g, ln2_b):
    bp, lp, _ = x_prompt.shape
    bs, ls, _ = x_sample.shape
    past_len = cache_mla_ckv.shape[2]
    xp = x_prompt.reshape(bp * lp, D_MODEL)
    xs = x_sample.reshape(bs * ls, D_MODEL)

    wr = jnp.pad(w_router, ((0, 0), (0, LANE - N_EXPERTS)))
    wrh = wr.astype(BF16)
    wrl = (wr - wrh.astype(F32)).astype(BF16)
    br = jnp.pad(b_router, (0, LANE - N_EXPERTS)).reshape(LANE, 1)

    past = (cache_mla_ckv, cache_mla_kpe, state_ret, cache_sb_k, cache_sb_v, cache_band_k, cache_band_v)
    ret_p, ret_s = [], []
    stacked_p = stacked_s = None
    for l in range(DEPTH):
        wuq = mla_w_uq[l]
        wuq = jnp.concatenate([wuq[:, :, :DN_A].reshape(Q_RANK, -1), wuq[:, :, DN_A:].reshape(Q_RANK, -1)], axis=1)
        wuq = jnp.pad(wuq, ((0, _round_up(Q_RANK, LANE) - Q_RANK), (0, 0))).astype(BF16)
        wukv = mla_w_ukv[l]
        wukv = jnp.concatenate([wukv[:, :, :DN_A].reshape(KV_RANK, -1), wukv[:, :, DN_A:].reshape(KV_RANK, -1)],
                               axis=1).astype(BF16)
        qn = jnp.pad(mla_q_norm[l], (0, _round_up(Q_RANK, LANE) - Q_RANK)).reshape(1, -1)
        kvn = mla_kv_norm[l].reshape(1, KV_RANK)
        w_in_p = _pack_w_in(w_in[l])
        wg = w_in[l][:, GATE_COL0:].astype(BF16)
        wb = w_branch[l].astype(BF16)
        wo = w_o[l].astype(BF16)
        g1, b1 = ln1_g[l].reshape(1, D_MODEL), ln1_b[l].reshape(1, D_MODEL)
        g2, b2 = ln2_g[l].reshape(1, D_MODEL), ln2_b[l].reshape(1, D_MODEL)
        wgu = jnp.concatenate([w_exp_gate[l], w_exp_up[l]], axis=-1).astype(BF16)
        wd = w_exp_down[l].astype(BF16)
        lw = (w_in_p, qn, wuq, kvn, wukv, ret_gn_g[l], ret_gn_b[l], band_rel_bias[l])

        br_p, s_ret_p, stacked_p = _token_mixers(xp, 0, None, lw + (bp, lp), prompt=True,
                                                 layer=l, stacked=stacked_p)
        br_s, s_ret_s, stacked_s = _token_mixers(xs, past_len, past, lw + (bs, ls), prompt=False,
                                                 layer=l, stacked=stacked_s)
        xp = _merge(xp, br_p, wg, wb, wo, g1, b1, tm=TOKEN_TILE)
        xs = _merge(xs, br_s, wg, wb, wo, g1, b1, tm=TOKEN_TILE)
        xp = _moe(xp, wrh, wrl, br, wgu, wd, g2, b2, tm=TOKEN_TILE)
        xs = _moe(xs, wrh, wrl, br, wgu, wd, g2, b2, tm=TOKEN_TILE)
        ret_p.append(s_ret_p)
        ret_s.append(s_ret_s)

    return ((xp.reshape(bp, lp, D_MODEL), xs.reshape(bs, ls, D_MODEL))
            + _state_outputs(stacked_p, ret_p, bp, lp, min(PREV_CHUNKS * CHUNK, lp))
            + _state_outputs(stacked_s, ret_s, bs, ls, ls))
```

```python
import functools

import jax
import jax.numpy as jnp
import numpy as np
from jax import lax
from jax.experimental import pallas as pl
from jax.experimental.pallas import tpu as pltpu

D_MODEL = 1024
DEPTH = 2
CHUNK = 64
N_BRANCH = 4
BRANCH_W = D_MODEL // 4
N_HEADS = 4
D_HEAD = BRANCH_W // N_HEADS
DN_A = 64
DR_A = 32
DQK_A = DN_A + DR_A
Q_RANK = (3 * D_MODEL) // 16
KV_RANK = D_MODEL // 8
PREV_CHUNKS = 8
REL_CLIP = 128
ROPE_BASE = 10000.0
N_EXPERTS = 16
N_GROUPS = 4
EXPERTS_PER_GROUP = N_EXPERTS // N_GROUPS
TOP_K = 2
D_EXPERT = D_MODEL // 4
ALPHA = (2.0 * DEPTH) ** 0.25
EPS = 1e-5
NEG_INF = -1e30
LOG2E = 1.4426950408889634
SB_DEAD_LOG2 = -150.0

F32 = jnp.float32
BF16 = jnp.bfloat16

V7X_VMEM_LIMIT = 56 * 1024 * 1024
LANE = 128
TOKEN_TILE = 512

_IN_NAMES = ("c_q", "c_kv", "k_pe", "rq", "rk", "rv", "rg", "sq", "sk", "sv", "bq", "bk", "bv")
_IN_WIDTH = (Q_RANK, KV_RANK, DR_A) + (BRANCH_W,) * 10
QA_W = N_HEADS * DN_A + N_HEADS * DR_A


def _round_up(n, m):
    return (n + m - 1) // m * m


_IN_OFF = {}
_off = 0
for _n, _w in zip(_IN_NAMES, _IN_WIDTH):
    _IN_OFF[_n] = (_off, _round_up(_w, LANE))
    _off += _round_up(_w, LANE)
IN_PACKED = _off
GATE_COL0 = sum(_IN_WIDTH)


def _params(sem):
    return pltpu.CompilerParams(dimension_semantics=sem, vmem_limit_bytes=V7X_VMEM_LIMIT)


def _nt_dot(a, b):
    return lax.dot_general(a, b, (((1,), (1,)), ((), ())), preferred_element_type=F32)


def _tn_dot(a, b):
    return lax.dot_general(a, b, (((0,), (0,)), ((), ())), preferred_element_type=F32)


def _dot(a, b):
    return jnp.dot(a, b, preferred_element_type=F32)


def _layer_norm(v, g, b):
    mu = jnp.mean(v, axis=-1, keepdims=True)
    d = v - mu
    var = jnp.mean(d * d, axis=-1, keepdims=True)
    return d * lax.rsqrt(var + EPS) * g + b


def _sigmoid(v):
    return 0.5 * jnp.tanh(0.5 * v) + 0.5


def _head_slice(h):
    return slice(h * D_HEAD, (h + 1) * D_HEAD)


def _stack_heads(q, lane_sets):
    lane = lax.broadcasted_iota(jnp.int32, q.shape, 1)
    zero = jnp.zeros_like(q)
    parts = []
    for h in range(N_HEADS):
        keep = None
        for lo, hi in lane_sets(h):
            m = jnp.logical_and(lane >= lo, lane < hi)
            keep = m if keep is None else jnp.logical_or(keep, m)
        parts.append(jnp.where(keep, q, zero))
    return jnp.concatenate(parts, axis=0)


def _own_lanes(h):
    return ((h * D_HEAD, (h + 1) * D_HEAD),)


def _mla_lanes(h):
    base = N_HEADS * DN_A
    return ((h * DN_A, (h + 1) * DN_A), (base + h * DR_A, base + (h + 1) * DR_A))


def _store_heads(o_ref, parts):
    for h, p in enumerate(parts):
        o_ref[:, _head_slice(h)] = p.astype(o_ref.dtype)


def _expand_kernel(ckv_ref, kpe_ref, w_ref, kf_ref, v_ref):
    kvx = _dot(ckv_ref[...].astype(BF16), w_ref[...])
    kp = kpe_ref[...]
    kpt = kp + pltpu.roll(kp, DR_A, 1) + pltpu.roll(kp, 2 * DR_A, 1) + pltpu.roll(kp, 3 * DR_A, 1)
    kf_ref[:, :N_HEADS * DN_A] = kvx[:, :N_HEADS * DN_A].astype(BF16)
    kf_ref[:, N_HEADS * DN_A:] = kpt.astype(BF16)
    v_ref[...] = kvx[:, N_HEADS * DN_A:].astype(BF16)


def _expand_latent(ckv, kpe_padded, w_ukv, tm):
    m = ckv.shape[0]
    assert m % tm == 0
    return pl.pallas_call(
        _expand_kernel,
        grid=(m // tm,),
        in_specs=[pl.BlockSpec((tm, KV_RANK), lambda i: (i, 0)),
                  pl.BlockSpec((tm, LANE), lambda i: (i, 0)),
                  pl.BlockSpec((KV_RANK, 2 * BRANCH_W), lambda i: (0, 0))],
        out_specs=[pl.BlockSpec((tm, QA_W), lambda i: (i, 0)),
                   pl.BlockSpec((tm, BRANCH_W), lambda i: (i, 0))],
        out_shape=[jax.ShapeDtypeStruct((m, QA_W), BF16),
                   jax.ShapeDtypeStruct((m, BRANCH_W), BF16)],
        compiler_params=_params(("parallel",)),
        name="expand_latent",
    )(ckv, kpe_padded, w_ukv)


def _rope_block(x, cos, sin_a, sin_b, half):
    return x * cos + pltpu.roll(x, LANE - half, 1) * sin_a + pltpu.roll(x, half, 1) * sin_b


def _inproj_kernel(x_ref, w_ref, qn_ref, wuq_ref, kvn_ref, wukv_ref,
                   c32_ref, a32_ref, b32_ref, c64_ref, a64_ref, b64_ref, kdec_ref,
                   qa_ref, ckv_ref, kpe_ref, kf_ref, va_ref,
                   rq_ref, rk_ref, rkd_ref, rv_ref, rg_ref,
                   sq_ref, sk_ref, sv_ref, bq_ref, bk_ref, bv_ref, qat_ref=None, vat_ref=None):
    z = _dot(x_ref[...].astype(BF16), w_ref[...])

    def seg(name):
        o, w = _IN_OFF[name]
        return z[:, o:o + w]

    cq = seg("c_q")
    cqn = cq * lax.rsqrt(jnp.sum(cq * cq, axis=1, keepdims=True) * (1.0 / Q_RANK) + EPS) * qn_ref[...]
    qa = _dot(cqn.astype(BF16), wuq_ref[...])
    scale_a = DQK_A ** -0.5 * LOG2E
    nope_w = N_HEADS * DN_A
    q_pe = _rope_block(qa[:, nope_w:], c32_ref[...], a32_ref[...], b32_ref[...], DR_A // 2)
    qa_s = jnp.concatenate([qa[:, :nope_w], q_pe], axis=1) * scale_a
    qa_ref[...] = qa_s.astype(BF16)
    if qat_ref is not None:
        qat_ref[...] = qa_s.T.astype(BF16)

    ckv_raw = seg("c_kv")
    ckv = ckv_raw * lax.rsqrt(jnp.mean(ckv_raw * ckv_raw, axis=1, keepdims=True) + EPS) * kvn_ref[...]
    ckv_ref[...] = ckv
    kvx = _dot(ckv.astype(BF16), wukv_ref[...])
    kp = _rope_block(seg("k_pe"), c32_ref[...], a32_ref[...], b32_ref[...], DR_A // 2)
    kpe_ref[...] = kp[:, :DR_A]
    kpt = kp + pltpu.roll(kp, DR_A, 1) + pltpu.roll(kp, 2 * DR_A, 1) + pltpu.roll(kp, 3 * DR_A, 1)
    kf_ref[:, :nope_w] = kvx[:, :nope_w].astype(BF16)
    kf_ref[:, nope_w:] = kpt.astype(BF16)
    va_ref[...] = kvx[:, nope_w:].astype(BF16)
    if vat_ref is not None:
        vat_ref[...] = kvx[:, nope_w:].T.astype(BF16)

    rq, rk = seg("rq"), seg("rk")
    kdec = kdec_ref[...]
    for blk in range(BRANCH_W // LANE):
        cols = slice(blk * LANE, (blk + 1) * LANE)
        rq_ref[:, cols] = _rope_block(rq[:, cols], c64_ref[...], a64_ref[...], b64_ref[...],
                                      D_HEAD // 2).astype(BF16)
        rkb = _rope_block(rk[:, cols], c64_ref[...], a64_ref[...], b64_ref[...], D_HEAD // 2) * (D_HEAD ** -0.5)
        rk_ref[:, cols] = rkb.astype(BF16)
        rkd_ref[:, cols] = (rkb * kdec[:, cols]).astype(BF16)
    rv_ref[...] = seg("rv").astype(BF16)
    rg_ref[...] = seg("rg")

    scale_h = D_HEAD ** -0.5 * LOG2E
    sq_ref[...] = (seg("sq") * scale_h).astype(BF16)
    sk_ref[...] = seg("sk")
    sv_ref[...] = seg("sv")
    bq_ref[...] = (seg("bq") * scale_h).astype(BF16)
    bk_ref[...] = seg("bk")
    bv_ref[...] = seg("bv")


_INPROJ_OUT = (
    (QA_W, BF16), (KV_RANK, F32), (DR_A, F32), (QA_W, BF16), (BRANCH_W, BF16),
    (BRANCH_W, BF16), (BRANCH_W, BF16), (BRANCH_W, BF16), (BRANCH_W, BF16), (BRANCH_W, F32),
    (BRANCH_W, BF16), (BRANCH_W, F32), (BRANCH_W, F32), (BRANCH_W, BF16), (BRANCH_W, F32), (BRANCH_W, F32))


_STATE_OUTS = (1, 2, 11, 12, 14, 15)


def _inproj_entry(*refs, n_in, n_alias):
    _inproj_kernel(*refs[:n_in], *refs[n_in + n_alias:])


def _inproj(x, w_in_p, qn, wuq, kvn, wukv, tables, *, tm, layer, stacked, seq_len=None):
    m = x.shape[0]
    tm = min(tm, m)
    assert m % tm == 0
    n_pos_tiles = tables[0].shape[0] // tm
    assert tables[0].shape[0] % tm == 0
    row = lambda i: (i, 0)
    const = lambda i: (0, 0)
    pos = lambda i: (i % n_pos_tiles, 0)
    in_specs = [pl.BlockSpec((tm, D_MODEL), row),
                pl.BlockSpec(w_in_p.shape, const),
                pl.BlockSpec(qn.shape, const),
                pl.BlockSpec(wuq.shape, const),
                pl.BlockSpec(kvn.shape, const),
                pl.BlockSpec(wukv.shape, const)]
    in_specs += [pl.BlockSpec((tm, t.shape[1]), pos) for t in tables]
    n_in = len(in_specs)
    out_specs, out_shape = [], []
    for k, (w, dt) in enumerate(_INPROJ_OUT):
        if k in _STATE_OUTS:
            out_specs.append(pl.BlockSpec((None, tm, w), lambda i: (layer, i, 0)))
            out_shape.append(jax.ShapeDtypeStruct((DEPTH, m, w), dt))
        else:
            out_specs.append(pl.BlockSpec((tm, w), row))
            out_shape.append(jax.ShapeDtypeStruct((m, w), dt))
    if seq_len is not None:
        assert seq_len % tm == 0 and m % seq_len == 0
        tps = seq_len // tm
        for w in (QA_W, BRANCH_W):
            out_specs.append(pl.BlockSpec((None, w, tm), lambda i: (i // tps, 0, i % tps)))
            out_shape.append(jax.ShapeDtypeStruct((m // seq_len, w, seq_len), BF16))
    aliases = {}
    extra = ()
    if stacked is not None:
        extra = tuple(stacked)
        in_specs += [pl.BlockSpec(memory_space=pl.ANY)] * len(extra)
        aliases = {n_in + k: o for k, o in enumerate(_STATE_OUTS)}
    return pl.pallas_call(
        functools.partial(_inproj_entry, n_in=n_in, n_alias=len(extra)),
        grid=(m // tm,),
        in_specs=in_specs,
        out_specs=out_specs,
        out_shape=out_shape,
        input_output_aliases=aliases,
        compiler_params=_params(("parallel",)),
        name="inproj",
    )(x, w_in_p, qn, wuq, kvn, wukv, *tables, *extra)


def _tile_spec(src, rows):
    arr, layer = src
    return pl.BlockSpec((None, None, rows, arr.shape[3]), lambda i, t: (layer, i, t, 0))


def _full_spec(src):
    arr, layer = src
    return pl.BlockSpec((None, None, arr.shape[2], arr.shape[3]), lambda i, t: (layer, i, 0, 0))


def _mla_kernel(q_ref, kfo_ref, vo_ref, kfp_ref, vp_ref, o_ref, *, tq, tk, n_own, n_past):
    qi = pl.program_id(1)
    qm = _stack_heads(q_ref[...], _mla_lanes)
    rows = N_HEADS * tq
    tko = kfo_ref.shape[0]
    n_loop = qi if n_past is None else n_past

    def softmax_pv(s, v, carry):
        m, accs = carry
        m_new = jnp.maximum(m, jnp.max(s, axis=1, keepdims=True))
        pb = jnp.exp2(s - m_new).astype(BF16)
        a = jnp.exp2(m - m_new)
        ones = jnp.ones((v.shape[0], D_HEAD), BF16)
        accs = tuple(a[h * tq:(h + 1) * tq] * accs[h]
                     + _dot(pb[h * tq:(h + 1) * tq], jnp.concatenate([v[:, _head_slice(h)], ones], axis=1))
                     for h in range(N_HEADS))
        return m_new, accs

    def past_scores(j):
        start = pl.multiple_of(jnp.minimum(j, jnp.maximum(n_loop - 1, 0)) * tk, tk)
        return _nt_dot(qm, kfp_ref[pl.ds(start, tk), :])

    row_q = lax.broadcasted_iota(jnp.int32, (rows, tko), 0) % tq
    col = lax.broadcasted_iota(jnp.int32, (rows, tko), 1)
    own_mask = jnp.logical_and(col < n_own, col // CHUNK <= row_q // CHUNK)
    carry = (jnp.full((rows, 1), NEG_INF, F32),
             tuple(jnp.zeros((tq, 2 * D_HEAD), F32) for _ in range(N_HEADS)))
    s_next = past_scores(0)
    carry = softmax_pv(jnp.where(own_mask, _nt_dot(qm, kfo_ref[...]), NEG_INF), vo_ref[...], carry)

    def body(j, c):
        s_cur, rest = c
        s_after = past_scores(j + 1)
        start = pl.multiple_of(j * tk, tk)
        return s_after, softmax_pv(s_cur, vp_ref[pl.ds(start, tk), :], rest)

    _, (_, accs) = lax.fori_loop(0, n_loop, body, (s_next, carry))
    _store_heads(o_ref, [accs[h][:, :D_HEAD] / accs[h][:, D_HEAD:D_HEAD + 1] for h in range(N_HEADS)])


def _mla_attn(q, kf_own, v_own, kf_past, v_past, *, tq, tk, n_own, causal_tiles):
    b, lq, _ = q.shape
    nqt = lq // tq
    tko = kf_own[0].shape[2] // nqt
    lp = kf_past[0].shape[2]
    assert lq % tq == 0 and lp % tk == 0
    return pl.pallas_call(
        functools.partial(_mla_kernel, tq=tq, tk=tk, n_own=n_own,
                          n_past=None if causal_tiles else lp // tk),
        grid=(b, nqt),
        in_specs=[pl.BlockSpec((None, tq, QA_W), lambda i, t: (i, t, 0)),
                  _tile_spec(kf_own, tko), _tile_spec(v_own, tko),
                  _full_spec(kf_past), _full_spec(v_past)],
        out_specs=pl.BlockSpec((None, tq, BRANCH_W), lambda i, t: (i, t, 0)),
        out_shape=jax.ShapeDtypeStruct((b, lq, BRANCH_W), BF16),
        compiler_params=_params(("parallel", "arbitrary")),
        name="mla_attn",
    )(q, kf_own[0], v_own[0], kf_past[0], v_past[0])


def _mla_kernel_t(qt_ref, kfo_ref, vto_ref, kfp_ref, vtp_ref, o_ref, *, tq, tk):
    qi = pl.program_id(1)
    qt = qt_ref[...]
    feat = lax.broadcasted_iota(jnp.int32, qt.shape, 0)
    parts = []
    for h in range(N_HEADS):
        keep = None
        for lo, hi in _mla_lanes(h):
            m = jnp.logical_and(feat >= lo, feat < hi)
            keep = m if keep is None else jnp.logical_or(keep, m)
        parts.append(jnp.where(keep, qt, jnp.zeros_like(qt)))
    qmt = jnp.concatenate(parts, axis=1)
    cols = N_HEADS * tq

    def softmax_pv(st, vt, carry):
        m, l, accs = carry
        m_new = jnp.maximum(m, jnp.max(st, axis=0, keepdims=True))
        pt = jnp.exp2(st - m_new)
        a = jnp.exp2(m - m_new)
        l = a * l + jnp.sum(pt, axis=0, keepdims=True)
        pb = pt.astype(BF16)
        accs = tuple(a[:, h * tq:(h + 1) * tq] * accs[h]
                     + _dot(vt[_head_slice(h), :], pb[:, h * tq:(h + 1) * tq]) for h in range(N_HEADS))
        return m_new, l, accs

    def past_scores(j):
        start = pl.multiple_of(jnp.minimum(j, jnp.maximum(qi - 1, 0)) * tk, tk)
        return _dot(kfp_ref[pl.ds(start, tk), :], qmt)

    key = lax.broadcasted_iota(jnp.int32, (tq, cols), 0)
    qry = lax.broadcasted_iota(jnp.int32, (tq, cols), 1) % tq
    own_mask = key // CHUNK <= qry // CHUNK
    carry = (jnp.full((1, cols), NEG_INF, F32), jnp.zeros((1, cols), F32),
             tuple(jnp.zeros((D_HEAD, tq), F32) for _ in range(N_HEADS)))
    s_next = past_scores(0)
    carry = softmax_pv(jnp.where(own_mask, _dot(kfo_ref[...], qmt), NEG_INF), vto_ref[...], carry)

    def body(j, c):
        s_cur, rest = c
        s_after = past_scores(j + 1)
        start = pl.multiple_of(j * tk, tk)
        return s_after, softmax_pv(s_cur, vtp_ref[:, pl.ds(start, tk)], rest)

    _, (_, l, accs) = lax.fori_loop(0, qi, body, (s_next, carry))
    out_t = jnp.concatenate([accs[h] / l[:, h * tq:(h + 1) * tq] for h in range(N_HEADS)], axis=0)
    o_ref[...] = out_t.T.astype(o_ref.dtype)


def _mla_attn_t(qt, kf, vt, *, tq):
    b, _, length = qt.shape
    assert length % tq == 0
    return pl.pallas_call(
        functools.partial(_mla_kernel_t, tq=tq, tk=tq),
        grid=(b, length // tq),
        in_specs=[pl.BlockSpec((None, QA_W, tq), lambda i, t: (i, 0, t)),
                  pl.BlockSpec((None, tq, QA_W), lambda i, t: (i, t, 0)),
                  pl.BlockSpec((None, BRANCH_W, tq), lambda i, t: (i, 0, t)),
                  pl.BlockSpec((None, length, QA_W), lambda i, t: (i, 0, 0)),
                  pl.BlockSpec((None, BRANCH_W, length), lambda i, t: (i, 0, 0))],
        out_specs=pl.BlockSpec((None, tq, BRANCH_W), lambda i, t: (i, t, 0)),
        out_shape=jax.ShapeDtypeStruct((b, length, BRANCH_W), BF16),
        compiler_params=_params(("parallel", "arbitrary")),
        name="mla_attn_t",
    )(qt, kf, vt, kf, vt)


def _sb_kernel(q_ref, ko_ref, vo_ref, kp_ref, vp_ref, o_ref, *, tq, tk, n_own, n_past):
    qi = pl.program_id(1)
    qm = _stack_heads(q_ref[...], _own_lanes)
    rows = N_HEADS * tq
    tko = ko_ref.shape[0]
    n_loop = qi if n_past is None else n_past

    def tri2(n):
        r = lax.broadcasted_iota(jnp.int32, (2 * n, n), 0) % n
        c = lax.broadcasted_iota(jnp.int32, (2 * n, n), 1)
        return jnp.where(r > c, 1.0, 0.0).astype(BF16)

    def weigh(z, v, carry, mask, tri):
        run, accs = carry
        neg_abs = lax.bitcast_convert_type(
            lax.bitcast_convert_type(z, jnp.uint32) | jnp.uint32(0x80000000), F32)
        t = jnp.log2(1.0 + jnp.exp2(neg_abs))
        log_beta = jnp.minimum(z, 0.0) - t
        log_stay = log_beta - z
        if mask is not None:
            log_stay = jnp.where(mask, log_stay, 0.0)
        hi = log_stay.astype(BF16)
        lo = (log_stay - hi.astype(F32)).astype(BF16)
        later = _dot(jnp.concatenate([hi, lo], axis=1), tri) + run
        w = jnp.exp2(log_beta + later)
        if mask is not None:
            w = jnp.where(mask, w, 0.0)
        wb = w.astype(BF16)
        vb = v.astype(BF16)
        accs = tuple(accs[h] + _dot(wb[h * tq:(h + 1) * tq], vb[:, _head_slice(h)])
                     for h in range(N_HEADS))
        return run + jnp.sum(log_stay, axis=1, keepdims=True), accs

    row_q = lax.broadcasted_iota(jnp.int32, (rows, tko), 0) % tq
    col = lax.broadcasted_iota(jnp.int32, (rows, tko), 1)
    own_mask = jnp.logical_and(col < n_own, col < row_q)

    def past_start(jj):
        return pl.multiple_of(jnp.clip(n_loop - 1 - jj, 0, kp_ref.shape[0] // tk - 1) * tk, tk)

    def past_scores(jj):
        return _nt_dot(qm, kp_ref[pl.ds(past_start(jj), tk), :].astype(BF16))

    def alive(run):
        return (jnp.max(run) > SB_DEAD_LOG2).astype(jnp.int32)

    z_next = past_scores(0)
    carry = (jnp.zeros((rows, 1), F32), tuple(jnp.zeros((tq, D_HEAD), F32) for _ in range(N_HEADS)))
    run, accs = weigh(_nt_dot(qm, ko_ref[...].astype(BF16)), vo_ref[...], carry, own_mask, tri2(tko))
    tri_past = tri2(tk)

    def cond(c):
        return jnp.logical_and(c[0] < n_loop, c[1] > 0)

    def body(c):
        jj, _, z_cur, run, accs = c
        z_after = past_scores(jj + 1)
        run, accs = weigh(z_cur, vp_ref[pl.ds(past_start(jj), tk), :], (run, accs), None, tri_past)
        return jj + 1, alive(run), z_after, run, accs

    out = lax.while_loop(cond, body, (jnp.int32(0), alive(run), z_next, run, accs))
    _store_heads(o_ref, out[4])


def _sb_attn(q, k_own, v_own, k_past, v_past, *, tq, tk, n_own, causal_tiles):
    b, lq, _ = q.shape
    nqt = lq // tq
    tko = k_own[0].shape[2] // nqt
    lp = k_past[0].shape[2]
    assert lq % tq == 0 and lp % tk == 0
    return pl.pallas_call(
        functools.partial(_sb_kernel, tq=tq, tk=tk, n_own=n_own,
                          n_past=None if causal_tiles else lp // tk),
        grid=(b, nqt),
        in_specs=[pl.BlockSpec((None, tq, BRANCH_W), lambda i, t: (i, t, 0)),
                  _tile_spec(k_own, tko), _tile_spec(v_own, tko),
                  _full_spec(k_past), _full_spec(v_past)],
        out_specs=pl.BlockSpec((None, tq, BRANCH_W), lambda i, t: (i, t, 0)),
        out_shape=jax.ShapeDtypeStruct((b, lq, BRANCH_W), BF16),
        compiler_params=_params(("parallel", "arbitrary")),
        name="sb_attn",
    )(q, k_own[0], v_own[0], k_past[0], v_past[0])


def _ret_kernel(q_ref, k_ref, kd_ref, v_ref, rg_ref, s0_ref, dec_ref, qdec_ref, gl_ref,
                gng_ref, gnb_ref, o_ref, sout_ref, state_ref, *, lc):
    c = pl.program_id(1)

    @pl.when(c == 0)
    def _():
        state_ref[...] = s0_ref[...]

    qm = _stack_heads(q_ref[...], _own_lanes)
    v = v_ref[...]
    state = state_ref[...]
    scores = (_nt_dot(qm, k_ref[...]) * dec_ref[...]).astype(BF16)
    cross = _dot(qm, state.astype(BF16)) * qdec_ref[...]
    kv_full = _tn_dot(kd_ref[...], v)
    new_state = gl_ref[...] * state + jnp.concatenate(
        [kv_full[_head_slice(h), _head_slice(h)] for h in range(N_HEADS)], axis=0)
    state_ref[...] = new_state

    rg = rg_ref[...]
    outs = []
    for h in range(N_HEADS):
        o = _dot(scores[h * lc:(h + 1) * lc], v[:, _head_slice(h)]) + cross[h * lc:(h + 1) * lc]
        mu = jnp.mean(o, axis=-1, keepdims=True)
        d = o - mu
        var = jnp.mean(d * d, axis=-1, keepdims=True)
        y = d * lax.rsqrt(var + EPS) * gng_ref[h] + gnb_ref[h]
        g = rg[:, _head_slice(h)]
        outs.append(y * (g * _sigmoid(g)))
    _store_heads(o_ref, outs)

    @pl.when(c == pl.num_programs(1) - 1)
    def _():
        sout_ref[...] = new_state


def _retention(q, k, kd, v, rg, s0, dec, qdec, gl, gng, gnb, *, lc):
    b, length, _ = q.shape
    assert length % lc == 0
    seq = lambda i, t: (i, t, 0)
    st = lambda i, t: (i, 0, 0)
    c2 = lambda i, t: (0, 0)
    c3 = lambda i, t: (0, 0, 0)
    return pl.pallas_call(
        functools.partial(_ret_kernel, lc=lc),
        grid=(b, length // lc),
        in_specs=[pl.BlockSpec((None, lc, BRANCH_W), seq)] * 5
        + [pl.BlockSpec((None, BRANCH_W, D_HEAD), st),
           pl.BlockSpec(dec.shape, c2), pl.BlockSpec(qdec.shape, c2), pl.BlockSpec(gl.shape, c2),
           pl.BlockSpec(gng.shape, c3), pl.BlockSpec(gnb.shape, c3)],
        out_specs=[pl.BlockSpec((None, lc, BRANCH_W), seq),
                   pl.BlockSpec((None, BRANCH_W, D_HEAD), st)],
        out_shape=[jax.ShapeDtypeStruct((b, length, BRANCH_W), BF16),
                   jax.ShapeDtypeStruct((b, BRANCH_W, D_HEAD), F32)],
        scratch_shapes=[pltpu.VMEM((BRANCH_W, D_HEAD), F32)],
        compiler_params=_params(("parallel", "arbitrary")),
        name="retention",
    )(q, k, kd, v, rg, s0, dec, qdec, gl, gng, gnb)


def _band_kernel(q_ref, k_ref, v_ref, bias_ref, o_ref, *, tq, win, back):
    qi = pl.program_id(1)
    start = pl.multiple_of(jnp.maximum(qi - back, 0) * tq, tq)
    k = k_ref[pl.ds(start, win), :].astype(BF16)
    v = v_ref[pl.ds(start, win), :].astype(BF16)
    qm = _stack_heads(q_ref[...], _own_lanes)
    s = _nt_dot(qm, k) + bias_ref[...]
    pb = jnp.exp2(s - jnp.max(s, axis=1, keepdims=True)).astype(BF16)
    ones = jnp.ones((win, D_HEAD), BF16)
    outs = []
    for h in range(N_HEADS):
        o = _dot(pb[h * tq:(h + 1) * tq], jnp.concatenate([v[:, _head_slice(h)], ones], axis=1))
        outs.append(o[:, :D_HEAD] / o[:, D_HEAD:D_HEAD + 1])
    _store_heads(o_ref, outs)


def _band_attn(q, k, v, bias, *, tq, win, back):
    b, lq, _ = q.shape
    nvar = bias.shape[0]
    assert lq % tq == 0
    return pl.pallas_call(
        functools.partial(_band_kernel, tq=tq, win=win, back=back),
        grid=(b, lq // tq),
        in_specs=[pl.BlockSpec((None, tq, BRANCH_W), lambda i, t: (i, t, 0)),
                  _full_spec(k), _full_spec(v),
                  pl.BlockSpec((None, N_HEADS * tq, win), lambda i, t: (jnp.minimum(t, nvar - 1), 0, 0))],
        out_specs=pl.BlockSpec((None, tq, BRANCH_W), lambda i, t: (i, t, 0)),
        out_shape=jax.ShapeDtypeStruct((b, lq, BRANCH_W), BF16),
        compiler_params=_params(("parallel", "arbitrary")),
        name="band_attn",
    )(q, k[0], v[0], bias)


def _merge_kernel(x_ref, ba_ref, bb_ref, bc_ref, bd_ref, wg_ref, wb_ref, wo_ref, g_ref, b_ref, o_ref):
    x = x_ref[...]
    xb = x.astype(BF16)
    merged = None
    for n, br_ref in enumerate((ba_ref, bb_ref, bc_ref, bd_ref)):
        logits = _dot(xb, wg_ref[:, n * D_MODEL:(n + 1) * D_MODEL])
        term = _dot(br_ref[...], wb_ref[n]) * _sigmoid(logits)
        merged = term if merged is None else merged + term
    mix = _dot(merged.astype(BF16), wo_ref[...])
    o_ref[...] = _layer_norm(ALPHA * x + mix, g_ref[...], b_ref[...])


def _merge(x, branches, wg, wb, wo, g, b, *, tm):
    m = x.shape[0]
    tm = min(tm, m)
    assert m % tm == 0
    const2 = lambda i: (0, 0)
    row = lambda i: (i, 0)
    return pl.pallas_call(
        _merge_kernel,
        grid=(m // tm,),
        in_specs=[pl.BlockSpec((tm, D_MODEL), row)]
        + [pl.BlockSpec((tm, BRANCH_W), row)] * N_BRANCH
        + [pl.BlockSpec((D_MODEL, N_BRANCH * D_MODEL), const2),
           pl.BlockSpec((N_BRANCH, BRANCH_W, D_MODEL), lambda i: (0, 0, 0)),
           pl.BlockSpec((D_MODEL, D_MODEL), const2),
           pl.BlockSpec((1, D_MODEL), const2),
           pl.BlockSpec((1, D_MODEL), const2)],
        out_specs=pl.BlockSpec((tm, D_MODEL), row),
        out_shape=jax.ShapeDtypeStruct((m, D_MODEL), F32),
        compiler_params=_params(("parallel",)),
        name="merge",
    )(x, *branches, wg, wb, wo, g, b)


def _route(aff_t, sel_t):
    def top2_sum(a, b, c, d):
        hi1, lo1 = jnp.maximum(a, b), jnp.minimum(a, b)
        hi2, lo2 = jnp.maximum(c, d), jnp.minimum(c, d)
        return jnp.maximum(hi1, hi2) + jnp.maximum(jnp.minimum(hi1, hi2), jnp.maximum(lo1, lo2))

    score = [top2_sum(*sel_t[g * EXPERTS_PER_GROUP:(g + 1) * EXPERTS_PER_GROUP])
             for g in range(N_GROUPS)]
    best_here = []
    for g in range(N_GROUPS):
        ok = None
        for o in range(N_GROUPS):
            if o == g:
                continue
            c = (score[g] > score[o]) if o < g else (score[g] >= score[o])
            ok = c if ok is None else jnp.logical_and(ok, c)
        best_here.append(ok)
    picked = []
    for e in range(N_EXPERTS):
        g = e // EXPERTS_PER_GROUP
        rank = jnp.zeros_like(sel_t[e])
        for o in range(g * EXPERTS_PER_GROUP, (g + 1) * EXPERTS_PER_GROUP):
            if o == e:
                continue
            ahead = (sel_t[o] >= sel_t[e]) if o < e else (sel_t[o] > sel_t[e])
            rank = rank + jnp.where(ahead, 1.0, 0.0)
        picked.append(jnp.where(jnp.logical_and(best_here[g], rank < TOP_K), aff_t[e], 0.0))
    total = picked[0]
    for e in range(1, N_EXPERTS):
        total = total + picked[e]
    return [p / total for p in picked]


def _moe_kernel(x_ref, wrh_ref, wrl_ref, br_ref, wgu_ref, wd_ref, g_ref, b_ref, o_ref, acc_ref):
    x = x_ref[...]
    xh = x.astype(BF16)
    xl = (x - xh.astype(F32)).astype(BF16)
    wrh = wrh_ref[...]
    logits = _dot(xh, wrh) + _dot(xl, wrh) + _dot(xh, wrl_ref[...])
    aff = _sigmoid(logits).T
    bias = br_ref[...]
    aff_t = [aff[e:e + 1, :] for e in range(N_EXPERTS)]
    sel_t = [aff_t[e] + bias[e:e + 1, :] for e in range(N_EXPERTS)]
    gate_rows = _route(aff_t, sel_t)
    tm = x.shape[0]
    gate_t = jnp.concatenate(gate_rows + [jnp.zeros((LANE - N_EXPERTS, tm), F32)], axis=0)
    gate = gate_t.T

    for e in range(N_EXPERTS):
        gu = _dot(xh, wgu_ref[e])
        gt, up = gu[:, :D_EXPERT], gu[:, D_EXPERT:]
        hmid = gt * _sigmoid(gt) * up
        y = _dot(hmid.astype(BF16), wd_ref[e]) * gate[:, e:e + 1]
        if e == 0:
            acc_ref[...] = y
        else:
            acc_ref[...] += y
    o_ref[...] = _layer_norm(ALPHA * x + acc_ref[...], g_ref[...], b_ref[...])


def _moe(x, wrh, wrl, br, wgu, wd, g, b, *, tm):
    m = x.shape[0]
    tm = min(tm, m)
    assert m % tm == 0
    const2 = lambda i: (0, 0)
    const3 = lambda i: (0, 0, 0)
    return pl.pallas_call(
        _moe_kernel,
        grid=(m // tm,),
        in_specs=[pl.BlockSpec((tm, D_MODEL), lambda i: (i, 0)),
                  pl.BlockSpec((D_MODEL, LANE), const2),
                  pl.BlockSpec((D_MODEL, LANE), const2),
                  pl.BlockSpec((LANE, 1), const2),
                  pl.BlockSpec((N_EXPERTS, D_MODEL, 2 * D_EXPERT), const3, pipeline_mode=pl.Buffered(1)),
                  pl.BlockSpec((N_EXPERTS, D_EXPERT, D_MODEL), const3, pipeline_mode=pl.Buffered(1)),
                  pl.BlockSpec((1, D_MODEL), const2),
                  pl.BlockSpec((1, D_MODEL), const2)],
        out_specs=pl.BlockSpec((tm, D_MODEL), lambda i: (i, 0)),
        out_shape=jax.ShapeDtypeStruct((m, D_MODEL), F32),
        scratch_shapes=[pltpu.VMEM((tm, D_MODEL), F32)],
        compiler_params=_params(("parallel",)),
        name="moe",
    )(x, wrh, wrl, br, wgu, wd, g, b)


def _rope_tables(pos, d):
    half = d // 2
    inv = jnp.power(ROPE_BASE, -jnp.arange(half, dtype=F32) / half)
    ang = pos.astype(F32)[:, None] * inv[None, :]
    cos, sin = jnp.cos(ang), jnp.sin(ang)
    zero = jnp.zeros_like(sin)
    rep = LANE // d
    cos_t = jnp.tile(jnp.concatenate([cos, cos], axis=1), (1, rep))
    sin_a = jnp.tile(jnp.concatenate([-sin, zero], axis=1), (1, rep))
    sin_b = jnp.tile(jnp.concatenate([zero, sin], axis=1), (1, rep))
    return cos_t, sin_a, sin_b


def _retention_tables(lc):
    log_g = jnp.log1p(-jnp.exp2(-5.0 - jnp.arange(N_HEADS, dtype=F32)))
    i = jnp.arange(lc, dtype=F32)
    diff = i[:, None] - i[None, :]
    dec = jnp.where(diff >= 0, jnp.exp(jnp.maximum(diff, 0.0)[None] * log_g[:, None, None]), 0.0)
    qdec = jnp.exp((i[None, :] + 1.0) * log_g[:, None])
    kdec = jnp.exp((lc - 1.0 - i)[None, :] * log_g[:, None])
    gl = jnp.exp(lc * log_g)
    dec = dec.reshape(N_HEADS * lc, lc)
    qdec = jnp.broadcast_to(qdec[:, :, None], (N_HEADS, lc, D_HEAD)).reshape(N_HEADS * lc, D_HEAD)
    gl = jnp.broadcast_to(gl[:, None, None], (N_HEADS, D_HEAD, D_HEAD)).reshape(BRANCH_W, D_HEAD)
    kdec = jnp.repeat(kdec.T, D_HEAD, axis=1)
    return dec, qdec, kdec, gl


def _band_bias(rel_bias, tq, win, q_minus_k0, valid):
    length = tq + win - 1
    d = np.arange(length) - (tq - 1) - q_minus_k0
    idx = np.clip(d, -REL_CLIP, REL_CLIP) + REL_CLIP
    g = rel_bias[:, idx].astype(F32) * LOG2E
    gp = jnp.concatenate([g, jnp.zeros((N_HEADS, 1), F32)], axis=1)
    m = jnp.tile(gp, (1, tq))[:, :tq * length].reshape(N_HEADS, tq, length)
    tile = m[:, :, tq - 1:tq - 1 + win]
    return jnp.where(valid[None], tile, NEG_INF).reshape(N_HEADS * tq, win)


def _pack_w_in(w_in_l):
    cols = []
    src = 0
    for w in _IN_WIDTH:
        seg = w_in_l[:, src:src + w]
        cols.append(jnp.pad(seg, ((0, 0), (0, _round_up(w, LANE) - w))))
        src += w
    return jnp.concatenate(cols, axis=1).astype(BF16)


def _pad_rows(t, n):
    return jnp.pad(t, ((0, 0), (0, n - t.shape[1]), (0, 0)))


def _token_mixers(x, pos0, past, lw, *, prompt, layer, stacked):
    (w_in_p, qn, wuq, kvn, wukv, gn_g, gn_b, rel_bias, b, length) = lw
    assert pos0 % CHUNK == 0
    pos = pos0 + jnp.arange(length)
    lc = 256 if prompt else length
    dec, qdec, kdec, gl = _retention_tables(lc)
    tables = list(_rope_tables(pos, DR_A)) + list(_rope_tables(pos, D_HEAD)) + [jnp.tile(kdec, (length // lc, 1))]
    if not prompt:
        tables = [jnp.tile(t, (b, 1)) for t in tables]
    outs = _inproj(x, w_in_p, qn, wuq, kvn, wukv, tables, tm=TOKEN_TILE if prompt else b * length,
                   layer=layer, stacked=stacked, seq_len=length if prompt else None)
    stacked = tuple(outs[k] for k in _STATE_OUTS)
    per_batch = lambda o: o.reshape(o.shape[:-2] + (b, length, o.shape[-1]))
    (q_a, ckv, kpe, kf, v_a, rq, rk, rkd, rv, rg, sq, sk, sv, bq, bk, bv) = [
        per_batch(o) for o in outs[:len(_INPROJ_OUT)]]
    here = lambda t: (t, layer)
    only = lambda t: (t[None], 0)

    if prompt:
        tq = 256
        o_a = _mla_attn_t(outs[-2], kf, outs[-1], tq=tq)
        o_c = _sb_attn(sq, here(sk), here(sv), here(sk), here(sv), tq=tq, tk=tq, n_own=tq, causal_tiles=True)
        s0 = jnp.zeros((b, BRANCH_W, D_HEAD), F32)
        win = 3 * tq
        i = np.arange(tq)[:, None]
        c = np.arange(win)[None, :]
        variants = []
        for t in range(3):
            qc, kc = i // CHUNK + t * (tq // CHUNK), c // CHUNK
            variants.append(_band_bias(rel_bias, tq, win, t * tq, (kc <= qc) & (kc >= qc - PREV_CHUNKS)))
        bias = jnp.stack(variants)
        o_d = _band_attn(bq, here(bk), here(bv), bias, tq=tq, win=win, back=2)
    else:
        c_ckv, c_kpe, s_prev, c_sk, c_sv, c_bk, c_bv = past
        n_past = c_ckv.shape[2]
        tko = LANE
        kf_c, v_c = _expand_latent(c_ckv[layer].reshape(b * n_past, KV_RANK),
                                   jnp.pad(c_kpe[layer].reshape(b * n_past, DR_A), ((0, 0), (0, LANE - DR_A))),
                                   wukv, 512)
        o_a = _mla_attn(q_a, only(_pad_rows(kf, tko)), only(_pad_rows(v_a, tko)),
                        only(kf_c.reshape(b, n_past, QA_W)), only(v_c.reshape(b, n_past, BRANCH_W)),
                        tq=length, tk=256, n_own=length, causal_tiles=False)
        o_c = _sb_attn(sq, only(_pad_rows(sk[layer], tko)), only(_pad_rows(sv[layer], tko)),
                       (c_sk.reshape(DEPTH, b, n_past, BRANCH_W), layer),
                       (c_sv.reshape(DEPTH, b, n_past, BRANCH_W), layer),
                       tq=length, tk=256, n_own=length, causal_tiles=False)
        s0 = s_prev[layer].reshape(b, BRANCH_W, D_HEAD)
        n_band = c_bk.shape[2]
        n_keys = n_band + length
        win = _round_up(n_keys, LANE)
        bk_all = _pad_rows(jnp.concatenate([c_bk[layer].reshape(b, n_band, BRANCH_W), bk[layer]], axis=1), win)
        bv_all = _pad_rows(jnp.concatenate([c_bv[layer].reshape(b, n_band, BRANCH_W), bv[layer]], axis=1), win)
        k_pos = pos0 - n_band + np.arange(win)
        q_pos = pos0 + np.arange(length)
        qc, kc = q_pos[:, None] // CHUNK, k_pos[None, :] // CHUNK
        valid = (np.arange(win)[None, :] < n_keys) & (k_pos[None, :] >= 0) & (kc <= qc) & (kc >= qc - PREV_CHUNKS)
        bias = _band_bias(rel_bias, length, win, n_band, valid)[None]
        o_d = _band_attn(bq, only(bk_all), only(bv_all), bias, tq=length, win=win, back=0)

    o_r, s_ret = _retention(rq, rk, rkd, rv, rg, s0, dec, qdec, gl,
                            gn_g.reshape(N_HEADS, 1, D_HEAD), gn_b.reshape(N_HEADS, 1, D_HEAD), lc=lc)
    s_ret = s_ret.reshape(b, N_HEADS, D_HEAD, D_HEAD)
    flat = lambda t: t.reshape(b * length, BRANCH_W)
    return (flat(o_a), flat(o_r), flat(o_c), flat(o_d)), s_ret, stacked


def _state_outputs(stacked, s_ret, b, length, keep):
    ckv, kpe, sk, sv, bk, bv = [t.reshape(DEPTH, b, length, t.shape[-1]) for t in stacked]
    heads4 = lambda t: t.reshape(DEPTH, b, length, N_HEADS, D_HEAD)
    return (ckv, kpe, jnp.stack(s_ret, axis=0), heads4(sk), heads4(sv),
            heads4(bk)[:, :, length - keep:], heads4(bv)[:, :, length - keep:])


def kernel(x_prompt, x_sample, cache_mla_ckv, cache_mla_kpe, state_ret, cache_sb_k, cache_sb_v, cache_band_k, cache_band_v, w_in, mla_q_norm, mla_w_uq, mla_kv_norm, mla_w_ukv, ret_gn_g, ret_gn_b, band_rel_bias, w_branch, w_o, ln1_g, ln1_b, w_router, b_router, w_exp_gate, w_exp_up, w_exp_down, ln2_g, ln2_b):
    bp, lp, _ = x_prompt.shape
    bs, ls, _ = x_sample.shape
    past_len = cache_mla_ckv.shape[2]
    xp = x_prompt.reshape(bp * lp, D_MODEL)
    xs = x_sample.reshape(bs * ls, D_MODEL)

    wr = jnp.pad(w_router, ((0, 0), (0, LANE - N_EXPERTS)))
    wrh = wr.astype(BF16)
    wrl = (wr - wrh.astype(F32)).astype(BF16)
    br = jnp.pad(b_router, (0, LANE - N_EXPERTS)).reshape(LANE, 1)

    past = (cache_mla_ckv, cache_mla_kpe, state_ret, cache_sb_k, cache_sb_v, cache_band_k, cache_band_v)
    ret_p, ret_s = [], []
    stacked_p = stacked_s = None
    for l in range(DEPTH):
        wuq = mla_w_uq[l]
        wuq = jnp.concatenate([wuq[:, :, :DN_A].reshape(Q_RANK, -1), wuq[:, :, DN_A:].reshape(Q_RANK, -1)], axis=1)
        wuq = jnp.pad(wuq, ((0, _round_up(Q_RANK, LANE) - Q_RANK), (0, 0))).astype(BF16)
        wukv = mla_w_ukv[l]
        wukv = jnp.concatenate([wukv[:, :, :DN_A].reshape(KV_RANK, -1), wukv[:, :, DN_A:].reshape(KV_RANK, -1)],
                               axis=1).astype(BF16)
        qn = jnp.pad(mla_q_norm[l], (0, _round_up(Q_RANK, LANE) - Q_RANK)).reshape(1, -1)
        kvn = mla_kv_norm[l].reshape(1, KV_RANK)
        w_in_p = _pack_w_in(w_in[l])
        wg = w_in[l][:, GATE_COL0:].astype(BF16)
        wb = w_branch[l].astype(BF16)
        wo = w_o[l].astype(BF16)
        g1, b1 = ln1_g[l].reshape(1, D_MODEL), ln1_b[l].reshape(1, D_MODEL)
        g2, b2 = ln2_g[l].reshape(1, D_MODEL), ln2_b[l].reshape(1, D_MODEL)
        wgu = jnp.concatenate([w_exp_gate[l], w_exp_up[l]], axis=-1).astype(BF16)
        wd = w_exp_down[l].astype(BF16)
        lw = (w_in_p, qn, wuq, kvn, wukv, ret_gn_g[l], ret_gn_b[l], band_rel_bias[l])

        br_p, s_ret_p, stacked_p = _token_mixers(xp, 0, None, lw + (bp, lp), prompt=True,
                                                 layer=l, stacked=stacked_p)
        br_s, s_ret_s, stacked_s = _token_mixers(xs, past_len, past, lw + (bs, ls), prompt=False,
                                                 layer=l, stacked=stacked_s)
        xp = _merge(xp, br_p, wg, wb, wo, g1, b1, tm=TOKEN_TILE)
        xs = _merge(xs, br_s, wg, wb, wo, g1, b1, tm=TOKEN_TILE)
        xp = _moe(xp, wrh, wrl, br, wgu, wd, g2, b2, tm=TOKEN_TILE)
        xs = _moe(xs, wrh, wrl, br, wgu, wd, g2, b2, tm=TOKEN_TILE)
        ret_p.append(s_ret_p)
        ret_s.append(s_ret_s)

    return ((xp.reshape(bp, lp, D_MODEL), xs.reshape(bs, ls, D_MODEL))
            + _state_outputs(stacked_p, ret_p, bp, lp, min(PREV_CHUNKS * CHUNK, lp))
            + _state_outputs(stacked_s, ret_s, bs, ls, ls))
```

```python
import functools

import jax
import jax.numpy as jnp
import numpy as np
from jax import lax
from jax.experimental import pallas as pl
from jax.experimental.pallas import tpu as pltpu

D_MODEL = 1024
DEPTH = 2
CHUNK = 64
N_BRANCH = 4
BRANCH_W = D_MODEL // 4
N_HEADS = 4
D_HEAD = BRANCH_W // N_HEADS
DN_A = 64
DR_A = 32
DQK_A = DN_A + DR_A
Q_RANK = (3 * D_MODEL) // 16
KV_RANK = D_MODEL // 8
PREV_CHUNKS = 8
REL_CLIP = 128
ROPE_BASE = 10000.0
N_EXPERTS = 16
N_GROUPS = 4
EXPERTS_PER_GROUP = N_EXPERTS // N_GROUPS
TOP_K = 2
D_EXPERT = D_MODEL // 4
ALPHA = (2.0 * DEPTH) ** 0.25
EPS = 1e-5
NEG_INF = -1e30
LOG2E = 1.4426950408889634
SB_DEAD_LOG2 = -150.0

F32 = jnp.float32
BF16 = jnp.bfloat16

V7X_VMEM_LIMIT = 56 * 1024 * 1024
LANE = 128
TOKEN_TILE = 512
RET_SEQS_PER_STEP = 2

_IN_NAMES = ("c_q", "c_kv", "k_pe", "rq", "rk", "rv", "rg", "sq", "sk", "sv", "bq", "bk", "bv")
_IN_WIDTH = (Q_RANK, KV_RANK, DR_A) + (BRANCH_W,) * 10
QA_W = N_HEADS * DN_A + N_HEADS * DR_A


def _round_up(n, m):
    return (n + m - 1) // m * m


_IN_OFF = {}
_off = 0
for _n, _w in zip(_IN_NAMES, _IN_WIDTH):
    _IN_OFF[_n] = (_off, _round_up(_w, LANE))
    _off += _round_up(_w, LANE)
IN_PACKED = _off
GATE_COL0 = sum(_IN_WIDTH)


def _params(sem):
    return pltpu.CompilerParams(dimension_semantics=sem, vmem_limit_bytes=V7X_VMEM_LIMIT)


def _nt_dot(a, b):
    return lax.dot_general(a, b, (((1,), (1,)), ((), ())), preferred_element_type=F32)


def _tn_dot(a, b):
    return lax.dot_general(a, b, (((0,), (0,)), ((), ())), preferred_element_type=F32)


def _dot(a, b):
    return jnp.dot(a, b, preferred_element_type=F32)


def _layer_norm(v, g, b):
    mu = jnp.mean(v, axis=-1, keepdims=True)
    d = v - mu
    var = jnp.mean(d * d, axis=-1, keepdims=True)
    return d * lax.rsqrt(var + EPS) * g + b


def _sigmoid(v):
    return 0.5 * jnp.tanh(0.5 * v) + 0.5


def _head_slice(h):
    return slice(h * D_HEAD, (h + 1) * D_HEAD)


def _stack_heads(q, lane_sets):
    lane = lax.broadcasted_iota(jnp.int32, q.shape, 1)
    zero = jnp.zeros_like(q)
    parts = []
    for h in range(N_HEADS):
        keep = None
        for lo, hi in lane_sets(h):
            m = jnp.logical_and(lane >= lo, lane < hi)
            keep = m if keep is None else jnp.logical_or(keep, m)
        parts.append(jnp.where(keep, q, zero))
    return jnp.concatenate(parts, axis=0)


def _own_lanes(h):
    return ((h * D_HEAD, (h + 1) * D_HEAD),)


def _mla_lanes(h):
    base = N_HEADS * DN_A
    return ((h * DN_A, (h + 1) * DN_A), (base + h * DR_A, base + (h + 1) * DR_A))


def _store_heads(o_ref, parts):
    for h, p in enumerate(parts):
        o_ref[:, _head_slice(h)] = p.astype(o_ref.dtype)


def _expand_kernel(ckv_ref, kpe_ref, w_ref, kf_ref, v_ref):
    kvx = _dot(ckv_ref[...].astype(BF16), w_ref[...])
    kp = kpe_ref[...]
    kpt = kp + pltpu.roll(kp, DR_A, 1) + pltpu.roll(kp, 2 * DR_A, 1) + pltpu.roll(kp, 3 * DR_A, 1)
    kf_ref[:, :N_HEADS * DN_A] = kvx[:, :N_HEADS * DN_A].astype(BF16)
    kf_ref[:, N_HEADS * DN_A:] = kpt.astype(BF16)
    v_ref[...] = kvx[:, N_HEADS * DN_A:].astype(BF16)


def _expand_latent(ckv, kpe_padded, w_ukv, tm):
    m = ckv.shape[0]
    assert m % tm == 0
    return pl.pallas_call(
        _expand_kernel,
        grid=(m // tm,),
        in_specs=[pl.BlockSpec((tm, KV_RANK), lambda i: (i, 0)),
                  pl.BlockSpec((tm, LANE), lambda i: (i, 0)),
                  pl.BlockSpec((KV_RANK, 2 * BRANCH_W), lambda i: (0, 0))],
        out_specs=[pl.BlockSpec((tm, QA_W), lambda i: (i, 0)),
                   pl.BlockSpec((tm, BRANCH_W), lambda i: (i, 0))],
        out_shape=[jax.ShapeDtypeStruct((m, QA_W), BF16),
                   jax.ShapeDtypeStruct((m, BRANCH_W), BF16)],
        compiler_params=_params(("parallel",)),
        name="expand_latent",
    )(ckv, kpe_padded, w_ukv)


def _rope_block(x, cos, sin_a, sin_b, half):
    return x * cos + pltpu.roll(x, LANE - half, 1) * sin_a + pltpu.roll(x, half, 1) * sin_b


def _inproj_kernel(x_ref, w_ref, qn_ref, wuq_ref, kvn_ref, wukv_ref,
                   c32_ref, a32_ref, b32_ref, c64_ref, a64_ref, b64_ref, kdec_ref,
                   qa_ref, ckv_ref, kpe_ref, kf_ref, va_ref,
                   rq_ref, rk_ref, rkd_ref, rv_ref, rg_ref,
                   sq_ref, sk_ref, sv_ref, bq_ref, bk_ref, bv_ref,
                   qat_ref=None, vat_ref=None, sqt_ref=None, svt_ref=None, bqt_ref=None, bvt_ref=None):
    z = _dot(x_ref[...].astype(BF16), w_ref[...])

    def seg(name):
        o, w = _IN_OFF[name]
        return z[:, o:o + w]

    cq = seg("c_q")
    cqn = cq * lax.rsqrt(jnp.sum(cq * cq, axis=1, keepdims=True) * (1.0 / Q_RANK) + EPS) * qn_ref[...]
    qa = _dot(cqn.astype(BF16), wuq_ref[...])
    scale_a = DQK_A ** -0.5 * LOG2E
    nope_w = N_HEADS * DN_A
    q_pe = _rope_block(qa[:, nope_w:], c32_ref[...], a32_ref[...], b32_ref[...], DR_A // 2)
    qa_s = jnp.concatenate([qa[:, :nope_w], q_pe], axis=1) * scale_a
    qa_ref[...] = qa_s.astype(BF16)
    if qat_ref is not None:
        qat_ref[...] = qa_s.T.astype(BF16)

    ckv_raw = seg("c_kv")
    ckv = ckv_raw * lax.rsqrt(jnp.mean(ckv_raw * ckv_raw, axis=1, keepdims=True) + EPS) * kvn_ref[...]
    ckv_ref[...] = ckv
    kvx = _dot(ckv.astype(BF16), wukv_ref[...])
    kp = _rope_block(seg("k_pe"), c32_ref[...], a32_ref[...], b32_ref[...], DR_A // 2)
    kpe_ref[...] = kp[:, :DR_A]
    kpt = kp + pltpu.roll(kp, DR_A, 1) + pltpu.roll(kp, 2 * DR_A, 1) + pltpu.roll(kp, 3 * DR_A, 1)
    kf_ref[:, :nope_w] = kvx[:, :nope_w].astype(BF16)
    kf_ref[:, nope_w:] = kpt.astype(BF16)
    va_ref[...] = kvx[:, nope_w:].astype(BF16)
    if vat_ref is not None:
        vat_ref[...] = kvx[:, nope_w:].T.astype(BF16)

    rq, rk = seg("rq"), seg("rk")
    kdec = kdec_ref[...]
    for blk in range(BRANCH_W // LANE):
        cols = slice(blk * LANE, (blk + 1) * LANE)
        rq_ref[:, cols] = _rope_block(rq[:, cols], c64_ref[...], a64_ref[...], b64_ref[...],
                                      D_HEAD // 2).astype(BF16)
        rkb = _rope_block(rk[:, cols], c64_ref[...], a64_ref[...], b64_ref[...], D_HEAD // 2) * (D_HEAD ** -0.5)
        rk_ref[:, cols] = rkb.astype(BF16)
        rkd_ref[:, cols] = (rkb * kdec[:, cols]).astype(BF16)
    rv_ref[...] = seg("rv").astype(BF16)
    rg_ref[...] = seg("rg")

    scale_h = D_HEAD ** -0.5 * LOG2E
    sq = seg("sq") * scale_h
    sq_ref[...] = sq.astype(BF16)
    sk_ref[...] = seg("sk")
    sv_ref[...] = seg("sv")
    if sqt_ref is not None:
        sqt_ref[...] = sq.T.astype(BF16)
        svt_ref[...] = seg("sv").T.astype(BF16)
    bq = seg("bq") * scale_h
    bq_ref[...] = bq.astype(BF16)
    bk_ref[...] = seg("bk")
    bv_ref[...] = seg("bv")
    if bqt_ref is not None:
        bqt_ref[...] = bq.T.astype(BF16)
        bvt_ref[...] = seg("bv").T.astype(BF16)


_INPROJ_OUT = (
    (QA_W, BF16), (KV_RANK, F32), (DR_A, F32), (QA_W, BF16), (BRANCH_W, BF16),
    (BRANCH_W, BF16), (BRANCH_W, BF16), (BRANCH_W, BF16), (BRANCH_W, BF16), (BRANCH_W, F32),
    (BRANCH_W, BF16), (BRANCH_W, F32), (BRANCH_W, F32), (BRANCH_W, BF16), (BRANCH_W, F32), (BRANCH_W, F32))


_STATE_OUTS = (1, 2, 11, 12, 14, 15)


def _inproj_entry(*refs, n_in, n_alias):
    _inproj_kernel(*refs[:n_in], *refs[n_in + n_alias:])


def _inproj(x, w_in_p, qn, wuq, kvn, wukv, tables, *, tm, layer, stacked, seq_len=None):
    m = x.shape[0]
    tm = min(tm, m)
    assert m % tm == 0
    n_pos_tiles = tables[0].shape[0] // tm
    assert tables[0].shape[0] % tm == 0
    row = lambda i: (i, 0)
    const = lambda i: (0, 0)
    pos = lambda i: (i % n_pos_tiles, 0)
    in_specs = [pl.BlockSpec((tm, D_MODEL), row),
                pl.BlockSpec(w_in_p.shape, const),
                pl.BlockSpec(qn.shape, const),
                pl.BlockSpec(wuq.shape, const),
                pl.BlockSpec(kvn.shape, const),
                pl.BlockSpec(wukv.shape, const)]
    in_specs += [pl.BlockSpec((tm, t.shape[1]), pos) for t in tables]
    n_in = len(in_specs)
    out_specs, out_shape = [], []
    for k, (w, dt) in enumerate(_INPROJ_OUT):
        if k in _STATE_OUTS:
            out_specs.append(pl.BlockSpec((None, tm, w), lambda i: (layer, i, 0)))
            out_shape.append(jax.ShapeDtypeStruct((DEPTH, m, w), dt))
        else:
            out_specs.append(pl.BlockSpec((tm, w), row))
            out_shape.append(jax.ShapeDtypeStruct((m, w), dt))
    if seq_len is not None:
        assert seq_len % tm == 0 and m % seq_len == 0
        tps = seq_len // tm
        for w in (QA_W,) + (BRANCH_W,) * 5:
            out_specs.append(pl.BlockSpec((None, w, tm), lambda i: (i // tps, 0, i % tps)))
            out_shape.append(jax.ShapeDtypeStruct((m // seq_len, w, seq_len), BF16))
    aliases = {}
    extra = ()
    if stacked is not None:
        extra = tuple(stacked)
        in_specs += [pl.BlockSpec(memory_space=pl.ANY)] * len(extra)
        aliases = {n_in + k: o for k, o in enumerate(_STATE_OUTS)}
    return pl.pallas_call(
        functools.partial(_inproj_entry, n_in=n_in, n_alias=len(extra)),
        grid=(m // tm,),
        in_specs=in_specs,
        out_specs=out_specs,
        out_shape=out_shape,
        input_output_aliases=aliases,
        compiler_params=_params(("parallel",)),
        name="inproj",
    )(x, w_in_p, qn, wuq, kvn, wukv, *tables, *extra)


def _tile_spec(src, rows):
    arr, layer = src
    return pl.BlockSpec((None, None, rows, arr.shape[3]), lambda i, t: (layer, i, t, 0))


def _full_spec(src):
    arr, layer = src
    return pl.BlockSpec((None, None, arr.shape[2], arr.shape[3]), lambda i, t: (layer, i, 0, 0))


def _mla_kernel(q_ref, kfo_ref, vo_ref, kfp_ref, vp_ref, o_ref, *, tq, tk, n_own, n_past):
    qi = pl.program_id(1)
    qm = _stack_heads(q_ref[...], _mla_lanes)
    rows = N_HEADS * tq
    tko = kfo_ref.shape[0]
    n_loop = qi if n_past is None else n_past

    def softmax_pv(s, v, carry):
        m, accs = carry
        m_new = jnp.maximum(m, jnp.max(s, axis=1, keepdims=True))
        pb = jnp.exp2(s - m_new).astype(BF16)
        a = jnp.exp2(m - m_new)
        ones = jnp.ones((v.shape[0], D_HEAD), BF16)
        accs = tuple(a[h * tq:(h + 1) * tq] * accs[h]
                     + _dot(pb[h * tq:(h + 1) * tq], jnp.concatenate([v[:, _head_slice(h)], ones], axis=1))
                     for h in range(N_HEADS))
        return m_new, accs

    def past_scores(j):
        start = pl.multiple_of(jnp.minimum(j, jnp.maximum(n_loop - 1, 0)) * tk, tk)
        return _nt_dot(qm, kfp_ref[pl.ds(start, tk), :])

    row_q = lax.broadcasted_iota(jnp.int32, (rows, tko), 0) % tq
    col = lax.broadcasted_iota(jnp.int32, (rows, tko), 1)
    own_mask = jnp.logical_and(col < n_own, col // CHUNK <= row_q // CHUNK)
    carry = (jnp.full((rows, 1), NEG_INF, F32),
             tuple(jnp.zeros((tq, 2 * D_HEAD), F32) for _ in range(N_HEADS)))
    s_next = past_scores(0)
    carry = softmax_pv(jnp.where(own_mask, _nt_dot(qm, kfo_ref[...]), NEG_INF), vo_ref[...], carry)

    def body(j, c):
        s_cur, rest = c
        s_after = past_scores(j + 1)
        start = pl.multiple_of(j * tk, tk)
        return s_after, softmax_pv(s_cur, vp_ref[pl.ds(start, tk), :], rest)

    _, (_, accs) = lax.fori_loop(0, n_loop, body, (s_next, carry))
    _store_heads(o_ref, [accs[h][:, :D_HEAD] / accs[h][:, D_HEAD:D_HEAD + 1] for h in range(N_HEADS)])


def _mla_attn(q, kf_own, v_own, kf_past, v_past, *, tq, tk, n_own, causal_tiles):
    b, lq, _ = q.shape
    nqt = lq // tq
    tko = kf_own[0].shape[2] // nqt
    lp = kf_past[0].shape[2]
    assert lq % tq == 0 and lp % tk == 0
    return pl.pallas_call(
        functools.partial(_mla_kernel, tq=tq, tk=tk, n_own=n_own,
                          n_past=None if causal_tiles else lp // tk),
        grid=(b, nqt),
        in_specs=[pl.BlockSpec((None, tq, QA_W), lambda i, t: (i, t, 0)),
                  _tile_spec(kf_own, tko), _tile_spec(v_own, tko),
                  _full_spec(kf_past), _full_spec(v_past)],
        out_specs=pl.BlockSpec((None, tq, BRANCH_W), lambda i, t: (i, t, 0)),
        out_shape=jax.ShapeDtypeStruct((b, lq, BRANCH_W), BF16),
        compiler_params=_params(("parallel", "arbitrary")),
        name="mla_attn",
    )(q, kf_own[0], v_own[0], kf_past[0], v_past[0])


def _mla_kernel_t(qt_ref, kfo_ref, vto_ref, kfp_ref, vtp_ref, o_ref, *, tq, tk):
    qi = pl.program_id(1)
    qt = qt_ref[...]
    feat = lax.broadcasted_iota(jnp.int32, qt.shape, 0)
    parts = []
    for h in range(N_HEADS):
        keep = None
        for lo, hi in _mla_lanes(h):
            m = jnp.logical_and(feat >= lo, feat < hi)
            keep = m if keep is None else jnp.logical_or(keep, m)
        parts.append(jnp.where(keep, qt, jnp.zeros_like(qt)))
    qmt = jnp.concatenate(parts, axis=1)
    cols = N_HEADS * tq

    def softmax_pv(st, vt, carry):
        m, l, accs = carry
        m_new = jnp.maximum(m, jnp.max(st, axis=0, keepdims=True))
        pt = jnp.exp2(st - m_new)
        a = jnp.exp2(m - m_new)
        l = a * l + jnp.sum(pt, axis=0, keepdims=True)
        pb = pt.astype(BF16)
        accs = tuple(a[:, h * tq:(h + 1) * tq] * accs[h]
                     + _dot(vt[_head_slice(h), :], pb[:, h * tq:(h + 1) * tq]) for h in range(N_HEADS))
        return m_new, l, accs

    def past_scores(j):
        start = pl.multiple_of(jnp.minimum(j, jnp.maximum(qi - 1, 0)) * tk, tk)
        return _dot(kfp_ref[pl.ds(start, tk), :], qmt)

    key = lax.broadcasted_iota(jnp.int32, (tq, cols), 0)
    qry = lax.broadcasted_iota(jnp.int32, (tq, cols), 1) % tq
    own_mask = key // CHUNK <= qry // CHUNK
    carry = (jnp.full((1, cols), NEG_INF, F32), jnp.zeros((1, cols), F32),
             tuple(jnp.zeros((D_HEAD, tq), F32) for _ in range(N_HEADS)))
    s_next = past_scores(0)
    carry = softmax_pv(jnp.where(own_mask, _dot(kfo_ref[...], qmt), NEG_INF), vto_ref[...], carry)

    def body(j, c):
        s_cur, rest = c
        s_after = past_scores(j + 1)
        start = pl.multiple_of(j * tk, tk)
        return s_after, softmax_pv(s_cur, vtp_ref[:, pl.ds(start, tk)], rest)

    _, (_, l, accs) = lax.fori_loop(0, qi, body, (s_next, carry))
    out_t = jnp.concatenate([accs[h] / l[:, h * tq:(h + 1) * tq] for h in range(N_HEADS)], axis=0)
    o_ref[...] = out_t.T.astype(o_ref.dtype)


def _mla_attn_t(qt, kf, vt, *, tq):
    b, _, length = qt.shape
    assert length % tq == 0
    return pl.pallas_call(
        functools.partial(_mla_kernel_t, tq=tq, tk=tq),
        grid=(b, length // tq),
        in_specs=[pl.BlockSpec((None, QA_W, tq), lambda i, t: (i, 0, t)),
                  pl.BlockSpec((None, tq, QA_W), lambda i, t: (i, t, 0)),
                  pl.BlockSpec((None, BRANCH_W, tq), lambda i, t: (i, 0, t)),
                  pl.BlockSpec((None, length, QA_W), lambda i, t: (i, 0, 0)),
                  pl.BlockSpec((None, BRANCH_W, length), lambda i, t: (i, 0, 0))],
        out_specs=pl.BlockSpec((None, tq, BRANCH_W), lambda i, t: (i, t, 0)),
        out_shape=jax.ShapeDtypeStruct((b, length, BRANCH_W), BF16),
        compiler_params=_params(("parallel", "arbitrary")),
        name="mla_attn_t",
    )(qt, kf, vt, kf, vt)


def _sb_kernel(q_ref, ko_ref, vo_ref, kp_ref, vp_ref, o_ref, *, tq, tk, n_own, n_past):
    qi = pl.program_id(1)
    qm = _stack_heads(q_ref[...], _own_lanes)
    rows = N_HEADS * tq
    tko = ko_ref.shape[0]
    n_loop = qi if n_past is None else n_past

    def tri2(n):
        r = lax.broadcasted_iota(jnp.int32, (2 * n, n), 0) % n
        c = lax.broadcasted_iota(jnp.int32, (2 * n, n), 1)
        return jnp.where(r > c, 1.0, 0.0).astype(BF16)

    def weigh(z, v, carry, mask, tri):
        run, accs = carry
        neg_abs = lax.bitcast_convert_type(
            lax.bitcast_convert_type(z, jnp.uint32) | jnp.uint32(0x80000000), F32)
        t = jnp.log2(1.0 + jnp.exp2(neg_abs))
        log_beta = jnp.minimum(z, 0.0) - t
        log_stay = log_beta - z
        if mask is not None:
            log_stay = jnp.where(mask, log_stay, 0.0)
        hi = log_stay.astype(BF16)
        lo = (log_stay - hi.astype(F32)).astype(BF16)
        later = _dot(jnp.concatenate([hi, lo], axis=1), tri) + run
        w = jnp.exp2(log_beta + later)
        if mask is not None:
            w = jnp.where(mask, w, 0.0)
        wb = w.astype(BF16)
        vb = v.astype(BF16)
        accs = tuple(accs[h] + _dot(wb[h * tq:(h + 1) * tq], vb[:, _head_slice(h)])
                     for h in range(N_HEADS))
        return run + jnp.sum(log_stay, axis=1, keepdims=True), accs

    row_q = lax.broadcasted_iota(jnp.int32, (rows, tko), 0) % tq
    col = lax.broadcasted_iota(jnp.int32, (rows, tko), 1)
    own_mask = jnp.logical_and(col < n_own, col < row_q)

    def past_start(jj):
        return pl.multiple_of(jnp.clip(n_loop - 1 - jj, 0, kp_ref.shape[0] // tk - 1) * tk, tk)

    def past_scores(jj):
        return _nt_dot(qm, kp_ref[pl.ds(past_start(jj), tk), :].astype(BF16))

    def alive(run):
        return (jnp.max(run) > SB_DEAD_LOG2).astype(jnp.int32)

    z_next = past_scores(0)
    carry = (jnp.zeros((rows, 1), F32), tuple(jnp.zeros((tq, D_HEAD), F32) for _ in range(N_HEADS)))
    run, accs = weigh(_nt_dot(qm, ko_ref[...].astype(BF16)), vo_ref[...], carry, own_mask, tri2(tko))
    tri_past = tri2(tk)

    def cond(c):
        return jnp.logical_and(c[0] < n_loop, c[1] > 0)

    def body(c):
        jj, _, z_cur, run, accs = c
        z_after = past_scores(jj + 1)
        run, accs = weigh(z_cur, vp_ref[pl.ds(past_start(jj), tk), :], (run, accs), None, tri_past)
        return jj + 1, alive(run), z_after, run, accs

    out = lax.while_loop(cond, body, (jnp.int32(0), alive(run), z_next, run, accs))
    _store_heads(o_ref, out[4])


def _sb_attn(q, k_own, v_own, k_past, v_past, *, tq, tk, n_own, causal_tiles):
    b, lq, _ = q.shape
    nqt = lq // tq
    tko = k_own[0].shape[2] // nqt
    lp = k_past[0].shape[2]
    assert lq % tq == 0 and lp % tk == 0
    return pl.pallas_call(
        functools.partial(_sb_kernel, tq=tq, tk=tk, n_own=n_own,
                          n_past=None if causal_tiles else lp // tk),
        grid=(b, nqt),
        in_specs=[pl.BlockSpec((None, tq, BRANCH_W), lambda i, t: (i, t, 0)),
                  _tile_spec(k_own, tko), _tile_spec(v_own, tko),
                  _full_spec(k_past), _full_spec(v_past)],
        out_specs=pl.BlockSpec((None, tq, BRANCH_W), lambda i, t: (i, t, 0)),
        out_shape=jax.ShapeDtypeStruct((b, lq, BRANCH_W), BF16),
        compiler_params=_params(("parallel", "arbitrary")),
        name="sb_attn",
    )(q, k_own[0], v_own[0], k_past[0], v_past[0])


def _sb_kernel_t(qt_ref, ko_ref, vto_ref, kp_ref, vtp_ref, o_ref, *, tq, tk):
    qi = pl.program_id(1)
    qt = qt_ref[...]
    feat = lax.broadcasted_iota(jnp.int32, qt.shape, 0)
    qmt = jnp.concatenate(
        [jnp.where(jnp.logical_and(feat >= h * D_HEAD, feat < (h + 1) * D_HEAD), qt, jnp.zeros_like(qt))
         for h in range(N_HEADS)], axis=1)
    cols = N_HEADS * tq

    def tri2(n):
        r = lax.broadcasted_iota(jnp.int32, (n, 2 * n), 0)
        c = lax.broadcasted_iota(jnp.int32, (n, 2 * n), 1) % n
        return jnp.where(c > r, 1.0, 0.0).astype(BF16)

    def weigh(zt, vt, carry, mask, tri):
        run, accs = carry
        neg_abs = lax.bitcast_convert_type(
            lax.bitcast_convert_type(zt, jnp.uint32) | jnp.uint32(0x80000000), F32)
        t = jnp.log2(1.0 + jnp.exp2(neg_abs))
        log_beta = jnp.minimum(zt, 0.0) - t
        log_stay = log_beta - zt
        if mask is not None:
            log_stay = jnp.where(mask, log_stay, 0.0)
        hi = log_stay.astype(BF16)
        lo = (log_stay - hi.astype(F32)).astype(BF16)
        later = _dot(tri, jnp.concatenate([hi, lo], axis=0)) + run
        w = jnp.exp2(log_beta + later)
        if mask is not None:
            w = jnp.where(mask, w, 0.0)
        wb = w.astype(BF16)
        accs = tuple(accs[h] + _dot(vt[_head_slice(h), :], wb[:, h * tq:(h + 1) * tq])
                     for h in range(N_HEADS))
        return run + jnp.sum(log_stay, axis=0, keepdims=True), accs

    def past_start(jj):
        return pl.multiple_of(jnp.clip(qi - 1 - jj, 0, kp_ref.shape[0] // tk - 1) * tk, tk)

    def past_scores(jj):
        return _dot(kp_ref[pl.ds(past_start(jj), tk), :].astype(BF16), qmt)

    def alive(run):
        return (jnp.max(run) > SB_DEAD_LOG2).astype(jnp.int32)

    key = lax.broadcasted_iota(jnp.int32, (tq, cols), 0)
    qry = lax.broadcasted_iota(jnp.int32, (tq, cols), 1) % tq
    own_mask = key < qry
    z_next = past_scores(0)
    carry = (jnp.zeros((1, cols), F32), tuple(jnp.zeros((D_HEAD, tq), F32) for _ in range(N_HEADS)))
    run, accs = weigh(_dot(ko_ref[...].astype(BF16), qmt), vto_ref[...], carry, own_mask, tri2(tq))
    tri_past = tri2(tk)

    def cond(c):
        return jnp.logical_and(c[0] < qi, c[1] > 0)

    def body(c):
        jj, _, z_cur, run, accs = c
        z_after = past_scores(jj + 1)
        run, accs = weigh(z_cur, vtp_ref[:, pl.ds(past_start(jj), tk)], (run, accs), None, tri_past)
        return jj + 1, alive(run), z_after, run, accs

    out = lax.while_loop(cond, body, (jnp.int32(0), alive(run), z_next, run, accs))
    o_ref[...] = jnp.concatenate(out[4], axis=0).T.astype(o_ref.dtype)


def _sb_attn_t(qt, k, vt, *, tq):
    b, _, length = qt.shape
    assert length % tq == 0
    return pl.pallas_call(
        functools.partial(_sb_kernel_t, tq=tq, tk=tq),
        grid=(b, length // tq),
        in_specs=[pl.BlockSpec((None, BRANCH_W, tq), lambda i, t: (i, 0, t)),
                  _tile_spec(k, tq),
                  pl.BlockSpec((None, BRANCH_W, tq), lambda i, t: (i, 0, t)),
                  _full_spec(k),
                  pl.BlockSpec((None, BRANCH_W, length), lambda i, t: (i, 0, 0))],
        out_specs=pl.BlockSpec((None, tq, BRANCH_W), lambda i, t: (i, t, 0)),
        out_shape=jax.ShapeDtypeStruct((b, length, BRANCH_W), BF16),
        compiler_params=_params(("parallel", "arbitrary")),
        name="sb_attn_t",
    )(qt, k[0], vt, k[0], vt)


def _ret_kernel(q_ref, k_ref, kd_ref, v_ref, rg_ref, s0_ref, dec_ref, qdec_ref, gl_ref,
                gng_ref, gnb_ref, o_ref, sout_ref, state_ref, *, lc, nseq):
    c = pl.program_id(1)

    @pl.when(c == 0)
    def _():
        state_ref[...] = s0_ref[...]

    for s in range(nseq):
        qm = _stack_heads(q_ref[s], _own_lanes)
        v = v_ref[s]
        state = state_ref[s]
        scores = (_nt_dot(qm, k_ref[s]) * dec_ref[...]).astype(BF16)
        cross = _dot(qm, state.astype(BF16)) * qdec_ref[...]
        kv_full = _tn_dot(kd_ref[s], v)
        new_state = gl_ref[...] * state + jnp.concatenate(
            [kv_full[_head_slice(h), _head_slice(h)] for h in range(N_HEADS)], axis=0)
        state_ref[s] = new_state

        rg = rg_ref[s]
        for h in range(N_HEADS):
            o = _dot(scores[h * lc:(h + 1) * lc], v[:, _head_slice(h)]) + cross[h * lc:(h + 1) * lc]
            mu = jnp.mean(o, axis=-1, keepdims=True)
            d = o - mu
            var = jnp.mean(d * d, axis=-1, keepdims=True)
            y = d * lax.rsqrt(var + EPS) * gng_ref[h] + gnb_ref[h]
            g = rg[:, _head_slice(h)]
            o_ref[s, :, _head_slice(h)] = (y * (g * _sigmoid(g))).astype(o_ref.dtype)

    @pl.when(c == pl.num_programs(1) - 1)
    def _():
        sout_ref[...] = state_ref[...]


def _retention(q, k, kd, v, rg, s0, dec, qdec, gl, gng, gnb, *, lc, nseq):
    b, length, _ = q.shape
    assert length % lc == 0 and b % nseq == 0
    seq = lambda i, t: (i, t, 0)
    st = lambda i, t: (i, 0, 0)
    c2 = lambda i, t: (0, 0)
    c3 = lambda i, t: (0, 0, 0)
    return pl.pallas_call(
        functools.partial(_ret_kernel, lc=lc, nseq=nseq),
        grid=(b // nseq, length // lc),
        in_specs=[pl.BlockSpec((nseq, lc, BRANCH_W), seq)] * 5
        + [pl.BlockSpec((nseq, BRANCH_W, D_HEAD), st),
           pl.BlockSpec(dec.shape, c2), pl.BlockSpec(qdec.shape, c2), pl.BlockSpec(gl.shape, c2),
           pl.BlockSpec(gng.shape, c3), pl.BlockSpec(gnb.shape, c3)],
        out_specs=[pl.BlockSpec((nseq, lc, BRANCH_W), seq),
                   pl.BlockSpec((nseq, BRANCH_W, D_HEAD), st)],
        out_shape=[jax.ShapeDtypeStruct((b, length, BRANCH_W), BF16),
                   jax.ShapeDtypeStruct((b, BRANCH_W, D_HEAD), F32)],
        scratch_shapes=[pltpu.VMEM((nseq, BRANCH_W, D_HEAD), F32)],
        compiler_params=_params(("parallel", "arbitrary")),
        name="retention",
    )(q, k, kd, v, rg, s0, dec, qdec, gl, gng, gnb)


def _band_kernel(q_ref, k_ref, v_ref, bias_ref, o_ref, *, tq, win, back):
    qi = pl.program_id(1)
    start = pl.multiple_of(jnp.maximum(qi - back, 0) * tq, tq)
    k = k_ref[pl.ds(start, win), :].astype(BF16)
    v = v_ref[pl.ds(start, win), :].astype(BF16)
    qm = _stack_heads(q_ref[...], _own_lanes)
    s = _nt_dot(qm, k) + bias_ref[...]
    pb = jnp.exp2(s - jnp.max(s, axis=1, keepdims=True)).astype(BF16)
    ones = jnp.ones((win, D_HEAD), BF16)
    outs = []
    for h in range(N_HEADS):
        o = _dot(pb[h * tq:(h + 1) * tq], jnp.concatenate([v[:, _head_slice(h)], ones], axis=1))
        outs.append(o[:, :D_HEAD] / o[:, D_HEAD:D_HEAD + 1])
    _store_heads(o_ref, outs)


def _band_attn(q, k, v, bias, *, tq, win, back):
    b, lq, _ = q.shape
    nvar = bias.shape[0]
    assert lq % tq == 0
    return pl.pallas_call(
        functools.partial(_band_kernel, tq=tq, win=win, back=back),
        grid=(b, lq // tq),
        in_specs=[pl.BlockSpec((None, tq, BRANCH_W), lambda i, t: (i, t, 0)),
                  _full_spec(k), _full_spec(v),
                  pl.BlockSpec((None, N_HEADS * tq, win), lambda i, t: (jnp.minimum(t, nvar - 1), 0, 0))],
        out_specs=pl.BlockSpec((None, tq, BRANCH_W), lambda i, t: (i, t, 0)),
        out_shape=jax.ShapeDtypeStruct((b, lq, BRANCH_W), BF16),
        compiler_params=_params(("parallel", "arbitrary")),
        name="band_attn",
    )(q, k[0], v[0], bias)


def _band_kernel_t(qt_ref, k_ref, vt_ref, bias_ref, o_ref, *, tq, win, back):
    qi = pl.program_id(1)
    start = pl.multiple_of(jnp.maximum(qi - back, 0) * tq, tq)
    k = k_ref[pl.ds(start, win), :].astype(BF16)
    vt = vt_ref[:, pl.ds(start, win)]
    qt = qt_ref[...]
    feat = lax.broadcasted_iota(jnp.int32, qt.shape, 0)
    qmt = jnp.concatenate(
        [jnp.where(jnp.logical_and(feat >= h * D_HEAD, feat < (h + 1) * D_HEAD), qt, jnp.zeros_like(qt))
         for h in range(N_HEADS)], axis=1)
    st = _dot(k, qmt) + bias_ref[...]
    pt = jnp.exp2(st - jnp.max(st, axis=0, keepdims=True))
    l = jnp.sum(pt, axis=0, keepdims=True)
    pb = pt.astype(BF16)
    out_t = jnp.concatenate(
        [_dot(vt[_head_slice(h), :], pb[:, h * tq:(h + 1) * tq]) / l[:, h * tq:(h + 1) * tq]
         for h in range(N_HEADS)], axis=0)
    o_ref[...] = out_t.T.astype(o_ref.dtype)


def _band_attn_t(qt, k, vt, bias_t, *, tq, win, back):
    b, _, length = qt.shape
    nvar = bias_t.shape[0]
    assert length % tq == 0
    return pl.pallas_call(
        functools.partial(_band_kernel_t, tq=tq, win=win, back=back),
        grid=(b, length // tq),
        in_specs=[pl.BlockSpec((None, BRANCH_W, tq), lambda i, t: (i, 0, t)),
                  _full_spec(k),
                  pl.BlockSpec((None, BRANCH_W, length), lambda i, t: (i, 0, 0)),
                  pl.BlockSpec((None, win, N_HEADS * tq), lambda i, t: (jnp.minimum(t, nvar - 1), 0, 0))],
        out_specs=pl.BlockSpec((None, tq, BRANCH_W), lambda i, t: (i, t, 0)),
        out_shape=jax.ShapeDtypeStruct((b, length, BRANCH_W), BF16),
        compiler_params=_params(("parallel", "arbitrary")),
        name="band_attn_t",
    )(qt, k[0], vt, bias_t)


def _merge_kernel(x_ref, ba_ref, bb_ref, bc_ref, bd_ref, wg_ref, wb_ref, wo_ref, g_ref, b_ref, o_ref):
    x = x_ref[...]
    xb = x.astype(BF16)
    merged = None
    for n, br_ref in enumerate((ba_ref, bb_ref, bc_ref, bd_ref)):
        logits = _dot(xb, wg_ref[:, n * D_MODEL:(n + 1) * D_MODEL])
        term = _dot(br_ref[...], wb_ref[n]) * _sigmoid(logits)
        merged = term if merged is None else merged + term
    mix = _dot(merged.astype(BF16), wo_ref[...])
    o_ref[...] = _layer_norm(ALPHA * x + mix, g_ref[...], b_ref[...])


def _merge(x, branches, wg, wb, wo, g, b, *, tm):
    m = x.shape[0]
    tm = min(tm, m)
    assert m % tm == 0
    const2 = lambda i: (0, 0)
    row = lambda i: (i, 0)
    return pl.pallas_call(
        _merge_kernel,
        grid=(m // tm,),
        in_specs=[pl.BlockSpec((tm, D_MODEL), row)]
        + [pl.BlockSpec((tm, BRANCH_W), row)] * N_BRANCH
        + [pl.BlockSpec((D_MODEL, N_BRANCH * D_MODEL), const2),
           pl.BlockSpec((N_BRANCH, BRANCH_W, D_MODEL), lambda i: (0, 0, 0)),
           pl.BlockSpec((D_MODEL, D_MODEL), const2),
           pl.BlockSpec((1, D_MODEL), const2),
           pl.BlockSpec((1, D_MODEL), const2)],
        out_specs=pl.BlockSpec((tm, D_MODEL), row),
        out_shape=jax.ShapeDtypeStruct((m, D_MODEL), F32),
        compiler_params=_params(("parallel",)),
        name="merge",
    )(x, *branches, wg, wb, wo, g, b)


def _route(aff_t, sel_t):
    def top2_sum(a, b, c, d):
        hi1, lo1 = jnp.maximum(a, b), jnp.minimum(a, b)
        hi2, lo2 = jnp.maximum(c, d), jnp.minimum(c, d)
        return jnp.maximum(hi1, hi2) + jnp.maximum(jnp.minimum(hi1, hi2), jnp.maximum(lo1, lo2))

    score = [top2_sum(*sel_t[g * EXPERTS_PER_GROUP:(g + 1) * EXPERTS_PER_GROUP])
             for g in range(N_GROUPS)]
    best_here = []
    for g in range(N_GROUPS):
        ok = None
        for o in range(N_GROUPS):
            if o == g:
                continue
            c = (score[g] > score[o]) if o < g else (score[g] >= score[o])
            ok = c if ok is None else jnp.logical_and(ok, c)
        best_here.append(ok)
    picked = []
    for e in range(N_EXPERTS):
        g = e // EXPERTS_PER_GROUP
        rank = jnp.zeros_like(sel_t[e])
        for o in range(g * EXPERTS_PER_GROUP, (g + 1) * EXPERTS_PER_GROUP):
            if o == e:
                continue
            ahead = (sel_t[o] >= sel_t[e]) if o < e else (sel_t[o] > sel_t[e])
            rank = rank + jnp.where(ahead, 1.0, 0.0)
        picked.append(jnp.where(jnp.logical_and(best_here[g], rank < TOP_K), aff_t[e], 0.0))
    total = picked[0]
    for e in range(1, N_EXPERTS):
        total = total + picked[e]
    return [p / total for p in picked]


def _moe_kernel(x_ref, wrh_ref, wrl_ref, br_ref, wgu_ref, wd_ref, g_ref, b_ref, o_ref, acc_ref):
    x = x_ref[...]
    xh = x.astype(BF16)
    xl = (x - xh.astype(F32)).astype(BF16)
    wrh = wrh_ref[...]
    logits = _dot(xh, wrh) + _dot(xl, wrh) + _dot(xh, wrl_ref[...])
    aff = _sigmoid(logits).T
    bias = br_ref[...]
    aff_t = [aff[e:e + 1, :] for e in range(N_EXPERTS)]
    sel_t = [aff_t[e] + bias[e:e + 1, :] for e in range(N_EXPERTS)]
    gate_rows = _route(aff_t, sel_t)
    tm = x.shape[0]
    gate_t = jnp.concatenate(gate_rows + [jnp.zeros((LANE - N_EXPERTS, tm), F32)], axis=0)
    gate = gate_t.T

    for e in range(N_EXPERTS):
        gu = _dot(xh, wgu_ref[e])
        gt, up = gu[:, :D_EXPERT], gu[:, D_EXPERT:]
        hmid = gt * _sigmoid(gt) * up
        y = _dot(hmid.astype(BF16), wd_ref[e]) * gate[:, e:e + 1]
        if e == 0:
            acc_ref[...] = y
        else:
            acc_ref[...] += y
    o_ref[...] = _layer_norm(ALPHA * x + acc_ref[...], g_ref[...], b_ref[...])


def _moe(x, wrh, wrl, br, wgu, wd, g, b, *, tm):
    m = x.shape[0]
    tm = min(tm, m)
    assert m % tm == 0
    const2 = lambda i: (0, 0)
    const3 = lambda i: (0, 0, 0)
    return pl.pallas_call(
        _moe_kernel,
        grid=(m // tm,),
        in_specs=[pl.BlockSpec((tm, D_MODEL), lambda i: (i, 0)),
                  pl.BlockSpec((D_MODEL, LANE), const2),
                  pl.BlockSpec((D_MODEL, LANE), const2),
                  pl.BlockSpec((LANE, 1), const2),
                  pl.BlockSpec((N_EXPERTS, D_MODEL, 2 * D_EXPERT), const3, pipeline_mode=pl.Buffered(1)),
                  pl.BlockSpec((N_EXPERTS, D_EXPERT, D_MODEL), const3, pipeline_mode=pl.Buffered(1)),
                  pl.BlockSpec((1, D_MODEL), const2),
                  pl.BlockSpec((1, D_MODEL), const2)],
        out_specs=pl.BlockSpec((tm, D_MODEL), lambda i: (i, 0)),
        out_shape=jax.ShapeDtypeStruct((m, D_MODEL), F32),
        scratch_shapes=[pltpu.VMEM((tm, D_MODEL), F32)],
        compiler_params=_params(("parallel",)),
        name="moe",
    )(x, wrh, wrl, br, wgu, wd, g, b)


def _rope_tables(pos, d):
    half = d // 2
    inv = jnp.power(ROPE_BASE, -jnp.arange(half, dtype=F32) / half)
    ang = pos.astype(F32)[:, None] * inv[None, :]
    cos, sin = jnp.cos(ang), jnp.sin(ang)
    zero = jnp.zeros_like(sin)
    rep = LANE // d
    cos_t = jnp.tile(jnp.concatenate([cos, cos], axis=1), (1, rep))
    sin_a = jnp.tile(jnp.concatenate([-sin, zero], axis=1), (1, rep))
    sin_b = jnp.tile(jnp.concatenate([zero, sin], axis=1), (1, rep))
    return cos_t, sin_a, sin_b


def _retention_tables(lc):
    log_g = jnp.log1p(-jnp.exp2(-5.0 - jnp.arange(N_HEADS, dtype=F32)))
    i = jnp.arange(lc, dtype=F32)
    diff = i[:, None] - i[None, :]
    dec = jnp.where(diff >= 0, jnp.exp(jnp.maximum(diff, 0.0)[None] * log_g[:, None, None]), 0.0)
    qdec = jnp.exp((i[None, :] + 1.0) * log_g[:, None])
    kdec = jnp.exp((lc - 1.0 - i)[None, :] * log_g[:, None])
    gl = jnp.exp(lc * log_g)
    dec = dec.reshape(N_HEADS * lc, lc)
    qdec = jnp.broadcast_to(qdec[:, :, None], (N_HEADS, lc, D_HEAD)).reshape(N_HEADS * lc, D_HEAD)
    gl = jnp.broadcast_to(gl[:, None, None], (N_HEADS, D_HEAD, D_HEAD)).reshape(BRANCH_W, D_HEAD)
    kdec = jnp.repeat(kdec.T, D_HEAD, axis=1)
    return dec, qdec, kdec, gl


def _band_bias(rel_bias, tq, win, q_minus_k0, valid):
    length = tq + win - 1
    d = np.arange(length) - (tq - 1) - q_minus_k0
    idx = np.clip(d, -REL_CLIP, REL_CLIP) + REL_CLIP
    g = rel_bias[:, idx].astype(F32) * LOG2E
    gp = jnp.concatenate([g, jnp.zeros((N_HEADS, 1), F32)], axis=1)
    m = jnp.tile(gp, (1, tq))[:, :tq * length].reshape(N_HEADS, tq, length)
    tile = m[:, :, tq - 1:tq - 1 + win]
    return jnp.where(valid[None], tile, NEG_INF).reshape(N_HEADS * tq, win)


def _pack_w_in(w_in_l):
    cols = []
    src = 0
    for w in _IN_WIDTH:
        seg = w_in_l[:, src:src + w]
        cols.append(jnp.pad(seg, ((0, 0), (0, _round_up(w, LANE) - w))))
        src += w
    return jnp.concatenate(cols, axis=1).astype(BF16)


def _pad_rows(t, n):
    return jnp.pad(t, ((0, 0), (0, n - t.shape[1]), (0, 0)))


def _token_mixers(x, pos0, past, lw, *, prompt, layer, stacked):
    (w_in_p, qn, wuq, kvn, wukv, gn_g, gn_b, rel_bias, b, length) = lw
    assert pos0 % CHUNK == 0
    pos = pos0 + jnp.arange(length)
    lc = 256 if prompt else length
    dec, qdec, kdec, gl = _retention_tables(lc)
    tables = list(_rope_tables(pos, DR_A)) + list(_rope_tables(pos, D_HEAD)) + [jnp.tile(kdec, (length // lc, 1))]
    if not prompt:
        tables = [jnp.tile(t, (b, 1)) for t in tables]
    outs = _inproj(x, w_in_p, qn, wuq, kvn, wukv, tables, tm=TOKEN_TILE if prompt else b * length,
                   layer=layer, stacked=stacked, seq_len=length if prompt else None)
    stacked = tuple(outs[k] for k in _STATE_OUTS)
    per_batch = lambda o: o.reshape(o.shape[:-2] + (b, length, o.shape[-1]))
    (q_a, ckv, kpe, kf, v_a, rq, rk, rkd, rv, rg, sq, sk, sv, bq, bk, bv) = [
        per_batch(o) for o in outs[:len(_INPROJ_OUT)]]
    here = lambda t: (t, layer)
    only = lambda t: (t[None], 0)

    if prompt:
        tq = 256
        qa_t, va_t, sq_t, sv_t, bq_t, bv_t = outs[len(_INPROJ_OUT):]
        o_a = _mla_attn_t(qa_t, kf, va_t, tq=tq)
        o_c = _sb_attn_t(sq_t, here(sk), sv_t, tq=tq)
        s0 = jnp.zeros((b, BRANCH_W, D_HEAD), F32)
        win = 3 * tq
        i = np.arange(tq)[:, None]
        c = np.arange(win)[None, :]
        variants = []
        for t in range(3):
            qc, kc = i // CHUNK + t * (tq // CHUNK), c // CHUNK
            variants.append(_band_bias(rel_bias, tq, win, t * tq, (kc <= qc) & (kc >= qc - PREV_CHUNKS)))
        bias_t = jnp.stack(variants).transpose(0, 2, 1)
        o_d = _band_attn_t(bq_t, here(bk), bv_t, bias_t, tq=tq, win=win, back=2)
    else:
        c_ckv, c_kpe, s_prev, c_sk, c_sv, c_bk, c_bv = past
        n_past = c_ckv.shape[2]
        tko = LANE
        kf_c, v_c = _expand_latent(c_ckv[layer].reshape(b * n_past, KV_RANK),
                                   jnp.pad(c_kpe[layer].reshape(b * n_past, DR_A), ((0, 0), (0, LANE - DR_A))),
                                   wukv, 512)
        o_a = _mla_attn(q_a, only(_pad_rows(kf, tko)), only(_pad_rows(v_a, tko)),
                        only(kf_c.reshape(b, n_past, QA_W)), only(v_c.reshape(b, n_past, BRANCH_W)),
                        tq=length, tk=256, n_own=length, causal_tiles=False)
        o_c = _sb_attn(sq, only(_pad_rows(sk[layer], tko)), only(_pad_rows(sv[layer], tko)),
                       (c_sk.reshape(DEPTH, b, n_past, BRANCH_W), layer),
                       (c_sv.reshape(DEPTH, b, n_past, BRANCH_W), layer),
                       tq=length, tk=256, n_own=length, causal_tiles=False)
        s0 = s_prev[layer].reshape(b, BRANCH_W, D_HEAD)
        n_band = c_bk.shape[2]
        n_keys = n_band + length
        win = _round_up(n_keys, LANE)
        bk_all = _pad_rows(jnp.concatenate([c_bk[layer].reshape(b, n_band, BRANCH_W), bk[layer]], axis=1), win)
        bv_all = _pad_rows(jnp.concatenate([c_bv[layer].reshape(b, n_band, BRANCH_W), bv[layer]], axis=1), win)
        k_pos = pos0 - n_band + np.arange(win)
        q_pos = pos0 + np.arange(length)
        qc, kc = q_pos[:, None] // CHUNK, k_pos[None, :] // CHUNK
        valid = (np.arange(win)[None, :] < n_keys) & (k_pos[None, :] >= 0) & (kc <= qc) & (kc >= qc - PREV_CHUNKS)
        bias = _band_bias(rel_bias, length, win, n_band, valid)[None]
        o_d = _band_attn(bq, only(bk_all), only(bv_all), bias, tq=length, win=win, back=0)

    o_r, s_ret = _retention(rq, rk, rkd, rv, rg, s0, dec, qdec, gl,
                            gn_g.reshape(N_HEADS, 1, D_HEAD), gn_b.reshape(N_HEADS, 1, D_HEAD), lc=lc,
                            nseq=RET_SEQS_PER_STEP if b % RET_SEQS_PER_STEP == 0 else 1)
    s_ret = s_ret.reshape(b, N_HEADS, D_HEAD, D_HEAD)
    flat = lambda t: t.reshape(b * length, BRANCH_W)
    return (flat(o_a), flat(o_r), flat(o_c), flat(o_d)), s_ret, stacked


def _state_outputs(stacked, s_ret, b, length, keep):
    ckv, kpe, sk, sv, bk, bv = [t.reshape(DEPTH, b, length, t.shape[-1]) for t in stacked]
    heads4 = lambda t: t.reshape(DEPTH, b, length, N_HEADS, D_HEAD)
    return (ckv, kpe, jnp.stack(s_ret, axis=0), heads4(sk), heads4(sv),
            heads4(bk)[:, :, length - keep:], heads4(bv)[:, :, length - keep:])


def kernel(x_prompt, x_sample, cache_mla_ckv, cache_mla_kpe, state_ret, cache_sb_k, cache_sb_v, cache_band_k, cache_band_v, w_in, mla_q_norm, mla_w_uq, mla_kv_norm, mla_w_ukv, ret_gn_g, ret_gn_b, band_rel_bias, w_branch, w_o, ln1_g, ln1_b, w_router, b_router, w_exp_gate, w_exp_up, w_exp_down, ln2_g, ln2_b):
    bp, lp, _ = x_prompt.shape
    bs, ls, _ = x_sample.shape
    past_len = cache_mla_ckv.shape[2]
    xp = x_prompt.reshape(bp * lp, D_MODEL)
    xs = x_sample.reshape(bs * ls, D_MODEL)

    wr = jnp.pad(w_router, ((0, 0), (0, LANE - N_EXPERTS)))
    wrh = wr.astype(BF16)
    wrl = (wr - wrh.astype(F32)).astype(BF16)
    br = jnp.pad(b_router, (0, LANE - N_EXPERTS)).reshape(LANE, 1)

    past = (cache_mla_ckv, cache_mla_kpe, state_ret, cache_sb_k, cache_sb_v, cache_band_k, cache_band_v)
    ret_p, ret_s = [], []
    stacked_p = stacked_s = None
    for l in range(DEPTH):
        wuq = mla_w_uq[l]
        wuq = jnp.concatenate([wuq[:, :, :DN_A].reshape(Q_RANK, -1), wuq[:, :, DN_A:].reshape(Q_RANK, -1)], axis=1)
        wuq = jnp.pad(wuq, ((0, _round_up(Q_RANK, LANE) - Q_RANK), (0, 0))).astype(BF16)
        wukv = mla_w_ukv[l]
        wukv = jnp.concatenate([wukv[:, :, :DN_A].reshape(KV_RANK, -1), wukv[:, :, DN_A:].reshape(KV_RANK, -1)],
                               axis=1).astype(BF16)
        qn = jnp.pad(mla_q_norm[l], (0, _round_up(Q_RANK, LANE) - Q_RANK)).reshape(1, -1)
        kvn = mla_kv_norm[l].reshape(1, KV_RANK)
        w_in_p = _pack_w_in(w_in[l])
        wg = w_in[l][:, GATE_COL0:].astype(BF16)
        wb = w_branch[l].astype(BF16)
        wo = w_o[l].astype(BF16)
        g1, b1 = ln1_g[l].reshape(1, D_MODEL), ln1_b[l].reshape(1, D_MODEL)
        g2, b2 = ln2_g[l].reshape(1, D_MODEL), ln2_b[l].reshape(1, D_MODEL)
        wgu = jnp.concatenate([w_exp_gate[l], w_exp_up[l]], axis=-1).astype(BF16)
        wd = w_exp_down[l].astype(BF16)
        lw = (w_in_p, qn, wuq, kvn, wukv, ret_gn_g[l], ret_gn_b[l], band_rel_bias[l])

        br_p, s_ret_p, stacked_p = _token_mixers(xp, 0, None, lw + (bp, lp), prompt=True,
                                                 layer=l, stacked=stacked_p)
        br_s, s_ret_s, stacked_s = _token_mixers(xs, past_len, past, lw + (bs, ls), prompt=False,
                                                 layer=l, stacked=stacked_s)
        xp = _merge(xp, br_p, wg, wb, wo, g1, b1, tm=TOKEN_TILE)
        xs = _merge(xs, br_s, wg, wb, wo, g1, b1, tm=TOKEN_TILE)
        xp = _moe(xp, wrh, wrl, br, wgu, wd, g2, b2, tm=TOKEN_TILE)
        xs = _moe(xs, wrh, wrl, br, wgu, wd, g2, b2, tm=TOKEN_TILE)
        ret_p.append(s_ret_p)
        ret_s.append(s_ret_s)

    return ((xp.reshape(bp, lp, D_MODEL), xs.reshape(bs, ls, D_MODEL))
            + _state_outputs(stacked_p, ret_p, bp, lp, min(PREV_CHUNKS * CHUNK, lp))
            + _state_outputs(stacked_s, ret_s, bs, ls, ls))
```

```python
import functools

import jax
import jax.numpy as jnp
import numpy as np
from jax import lax
from jax.experimental import pallas as pl
from jax.experimental.pallas import tpu as pltpu

D_MODEL = 1024
DEPTH = 2
CHUNK = 64
N_BRANCH = 4
BRANCH_W = D_MODEL // 4
N_HEADS = 4
D_HEAD = BRANCH_W // N_HEADS
DN_A = 64
DR_A = 32
DQK_A = DN_A + DR_A
Q_RANK = (3 * D_MODEL) // 16
KV_RANK = D_MODEL // 8
PREV_CHUNKS = 8
REL_CLIP = 128
ROPE_BASE = 10000.0
N_EXPERTS = 16
N_GROUPS = 4
EXPERTS_PER_GROUP = N_EXPERTS // N_GROUPS
TOP_K = 2
D_EXPERT = D_MODEL // 4
ALPHA = (2.0 * DEPTH) ** 0.25
EPS = 1e-5
NEG_INF = -1e30
LOG2E = 1.4426950408889634
SB_DEAD_LOG2 = -150.0

F32 = jnp.float32
BF16 = jnp.bfloat16

V7X_VMEM_LIMIT = 56 * 1024 * 1024
LANE = 128
TOKEN_TILE = 512
RET_SEQS_PER_STEP = 2

_IN_NAMES = ("c_q", "c_kv", "k_pe", "rq", "rk", "rv", "rg", "sq", "sk", "sv", "bq", "bk", "bv")
_IN_WIDTH = (Q_RANK, KV_RANK, DR_A) + (BRANCH_W,) * 10
QA_W = N_HEADS * DN_A + N_HEADS * DR_A


def _round_up(n, m):
    return (n + m - 1) // m * m


_IN_OFF = {}
_off = 0
for _n, _w in zip(_IN_NAMES, _IN_WIDTH):
    _IN_OFF[_n] = (_off, _round_up(_w, LANE))
    _off += _round_up(_w, LANE)
IN_PACKED = _off
GATE_COL0 = sum(_IN_WIDTH)


def _params(sem):
    return pltpu.CompilerParams(dimension_semantics=sem, vmem_limit_bytes=V7X_VMEM_LIMIT)


def _nt_dot(a, b):
    return lax.dot_general(a, b, (((1,), (1,)), ((), ())), preferred_element_type=F32)


def _tn_dot(a, b):
    return lax.dot_general(a, b, (((0,), (0,)), ((), ())), preferred_element_type=F32)


def _dot(a, b):
    return jnp.dot(a, b, preferred_element_type=F32)


def _layer_norm(v, g, b):
    mu = jnp.mean(v, axis=-1, keepdims=True)
    d = v - mu
    var = jnp.mean(d * d, axis=-1, keepdims=True)
    return d * lax.rsqrt(var + EPS) * g + b


def _sigmoid(v):
    return 0.5 * jnp.tanh(0.5 * v) + 0.5


def _head_slice(h):
    return slice(h * D_HEAD, (h + 1) * D_HEAD)


def _stack_heads(q, lane_sets):
    lane = lax.broadcasted_iota(jnp.int32, q.shape, 1)
    zero = jnp.zeros_like(q)
    parts = []
    for h in range(N_HEADS):
        keep = None
        for lo, hi in lane_sets(h):
            m = jnp.logical_and(lane >= lo, lane < hi)
            keep = m if keep is None else jnp.logical_or(keep, m)
        parts.append(jnp.where(keep, q, zero))
    return jnp.concatenate(parts, axis=0)


def _own_lanes(h):
    return ((h * D_HEAD, (h + 1) * D_HEAD),)


def _mla_lanes(h):
    base = N_HEADS * DN_A
    return ((h * DN_A, (h + 1) * DN_A), (base + h * DR_A, base + (h + 1) * DR_A))


def _store_heads(o_ref, parts):
    for h, p in enumerate(parts):
        o_ref[:, _head_slice(h)] = p.astype(o_ref.dtype)


def _expand_kernel(ckv_ref, kpe_ref, w_ref, kf_ref, v_ref):
    kvx = _dot(ckv_ref[...].astype(BF16), w_ref[...])
    kp = kpe_ref[...].astype(BF16)
    kf_ref[:, :N_HEADS * DN_A] = kvx[:, :N_HEADS * DN_A].astype(BF16)
    kf_ref[:, N_HEADS * DN_A:] = jnp.concatenate([kp] * N_HEADS, axis=1)
    v_ref[...] = kvx[:, N_HEADS * DN_A:].astype(BF16)


def _expand_latent(ckv, kpe, w_ukv, layer, tm):
    m = ckv.shape[1]
    assert m % tm == 0
    return pl.pallas_call(
        _expand_kernel,
        grid=(m // tm,),
        in_specs=[pl.BlockSpec((None, tm, KV_RANK), lambda i: (layer, i, 0)),
                  pl.BlockSpec((None, tm, DR_A), lambda i: (layer, i, 0)),
                  pl.BlockSpec((KV_RANK, 2 * BRANCH_W), lambda i: (0, 0))],
        out_specs=[pl.BlockSpec((tm, QA_W), lambda i: (i, 0)),
                   pl.BlockSpec((tm, BRANCH_W), lambda i: (i, 0))],
        out_shape=[jax.ShapeDtypeStruct((m, QA_W), BF16),
                   jax.ShapeDtypeStruct((m, BRANCH_W), BF16)],
        compiler_params=_params(("parallel",)),
        name="expand_latent",
    )(ckv, kpe, w_ukv)


def _rope_block(x, cos, sin_a, sin_b, half):
    return x * cos + pltpu.roll(x, LANE - half, 1) * sin_a + pltpu.roll(x, half, 1) * sin_b


def _inproj_kernel(x_ref, w_ref, qn_ref, wuq_ref, kvn_ref, wukv_ref,
                   c32_ref, a32_ref, b32_ref, c64_ref, a64_ref, b64_ref, kdec_ref,
                   qa_ref, ckv_ref, kpe_ref, kf_ref, va_ref,
                   rq_ref, rk_ref, rkd_ref, rv_ref, rg_ref,
                   sq_ref, sk_ref, sv_ref, bq_ref, bk_ref, bv_ref, t_refs=(), keep_refs=()):
    qat_ref, vat_ref, sqt_ref, svt_ref, bqt_ref, bvt_ref = t_refs if t_refs else (None,) * 6
    z = _dot(x_ref[...].astype(BF16), w_ref[...])

    def seg(name):
        o, w = _IN_OFF[name]
        return z[:, o:o + w]

    cq = seg("c_q")
    cqn = cq * lax.rsqrt(jnp.sum(cq * cq, axis=1, keepdims=True) * (1.0 / Q_RANK) + EPS) * qn_ref[...]
    qa = _dot(cqn.astype(BF16), wuq_ref[...])
    scale_a = DQK_A ** -0.5 * LOG2E
    nope_w = N_HEADS * DN_A
    q_pe = _rope_block(qa[:, nope_w:], c32_ref[...], a32_ref[...], b32_ref[...], DR_A // 2)
    qa_s = jnp.concatenate([qa[:, :nope_w], q_pe], axis=1) * scale_a
    qa_ref[...] = qa_s.astype(BF16)
    if qat_ref is not None:
        qat_ref[...] = qa_s.T.astype(BF16)

    ckv_raw = seg("c_kv")
    ckv = ckv_raw * lax.rsqrt(jnp.mean(ckv_raw * ckv_raw, axis=1, keepdims=True) + EPS) * kvn_ref[...]
    ckv_ref[...] = ckv
    kvx = _dot(ckv.astype(BF16), wukv_ref[...])
    kp = _rope_block(seg("k_pe"), c32_ref[...], a32_ref[...], b32_ref[...], DR_A // 2)
    kpe_ref[...] = kp[:, :DR_A]
    kpt = kp + pltpu.roll(kp, DR_A, 1) + pltpu.roll(kp, 2 * DR_A, 1) + pltpu.roll(kp, 3 * DR_A, 1)
    kf_ref[:, :nope_w] = kvx[:, :nope_w].astype(BF16)
    kf_ref[:, nope_w:] = kpt.astype(BF16)
    va_ref[...] = kvx[:, nope_w:].astype(BF16)
    if vat_ref is not None:
        vat_ref[...] = kvx[:, nope_w:].T.astype(BF16)

    rq, rk = seg("rq"), seg("rk")
    kdec = kdec_ref[...]
    for blk in range(BRANCH_W // LANE):
        cols = slice(blk * LANE, (blk + 1) * LANE)
        rq_ref[:, cols] = _rope_block(rq[:, cols], c64_ref[...], a64_ref[...], b64_ref[...],
                                      D_HEAD // 2).astype(BF16)
        rkb = _rope_block(rk[:, cols], c64_ref[...], a64_ref[...], b64_ref[...], D_HEAD // 2) * (D_HEAD ** -0.5)
        rk_ref[:, cols] = rkb.astype(BF16)
        rkd_ref[:, cols] = (rkb * kdec[:, cols]).astype(BF16)
    rv_ref[...] = seg("rv").astype(BF16)
    rg_ref[...] = seg("rg")

    scale_h = D_HEAD ** -0.5 * LOG2E
    sq = seg("sq") * scale_h
    sq_ref[...] = sq.astype(BF16)
    sk_ref[...] = seg("sk")
    sv_ref[...] = seg("sv")
    if sqt_ref is not None:
        sqt_ref[...] = sq.T.astype(BF16)
        svt_ref[...] = seg("sv").T.astype(BF16)
    bq = seg("bq") * scale_h
    bq_ref[...] = bq.astype(BF16)
    bk_ref[...] = seg("bk")
    bv_ref[...] = seg("bv")
    if bqt_ref is not None:
        bqt_ref[...] = bq.T.astype(BF16)
        bvt_ref[...] = seg("bv").T.astype(BF16)
    if keep_refs:
        keep_refs[0][...] = seg("bk")
        keep_refs[1][...] = seg("bv")


_INPROJ_OUT = (
    (QA_W, BF16), (KV_RANK, F32), (DR_A, F32), (QA_W, BF16), (BRANCH_W, BF16),
    (BRANCH_W, BF16), (BRANCH_W, BF16), (BRANCH_W, BF16), (BRANCH_W, BF16), (BRANCH_W, F32),
    (BRANCH_W, BF16), (BRANCH_W, F32), (BRANCH_W, F32), (BRANCH_W, BF16), (BRANCH_W, F32), (BRANCH_W, F32))


_STATE_OUTS = (1, 2, 11, 12, 14, 15)


def _inproj_entry(*refs, n_in, n_alias, n_t, n_keep):
    outs = refs[n_in + n_alias:]
    n_main = len(_INPROJ_OUT)
    _inproj_kernel(*refs[:n_in], *outs[:n_main], t_refs=outs[n_main:n_main + n_t],
                   keep_refs=outs[n_main + n_t:n_main + n_t + n_keep])


def _inproj(x, w_in_p, qn, wuq, kvn, wukv, tables, *, tm, layer, stacked, seq_len=None, keep_rows=None):
    m = x.shape[0]
    tm = min(tm, m)
    assert m % tm == 0
    n_pos_tiles = tables[0].shape[0] // tm
    assert tables[0].shape[0] % tm == 0
    row = lambda i: (i, 0)
    const = lambda i: (0, 0)
    pos = lambda i: (i % n_pos_tiles, 0)
    in_specs = [pl.BlockSpec((tm, D_MODEL), row),
                pl.BlockSpec(w_in_p.shape, const),
                pl.BlockSpec(qn.shape, const),
                pl.BlockSpec(wuq.shape, const),
                pl.BlockSpec(kvn.shape, const),
                pl.BlockSpec(wukv.shape, const)]
    in_specs += [pl.BlockSpec((tm, t.shape[1]), pos) for t in tables]
    n_in = len(in_specs)
    out_specs, out_shape = [], []
    for k, (w, dt) in enumerate(_INPROJ_OUT):
        if k in _STATE_OUTS:
            out_specs.append(pl.BlockSpec((None, tm, w), lambda i: (layer, i, 0)))
            out_shape.append(jax.ShapeDtypeStruct((DEPTH, m, w), dt))
        else:
            out_specs.append(pl.BlockSpec((tm, w), row))
            out_shape.append(jax.ShapeDtypeStruct((m, w), dt))
    if seq_len is not None:
        assert seq_len % tm == 0 and m % seq_len == 0
        tps = seq_len // tm
        for w in (QA_W,) + (BRANCH_W,) * 5:
            out_specs.append(pl.BlockSpec((None, w, tm), lambda i: (i // tps, 0, i % tps)))
            out_shape.append(jax.ShapeDtypeStruct((m // seq_len, w, seq_len), BF16))
    n_t = len(out_specs) - len(_INPROJ_OUT)
    state_outs = list(_STATE_OUTS)
    if keep_rows is not None:
        assert seq_len is not None and keep_rows % tm == 0 and keep_rows <= seq_len
        tps, ktiles = seq_len // tm, keep_rows // tm
        kept = lambda i: (layer, (i // tps) * ktiles + jnp.maximum(i % tps - (tps - ktiles), 0), 0)
        for _ in range(2):
            state_outs.append(len(out_specs))
            out_specs.append(pl.BlockSpec((None, tm, BRANCH_W), kept))
            out_shape.append(jax.ShapeDtypeStruct((DEPTH, m // seq_len * keep_rows, BRANCH_W), F32))
    aliases = {}
    extra = ()
    if stacked is not None:
        extra = tuple(stacked)
        assert len(extra) == len(state_outs)
        in_specs += [pl.BlockSpec(memory_space=pl.ANY)] * len(extra)
        aliases = {n_in + k: o for k, o in enumerate(state_outs)}
    outs = pl.pallas_call(
        functools.partial(_inproj_entry, n_in=n_in, n_alias=len(extra), n_t=n_t,
                          n_keep=len(state_outs) - len(_STATE_OUTS)),
        grid=(m // tm,),
        in_specs=in_specs,
        out_specs=out_specs,
        out_shape=out_shape,
        input_output_aliases=aliases,
        compiler_params=_params(("arbitrary",)),
        name="inproj",
    )(x, w_in_p, qn, wuq, kvn, wukv, *tables, *extra)
    return outs, tuple(outs[k] for k in state_outs)


def _tile_spec(src, rows):
    arr, layer = src
    return pl.BlockSpec((None, None, rows, arr.shape[3]), lambda i, t: (layer, i, t, 0))


def _full_spec(src):
    arr, layer = src
    return pl.BlockSpec((None, None, arr.shape[2], arr.shape[3]), lambda i, t: (layer, i, 0, 0))


def _mla_kernel(q_ref, kfo_ref, vo_ref, kfp_ref, vp_ref, o_ref, *, tq, tk, n_own, n_past):
    qi = pl.program_id(1)
    qm = _stack_heads(q_ref[...], _mla_lanes)
    rows = N_HEADS * tq
    tko = kfo_ref.shape[0]
    n_loop = qi if n_past is None else n_past

    def softmax_pv(s, v, carry):
        m, accs = carry
        m_new = jnp.maximum(m, jnp.max(s, axis=1, keepdims=True))
        pb = jnp.exp2(s - m_new).astype(BF16)
        a = jnp.exp2(m - m_new)
        ones = jnp.ones((v.shape[0], D_HEAD), BF16)
        accs = tuple(a[h * tq:(h + 1) * tq] * accs[h]
                     + _dot(pb[h * tq:(h + 1) * tq], jnp.concatenate([v[:, _head_slice(h)], ones], axis=1))
                     for h in range(N_HEADS))
        return m_new, accs

    def past_scores(j):
        start = pl.multiple_of(jnp.minimum(j, jnp.maximum(n_loop - 1, 0)) * tk, tk)
        return _nt_dot(qm, kfp_ref[pl.ds(start, tk), :])

    row_q = lax.broadcasted_iota(jnp.int32, (rows, tko), 0) % tq
    col = lax.broadcasted_iota(jnp.int32, (rows, tko), 1)
    own_mask = jnp.logical_and(col < n_own, col // CHUNK <= row_q // CHUNK)
    carry = (jnp.full((rows, 1), NEG_INF, F32),
             tuple(jnp.zeros((tq, 2 * D_HEAD), F32) for _ in range(N_HEADS)))
    s_next = past_scores(0)
    carry = softmax_pv(jnp.where(own_mask, _nt_dot(qm, kfo_ref[...]), NEG_INF), vo_ref[...], carry)

    def body(j, c):
        s_cur, rest = c
        s_after = past_scores(j + 1)
        start = pl.multiple_of(j * tk, tk)
        return s_after, softmax_pv(s_cur, vp_ref[pl.ds(start, tk), :], rest)

    _, (_, accs) = lax.fori_loop(0, n_loop, body, (s_next, carry))
    _store_heads(o_ref, [accs[h][:, :D_HEAD] / accs[h][:, D_HEAD:D_HEAD + 1] for h in range(N_HEADS)])


def _mla_attn(q, kf_own, v_own, kf_past, v_past, *, tq, tk, n_own, causal_tiles):
    b, lq, _ = q.shape
    nqt = lq // tq
    tko = kf_own[0].shape[2] // nqt
    lp = kf_past[0].shape[2]
    assert lq % tq == 0 and lp % tk == 0
    return pl.pallas_call(
        functools.partial(_mla_kernel, tq=tq, tk=tk, n_own=n_own,
                          n_past=None if causal_tiles else lp // tk),
        grid=(b, nqt),
        in_specs=[pl.BlockSpec((None, tq, QA_W), lambda i, t: (i, t, 0)),
                  _tile_spec(kf_own, tko), _tile_spec(v_own, tko),
                  _full_spec(kf_past), _full_spec(v_past)],
        out_specs=pl.BlockSpec((None, tq, BRANCH_W), lambda i, t: (i, t, 0)),
        out_shape=jax.ShapeDtypeStruct((b, lq, BRANCH_W), BF16),
        compiler_params=_params(("parallel", "arbitrary")),
        name="mla_attn",
    )(q, kf_own[0], v_own[0], kf_past[0], v_past[0])


def _mla_kernel_t(qt_ref, kfo_ref, vto_ref, kfp_ref, vtp_ref, o_ref, *, tq, tk):
    qi = pl.program_id(1)
    qt = qt_ref[...]
    feat = lax.broadcasted_iota(jnp.int32, qt.shape, 0)
    parts = []
    for h in range(N_HEADS):
        keep = None
        for lo, hi in _mla_lanes(h):
            m = jnp.logical_and(feat >= lo, feat < hi)
            keep = m if keep is None else jnp.logical_or(keep, m)
        parts.append(jnp.where(keep, qt, jnp.zeros_like(qt)))
    qmt = jnp.concatenate(parts, axis=1)
    cols = N_HEADS * tq

    def softmax_pv(st, vt, carry):
        m, l, accs = carry
        m_new = jnp.maximum(m, jnp.max(st, axis=0, keepdims=True))
        pt = jnp.exp2(st - m_new)
        a = jnp.exp2(m - m_new)
        l = a * l + jnp.sum(pt, axis=0, keepdims=True)
        pb = pt.astype(BF16)
        accs = tuple(a[:, h * tq:(h + 1) * tq] * accs[h]
                     + _dot(vt[_head_slice(h), :], pb[:, h * tq:(h + 1) * tq]) for h in range(N_HEADS))
        return m_new, l, accs

    def past_scores(j):
        start = pl.multiple_of(jnp.minimum(j, jnp.maximum(qi - 1, 0)) * tk, tk)
        return _dot(kfp_ref[pl.ds(start, tk), :], qmt)

    key = lax.broadcasted_iota(jnp.int32, (tq, cols), 0)
    qry = lax.broadcasted_iota(jnp.int32, (tq, cols), 1) % tq
    own_mask = key // CHUNK <= qry // CHUNK
    carry = (jnp.full((1, cols), NEG_INF, F32), jnp.zeros((1, cols), F32),
             tuple(jnp.zeros((D_HEAD, tq), F32) for _ in range(N_HEADS)))
    s_next = past_scores(0)
    carry = softmax_pv(jnp.where(own_mask, _dot(kfo_ref[...], qmt), NEG_INF), vto_ref[...], carry)

    def body(j, c):
        s_cur, rest = c
        s_after = past_scores(j + 1)
        start = pl.multiple_of(j * tk, tk)
        return s_after, softmax_pv(s_cur, vtp_ref[:, pl.ds(start, tk)], rest)

    _, (_, l, accs) = lax.fori_loop(0, qi, body, (s_next, carry))
    out_t = jnp.concatenate([accs[h] / l[:, h * tq:(h + 1) * tq] for h in range(N_HEADS)], axis=0)
    o_ref[...] = out_t.T.astype(o_ref.dtype)


def _mla_attn_t(qt, kf, vt, *, tq):
    b, _, length = qt.shape
    assert length % tq == 0
    return pl.pallas_call(
        functools.partial(_mla_kernel_t, tq=tq, tk=tq),
        grid=(b, length // tq),
        in_specs=[pl.BlockSpec((None, QA_W, tq), lambda i, t: (i, 0, t)),
                  pl.BlockSpec((None, tq, QA_W), lambda i, t: (i, t, 0)),
                  pl.BlockSpec((None, BRANCH_W, tq), lambda i, t: (i, 0, t)),
                  pl.BlockSpec((None, length, QA_W), lambda i, t: (i, 0, 0)),
                  pl.BlockSpec((None, BRANCH_W, length), lambda i, t: (i, 0, 0))],
        out_specs=pl.BlockSpec((None, tq, BRANCH_W), lambda i, t: (i, t, 0)),
        out_shape=jax.ShapeDtypeStruct((b, length, BRANCH_W), BF16),
        compiler_params=_params(("parallel", "arbitrary")),
        name="mla_attn_t",
    )(qt, kf, vt, kf, vt)


def _sb_kernel(q_ref, ko_ref, vo_ref, kp_ref, vp_ref, o_ref, *, tq, tk, n_own, n_past):
    qi = pl.program_id(1)
    qm = _stack_heads(q_ref[...], _own_lanes)
    rows = N_HEADS * tq
    tko = ko_ref.shape[0]
    n_loop = qi if n_past is None else n_past

    def tri2(n):
        r = lax.broadcasted_iota(jnp.int32, (2 * n, n), 0) % n
        c = lax.broadcasted_iota(jnp.int32, (2 * n, n), 1)
        return jnp.where(r > c, 1.0, 0.0).astype(BF16)

    def weigh(z, v, carry, mask, tri):
        run, accs = carry
        neg_abs = lax.bitcast_convert_type(
            lax.bitcast_convert_type(z, jnp.uint32) | jnp.uint32(0x80000000), F32)
        t = jnp.log2(1.0 + jnp.exp2(neg_abs))
        log_beta = jnp.minimum(z, 0.0) - t
        log_stay = log_beta - z
        if mask is not None:
            log_stay = jnp.where(mask, log_stay, 0.0)
        hi = log_stay.astype(BF16)
        lo = (log_stay - hi.astype(F32)).astype(BF16)
        later = _dot(jnp.concatenate([hi, lo], axis=1), tri) + run
        w = jnp.exp2(log_beta + later)
        if mask is not None:
            w = jnp.where(mask, w, 0.0)
        wb = w.astype(BF16)
        vb = v.astype(BF16)
        accs = tuple(accs[h] + _dot(wb[h * tq:(h + 1) * tq], vb[:, _head_slice(h)])
                     for h in range(N_HEADS))
        return run + jnp.sum(log_stay, axis=1, keepdims=True), accs

    row_q = lax.broadcasted_iota(jnp.int32, (rows, tko), 0) % tq
    col = lax.broadcasted_iota(jnp.int32, (rows, tko), 1)
    own_mask = jnp.logical_and(col < n_own, col < row_q)

    def past_start(jj):
        return pl.multiple_of(jnp.clip(n_loop - 1 - jj, 0, kp_ref.shape[0] // tk - 1) * tk, tk)

    def past_scores(jj):
        return _nt_dot(qm, kp_ref[pl.ds(past_start(jj), tk), :].astype(BF16))

    def alive(run):
        return (jnp.max(run) > SB_DEAD_LOG2).astype(jnp.int32)

    z_next = past_scores(0)
    carry = (jnp.zeros((rows, 1), F32), tuple(jnp.zeros((tq, D_HEAD), F32) for _ in range(N_HEADS)))
    run, accs = weigh(_nt_dot(qm, ko_ref[...].astype(BF16)), vo_ref[...], carry, own_mask, tri2(tko))
    tri_past = tri2(tk)

    def cond(c):
        return jnp.logical_and(c[0] < n_loop, c[1] > 0)

    def body(c):
        jj, _, z_cur, run, accs = c
        z_after = past_scores(jj + 1)
        run, accs = weigh(z_cur, vp_ref[pl.ds(past_start(jj), tk), :], (run, accs), None, tri_past)
        return jj + 1, alive(run), z_after, run, accs

    out = lax.while_loop(cond, body, (jnp.int32(0), alive(run), z_next, run, accs))
    _store_heads(o_ref, out[4])


def _sb_attn(q, k_own, v_own, k_past, v_past, *, tq, tk, n_own, causal_tiles):
    b, lq, _ = q.shape
    nqt = lq // tq
    tko = k_own[0].shape[2] // nqt
    lp = k_past[0].shape[2]
    assert lq % tq == 0 and lp % tk == 0
    return pl.pallas_call(
        functools.partial(_sb_kernel, tq=tq, tk=tk, n_own=n_own,
                          n_past=None if causal_tiles else lp // tk),
        grid=(b, nqt),
        in_specs=[pl.BlockSpec((None, tq, BRANCH_W), lambda i, t: (i, t, 0)),
                  _tile_spec(k_own, tko), _tile_spec(v_own, tko),
                  _full_spec(k_past), _full_spec(v_past)],
        out_specs=pl.BlockSpec((None, tq, BRANCH_W), lambda i, t: (i, t, 0)),
        out_shape=jax.ShapeDtypeStruct((b, lq, BRANCH_W), BF16),
        compiler_params=_params(("parallel", "arbitrary")),
        name="sb_attn",
    )(q, k_own[0], v_own[0], k_past[0], v_past[0])


def _sb_kernel_t(qt_ref, ko_ref, vto_ref, kp_ref, vtp_ref, o_ref, *, tq, tk):
    qi = pl.program_id(1)
    qt = qt_ref[...]
    feat = lax.broadcasted_iota(jnp.int32, qt.shape, 0)
    qmt = jnp.concatenate(
        [jnp.where(jnp.logical_and(feat >= h * D_HEAD, feat < (h + 1) * D_HEAD), qt, jnp.zeros_like(qt))
         for h in range(N_HEADS)], axis=1)
    cols = N_HEADS * tq

    def tri2(n):
        r = lax.broadcasted_iota(jnp.int32, (n, 2 * n), 0)
        c = lax.broadcasted_iota(jnp.int32, (n, 2 * n), 1) % n
        return jnp.where(c > r, 1.0, 0.0).astype(BF16)

    def weigh(zt, vt, carry, mask, tri):
        run, accs = carry
        neg_abs = lax.bitcast_convert_type(
            lax.bitcast_convert_type(zt, jnp.uint32) | jnp.uint32(0x80000000), F32)
        t = jnp.log2(1.0 + jnp.exp2(neg_abs))
        log_beta = jnp.minimum(zt, 0.0) - t
        log_stay = log_beta - zt
        if mask is not None:
            log_stay = jnp.where(mask, log_stay, 0.0)
        hi = log_stay.astype(BF16)
        lo = (log_stay - hi.astype(F32)).astype(BF16)
        later = _dot(tri, jnp.concatenate([hi, lo], axis=0)) + run
        w = jnp.exp2(log_beta + later)
        if mask is not None:
            w = jnp.where(mask, w, 0.0)
        wb = w.astype(BF16)
        accs = tuple(accs[h] + _dot(vt[_head_slice(h), :], wb[:, h * tq:(h + 1) * tq])
                     for h in range(N_HEADS))
        return run + jnp.sum(log_stay, axis=0, keepdims=True), accs

    def past_start(jj):
        return pl.multiple_of(jnp.clip(qi - 1 - jj, 0, kp_ref.shape[0] // tk - 1) * tk, tk)

    def past_scores(jj):
        return _dot(kp_ref[pl.ds(past_start(jj), tk), :].astype(BF16), qmt)

    def alive(run):
        return (jnp.max(run) > SB_DEAD_LOG2).astype(jnp.int32)

    key = lax.broadcasted_iota(jnp.int32, (tq, cols), 0)
    qry = lax.broadcasted_iota(jnp.int32, (tq, cols), 1) % tq
    own_mask = key < qry
    z_next = past_scores(0)
    carry = (jnp.zeros((1, cols), F32), tuple(jnp.zeros((D_HEAD, tq), F32) for _ in range(N_HEADS)))
    run, accs = weigh(_dot(ko_ref[...].astype(BF16), qmt), vto_ref[...], carry, own_mask, tri2(tq))
    tri_past = tri2(tk)

    def cond(c):
        return jnp.logical_and(c[0] < qi, c[1] > 0)

    def body(c):
        jj, _, z_cur, run, accs = c
        z_after = past_scores(jj + 1)
        run, accs = weigh(z_cur, vtp_ref[:, pl.ds(past_start(jj), tk)], (run, accs), None, tri_past)
        return jj + 1, alive(run), z_after, run, accs

    out = lax.while_loop(cond, body, (jnp.int32(0), alive(run), z_next, run, accs))
    o_ref[...] = jnp.concatenate(out[4], axis=0).T.astype(o_ref.dtype)


def _sb_attn_t(qt, k, vt, *, tq):
    b, _, length = qt.shape
    assert length % tq == 0
    return pl.pallas_call(
        functools.partial(_sb_kernel_t, tq=tq, tk=tq),
        grid=(b, length // tq),
        in_specs=[pl.BlockSpec((None, BRANCH_W, tq), lambda i, t: (i, 0, t)),
                  _tile_spec(k, tq),
                  pl.BlockSpec((None, BRANCH_W, tq), lambda i, t: (i, 0, t)),
                  _full_spec(k),
                  pl.BlockSpec((None, BRANCH_W, length), lambda i, t: (i, 0, 0))],
        out_specs=pl.BlockSpec((None, tq, BRANCH_W), lambda i, t: (i, t, 0)),
        out_shape=jax.ShapeDtypeStruct((b, length, BRANCH_W), BF16),
        compiler_params=_params(("parallel", "arbitrary")),
        name="sb_attn_t",
    )(qt, k[0], vt, k[0], vt)


def _ret_kernel(q_ref, k_ref, kd_ref, v_ref, rg_ref, s0_ref, dec_ref, qdec_ref, gl_ref,
                gng_ref, gnb_ref, o_ref, sout_ref, state_ref, *, lc, nseq):
    c = pl.program_id(1)

    @pl.when(c == 0)
    def _():
        state_ref[...] = s0_ref[...]

    for s in range(nseq):
        qm = _stack_heads(q_ref[s], _own_lanes)
        v = v_ref[s]
        state = state_ref[s]
        scores = (_nt_dot(qm, k_ref[s]) * dec_ref[...]).astype(BF16)
        cross = _dot(qm, state.astype(BF16)) * qdec_ref[...]
        kv_full = _tn_dot(kd_ref[s], v)
        new_state = gl_ref[...] * state + jnp.concatenate(
            [kv_full[_head_slice(h), _head_slice(h)] for h in range(N_HEADS)], axis=0)
        state_ref[s] = new_state

        rg = rg_ref[s]
        for h in range(N_HEADS):
            o = _dot(scores[h * lc:(h + 1) * lc], v[:, _head_slice(h)]) + cross[h * lc:(h + 1) * lc]
            mu = jnp.mean(o, axis=-1, keepdims=True)
            d = o - mu
            var = jnp.mean(d * d, axis=-1, keepdims=True)
            y = d * lax.rsqrt(var + EPS) * gng_ref[h] + gnb_ref[h]
            g = rg[:, _head_slice(h)]
            o_ref[s, :, _head_slice(h)] = (y * (g * _sigmoid(g))).astype(o_ref.dtype)

    @pl.when(c == pl.num_programs(1) - 1)
    def _():
        sout_ref[...] = state_ref[...]


def _retention(q, k, kd, v, rg, s0, dec, qdec, gl, gng, gnb, *, lc, nseq):
    b, length, _ = q.shape
    assert length % lc == 0 and b % nseq == 0
    seq = lambda i, t: (i, t, 0)
    st = lambda i, t: (i, 0, 0)
    c2 = lambda i, t: (0, 0)
    c3 = lambda i, t: (0, 0, 0)
    return pl.pallas_call(
        functools.partial(_ret_kernel, lc=lc, nseq=nseq),
        grid=(b // nseq, length // lc),
        in_specs=[pl.BlockSpec((nseq, lc, BRANCH_W), seq)] * 5
        + [pl.BlockSpec((nseq, BRANCH_W, D_HEAD), st),
           pl.BlockSpec(dec.shape, c2), pl.BlockSpec(qdec.shape, c2), pl.BlockSpec(gl.shape, c2),
           pl.BlockSpec(gng.shape, c3), pl.BlockSpec(gnb.shape, c3)],
        out_specs=[pl.BlockSpec((nseq, lc, BRANCH_W), seq),
                   pl.BlockSpec((nseq, BRANCH_W, D_HEAD), st)],
        out_shape=[jax.ShapeDtypeStruct((b, length, BRANCH_W), BF16),
                   jax.ShapeDtypeStruct((b, BRANCH_W, D_HEAD), F32)],
        scratch_shapes=[pltpu.VMEM((nseq, BRANCH_W, D_HEAD), F32)],
        compiler_params=_params(("parallel", "arbitrary")),
        name="retention",
    )(q, k, kd, v, rg, s0, dec, qdec, gl, gng, gnb)


def _band_kernel(q_ref, k_ref, v_ref, bias_ref, o_ref, *, tq, win, back):
    qi = pl.program_id(1)
    start = pl.multiple_of(jnp.maximum(qi - back, 0) * tq, tq)
    k = k_ref[pl.ds(start, win), :].astype(BF16)
    v = v_ref[pl.ds(start, win), :].astype(BF16)
    qm = _stack_heads(q_ref[...], _own_lanes)
    s = _nt_dot(qm, k) + bias_ref[...]
    pb = jnp.exp2(s - jnp.max(s, axis=1, keepdims=True)).astype(BF16)
    ones = jnp.ones((win, D_HEAD), BF16)
    outs = []
    for h in range(N_HEADS):
        o = _dot(pb[h * tq:(h + 1) * tq], jnp.concatenate([v[:, _head_slice(h)], ones], axis=1))
        outs.append(o[:, :D_HEAD] / o[:, D_HEAD:D_HEAD + 1])
    _store_heads(o_ref, outs)


def _band_attn(q, k, v, bias, *, tq, win, back):
    b, lq, _ = q.shape
    nvar = bias.shape[0]
    assert lq % tq == 0
    return pl.pallas_call(
        functools.partial(_band_kernel, tq=tq, win=win, back=back),
        grid=(b, lq // tq),
        in_specs=[pl.BlockSpec((None, tq, BRANCH_W), lambda i, t: (i, t, 0)),
                  _full_spec(k), _full_spec(v),
                  pl.BlockSpec((None, N_HEADS * tq, win), lambda i, t: (jnp.minimum(t, nvar - 1), 0, 0))],
        out_specs=pl.BlockSpec((None, tq, BRANCH_W), lambda i, t: (i, t, 0)),
        out_shape=jax.ShapeDtypeStruct((b, lq, BRANCH_W), BF16),
        compiler_params=_params(("parallel", "arbitrary")),
        name="band_attn",
    )(q, k[0], v[0], bias)


def _band_kernel_t(qt_ref, k_ref, vt_ref, bias_ref, o_ref, *, tq, win, back):
    qi = pl.program_id(1)
    start = pl.multiple_of(jnp.maximum(qi - back, 0) * tq, tq)
    k = k_ref[pl.ds(start, win), :].astype(BF16)
    vt = vt_ref[:, pl.ds(start, win)]
    qt = qt_ref[...]
    feat = lax.broadcasted_iota(jnp.int32, qt.shape, 0)
    qmt = jnp.concatenate(
        [jnp.where(jnp.logical_and(feat >= h * D_HEAD, feat < (h + 1) * D_HEAD), qt, jnp.zeros_like(qt))
         for h in range(N_HEADS)], axis=1)
    st = _dot(k, qmt) + bias_ref[...]
    pt = jnp.exp2(st - jnp.max(st, axis=0, keepdims=True))
    l = jnp.sum(pt, axis=0, keepdims=True)
    pb = pt.astype(BF16)
    out_t = jnp.concatenate(
        [_dot(vt[_head_slice(h), :], pb[:, h * tq:(h + 1) * tq]) / l[:, h * tq:(h + 1) * tq]
         for h in range(N_HEADS)], axis=0)
    o_ref[...] = out_t.T.astype(o_ref.dtype)


def _band_attn_t(qt, k, vt, bias_t, *, tq, win, back):
    b, _, length = qt.shape
    nvar = bias_t.shape[0]
    assert length % tq == 0
    return pl.pallas_call(
        functools.partial(_band_kernel_t, tq=tq, win=win, back=back),
        grid=(b, length // tq),
        in_specs=[pl.BlockSpec((None, BRANCH_W, tq), lambda i, t: (i, 0, t)),
                  _full_spec(k),
                  pl.BlockSpec((None, BRANCH_W, length), lambda i, t: (i, 0, 0)),
                  pl.BlockSpec((None, win, N_HEADS * tq), lambda i, t: (jnp.minimum(t, nvar - 1), 0, 0))],
        out_specs=pl.BlockSpec((None, tq, BRANCH_W), lambda i, t: (i, t, 0)),
        out_shape=jax.ShapeDtypeStruct((b, length, BRANCH_W), BF16),
        compiler_params=_params(("parallel", "arbitrary")),
        name="band_attn_t",
    )(qt, k[0], vt, bias_t)


def _merge_kernel(x_ref, ba_ref, bb_ref, bc_ref, bd_ref, wg_ref, wb_ref, wo_ref, g_ref, b_ref, o_ref):
    x = x_ref[...]
    xb = x.astype(BF16)
    merged = None
    for n, br_ref in enumerate((ba_ref, bb_ref, bc_ref, bd_ref)):
        logits = _dot(xb, wg_ref[:, n * D_MODEL:(n + 1) * D_MODEL])
        term = _dot(br_ref[...], wb_ref[n]) * _sigmoid(logits)
        merged = term if merged is None else merged + term
    mix = _dot(merged.astype(BF16), wo_ref[...])
    o_ref[...] = _layer_norm(ALPHA * x + mix, g_ref[...], b_ref[...])


def _merge(x, branches, wg, wb, wo, g, b, *, tm):
    m = x.shape[0]
    tm = min(tm, m)
    assert m % tm == 0
    const2 = lambda i: (0, 0)
    row = lambda i: (i, 0)
    return pl.pallas_call(
        _merge_kernel,
        grid=(m // tm,),
        in_specs=[pl.BlockSpec((tm, D_MODEL), row)]
        + [pl.BlockSpec((tm, BRANCH_W), row)] * N_BRANCH
        + [pl.BlockSpec((D_MODEL, N_BRANCH * D_MODEL), const2),
           pl.BlockSpec((N_BRANCH, BRANCH_W, D_MODEL), lambda i: (0, 0, 0)),
           pl.BlockSpec((D_MODEL, D_MODEL), const2),
           pl.BlockSpec((1, D_MODEL), const2),
           pl.BlockSpec((1, D_MODEL), const2)],
        out_specs=pl.BlockSpec((tm, D_MODEL), row),
        out_shape=jax.ShapeDtypeStruct((m, D_MODEL), F32),
        compiler_params=_params(("parallel",)),
        name="merge",
    )(x, *branches, wg, wb, wo, g, b)


def _route(aff_t, sel_t):
    def top2_sum(a, b, c, d):
        hi1, lo1 = jnp.maximum(a, b), jnp.minimum(a, b)
        hi2, lo2 = jnp.maximum(c, d), jnp.minimum(c, d)
        return jnp.maximum(hi1, hi2) + jnp.maximum(jnp.minimum(hi1, hi2), jnp.maximum(lo1, lo2))

    score = [top2_sum(*sel_t[g * EXPERTS_PER_GROUP:(g + 1) * EXPERTS_PER_GROUP])
             for g in range(N_GROUPS)]
    best_here = []
    for g in range(N_GROUPS):
        ok = None
        for o in range(N_GROUPS):
            if o == g:
                continue
            c = (score[g] > score[o]) if o < g else (score[g] >= score[o])
            ok = c if ok is None else jnp.logical_and(ok, c)
        best_here.append(ok)
    picked = []
    for e in range(N_EXPERTS):
        g = e // EXPERTS_PER_GROUP
        rank = jnp.zeros_like(sel_t[e])
        for o in range(g * EXPERTS_PER_GROUP, (g + 1) * EXPERTS_PER_GROUP):
            if o == e:
                continue
            ahead = (sel_t[o] >= sel_t[e]) if o < e else (sel_t[o] > sel_t[e])
            rank = rank + jnp.where(ahead, 1.0, 0.0)
        picked.append(jnp.where(jnp.logical_and(best_here[g], rank < TOP_K), aff_t[e], 0.0))
    total = picked[0]
    for e in range(1, N_EXPERTS):
        total = total + picked[e]
    return [p / total for p in picked]


def _moe_kernel(x_ref, wrh_ref, wrl_ref, br_ref, wg_ref, wu_ref, wd_ref, g_ref, b_ref, o_ref, acc_ref):
    x = x_ref[...]
    xh = x.astype(BF16)
    xl = (x - xh.astype(F32)).astype(BF16)
    wrh = wrh_ref[...]
    logits = _dot(xh, wrh) + _dot(xl, wrh) + _dot(xh, wrl_ref[...])
    aff = _sigmoid(logits).T
    bias = br_ref[...]
    aff_t = [aff[e:e + 1, :] for e in range(N_EXPERTS)]
    sel_t = [aff_t[e] + bias[e:e + 1, :] for e in range(N_EXPERTS)]
    gate_rows = _route(aff_t, sel_t)
    tm = x.shape[0]
    gate_t = jnp.concatenate(gate_rows + [jnp.zeros((LANE - N_EXPERTS, tm), F32)], axis=0)
    gate = gate_t.T

    for e in range(N_EXPERTS):
        gt = _dot(xh, wg_ref[e])
        hmid = gt * _sigmoid(gt) * _dot(xh, wu_ref[e])
        y = _dot(hmid.astype(BF16), wd_ref[e]) * gate[:, e:e + 1]
        if e == 0:
            acc_ref[...] = y
        else:
            acc_ref[...] += y
    o_ref[...] = _layer_norm(ALPHA * x + acc_ref[...], g_ref[...], b_ref[...])


def _moe(x, wrh, wrl, br, wg, wu, wd, g, b, *, tm, layer):
    m = x.shape[0]
    tm = min(tm, m)
    assert m % tm == 0
    const2 = lambda i: (0, 0)
    this_layer = lambda i: (layer, 0, 0, 0)
    return pl.pallas_call(
        _moe_kernel,
        grid=(m // tm,),
        in_specs=[pl.BlockSpec((tm, D_MODEL), lambda i: (i, 0)),
                  pl.BlockSpec((D_MODEL, LANE), const2),
                  pl.BlockSpec((D_MODEL, LANE), const2),
                  pl.BlockSpec((LANE, 1), const2),
                  pl.BlockSpec((None, N_EXPERTS, D_MODEL, D_EXPERT), this_layer, pipeline_mode=pl.Buffered(1)),
                  pl.BlockSpec((None, N_EXPERTS, D_MODEL, D_EXPERT), this_layer, pipeline_mode=pl.Buffered(1)),
                  pl.BlockSpec((None, N_EXPERTS, D_EXPERT, D_MODEL), this_layer, pipeline_mode=pl.Buffered(1)),
                  pl.BlockSpec((1, D_MODEL), const2),
                  pl.BlockSpec((1, D_MODEL), const2)],
        out_specs=pl.BlockSpec((tm, D_MODEL), lambda i: (i, 0)),
        out_shape=jax.ShapeDtypeStruct((m, D_MODEL), F32),
        scratch_shapes=[pltpu.VMEM((tm, D_MODEL), F32)],
        compiler_params=_params(("parallel",)),
        name="moe",
    )(x, wrh, wrl, br, wg, wu, wd, g, b)


def _rope_tables(pos, d):
    half = d // 2
    inv = jnp.power(ROPE_BASE, -jnp.arange(half, dtype=F32) / half)
    ang = pos.astype(F32)[:, None] * inv[None, :]
    cos, sin = jnp.cos(ang), jnp.sin(ang)
    zero = jnp.zeros_like(sin)
    rep = LANE // d
    cos_t = jnp.tile(jnp.concatenate([cos, cos], axis=1), (1, rep))
    sin_a = jnp.tile(jnp.concatenate([-sin, zero], axis=1), (1, rep))
    sin_b = jnp.tile(jnp.concatenate([zero, sin], axis=1), (1, rep))
    return cos_t, sin_a, sin_b


def _retention_tables(lc):
    log_g = jnp.log1p(-jnp.exp2(-5.0 - jnp.arange(N_HEADS, dtype=F32)))
    i = jnp.arange(lc, dtype=F32)
    diff = i[:, None] - i[None, :]
    dec = jnp.where(diff >= 0, jnp.exp(jnp.maximum(diff, 0.0)[None] * log_g[:, None, None]), 0.0)
    qdec = jnp.exp((i[None, :] + 1.0) * log_g[:, None])
    kdec = jnp.exp((lc - 1.0 - i)[None, :] * log_g[:, None])
    gl = jnp.exp(lc * log_g)
    dec = dec.reshape(N_HEADS * lc, lc)
    qdec = jnp.broadcast_to(qdec[:, :, None], (N_HEADS, lc, D_HEAD)).reshape(N_HEADS * lc, D_HEAD)
    gl = jnp.broadcast_to(gl[:, None, None], (N_HEADS, D_HEAD, D_HEAD)).reshape(BRANCH_W, D_HEAD)
    kdec = jnp.repeat(kdec.T, D_HEAD, axis=1)
    return dec, qdec, kdec, gl


def _band_bias(rel_bias, tq, win, q_minus_k0, valid):
    length = tq + win - 1
    d = np.arange(length) - (tq - 1) - q_minus_k0
    idx = np.clip(d, -REL_CLIP, REL_CLIP) + REL_CLIP
    g = rel_bias[:, idx].astype(F32) * LOG2E
    gp = jnp.concatenate([g, jnp.zeros((N_HEADS, 1), F32)], axis=1)
    m = jnp.tile(gp, (1, tq))[:, :tq * length].reshape(N_HEADS, tq, length)
    tile = m[:, :, tq - 1:tq - 1 + win]
    return jnp.where(valid[None], tile, NEG_INF).reshape(N_HEADS * tq, win)


def _pack_w_in(w_in_l):
    cols = []
    src = 0
    for w in _IN_WIDTH:
        seg = w_in_l[:, src:src + w]
        cols.append(jnp.pad(seg, ((0, 0), (0, _round_up(w, LANE) - w))))
        src += w
    return jnp.concatenate(cols, axis=1).astype(BF16)


def _pad_rows(t, n):
    return jnp.pad(t, ((0, 0), (0, n - t.shape[1]), (0, 0)))


def _token_mixers(x, pos0, past, lw, *, prompt, layer, stacked):
    (w_in_p, qn, wuq, kvn, wukv, gn_g, gn_b, rel_bias, b, length) = lw
    assert pos0 % CHUNK == 0
    pos = pos0 + jnp.arange(length)
    lc = 256 if prompt else length
    dec, qdec, kdec, gl = _retention_tables(lc)
    tables = list(_rope_tables(pos, DR_A)) + list(_rope_tables(pos, D_HEAD)) + [jnp.tile(kdec, (length // lc, 1))]
    if not prompt:
        tables = [jnp.tile(t, (b, 1)) for t in tables]
    keep = min(PREV_CHUNKS * CHUNK, length)
    outs, stacked = _inproj(x, w_in_p, qn, wuq, kvn, wukv, tables, tm=TOKEN_TILE if prompt else b * length,
                            layer=layer, stacked=stacked, seq_len=length if prompt else None,
                            keep_rows=keep if prompt and keep < length else None)
    per_batch = lambda o: o.reshape(o.shape[:-2] + (b, length, o.shape[-1]))
    (q_a, ckv, kpe, kf, v_a, rq, rk, rkd, rv, rg, sq, sk, sv, bq, bk, bv) = [
        per_batch(o) for o in outs[:len(_INPROJ_OUT)]]
    here = lambda t: (t, layer)
    only = lambda t: (t[None], 0)

    if prompt:
        tq = 256
        qa_t, va_t, sq_t, sv_t, bq_t, bv_t = outs[len(_INPROJ_OUT):len(_INPROJ_OUT) + 6]
        o_a = _mla_attn_t(qa_t, kf, va_t, tq=tq)
        o_c = _sb_attn_t(sq_t, here(sk), sv_t, tq=tq)
        s0 = jnp.zeros((b, BRANCH_W, D_HEAD), F32)
        win = 3 * tq
        i = np.arange(tq)[:, None]
        c = np.arange(win)[None, :]
        variants = []
        for t in range(3):
            qc, kc = i // CHUNK + t * (tq // CHUNK), c // CHUNK
            variants.append(_band_bias(rel_bias, tq, win, t * tq, (kc <= qc) & (kc >= qc - PREV_CHUNKS)))
        bias_t = jnp.stack(variants).transpose(0, 2, 1)
        o_d = _band_attn_t(bq_t, here(bk), bv_t, bias_t, tq=tq, win=win, back=2)
    else:
        c_ckv, c_kpe, s_prev, c_sk, c_sv, c_bk, c_bv = past
        n_past = c_ckv.shape[2]
        tko = LANE
        kf_c, v_c = _expand_latent(c_ckv.reshape(DEPTH, b * n_past, KV_RANK),
                                   c_kpe.reshape(DEPTH, b * n_past, DR_A), wukv, layer, TOKEN_TILE)
        o_a = _mla_attn(q_a, only(_pad_rows(kf, tko)), only(_pad_rows(v_a, tko)),
                        only(kf_c.reshape(b, n_past, QA_W)), only(v_c.reshape(b, n_past, BRANCH_W)),
                        tq=length, tk=256, n_own=length, causal_tiles=False)
        o_c = _sb_attn(sq, only(_pad_rows(sk[layer], tko)), only(_pad_rows(sv[layer], tko)),
                       (c_sk.reshape(DEPTH, b, n_past, BRANCH_W), layer),
                       (c_sv.reshape(DEPTH, b, n_past, BRANCH_W), layer),
                       tq=length, tk=256, n_own=length, causal_tiles=False)
        s0 = s_prev[layer].reshape(b, BRANCH_W, D_HEAD)
        n_band = c_bk.shape[2]
        n_keys = n_band + length
        win = _round_up(n_keys, LANE)
        bk_all = _pad_rows(jnp.concatenate([c_bk[layer].reshape(b, n_band, BRANCH_W), bk[layer]], axis=1), win)
        bv_all = _pad_rows(jnp.concatenate([c_bv[layer].reshape(b, n_band, BRANCH_W), bv[layer]], axis=1), win)
        k_pos = pos0 - n_band + np.arange(win)
        q_pos = pos0 + np.arange(length)
        qc, kc = q_pos[:, None] // CHUNK, k_pos[None, :] // CHUNK
        valid = (np.arange(win)[None, :] < n_keys) & (k_pos[None, :] >= 0) & (kc <= qc) & (kc >= qc - PREV_CHUNKS)
        bias = _band_bias(rel_bias, length, win, n_band, valid)[None]
        o_d = _band_attn(bq, only(bk_all), only(bv_all), bias, tq=length, win=win, back=0)

    o_r, s_ret = _retention(rq, rk, rkd, rv, rg, s0, dec, qdec, gl,
                            gn_g.reshape(N_HEADS, 1, D_HEAD), gn_b.reshape(N_HEADS, 1, D_HEAD), lc=lc,
                            nseq=RET_SEQS_PER_STEP if b % RET_SEQS_PER_STEP == 0 else 1)
    s_ret = s_ret.reshape(b, N_HEADS, D_HEAD, D_HEAD)
    flat = lambda t: t.reshape(b * length, BRANCH_W)
    return (flat(o_a), flat(o_r), flat(o_c), flat(o_d)), s_ret, stacked


def _state_outputs(stacked, s_ret, b, length):
    per_seq = lambda t: t.reshape(DEPTH, b, t.shape[1] // b, t.shape[-1])
    heads4 = lambda t: per_seq(t).reshape(DEPTH, b, t.shape[1] // b, N_HEADS, D_HEAD)
    ckv, kpe, sk, sv, bk, bv = stacked[:6]
    if len(stacked) > 6:
        bk, bv = stacked[6:]
    return (per_seq(ckv), per_seq(kpe), jnp.stack(s_ret, axis=0), heads4(sk), heads4(sv), heads4(bk), heads4(bv))


def kernel(x_prompt, x_sample, cache_mla_ckv, cache_mla_kpe, state_ret, cache_sb_k, cache_sb_v, cache_band_k, cache_band_v, w_in, mla_q_norm, mla_w_uq, mla_kv_norm, mla_w_ukv, ret_gn_g, ret_gn_b, band_rel_bias, w_branch, w_o, ln1_g, ln1_b, w_router, b_router, w_exp_gate, w_exp_up, w_exp_down, ln2_g, ln2_b):
    bp, lp, _ = x_prompt.shape
    bs, ls, _ = x_sample.shape
    past_len = cache_mla_ckv.shape[2]
    xp = x_prompt.reshape(bp * lp, D_MODEL)
    xs = x_sample.reshape(bs * ls, D_MODEL)

    wr = jnp.pad(w_router, ((0, 0), (0, LANE - N_EXPERTS)))
    wrh = wr.astype(BF16)
    wrl = (wr - wrh.astype(F32)).astype(BF16)
    br = jnp.pad(b_router, (0, LANE - N_EXPERTS)).reshape(LANE, 1)

    we_gate, we_up, we_down = w_exp_gate.astype(BF16), w_exp_up.astype(BF16), w_exp_down.astype(BF16)
    past = (cache_mla_ckv, cache_mla_kpe, state_ret, cache_sb_k, cache_sb_v, cache_band_k, cache_band_v)
    ret_p, ret_s = [], []
    stacked_p = stacked_s = None
    for l in range(DEPTH):
        wuq = mla_w_uq[l]
        wuq = jnp.concatenate([wuq[:, :, :DN_A].reshape(Q_RANK, -1), wuq[:, :, DN_A:].reshape(Q_RANK, -1)], axis=1)
        wuq = jnp.pad(wuq, ((0, _round_up(Q_RANK, LANE) - Q_RANK), (0, 0))).astype(BF16)
        wukv = mla_w_ukv[l]
        wukv = jnp.concatenate([wukv[:, :, :DN_A].reshape(KV_RANK, -1), wukv[:, :, DN_A:].reshape(KV_RANK, -1)],
                               axis=1).astype(BF16)
        qn = jnp.pad(mla_q_norm[l], (0, _round_up(Q_RANK, LANE) - Q_RANK)).reshape(1, -1)
        kvn = mla_kv_norm[l].reshape(1, KV_RANK)
        w_in_p = _pack_w_in(w_in[l])
        wg = w_in[l][:, GATE_COL0:].astype(BF16)
        wb = w_branch[l].astype(BF16)
        wo = w_o[l].astype(BF16)
        g1, b1 = ln1_g[l].reshape(1, D_MODEL), ln1_b[l].reshape(1, D_MODEL)
        g2, b2 = ln2_g[l].reshape(1, D_MODEL), ln2_b[l].reshape(1, D_MODEL)
        lw =(w_in_p, qn, wuq, kvn, wukv, ret_gn_g[l], ret_gn_b[l], band_rel_bias[l])

        br_p, s_ret_p, stacked_p = _token_mixers(xp, 0, None, lw + (bp, lp), prompt=True,
                                                 layer=l, stacked=stacked_p)
        br_s, s_ret_s, stacked_s = _token_mixers(xs, past_len, past, lw + (bs, ls), prompt=False,
                                                 layer=l, stacked=stacked_s)
        xp = _merge(xp, br_p, wg, wb, wo, g1, b1, tm=TOKEN_TILE)
        xs = _merge(xs, br_s, wg, wb, wo, g1, b1, tm=TOKEN_TILE)
        xp = _moe(xp, wrh, wrl, br, we_gate, we_up, we_down, g2, b2, tm=TOKEN_TILE, layer=l)
        xs = _moe(xs, wrh, wrl, br, we_gate, we_up, we_down, g2, b2, tm=TOKEN_TILE, layer=l)
        ret_p.append(s_ret_p)
        ret_s.append(s_ret_s)

    return ((xp.reshape(bp, lp, D_MODEL), xs.reshape(bs, ls, D_MODEL))
            + _state_outputs(stacked_p, ret_p, bp, lp)
            + _state_outputs(stacked_s, ret_s, bs, ls))
```

```python
import functools

import jax
import jax.numpy as jnp
import numpy as np
from jax import lax
from jax.experimental import pallas as pl
from jax.experimental.pallas import tpu as pltpu

D_MODEL = 1024
DEPTH = 2
CHUNK = 64
N_BRANCH = 4
BRANCH_W = D_MODEL // 4
N_HEADS = 4
D_HEAD = BRANCH_W // N_HEADS
DN_A = 64
DR_A = 32
DQK_A = DN_A + DR_A
Q_RANK = (3 * D_MODEL) // 16
KV_RANK = D_MODEL // 8
PREV_CHUNKS = 8
REL_CLIP = 128
ROPE_BASE = 10000.0
N_EXPERTS = 16
N_GROUPS = 4
EXPERTS_PER_GROUP = N_EXPERTS // N_GROUPS
TOP_K = 2
D_EXPERT = D_MODEL // 4
ALPHA = (2.0 * DEPTH) ** 0.25
EPS = 1e-5
NEG_INF = -1e30
LOG2E = 1.4426950408889634
SB_DEAD_LOG2 = -150.0

F32 = jnp.float32
BF16 = jnp.bfloat16

V7X_VMEM_LIMIT = 56 * 1024 * 1024
LANE = 128
TOKEN_TILE = 512
SEQS_PER_STEP = 2

_IN_NAMES = ("c_q", "c_kv", "k_pe", "rq", "rk", "rv", "rg", "sq", "sk", "sv", "bq", "bk", "bv")
_IN_WIDTH = (Q_RANK, KV_RANK, DR_A) + (BRANCH_W,) * 10
QA_W = N_HEADS * DN_A + N_HEADS * DR_A


def _round_up(n, m):
    return (n + m - 1) // m * m


_IN_OFF = {}
_off = 0
for _n, _w in zip(_IN_NAMES, _IN_WIDTH):
    _IN_OFF[_n] = (_off, _round_up(_w, LANE))
    _off += _round_up(_w, LANE)
IN_PACKED = _off
GATE_COL0 = sum(_IN_WIDTH)


def _params(sem):
    return pltpu.CompilerParams(dimension_semantics=sem, vmem_limit_bytes=V7X_VMEM_LIMIT)


def _nt_dot(a, b):
    return lax.dot_general(a, b, (((1,), (1,)), ((), ())), preferred_element_type=F32)


def _tn_dot(a, b):
    return lax.dot_general(a, b, (((0,), (0,)), ((), ())), preferred_element_type=F32)


def _dot(a, b):
    return jnp.dot(a, b, preferred_element_type=F32)


def _layer_norm(v, g, b):
    mu = jnp.mean(v, axis=-1, keepdims=True)
    d = v - mu
    var = jnp.mean(d * d, axis=-1, keepdims=True)
    return d * lax.rsqrt(var + EPS) * g + b


def _sigmoid(v):
    return 0.5 * jnp.tanh(0.5 * v) + 0.5


def _head_slice(h):
    return slice(h * D_HEAD, (h + 1) * D_HEAD)


def _stack_heads(q, lane_sets):
    lane = lax.broadcasted_iota(jnp.int32, q.shape, 1)
    zero = jnp.zeros_like(q)
    parts = []
    for h in range(N_HEADS):
        keep = None
        for lo, hi in lane_sets(h):
            m = jnp.logical_and(lane >= lo, lane < hi)
            keep = m if keep is None else jnp.logical_or(keep, m)
        parts.append(jnp.where(keep, q, zero))
    return jnp.concatenate(parts, axis=0)


def _own_lanes(h):
    return ((h * D_HEAD, (h + 1) * D_HEAD),)


def _mla_lanes(h):
    base = N_HEADS * DN_A
    return ((h * DN_A, (h + 1) * DN_A), (base + h * DR_A, base + (h + 1) * DR_A))


def _store_heads(o_ref, parts):
    for h, p in enumerate(parts):
        o_ref[:, _head_slice(h)] = p.astype(o_ref.dtype)


def _expand_kernel(ckv_ref, kpe_ref, w_ref, kf_ref, v_ref):
    kvx = _dot(ckv_ref[...].astype(BF16), w_ref[...])
    kp = kpe_ref[...].astype(BF16)
    kf_ref[:, :N_HEADS * DN_A] = kvx[:, :N_HEADS * DN_A].astype(BF16)
    kf_ref[:, N_HEADS * DN_A:] = jnp.concatenate([kp] * N_HEADS, axis=1)
    v_ref[...] = kvx[:, N_HEADS * DN_A:].astype(BF16)


def _expand_latent(ckv, kpe, w_ukv, layer, tm):
    m = ckv.shape[1]
    assert m % tm == 0
    return pl.pallas_call(
        _expand_kernel,
        grid=(m // tm,),
        in_specs=[pl.BlockSpec((None, tm, KV_RANK), lambda i: (layer, i, 0)),
                  pl.BlockSpec((None, tm, DR_A), lambda i: (layer, i, 0)),
                  pl.BlockSpec((KV_RANK, 2 * BRANCH_W), lambda i: (0, 0))],
        out_specs=[pl.BlockSpec((tm, QA_W), lambda i: (i, 0)),
                   pl.BlockSpec((tm, BRANCH_W), lambda i: (i, 0))],
        out_shape=[jax.ShapeDtypeStruct((m, QA_W), BF16),
                   jax.ShapeDtypeStruct((m, BRANCH_W), BF16)],
        compiler_params=_params(("parallel",)),
        name="expand_latent",
    )(ckv, kpe, w_ukv)


def _rope_block(x, cos, sin_a, sin_b, half):
    return x * cos + pltpu.roll(x, LANE - half, 1) * sin_a + pltpu.roll(x, half, 1) * sin_b


def _inproj_kernel(x_ref, w_ref, qn_ref, wuq_ref, kvn_ref, wukv_ref,
                   c32_ref, a32_ref, b32_ref, c64_ref, a64_ref, b64_ref, kdec_ref,
                   qa_ref, ckv_ref, kpe_ref, kf_ref, va_ref,
                   rq_ref, rk_ref, rkd_ref, rv_ref, rg_ref,
                   sq_ref, sk_ref, sv_ref, bq_ref, bk_ref, bv_ref, t_refs=(), keep_refs=()):
    qat_ref, vat_ref, sqt_ref, svt_ref, bqt_ref, bvt_ref = t_refs if t_refs else (None,) * 6
    z = _dot(x_ref[...].astype(BF16), w_ref[...])

    def seg(name):
        o, w = _IN_OFF[name]
        return z[:, o:o + w]

    cq = seg("c_q")
    cqn = cq * lax.rsqrt(jnp.sum(cq * cq, axis=1, keepdims=True) * (1.0 / Q_RANK) + EPS) * qn_ref[...]
    qa = _dot(cqn.astype(BF16), wuq_ref[...])
    scale_a = DQK_A ** -0.5 * LOG2E
    nope_w = N_HEADS * DN_A
    q_pe = _rope_block(qa[:, nope_w:], c32_ref[...], a32_ref[...], b32_ref[...], DR_A // 2)
    qa_s = jnp.concatenate([qa[:, :nope_w], q_pe], axis=1) * scale_a
    qa_ref[...] = qa_s.astype(BF16)
    if qat_ref is not None:
        qat_ref[...] = qa_s.T.astype(BF16)

    ckv_raw = seg("c_kv")
    ckv = ckv_raw * lax.rsqrt(jnp.mean(ckv_raw * ckv_raw, axis=1, keepdims=True) + EPS) * kvn_ref[...]
    ckv_ref[...] = ckv
    kvx = _dot(ckv.astype(BF16), wukv_ref[...])
    kp = _rope_block(seg("k_pe"), c32_ref[...], a32_ref[...], b32_ref[...], DR_A // 2)
    kpe_ref[...] = kp[:, :DR_A]
    kpt = kp + pltpu.roll(kp, DR_A, 1) + pltpu.roll(kp, 2 * DR_A, 1) + pltpu.roll(kp, 3 * DR_A, 1)
    kf_ref[:, :nope_w] = kvx[:, :nope_w].astype(BF16)
    kf_ref[:, nope_w:] = kpt.astype(BF16)
    va_ref[...] = kvx[:, nope_w:].astype(BF16)
    if vat_ref is not None:
        vat_ref[...] = kvx[:, nope_w:].T.astype(BF16)

    rq, rk = seg("rq"), seg("rk")
    kdec = kdec_ref[...]
    for blk in range(BRANCH_W // LANE):
        cols = slice(blk * LANE, (blk + 1) * LANE)
        rq_ref[:, cols] = _rope_block(rq[:, cols], c64_ref[...], a64_ref[...], b64_ref[...],
                                      D_HEAD // 2).astype(BF16)
        rkb = _rope_block(rk[:, cols], c64_ref[...], a64_ref[...], b64_ref[...], D_HEAD // 2) * (D_HEAD ** -0.5)
        rk_ref[:, cols] = rkb.astype(BF16)
        rkd_ref[:, cols] = (rkb * kdec[:, cols]).astype(BF16)
    rv_ref[...] = seg("rv").astype(BF16)
    rg_ref[...] = seg("rg")

    scale_h = D_HEAD ** -0.5 * LOG2E
    sq = seg("sq") * scale_h
    sq_ref[...] = sq.astype(BF16)
    sk_ref[...] = seg("sk")
    sv_ref[...] = seg("sv")
    if sqt_ref is not None:
        sqt_ref[...] = sq.T.astype(BF16)
        svt_ref[...] = seg("sv").T.astype(BF16)
    bq = seg("bq") * scale_h
    bq_ref[...] = bq.astype(BF16)
    bk_ref[...] = seg("bk")
    bv_ref[...] = seg("bv")
    if bqt_ref is not None:
        bqt_ref[...] = bq.T.astype(BF16)
        bvt_ref[...] = seg("bv").T.astype(BF16)
    if keep_refs:
        keep_refs[0][...] = seg("bk")
        keep_refs[1][...] = seg("bv")


_INPROJ_OUT = (
    (QA_W, BF16), (KV_RANK, F32), (DR_A, F32), (QA_W, BF16), (BRANCH_W, BF16),
    (BRANCH_W, BF16), (BRANCH_W, BF16), (BRANCH_W, BF16), (BRANCH_W, BF16), (BRANCH_W, F32),
    (BRANCH_W, BF16), (BRANCH_W, F32), (BRANCH_W, F32), (BRANCH_W, BF16), (BRANCH_W, F32), (BRANCH_W, F32))


_STATE_OUTS = (1, 2, 11, 12, 14, 15)


def _inproj_entry(*refs, n_in, n_alias, n_t, n_keep):
    outs = refs[n_in + n_alias:]
    n_main = len(_INPROJ_OUT)
    _inproj_kernel(*refs[:n_in], *outs[:n_main], t_refs=outs[n_main:n_main + n_t],
                   keep_refs=outs[n_main + n_t:n_main + n_t + n_keep])


def _inproj(x, w_in_p, qn, wuq, kvn, wukv, tables, *, tm, layer, stacked, seq_len=None, keep_rows=None):
    m = x.shape[0]
    tm = min(tm, m)
    assert m % tm == 0
    n_pos_tiles = tables[0].shape[0] // tm
    assert tables[0].shape[0] % tm == 0
    row = lambda i: (i, 0)
    const = lambda i: (0, 0)
    pos = lambda i: (i % n_pos_tiles, 0)
    in_specs = [pl.BlockSpec((tm, D_MODEL), row),
                pl.BlockSpec(w_in_p.shape, const),
                pl.BlockSpec(qn.shape, const),
                pl.BlockSpec(wuq.shape, const),
                pl.BlockSpec(kvn.shape, const),
                pl.BlockSpec(wukv.shape, const)]
    in_specs += [pl.BlockSpec((tm, t.shape[1]), pos) for t in tables]
    n_in = len(in_specs)
    out_specs, out_shape = [], []
    for k, (w, dt) in enumerate(_INPROJ_OUT):
        if k in _STATE_OUTS:
            out_specs.append(pl.BlockSpec((None, tm, w), lambda i: (layer, i, 0)))
            out_shape.append(jax.ShapeDtypeStruct((DEPTH, m, w), dt))
        else:
            out_specs.append(pl.BlockSpec((tm, w), row))
            out_shape.append(jax.ShapeDtypeStruct((m, w), dt))
    if seq_len is not None:
        assert seq_len % tm == 0 and m % seq_len == 0
        tps = seq_len // tm
        for w in (QA_W,) + (BRANCH_W,) * 5:
            out_specs.append(pl.BlockSpec((None, w, tm), lambda i: (i // tps, 0, i % tps)))
            out_shape.append(jax.ShapeDtypeStruct((m // seq_len, w, seq_len), BF16))
    n_t = len(out_specs) - len(_INPROJ_OUT)
    state_outs = list(_STATE_OUTS)
    if keep_rows is not None:
        assert seq_len is not None and keep_rows % tm == 0 and keep_rows <= seq_len
        tps, ktiles = seq_len // tm, keep_rows // tm
        kept = lambda i: (layer, (i // tps) * ktiles + jnp.maximum(i % tps - (tps - ktiles), 0), 0)
        for _ in range(2):
            state_outs.append(len(out_specs))
            out_specs.append(pl.BlockSpec((None, tm, BRANCH_W), kept))
            out_shape.append(jax.ShapeDtypeStruct((DEPTH, m // seq_len * keep_rows, BRANCH_W), F32))
    aliases = {}
    extra = ()
    if stacked is not None:
        extra = tuple(stacked)
        assert len(extra) == len(state_outs)
        in_specs += [pl.BlockSpec(memory_space=pl.ANY)] * len(extra)
        aliases = {n_in + k: o for k, o in enumerate(state_outs)}
    outs = pl.pallas_call(
        functools.partial(_inproj_entry, n_in=n_in, n_alias=len(extra), n_t=n_t,
                          n_keep=len(state_outs) - len(_STATE_OUTS)),
        grid=(m // tm,),
        in_specs=in_specs,
        out_specs=out_specs,
        out_shape=out_shape,
        input_output_aliases=aliases,
        compiler_params=_params(("arbitrary",)),
        name="inproj",
    )(x, w_in_p, qn, wuq, kvn, wukv, *tables, *extra)
    return outs, tuple(outs[k] for k in state_outs)


def _tile_spec(src, rows):
    arr, layer = src
    return pl.BlockSpec((None, None, rows, arr.shape[3]), lambda i, t: (layer, i, t, 0))


def _full_spec(src):
    arr, layer = src
    return pl.BlockSpec((None, None, arr.shape[2], arr.shape[3]), lambda i, t: (layer, i, 0, 0))


def _mla_kernel(q_ref, kfo_ref, vo_ref, kfp_ref, vp_ref, o_ref, *, tq, tk, n_own, n_past):
    qi = pl.program_id(1)
    qm = _stack_heads(q_ref[...], _mla_lanes)
    rows = N_HEADS * tq
    tko = kfo_ref.shape[0]
    n_loop = qi if n_past is None else n_past

    def softmax_pv(s, v, carry):
        m, accs = carry
        m_new = jnp.maximum(m, jnp.max(s, axis=1, keepdims=True))
        pb = jnp.exp2(s - m_new).astype(BF16)
        a = jnp.exp2(m - m_new)
        ones = jnp.ones((v.shape[0], D_HEAD), BF16)
        accs = tuple(a[h * tq:(h + 1) * tq] * accs[h]
                     + _dot(pb[h * tq:(h + 1) * tq], jnp.concatenate([v[:, _head_slice(h)], ones], axis=1))
                     for h in range(N_HEADS))
        return m_new, accs

    def past_scores(j):
        start = pl.multiple_of(jnp.minimum(j, jnp.maximum(n_loop - 1, 0)) * tk, tk)
        return _nt_dot(qm, kfp_ref[pl.ds(start, tk), :])

    row_q = lax.broadcasted_iota(jnp.int32, (rows, tko), 0) % tq
    col = lax.broadcasted_iota(jnp.int32, (rows, tko), 1)
    own_mask = jnp.logical_and(col < n_own, col // CHUNK <= row_q // CHUNK)
    carry = (jnp.full((rows, 1), NEG_INF, F32),
             tuple(jnp.zeros((tq, 2 * D_HEAD), F32) for _ in range(N_HEADS)))
    s_next = past_scores(0)
    carry = softmax_pv(jnp.where(own_mask, _nt_dot(qm, kfo_ref[...]), NEG_INF), vo_ref[...], carry)

    def body(j, c):
        s_cur, rest = c
        s_after = past_scores(j + 1)
        start = pl.multiple_of(j * tk, tk)
        return s_after, softmax_pv(s_cur, vp_ref[pl.ds(start, tk), :], rest)

    _, (_, accs) = lax.fori_loop(0, n_loop, body, (s_next, carry))
    _store_heads(o_ref, [accs[h][:, :D_HEAD] / accs[h][:, D_HEAD:D_HEAD + 1] for h in range(N_HEADS)])


def _mla_attn(q, kf_own, v_own, kf_past, v_past, *, tq, tk, n_own, causal_tiles):
    b, lq, _ = q.shape
    nqt = lq // tq
    tko = kf_own[0].shape[2] // nqt
    lp = kf_past[0].shape[2]
    assert lq % tq == 0 and lp % tk == 0
    return pl.pallas_call(
        functools.partial(_mla_kernel, tq=tq, tk=tk, n_own=n_own,
                          n_past=None if causal_tiles else lp // tk),
        grid=(b, nqt),
        in_specs=[pl.BlockSpec((None, tq, QA_W), lambda i, t: (i, t, 0)),
                  _tile_spec(kf_own, tko), _tile_spec(v_own, tko),
                  _full_spec(kf_past), _full_spec(v_past)],
        out_specs=pl.BlockSpec((None, tq, BRANCH_W), lambda i, t: (i, t, 0)),
        out_shape=jax.ShapeDtypeStruct((b, lq, BRANCH_W), BF16),
        compiler_params=_params(("parallel", "arbitrary")),
        name="mla_attn",
    )(q, kf_own[0], v_own[0], kf_past[0], v_past[0])


def _mla_kernel_t(qt_ref, kfo_ref, vto_ref, kfp_ref, vtp_ref, o_ref, *, tq, tk, nseq):
    qi = pl.program_id(1)
    cols = N_HEADS * tq
    qmts = []
    for s in range(nseq):
        qt = qt_ref[s]
        feat = lax.broadcasted_iota(jnp.int32, qt.shape, 0)
        parts = []
        for h in range(N_HEADS):
            keep = None
            for lo, hi in _mla_lanes(h):
                m = jnp.logical_and(feat >= lo, feat < hi)
                keep = m if keep is None else jnp.logical_or(keep, m)
            parts.append(jnp.where(keep, qt, jnp.zeros_like(qt)))
        qmts.append(jnp.concatenate(parts, axis=1))

    def softmax_pv(st, vt, carry):
        m, l, accs = carry
        m_new = jnp.maximum(m, jnp.max(st, axis=0, keepdims=True))
        pt = jnp.exp2(st - m_new)
        a = jnp.exp2(m - m_new)
        l = a * l + jnp.sum(pt, axis=0, keepdims=True)
        pb = pt.astype(BF16)
        accs = tuple(a[:, h * tq:(h + 1) * tq] * accs[h]
                     + _dot(vt[_head_slice(h), :], pb[:, h * tq:(h + 1) * tq]) for h in range(N_HEADS))
        return m_new, l, accs

    def past_scores(s, j):
        start = pl.multiple_of(jnp.minimum(j, jnp.maximum(qi - 1, 0)) * tk, tk)
        return _dot(kfp_ref[s, pl.ds(start, tk), :], qmts[s])

    key = lax.broadcasted_iota(jnp.int32, (tq, cols), 0)
    qry = lax.broadcasted_iota(jnp.int32, (tq, cols), 1) % tq
    own_mask = key // CHUNK <= qry // CHUNK
    s_next, carries = [], []
    for s in range(nseq):
        empty = (jnp.full((1, cols), NEG_INF, F32), jnp.zeros((1, cols), F32),
                 tuple(jnp.zeros((D_HEAD, tq), F32) for _ in range(N_HEADS)))
        s_next.append(past_scores(s, 0))
        carries.append(softmax_pv(jnp.where(own_mask, _dot(kfo_ref[s], qmts[s]), NEG_INF), vto_ref[s], empty))

    def body(j, c):
        s_cur, rest = c
        start = pl.multiple_of(j * tk, tk)
        s_after = tuple(past_scores(s, j + 1) for s in range(nseq))
        return s_after, tuple(softmax_pv(s_cur[s], vtp_ref[s, :, pl.ds(start, tk)], rest[s])
                              for s in range(nseq))

    _, done = lax.fori_loop(0, qi, body, (tuple(s_next), tuple(carries)))
    for s in range(nseq):
        _, l, accs = done[s]
        out_t = jnp.concatenate([accs[h] / l[:, h * tq:(h + 1) * tq] for h in range(N_HEADS)], axis=0)
        o_ref[s] = out_t.T.astype(o_ref.dtype)


def _mla_attn_t(qt, kf, vt, *, tq, nseq):
    b, _, length = qt.shape
    assert length % tq == 0 and b % nseq == 0
    return pl.pallas_call(
        functools.partial(_mla_kernel_t, tq=tq, tk=tq, nseq=nseq),
        grid=(b // nseq, length // tq),
        in_specs=[pl.BlockSpec((nseq, QA_W, tq), lambda i, t: (i, 0, t)),
                  pl.BlockSpec((nseq, tq, QA_W), lambda i, t: (i, t, 0)),
                  pl.BlockSpec((nseq, BRANCH_W, tq), lambda i, t: (i, 0, t)),
                  pl.BlockSpec((nseq, length, QA_W), lambda i, t: (i, 0, 0)),
                  pl.BlockSpec((nseq, BRANCH_W, length), lambda i, t: (i, 0, 0))],
        out_specs=pl.BlockSpec((nseq, tq, BRANCH_W), lambda i, t: (i, t, 0)),
        out_shape=jax.ShapeDtypeStruct((b, length, BRANCH_W), BF16),
        compiler_params=_params(("parallel", "arbitrary")),
        name="mla_attn_t",
    )(qt, kf, vt, kf, vt)


def _sb_kernel(q_ref, ko_ref, vo_ref, kp_ref, vp_ref, o_ref, *, tq, tk, n_own, n_past):
    qi = pl.program_id(1)
    qm = _stack_heads(q_ref[...], _own_lanes)
    rows = N_HEADS * tq
    tko = ko_ref.shape[0]
    n_loop = qi if n_past is None else n_past

    def tri2(n):
        r = lax.broadcasted_iota(jnp.int32, (2 * n, n), 0) % n
        c = lax.broadcasted_iota(jnp.int32, (2 * n, n), 1)
        return jnp.where(r > c, 1.0, 0.0).astype(BF16)

    def weigh(z, v, carry, mask, tri):
        run, accs = carry
        neg_abs = lax.bitcast_convert_type(
            lax.bitcast_convert_type(z, jnp.uint32) | jnp.uint32(0x80000000), F32)
        t = jnp.log2(1.0 + jnp.exp2(neg_abs))
        log_beta = jnp.minimum(z, 0.0) - t
        log_stay = log_beta - z
        if mask is not None:
            log_stay = jnp.where(mask, log_stay, 0.0)
        hi = log_stay.astype(BF16)
        lo = (log_stay - hi.astype(F32)).astype(BF16)
        later = _dot(jnp.concatenate([hi, lo], axis=1), tri) + run
        w = jnp.exp2(log_beta + later)
        if mask is not None:
            w = jnp.where(mask, w, 0.0)
        wb = w.astype(BF16)
        vb = v.astype(BF16)
        accs = tuple(accs[h] + _dot(wb[h * tq:(h + 1) * tq], vb[:, _head_slice(h)])
                     for h in range(N_HEADS))
        return run + jnp.sum(log_stay, axis=1, keepdims=True), accs

    row_q = lax.broadcasted_iota(jnp.int32, (rows, tko), 0) % tq
    col = lax.broadcasted_iota(jnp.int32, (rows, tko), 1)
    own_mask = jnp.logical_and(col < n_own, col < row_q)

    def past_start(jj):
        return pl.multiple_of(jnp.clip(n_loop - 1 - jj, 0, kp_ref.shape[0] // tk - 1) * tk, tk)

    def past_scores(jj):
        return _nt_dot(qm, kp_ref[pl.ds(past_start(jj), tk), :].astype(BF16))

    def alive(run):
        return (jnp.max(run) > SB_DEAD_LOG2).astype(jnp.int32)

    z_next = past_scores(0)
    carry = (jnp.zeros((rows, 1), F32), tuple(jnp.zeros((tq, D_HEAD), F32) for _ in range(N_HEADS)))
    run, accs = weigh(_nt_dot(qm, ko_ref[...].astype(BF16)), vo_ref[...], carry, own_mask, tri2(tko))
    tri_past = tri2(tk)

    def cond(c):
        return jnp.logical_and(c[0] < n_loop, c[1] > 0)

    def body(c):
        jj, _, z_cur, run, accs = c
        z_after = past_scores(jj + 1)
        run, accs = weigh(z_cur, vp_ref[pl.ds(past_start(jj), tk), :], (run, accs), None, tri_past)
        return jj + 1, alive(run), z_after, run, accs

    out = lax.while_loop(cond, body, (jnp.int32(0), alive(run), z_next, run, accs))
    _store_heads(o_ref, out[4])


def _sb_attn(q, k_own, v_own, k_past, v_past, *, tq, tk, n_own, causal_tiles):
    b, lq, _ = q.shape
    nqt = lq // tq
    tko = k_own[0].shape[2] // nqt
    lp = k_past[0].shape[2]
    assert lq % tq == 0 and lp % tk == 0
    return pl.pallas_call(
        functools.partial(_sb_kernel, tq=tq, tk=tk, n_own=n_own,
                          n_past=None if causal_tiles else lp // tk),
        grid=(b, nqt),
        in_specs=[pl.BlockSpec((None, tq, BRANCH_W), lambda i, t: (i, t, 0)),
                  _tile_spec(k_own, tko), _tile_spec(v_own, tko),
                  _full_spec(k_past), _full_spec(v_past)],
        out_specs=pl.BlockSpec((None, tq, BRANCH_W), lambda i, t: (i, t, 0)),
        out_shape=jax.ShapeDtypeStruct((b, lq, BRANCH_W), BF16),
        compiler_params=_params(("parallel", "arbitrary")),
        name="sb_attn",
    )(q, k_own[0], v_own[0], k_past[0], v_past[0])


def _sb_kernel_t(qt_ref, ko_ref, vto_ref, kp_ref, vtp_ref, o_ref, *, tq, tk):
    qi = pl.program_id(1)
    qt = qt_ref[...]
    feat = lax.broadcasted_iota(jnp.int32, qt.shape, 0)
    qmt = jnp.concatenate(
        [jnp.where(jnp.logical_and(feat >= h * D_HEAD, feat < (h + 1) * D_HEAD), qt, jnp.zeros_like(qt))
         for h in range(N_HEADS)], axis=1)
    cols = N_HEADS * tq

    def tri2(n):
        r = lax.broadcasted_iota(jnp.int32, (n, 2 * n), 0)
        c = lax.broadcasted_iota(jnp.int32, (n, 2 * n), 1) % n
        return jnp.where(c > r, 1.0, 0.0).astype(BF16)

    def weigh(zt, vt, carry, mask, tri):
        run, accs = carry
        neg_abs = lax.bitcast_convert_type(
            lax.bitcast_convert_type(zt, jnp.uint32) | jnp.uint32(0x80000000), F32)
        t = jnp.log2(1.0 + jnp.exp2(neg_abs))
        log_beta = jnp.minimum(zt, 0.0) - t
        log_stay = log_beta - zt
        if mask is not None:
            log_stay = jnp.where(mask, log_stay, 0.0)
        hi = log_stay.astype(BF16)
        lo = (log_stay - hi.astype(F32)).astype(BF16)
        later = _dot(tri, jnp.concatenate([hi, lo], axis=0)) + run
        w = jnp.exp2(log_beta + later)
        if mask is not None:
            w = jnp.where(mask, w, 0.0)
        wb = w.astype(BF16)
        accs = tuple(accs[h] + _dot(vt[_head_slice(h), :], wb[:, h * tq:(h + 1) * tq])
                     for h in range(N_HEADS))
        return run + jnp.sum(log_stay, axis=0, keepdims=True), accs

    def past_start(jj):
        return pl.multiple_of(jnp.clip(qi - 1 - jj, 0, kp_ref.shape[0] // tk - 1) * tk, tk)

    def past_scores(jj):
        return _dot(kp_ref[pl.ds(past_start(jj), tk), :].astype(BF16), qmt)

    def alive(run):
        return (jnp.max(run) > SB_DEAD_LOG2).astype(jnp.int32)

    key = lax.broadcasted_iota(jnp.int32, (tq, cols), 0)
    qry = lax.broadcasted_iota(jnp.int32, (tq, cols), 1) % tq
    own_mask = key < qry
    z_next = past_scores(0)
    carry = (jnp.zeros((1, cols), F32), tuple(jnp.zeros((D_HEAD, tq), F32) for _ in range(N_HEADS)))
    run, accs = weigh(_dot(ko_ref[...].astype(BF16), qmt), vto_ref[...], carry, own_mask, tri2(tq))
    tri_past = tri2(tk)

    def cond(c):
        return jnp.logical_and(c[0] < qi, c[1] > 0)

    def body(c):
        jj, _, z_cur, run, accs = c
        z_after = past_scores(jj + 1)
        run, accs = weigh(z_cur, vtp_ref[:, pl.ds(past_start(jj), tk)], (run, accs), None, tri_past)
        return jj + 1, alive(run), z_after, run, accs

    out = lax.while_loop(cond, body, (jnp.int32(0), alive(run), z_next, run, accs))
    o_ref[...] = jnp.concatenate(out[4], axis=0).T.astype(o_ref.dtype)


def _sb_attn_t(qt, k, vt, *, tq):
    b, _, length = qt.shape
    assert length % tq == 0
    return pl.pallas_call(
        functools.partial(_sb_kernel_t, tq=tq, tk=tq),
        grid=(b, length // tq),
        in_specs=[pl.BlockSpec((None, BRANCH_W, tq), lambda i, t: (i, 0, t)),
                  _tile_spec(k, tq),
                  pl.BlockSpec((None, BRANCH_W, tq), lambda i, t: (i, 0, t)),
                  _full_spec(k),
                  pl.BlockSpec((None, BRANCH_W, length), lambda i, t: (i, 0, 0))],
        out_specs=pl.BlockSpec((None, tq, BRANCH_W), lambda i, t: (i, t, 0)),
        out_shape=jax.ShapeDtypeStruct((b, length, BRANCH_W), BF16),
        compiler_params=_params(("parallel", "arbitrary")),
        name="sb_attn_t",
    )(qt, k[0], vt, k[0], vt)


def _ret_kernel(q_ref, k_ref, kd_ref, v_ref, rg_ref, s0_ref, dec_ref, qdec_ref, gl_ref,
                gng_ref, gnb_ref, o_ref, sout_ref, state_ref, *, lc, nseq):
    c = pl.program_id(1)

    @pl.when(c == 0)
    def _():
        state_ref[...] = s0_ref[...]

    for s in range(nseq):
        qm = _stack_heads(q_ref[s], _own_lanes)
        v = v_ref[s]
        state = state_ref[s]
        scores = (_nt_dot(qm, k_ref[s]) * dec_ref[...]).astype(BF16)
        cross = _dot(qm, state.astype(BF16)) * qdec_ref[...]
        kv_full = _tn_dot(kd_ref[s], v)
        new_state = gl_ref[...] * state + jnp.concatenate(
            [kv_full[_head_slice(h), _head_slice(h)] for h in range(N_HEADS)], axis=0)
        state_ref[s] = new_state

        rg = rg_ref[s]
        for h in range(N_HEADS):
            o = _dot(scores[h * lc:(h + 1) * lc], v[:, _head_slice(h)]) + cross[h * lc:(h + 1) * lc]
            mu = jnp.mean(o, axis=-1, keepdims=True)
            d = o - mu
            var = jnp.mean(d * d, axis=-1, keepdims=True)
            y = d * lax.rsqrt(var + EPS) * gng_ref[h] + gnb_ref[h]
            g = rg[:, _head_slice(h)]
            o_ref[s, :, _head_slice(h)] = (y * (g * _sigmoid(g))).astype(o_ref.dtype)

    @pl.when(c == pl.num_programs(1) - 1)
    def _():
        sout_ref[...] = state_ref[...]


def _retention(q, k, kd, v, rg, s0, dec, qdec, gl, gng, gnb, *, lc, nseq):
    b, length, _ = q.shape
    assert length % lc == 0 and b % nseq == 0
    seq = lambda i, t: (i, t, 0)
    st = lambda i, t: (i, 0, 0)
    c2 = lambda i, t: (0, 0)
    c3 = lambda i, t: (0, 0, 0)
    return pl.pallas_call(
        functools.partial(_ret_kernel, lc=lc, nseq=nseq),
        grid=(b // nseq, length // lc),
        in_specs=[pl.BlockSpec((nseq, lc, BRANCH_W), seq)] * 5
        + [pl.BlockSpec((nseq, BRANCH_W, D_HEAD), st),
           pl.BlockSpec(dec.shape, c2), pl.BlockSpec(qdec.shape, c2), pl.BlockSpec(gl.shape, c2),
           pl.BlockSpec(gng.shape, c3), pl.BlockSpec(gnb.shape, c3)],
        out_specs=[pl.BlockSpec((nseq, lc, BRANCH_W), seq),
                   pl.BlockSpec((nseq, BRANCH_W, D_HEAD), st)],
        out_shape=[jax.ShapeDtypeStruct((b, length, BRANCH_W), BF16),
                   jax.ShapeDtypeStruct((b, BRANCH_W, D_HEAD), F32)],
        scratch_shapes=[pltpu.VMEM((nseq, BRANCH_W, D_HEAD), F32)],
        compiler_params=_params(("parallel", "arbitrary")),
        name="retention",
    )(q, k, kd, v, rg, s0, dec, qdec, gl, gng, gnb)


def _band_kernel(q_ref, k_ref, v_ref, bias_ref, o_ref, *, tq, win, back):
    qi = pl.program_id(1)
    start = pl.multiple_of(jnp.maximum(qi - back, 0) * tq, tq)
    k = k_ref[pl.ds(start, win), :].astype(BF16)
    v = v_ref[pl.ds(start, win), :].astype(BF16)
    qm = _stack_heads(q_ref[...], _own_lanes)
    s = _nt_dot(qm, k) + bias_ref[...]
    pb = jnp.exp2(s - jnp.max(s, axis=1, keepdims=True)).astype(BF16)
    ones = jnp.ones((win, D_HEAD), BF16)
    outs = []
    for h in range(N_HEADS):
        o = _dot(pb[h * tq:(h + 1) * tq], jnp.concatenate([v[:, _head_slice(h)], ones], axis=1))
        outs.append(o[:, :D_HEAD] / o[:, D_HEAD:D_HEAD + 1])
    _store_heads(o_ref, outs)


def _band_attn(q, k, v, bias, *, tq, win, back):
    b, lq, _ = q.shape
    nvar = bias.shape[0]
    assert lq % tq == 0
    return pl.pallas_call(
        functools.partial(_band_kernel, tq=tq, win=win, back=back),
        grid=(b, lq // tq),
        in_specs=[pl.BlockSpec((None, tq, BRANCH_W), lambda i, t: (i, t, 0)),
                  _full_spec(k), _full_spec(v),
                  pl.BlockSpec((None, N_HEADS * tq, win), lambda i, t: (jnp.minimum(t, nvar - 1), 0, 0))],
        out_specs=pl.BlockSpec((None, tq, BRANCH_W), lambda i, t: (i, t, 0)),
        out_shape=jax.ShapeDtypeStruct((b, lq, BRANCH_W), BF16),
        compiler_params=_params(("parallel", "arbitrary")),
        name="band_attn",
    )(q, k[0], v[0], bias)


def _band_kernel_t(qt_ref, k_ref, vt_ref, bias_ref, o_ref, *, tq, win, back, nseq):
    qi = pl.program_id(1)
    start = pl.multiple_of(jnp.maximum(qi - back, 0) * tq, tq)
    for s in range(nseq):
        k = k_ref[s, pl.ds(start, win), :].astype(BF16)
        vt = vt_ref[s, :, pl.ds(start, win)]
        qt = qt_ref[s]
        feat = lax.broadcasted_iota(jnp.int32, qt.shape, 0)
        qmt = jnp.concatenate(
            [jnp.where(jnp.logical_and(feat >= h * D_HEAD, feat < (h + 1) * D_HEAD), qt, jnp.zeros_like(qt))
             for h in range(N_HEADS)], axis=1)
        st = _dot(k, qmt) + bias_ref[...]
        pt = jnp.exp2(st - jnp.max(st, axis=0, keepdims=True))
        l = jnp.sum(pt, axis=0, keepdims=True)
        pb = pt.astype(BF16)
        out_t = jnp.concatenate(
            [_dot(vt[_head_slice(h), :], pb[:, h * tq:(h + 1) * tq]) / l[:, h * tq:(h + 1) * tq]
             for h in range(N_HEADS)], axis=0)
        o_ref[s] = out_t.T.astype(o_ref.dtype)


def _band_attn_t(qt, k, vt, bias_t, *, tq, win, back, nseq):
    b, _, length = qt.shape
    nvar = bias_t.shape[0]
    k_arr, layer = k
    assert length % tq == 0 and b % nseq == 0
    return pl.pallas_call(
        functools.partial(_band_kernel_t, tq=tq, win=win, back=back, nseq=nseq),
        grid=(b // nseq, length // tq),
        in_specs=[pl.BlockSpec((nseq, BRANCH_W, tq), lambda i, t: (i, 0, t)),
                  pl.BlockSpec((None, nseq, k_arr.shape[2], BRANCH_W), lambda i, t: (layer, i, 0, 0)),
                  pl.BlockSpec((nseq, BRANCH_W, length), lambda i, t: (i, 0, 0)),
                  pl.BlockSpec((None, win, N_HEADS * tq), lambda i, t: (jnp.minimum(t, nvar - 1), 0, 0))],
        out_specs=pl.BlockSpec((nseq, tq, BRANCH_W), lambda i, t: (i, t, 0)),
        out_shape=jax.ShapeDtypeStruct((b, length, BRANCH_W), BF16),
        compiler_params=_params(("parallel", "arbitrary")),
        name="band_attn_t",
    )(qt, k_arr, vt, bias_t)


def _merge_kernel(x_ref, ba_ref, bb_ref, bc_ref, bd_ref, wg_ref, wb_ref, wo_ref, g_ref, b_ref, o_ref):
    x = x_ref[...]
    xb = x.astype(BF16)
    merged = None
    for n, br_ref in enumerate((ba_ref, bb_ref, bc_ref, bd_ref)):
        logits = _dot(xb, wg_ref[:, n * D_MODEL:(n + 1) * D_MODEL])
        term = _dot(br_ref[...], wb_ref[n]) * _sigmoid(logits)
        merged = term if merged is None else merged + term
    mix = _dot(merged.astype(BF16), wo_ref[...])
    o_ref[...] = _layer_norm(ALPHA * x + mix, g_ref[...], b_ref[...])


def _merge(x, branches, wg, wb, wo, g, b, *, tm):
    m = x.shape[0]
    tm = min(tm, m)
    assert m % tm == 0
    const2 = lambda i: (0, 0)
    row = lambda i: (i, 0)
    return pl.pallas_call(
        _merge_kernel,
        grid=(m // tm,),
        in_specs=[pl.BlockSpec((tm, D_MODEL), row)]
        + [pl.BlockSpec((tm, BRANCH_W), row)] * N_BRANCH
        + [pl.BlockSpec((D_MODEL, N_BRANCH * D_MODEL), const2),
           pl.BlockSpec((N_BRANCH, BRANCH_W, D_MODEL), lambda i: (0, 0, 0)),
           pl.BlockSpec((D_MODEL, D_MODEL), const2),
           pl.BlockSpec((1, D_MODEL), const2),
           pl.BlockSpec((1, D_MODEL), const2)],
        out_specs=pl.BlockSpec((tm, D_MODEL), row),
        out_shape=jax.ShapeDtypeStruct((m, D_MODEL), F32),
        compiler_params=_params(("parallel",)),
        name="merge",
    )(x, *branches, wg, wb, wo, g, b)


def _route(aff_t, sel_t):
    def top2_sum(a, b, c, d):
        hi1, lo1 = jnp.maximum(a, b), jnp.minimum(a, b)
        hi2, lo2 = jnp.maximum(c, d), jnp.minimum(c, d)
        return jnp.maximum(hi1, hi2) + jnp.maximum(jnp.minimum(hi1, hi2), jnp.maximum(lo1, lo2))

    score = [top2_sum(*sel_t[g * EXPERTS_PER_GROUP:(g + 1) * EXPERTS_PER_GROUP])
             for g in range(N_GROUPS)]
    best_here = []
    for g in range(N_GROUPS):
        ok = None
        for o in range(N_GROUPS):
            if o == g:
                continue
            c = (score[g] > score[o]) if o < g else (score[g] >= score[o])
            ok = c if ok is None else jnp.logical_and(ok, c)
        best_here.append(ok)
    picked = []
    for e in range(N_EXPERTS):
        g = e // EXPERTS_PER_GROUP
        rank = jnp.zeros_like(sel_t[e])
        for o in range(g * EXPERTS_PER_GROUP, (g + 1) * EXPERTS_PER_GROUP):
            if o == e:
                continue
            ahead = (sel_t[o] >= sel_t[e]) if o < e else (sel_t[o] > sel_t[e])
            rank = rank + jnp.where(ahead, 1.0, 0.0)
        picked.append(jnp.where(jnp.logical_and(best_here[g], rank < TOP_K), aff_t[e], 0.0))
    total = picked[0]
    for e in range(1, N_EXPERTS):
        total = total + picked[e]
    return [p / total for p in picked]


def _moe_kernel(x_ref, wrh_ref, wrl_ref, br_ref, wg_ref, wu_ref, wd_ref, g_ref, b_ref, o_ref, acc_ref):
    x = x_ref[...]
    xh = x.astype(BF16)
    xl = (x - xh.astype(F32)).astype(BF16)
    wrh = wrh_ref[...]
    logits = _dot(xh, wrh) + _dot(xl, wrh) + _dot(xh, wrl_ref[...])
    aff = _sigmoid(logits).T
    bias = br_ref[...]
    aff_t = [aff[e:e + 1, :] for e in range(N_EXPERTS)]
    sel_t = [aff_t[e] + bias[e:e + 1, :] for e in range(N_EXPERTS)]
    gate_rows = _route(aff_t, sel_t)
    tm = x.shape[0]
    gate_t = jnp.concatenate(gate_rows + [jnp.zeros((LANE - N_EXPERTS, tm), F32)], axis=0)
    gate = gate_t.T

    for e in range(N_EXPERTS):
        gt = _dot(xh, wg_ref[e])
        hmid = gt * _sigmoid(gt) * _dot(xh, wu_ref[e])
        y = _dot(hmid.astype(BF16), wd_ref[e]) * gate[:, e:e + 1]
        if e == 0:
            acc_ref[...] = y
        else:
            acc_ref[...] += y
    o_ref[...] = _layer_norm(ALPHA * x + acc_ref[...], g_ref[...], b_ref[...])


def _moe(x, wrh, wrl, br, wg, wu, wd, g, b, *, tm, layer):
    m = x.shape[0]
    tm = min(tm, m)
    assert m % tm == 0
    const2 = lambda i: (0, 0)
    this_layer = lambda i: (layer, 0, 0, 0)
    return pl.pallas_call(
        _moe_kernel,
        grid=(m // tm,),
        in_specs=[pl.BlockSpec((tm, D_MODEL), lambda i: (i, 0)),
                  pl.BlockSpec((D_MODEL, LANE), const2),
                  pl.BlockSpec((D_MODEL, LANE), const2),
                  pl.BlockSpec((LANE, 1), const2),
                  pl.BlockSpec((None, N_EXPERTS, D_MODEL, D_EXPERT), this_layer, pipeline_mode=pl.Buffered(1)),
                  pl.BlockSpec((None, N_EXPERTS, D_MODEL, D_EXPERT), this_layer, pipeline_mode=pl.Buffered(1)),
                  pl.BlockSpec((None, N_EXPERTS, D_EXPERT, D_MODEL), this_layer, pipeline_mode=pl.Buffered(1)),
                  pl.BlockSpec((1, D_MODEL), const2),
                  pl.BlockSpec((1, D_MODEL), const2)],
        out_specs=pl.BlockSpec((tm, D_MODEL), lambda i: (i, 0)),
        out_shape=jax.ShapeDtypeStruct((m, D_MODEL), F32),
        scratch_shapes=[pltpu.VMEM((tm, D_MODEL), F32)],
        compiler_params=_params(("parallel",)),
        name="moe",
    )(x, wrh, wrl, br, wg, wu, wd, g, b)


def _rope_tables(pos, d):
    half = d // 2
    inv = jnp.power(ROPE_BASE, -jnp.arange(half, dtype=F32) / half)
    ang = pos.astype(F32)[:, None] * inv[None, :]
    cos, sin = jnp.cos(ang), jnp.sin(ang)
    zero = jnp.zeros_like(sin)
    rep = LANE // d
    cos_t = jnp.tile(jnp.concatenate([cos, cos], axis=1), (1, rep))
    sin_a = jnp.tile(jnp.concatenate([-sin, zero], axis=1), (1, rep))
    sin_b = jnp.tile(jnp.concatenate([zero, sin], axis=1), (1, rep))
    return cos_t, sin_a, sin_b


def _retention_tables(lc):
    log_g = jnp.log1p(-jnp.exp2(-5.0 - jnp.arange(N_HEADS, dtype=F32)))
    i = jnp.arange(lc, dtype=F32)
    diff = i[:, None] - i[None, :]
    dec = jnp.where(diff >= 0, jnp.exp(jnp.maximum(diff, 0.0)[None] * log_g[:, None, None]), 0.0)
    qdec = jnp.exp((i[None, :] + 1.0) * log_g[:, None])
    kdec = jnp.exp((lc - 1.0 - i)[None, :] * log_g[:, None])
    gl = jnp.exp(lc * log_g)
    dec = dec.reshape(N_HEADS * lc, lc)
    qdec = jnp.broadcast_to(qdec[:, :, None], (N_HEADS, lc, D_HEAD)).reshape(N_HEADS * lc, D_HEAD)
    gl = jnp.broadcast_to(gl[:, None, None], (N_HEADS, D_HEAD, D_HEAD)).reshape(BRANCH_W, D_HEAD)
    kdec = jnp.repeat(kdec.T, D_HEAD, axis=1)
    return dec, qdec, kdec, gl


def _band_bias(rel_bias, tq, win, q_minus_k0, valid):
    length = tq + win - 1
    d = np.arange(length) - (tq - 1) - q_minus_k0
    idx = np.clip(d, -REL_CLIP, REL_CLIP) + REL_CLIP
    g = rel_bias[:, idx].astype(F32) * LOG2E
    gp = jnp.concatenate([g, jnp.zeros((N_HEADS, 1), F32)], axis=1)
    m = jnp.tile(gp, (1, tq))[:, :tq * length].reshape(N_HEADS, tq, length)
    tile = m[:, :, tq - 1:tq - 1 + win]
    return jnp.where(valid[None], tile, NEG_INF).reshape(N_HEADS * tq, win)


def _pack_w_in(w_in_l):
    cols = []
    src = 0
    for w in _IN_WIDTH:
        seg = w_in_l[:, src:src + w]
        cols.append(jnp.pad(seg, ((0, 0), (0, _round_up(w, LANE) - w))))
        src += w
    return jnp.concatenate(cols, axis=1).astype(BF16)


def _pad_rows(t, n):
    return jnp.pad(t, ((0, 0), (0, n - t.shape[1]), (0, 0)))


def _token_mixers(x, pos0, past, lw, *, prompt, layer, stacked):
    (w_in_p, qn, wuq, kvn, wukv, gn_g, gn_b, rel_bias, b, length) = lw
    assert pos0 % CHUNK == 0
    pos = pos0 + jnp.arange(length)
    lc = 256 if prompt else length
    dec, qdec, kdec, gl = _retention_tables(lc)
    tables = list(_rope_tables(pos, DR_A)) + list(_rope_tables(pos, D_HEAD)) + [jnp.tile(kdec, (length // lc, 1))]
    if not prompt:
        tables = [jnp.tile(t, (b, 1)) for t in tables]
    keep = min(PREV_CHUNKS * CHUNK, length)
    outs, stacked = _inproj(x, w_in_p, qn, wuq, kvn, wukv, tables, tm=TOKEN_TILE if prompt else b * length,
                            layer=layer, stacked=stacked, seq_len=length if prompt else None,
                            keep_rows=keep if prompt and keep < length else None)
    per_batch = lambda o: o.reshape(o.shape[:-2] + (b, length, o.shape[-1]))
    (q_a, ckv, kpe, kf, v_a, rq, rk, rkd, rv, rg, sq, sk, sv, bq, bk, bv) = [
        per_batch(o) for o in outs[:len(_INPROJ_OUT)]]
    here = lambda t: (t, layer)
    only = lambda t: (t[None], 0)

    if prompt:
        tq = 256
        qa_t, va_t, sq_t, sv_t, bq_t, bv_t = outs[len(_INPROJ_OUT):len(_INPROJ_OUT) + 6]
        nseq = SEQS_PER_STEP if b % SEQS_PER_STEP == 0 else 1
        o_a = _mla_attn_t(qa_t, kf, va_t, tq=tq, nseq=nseq)
        o_c = _sb_attn_t(sq_t, here(sk), sv_t, tq=tq)
        s0 = jnp.zeros((b, BRANCH_W, D_HEAD), F32)
        win = 3 * tq
        i = np.arange(tq)[:, None]
        c = np.arange(win)[None, :]
        variants = []
        for t in range(3):
            qc, kc = i // CHUNK + t * (tq // CHUNK), c // CHUNK
            variants.append(_band_bias(rel_bias, tq, win, t * tq, (kc <= qc) & (kc >= qc - PREV_CHUNKS)))
        bias_t = jnp.stack(variants).transpose(0, 2, 1)
        o_d = _band_attn_t(bq_t, here(bk), bv_t, bias_t, tq=tq, win=win, back=2, nseq=nseq)
    else:
        c_ckv, c_kpe, s_prev, c_sk, c_sv, c_bk, c_bv = past
        n_past = c_ckv.shape[2]
        tko = LANE
        kf_c, v_c = _expand_latent(c_ckv.reshape(DEPTH, b * n_past, KV_RANK),
                                   c_kpe.reshape(DEPTH, b * n_past, DR_A), wukv, layer, TOKEN_TILE)
        o_a = _mla_attn(q_a, only(_pad_rows(kf, tko)), only(_pad_rows(v_a, tko)),
                        only(kf_c.reshape(b, n_past, QA_W)), only(v_c.reshape(b, n_past, BRANCH_W)),
                        tq=length, tk=256, n_own=length, causal_tiles=False)
        o_c = _sb_attn(sq, only(_pad_rows(sk[layer], tko)), only(_pad_rows(sv[layer], tko)),
                       (c_sk.reshape(DEPTH, b, n_past, BRANCH_W), layer),
                       (c_sv.reshape(DEPTH, b, n_past, BRANCH_W), layer),
                       tq=length, tk=256, n_own=length, causal_tiles=False)
        s0 = s_prev[layer].reshape(b, BRANCH_W, D_HEAD)
        n_band = c_bk.shape[2]
        n_keys = n_band + length
        win = _round_up(n_keys, LANE)
        bk_all = _pad_rows(jnp.concatenate([c_bk[layer].reshape(b, n_band, BRANCH_W), bk[layer]], axis=1), win)
        bv_all = _pad_rows(jnp.concatenate([c_bv[layer].reshape(b, n_band, BRANCH_W), bv[layer]], axis=1), win)
        k_pos = pos0 - n_band + np.arange(win)
        q_pos = pos0 + np.arange(length)
        qc, kc = q_pos[:, None] // CHUNK, k_pos[None, :] // CHUNK
        valid = (np.arange(win)[None, :] < n_keys) & (k_pos[None, :] >= 0) & (kc <= qc) & (kc >= qc - PREV_CHUNKS)
        bias = _band_bias(rel_bias, length, win, n_band, valid)[None]
        o_d = _band_attn(bq, only(bk_all), only(bv_all), bias, tq=length, win=win, back=0)

    o_r, s_ret = _retention(rq, rk, rkd, rv, rg, s0, dec, qdec, gl,
                            gn_g.reshape(N_HEADS, 1, D_HEAD), gn_b.reshape(N_HEADS, 1, D_HEAD), lc=lc,
                            nseq=SEQS_PER_STEP if b % SEQS_PER_STEP == 0 else 1)
    s_ret = s_ret.reshape(b, N_HEADS, D_HEAD, D_HEAD)
    flat = lambda t: t.reshape(b * length, BRANCH_W)
    return (flat(o_a), flat(o_r), flat(o_c), flat(o_d)), s_ret, stacked


def _state_outputs(stacked, s_ret, b, length):
    per_seq = lambda t: t.reshape(DEPTH, b, t.shape[1] // b, t.shape[-1])
    heads4 = lambda t: per_seq(t).reshape(DEPTH, b, t.shape[1] // b, N_HEADS, D_HEAD)
    ckv, kpe, sk, sv, bk, bv = stacked[:6]
    if len(stacked) > 6:
        bk, bv = stacked[6:]
    return (per_seq(ckv), per_seq(kpe), jnp.stack(s_ret, axis=0), heads4(sk), heads4(sv), heads4(bk), heads4(bv))


def kernel(x_prompt, x_sample, cache_mla_ckv, cache_mla_kpe, state_ret, cache_sb_k, cache_sb_v, cache_band_k, cache_band_v, w_in, mla_q_norm, mla_w_uq, mla_kv_norm, mla_w_ukv, ret_gn_g, ret_gn_b, band_rel_bias, w_branch, w_o, ln1_g, ln1_b, w_router, b_router, w_exp_gate, w_exp_up, w_exp_down, ln2_g, ln2_b):
    bp, lp, _ = x_prompt.shape
    bs, ls, _ = x_sample.shape
    past_len = cache_mla_ckv.shape[2]
    xp = x_prompt.reshape(bp * lp, D_MODEL)
    xs = x_sample.reshape(bs * ls, D_MODEL)

    wr = jnp.pad(w_router, ((0, 0), (0, LANE - N_EXPERTS)))
    wrh = wr.astype(BF16)
    wrl = (wr - wrh.astype(F32)).astype(BF16)
    br = jnp.pad(b_router, (0, LANE - N_EXPERTS)).reshape(LANE, 1)

    we_gate, we_up, we_down = w_exp_gate.astype(BF16), w_exp_up.astype(BF16), w_exp_down.astype(BF16)
    past = (cache_mla_ckv, cache_mla_kpe, state_ret, cache_sb_k, cache_sb_v, cache_band_k, cache_band_v)
    ret_p, ret_s = [], []
    stacked_p = stacked_s = None
    for l in range(DEPTH):
        wuq = mla_w_uq[l]
        wuq = jnp.concatenate([wuq[:, :, :DN_A].reshape(Q_RANK, -1), wuq[:, :, DN_A:].reshape(Q_RANK, -1)], axis=1)
        wuq = jnp.pad(wuq, ((0, _round_up(Q_RANK, LANE) - Q_RANK), (0, 0))).astype(BF16)
        wukv = mla_w_ukv[l]
        wukv = jnp.concatenate([wukv[:, :, :DN_A].reshape(KV_RANK, -1), wukv[:, :, DN_A:].reshape(KV_RANK, -1)],
                               axis=1).astype(BF16)
        qn = jnp.pad(mla_q_norm[l], (0, _round_up(Q_RANK, LANE) - Q_RANK)).reshape(1, -1)
        kvn = mla_kv_norm[l].reshape(1, KV_RANK)
        w_in_p = _pack_w_in(w_in[l])
        wg = w_in[l][:, GATE_COL0:].astype(BF16)
        wb = w_branch[l].astype(BF16)
        wo = w_o[l].astype(BF16)
        g1, b1 = ln1_g[l].reshape(1, D_MODEL), ln1_b[l].reshape(1, D_MODEL)
        g2, b2 = ln2_g[l].reshape(1, D_MODEL), ln2_b[l].reshape(1, D_MODEL)
        lw =(w_in_p, qn, wuq, kvn, wukv, ret_gn_g[l], ret_gn_b[l], band_rel_bias[l])

        br_p, s_ret_p, stacked_p = _token_mixers(xp, 0, None, lw + (bp, lp), prompt=True,
                                                 layer=l, stacked=stacked_p)
        br_s, s_ret_s, stacked_s = _token_mixers(xs, past_len, past, lw + (bs, ls), prompt=False,
                                                 layer=l, stacked=stacked_s)
        xp = _merge(xp, br_p, wg, wb, wo, g1, b1, tm=TOKEN_TILE)
        xs = _merge(xs, br_s, wg, wb, wo, g1, b1, tm=TOKEN_TILE)
        xp = _moe(xp, wrh, wrl, br, we_gate, we_up, we_down, g2, b2, tm=TOKEN_TILE, layer=l)
        xs = _moe(xs, wrh, wrl, br, we_gate, we_up, we_down, g2, b2, tm=TOKEN_TILE, layer=l)
        ret_p.append(s_ret_p)
        ret_s.append(s_ret_s)

    return ((xp.reshape(bp, lp, D_MODEL), xs.reshape(bs, ls, D_MODEL))
            + _state_outputs(stacked_p, ret_p, bp, lp)
            + _state_outputs(stacked_s, ret_s, bs, ls))
```

```python
import functools

import jax
import jax.numpy as jnp
import numpy as np
from jax import lax
from jax.experimental import pallas as pl
from jax.experimental.pallas import tpu as pltpu

D_MODEL = 1024
DEPTH = 2
CHUNK = 64
N_BRANCH = 4
BRANCH_W = D_MODEL // 4
N_HEADS = 4
D_HEAD = BRANCH_W // N_HEADS
DN_A = 64
DR_A = 32
DQK_A = DN_A + DR_A
Q_RANK = (3 * D_MODEL) // 16
KV_RANK = D_MODEL // 8
PREV_CHUNKS = 8
REL_CLIP = 128
ROPE_BASE = 10000.0
N_EXPERTS = 16
N_GROUPS = 4
EXPERTS_PER_GROUP = N_EXPERTS // N_GROUPS
TOP_K = 2
D_EXPERT = D_MODEL // 4
ALPHA = (2.0 * DEPTH) ** 0.25
EPS = 1e-5
NEG_INF = -1e30
LOG2E = 1.4426950408889634
SB_DEAD_LOG2 = -150.0

F32 = jnp.float32
BF16 = jnp.bfloat16

V7X_VMEM_LIMIT = 56 * 1024 * 1024
LANE = 128
TOKEN_TILE = 512
SEQS_PER_STEP = 2

_IN_NAMES = ("c_q", "c_kv", "k_pe", "rq", "rk", "rv", "rg", "sq", "sk", "sv", "bq", "bk", "bv")
_IN_WIDTH = (Q_RANK, KV_RANK, DR_A) + (BRANCH_W,) * 10
QA_W = N_HEADS * DN_A + N_HEADS * DR_A


def _round_up(n, m):
    return (n + m - 1) // m * m


_IN_OFF = {}
_off = 0
for _n, _w in zip(_IN_NAMES, _IN_WIDTH):
    _IN_OFF[_n] = (_off, _round_up(_w, LANE))
    _off += _round_up(_w, LANE)
IN_PACKED = _off
GATE_COL0 = sum(_IN_WIDTH)


def _params(sem):
    return pltpu.CompilerParams(dimension_semantics=sem, vmem_limit_bytes=V7X_VMEM_LIMIT)


def _nt_dot(a, b):
    return lax.dot_general(a, b, (((1,), (1,)), ((), ())), preferred_element_type=F32)


def _tn_dot(a, b):
    return lax.dot_general(a, b, (((0,), (0,)), ((), ())), preferred_element_type=F32)


def _dot(a, b):
    return jnp.dot(a, b, preferred_element_type=F32)


def _layer_norm(v, g, b):
    mu = jnp.mean(v, axis=-1, keepdims=True)
    d = v - mu
    var = jnp.mean(d * d, axis=-1, keepdims=True)
    return d * lax.rsqrt(var + EPS) * g + b


def _sigmoid(v):
    return 0.5 * jnp.tanh(0.5 * v) + 0.5


def _head_slice(h):
    return slice(h * D_HEAD, (h + 1) * D_HEAD)


def _stack_heads(q, lane_sets):
    lane = lax.broadcasted_iota(jnp.int32, q.shape, 1)
    zero = jnp.zeros_like(q)
    parts = []
    for h in range(N_HEADS):
        keep = None
        for lo, hi in lane_sets(h):
            m = jnp.logical_and(lane >= lo, lane < hi)
            keep = m if keep is None else jnp.logical_or(keep, m)
        parts.append(jnp.where(keep, q, zero))
    return jnp.concatenate(parts, axis=0)


def _own_lanes(h):
    return ((h * D_HEAD, (h + 1) * D_HEAD),)


def _mla_lanes(h):
    base = N_HEADS * DN_A
    return ((h * DN_A, (h + 1) * DN_A), (base + h * DR_A, base + (h + 1) * DR_A))


def _store_heads(o_ref, parts):
    for h, p in enumerate(parts):
        o_ref[:, _head_slice(h)] = p.astype(o_ref.dtype)


def _expand_kernel(ckv_ref, kpe_ref, w_ref, kf_ref, v_ref):
    kvx = _dot(ckv_ref[...].astype(BF16), w_ref[...])
    kp = kpe_ref[...].astype(BF16)
    kf_ref[:, :N_HEADS * DN_A] = kvx[:, :N_HEADS * DN_A].astype(BF16)
    kf_ref[:, N_HEADS * DN_A:] = jnp.concatenate([kp] * N_HEADS, axis=1)
    v_ref[...] = kvx[:, N_HEADS * DN_A:].astype(BF16)


def _expand_latent(ckv, kpe, w_ukv, layer, tm):
    m = ckv.shape[1]
    assert m % tm == 0
    return pl.pallas_call(
        _expand_kernel,
        grid=(m // tm,),
        in_specs=[pl.BlockSpec((None, tm, KV_RANK), lambda i: (layer, i, 0)),
                  pl.BlockSpec((None, tm, DR_A), lambda i: (layer, i, 0)),
                  pl.BlockSpec((KV_RANK, 2 * BRANCH_W), lambda i: (0, 0))],
        out_specs=[pl.BlockSpec((tm, QA_W), lambda i: (i, 0)),
                   pl.BlockSpec((tm, BRANCH_W), lambda i: (i, 0))],
        out_shape=[jax.ShapeDtypeStruct((m, QA_W), BF16),
                   jax.ShapeDtypeStruct((m, BRANCH_W), BF16)],
        compiler_params=_params(("parallel",)),
        name="expand_latent",
    )(ckv, kpe, w_ukv)


def _rope_block(x, cos, sin_a, sin_b, half):
    return x * cos + pltpu.roll(x, LANE - half, 1) * sin_a + pltpu.roll(x, half, 1) * sin_b


def _inproj_kernel(x_ref, w_ref, qn_ref, wuq_ref, kvn_ref, wukv_ref,
                   c32_ref, a32_ref, b32_ref, c64_ref, a64_ref, b64_ref, kdec_ref,
                   qa_ref, ckv_ref, kpe_ref, kf_ref, va_ref,
                   rq_ref, rk_ref, rkd_ref, rv_ref, rg_ref,
                   sq_ref, sk_ref, sv_ref, bq_ref, bk_ref, bv_ref, t_refs=(), keep_refs=()):
    qat_ref, vat_ref, sqt_ref, svt_ref, bqt_ref, bvt_ref = t_refs if t_refs else (None,) * 6
    z = _dot(x_ref[...].astype(BF16), w_ref[...])

    def seg(name):
        o, w = _IN_OFF[name]
        return z[:, o:o + w]

    cq = seg("c_q")
    cqn = cq * lax.rsqrt(jnp.sum(cq * cq, axis=1, keepdims=True) * (1.0 / Q_RANK) + EPS) * qn_ref[...]
    qa = _dot(cqn.astype(BF16), wuq_ref[...])
    scale_a = DQK_A ** -0.5 * LOG2E
    nope_w = N_HEADS * DN_A
    q_pe = _rope_block(qa[:, nope_w:], c32_ref[...], a32_ref[...], b32_ref[...], DR_A // 2)
    qa_s = jnp.concatenate([qa[:, :nope_w], q_pe], axis=1) * scale_a
    qa_ref[...] = qa_s.astype(BF16)
    if qat_ref is not None:
        qat_ref[...] = qa_s.T.astype(BF16)

    ckv_raw = seg("c_kv")
    ckv = ckv_raw * lax.rsqrt(jnp.mean(ckv_raw * ckv_raw, axis=1, keepdims=True) + EPS) * kvn_ref[...]
    ckv_ref[...] = ckv
    kvx = _dot(ckv.astype(BF16), wukv_ref[...])
    kp = _rope_block(seg("k_pe"), c32_ref[...], a32_ref[...], b32_ref[...], DR_A // 2)
    kpe_ref[...] = kp[:, :DR_A]
    kpt = kp + pltpu.roll(kp, DR_A, 1) + pltpu.roll(kp, 2 * DR_A, 1) + pltpu.roll(kp, 3 * DR_A, 1)
    kf_ref[:, :nope_w] = kvx[:, :nope_w].astype(BF16)
    kf_ref[:, nope_w:] = kpt.astype(BF16)
    va_ref[...] = kvx[:, nope_w:].astype(BF16)
    if vat_ref is not None:
        vat_ref[...] = kvx[:, nope_w:].T.astype(BF16)

    rq, rk = seg("rq"), seg("rk")
    kdec = kdec_ref[...]
    for blk in range(BRANCH_W // LANE):
        cols = slice(blk * LANE, (blk + 1) * LANE)
        rq_ref[:, cols] = _rope_block(rq[:, cols], c64_ref[...], a64_ref[...], b64_ref[...],
                                      D_HEAD // 2).astype(BF16)
        rkb = _rope_block(rk[:, cols], c64_ref[...], a64_ref[...], b64_ref[...], D_HEAD // 2) * (D_HEAD ** -0.5)
        rk_ref[:, cols] = rkb.astype(BF16)
        rkd_ref[:, cols] = (rkb * kdec[:, cols]).astype(BF16)
    rv_ref[...] = seg("rv").astype(BF16)
    rg_ref[...] = seg("rg")

    scale_h = D_HEAD ** -0.5 * LOG2E
    sq = seg("sq") * scale_h
    sq_ref[...] = sq.astype(BF16)
    sk_ref[...] = seg("sk")
    sv_ref[...] = seg("sv")
    if sqt_ref is not None:
        sqt_ref[...] = sq.T.astype(BF16)
        svt_ref[...] = seg("sv").T.astype(BF16)
    bq = seg("bq") * scale_h
    bq_ref[...] = bq.astype(BF16)
    bk_ref[...] = seg("bk")
    bv_ref[...] = seg("bv")
    if bqt_ref is not None:
        bqt_ref[...] = bq.T.astype(BF16)
        bvt_ref[...] = seg("bv").T.astype(BF16)
    if keep_refs:
        keep_refs[0][...] = seg("bk")
        keep_refs[1][...] = seg("bv")


_INPROJ_OUT = (
    (QA_W, BF16), (KV_RANK, F32), (DR_A, F32), (QA_W, BF16), (BRANCH_W, BF16),
    (BRANCH_W, BF16), (BRANCH_W, BF16), (BRANCH_W, BF16), (BRANCH_W, BF16), (BRANCH_W, F32),
    (BRANCH_W, BF16), (BRANCH_W, F32), (BRANCH_W, F32), (BRANCH_W, BF16), (BRANCH_W, F32), (BRANCH_W, F32))


_STATE_OUTS = (1, 2, 11, 12, 14, 15)


def _inproj_entry(*refs, n_in, n_alias, n_t, n_keep):
    outs = refs[n_in + n_alias:]
    n_main = len(_INPROJ_OUT)
    _inproj_kernel(*refs[:n_in], *outs[:n_main], t_refs=outs[n_main:n_main + n_t],
                   keep_refs=outs[n_main + n_t:n_main + n_t + n_keep])


def _inproj(x, w_in_p, qn, wuq, kvn, wukv, tables, *, tm, layer, stacked, seq_len=None, keep_rows=None):
    m = x.shape[0]
    tm = min(tm, m)
    assert m % tm == 0
    n_pos_tiles = tables[0].shape[0] // tm
    assert tables[0].shape[0] % tm == 0
    row = lambda i: (i, 0)
    const = lambda i: (0, 0)
    pos = lambda i: (i % n_pos_tiles, 0)
    in_specs = [pl.BlockSpec((tm, D_MODEL), row),
                pl.BlockSpec(w_in_p.shape, const),
                pl.BlockSpec(qn.shape, const),
                pl.BlockSpec(wuq.shape, const),
                pl.BlockSpec(kvn.shape, const),
                pl.BlockSpec(wukv.shape, const)]
    in_specs += [pl.BlockSpec((tm, t.shape[1]), pos) for t in tables]
    n_in = len(in_specs)
    out_specs, out_shape = [], []
    for k, (w, dt) in enumerate(_INPROJ_OUT):
        if k in _STATE_OUTS:
            out_specs.append(pl.BlockSpec((None, tm, w), lambda i: (layer, i, 0)))
            out_shape.append(jax.ShapeDtypeStruct((DEPTH, m, w), dt))
        else:
            out_specs.append(pl.BlockSpec((tm, w), row))
            out_shape.append(jax.ShapeDtypeStruct((m, w), dt))
    if seq_len is not None:
        assert seq_len % tm == 0 and m % seq_len == 0
        tps = seq_len // tm
        for w in (QA_W,) + (BRANCH_W,) * 5:
            out_specs.append(pl.BlockSpec((None, w, tm), lambda i: (i // tps, 0, i % tps)))
            out_shape.append(jax.ShapeDtypeStruct((m // seq_len, w, seq_len), BF16))
    n_t = len(out_specs) - len(_INPROJ_OUT)
    state_outs = list(_STATE_OUTS)
    if keep_rows is not None:
        assert seq_len is not None and keep_rows % tm == 0 and keep_rows <= seq_len
        tps, ktiles = seq_len // tm, keep_rows // tm
        kept = lambda i: (layer, (i // tps) * ktiles + jnp.maximum(i % tps - (tps - ktiles), 0), 0)
        for _ in range(2):
            state_outs.append(len(out_specs))
            out_specs.append(pl.BlockSpec((None, tm, BRANCH_W), kept))
            out_shape.append(jax.ShapeDtypeStruct((DEPTH, m // seq_len * keep_rows, BRANCH_W), F32))
    aliases = {}
    extra = ()
    if stacked is not None:
        extra = tuple(stacked)
        assert len(extra) == len(state_outs)
        in_specs += [pl.BlockSpec(memory_space=pl.ANY)] * len(extra)
        aliases = {n_in + k: o for k, o in enumerate(state_outs)}
    outs = pl.pallas_call(
        functools.partial(_inproj_entry, n_in=n_in, n_alias=len(extra), n_t=n_t,
                          n_keep=len(state_outs) - len(_STATE_OUTS)),
        grid=(m // tm,),
        in_specs=in_specs,
        out_specs=out_specs,
        out_shape=out_shape,
        input_output_aliases=aliases,
        compiler_params=_params(("arbitrary",)),
        name="inproj",
    )(x, w_in_p, qn, wuq, kvn, wukv, *tables, *extra)
    return outs, tuple(outs[k] for k in state_outs)


def _tile_spec(src, rows):
    arr, layer = src
    return pl.BlockSpec((None, None, rows, arr.shape[3]), lambda i, t: (layer, i, t, 0))


def _full_spec(src):
    arr, layer = src
    return pl.BlockSpec((None, None, arr.shape[2], arr.shape[3]), lambda i, t: (layer, i, 0, 0))


def _mla_kernel(q_ref, kfo_ref, vo_ref, kfp_ref, vp_ref, o_ref, *, tq, tk, n_own, n_past):
    qi = pl.program_id(1)
    qm = _stack_heads(q_ref[...], _mla_lanes)
    rows = N_HEADS * tq
    tko = kfo_ref.shape[0]
    n_loop = qi if n_past is None else n_past

    def softmax_pv(s, v, carry):
        m, accs = carry
        m_new = jnp.maximum(m, jnp.max(s, axis=1, keepdims=True))
        pb = jnp.exp2(s - m_new).astype(BF16)
        a = jnp.exp2(m - m_new)
        ones = jnp.ones((v.shape[0], D_HEAD), BF16)
        accs = tuple(a[h * tq:(h + 1) * tq] * accs[h]
                     + _dot(pb[h * tq:(h + 1) * tq], jnp.concatenate([v[:, _head_slice(h)], ones], axis=1))
                     for h in range(N_HEADS))
        return m_new, accs

    def past_scores(j):
        start = pl.multiple_of(jnp.minimum(j, jnp.maximum(n_loop - 1, 0)) * tk, tk)
        return _nt_dot(qm, kfp_ref[pl.ds(start, tk), :])

    row_q = lax.broadcasted_iota(jnp.int32, (rows, tko), 0) % tq
    col = lax.broadcasted_iota(jnp.int32, (rows, tko), 1)
    own_mask = jnp.logical_and(col < n_own, col // CHUNK <= row_q // CHUNK)
    carry = (jnp.full((rows, 1), NEG_INF, F32),
             tuple(jnp.zeros((tq, 2 * D_HEAD), F32) for _ in range(N_HEADS)))
    s_next = past_scores(0)
    carry = softmax_pv(jnp.where(own_mask, _nt_dot(qm, kfo_ref[...]), NEG_INF), vo_ref[...], carry)

    def body(j, c):
        s_cur, rest = c
        s_after = past_scores(j + 1)
        start = pl.multiple_of(j * tk, tk)
        return s_after, softmax_pv(s_cur, vp_ref[pl.ds(start, tk), :], rest)

    _, (_, accs) = lax.fori_loop(0, n_loop, body, (s_next, carry))
    _store_heads(o_ref, [accs[h][:, :D_HEAD] / accs[h][:, D_HEAD:D_HEAD + 1] for h in range(N_HEADS)])


def _mla_attn(q, kf_own, v_own, kf_past, v_past, *, tq, tk, n_own, causal_tiles):
    b, lq, _ = q.shape
    nqt = lq // tq
    tko = kf_own[0].shape[2] // nqt
    lp = kf_past[0].shape[2]
    assert lq % tq == 0 and lp % tk == 0
    return pl.pallas_call(
        functools.partial(_mla_kernel, tq=tq, tk=tk, n_own=n_own,
                          n_past=None if causal_tiles else lp // tk),
        grid=(b, nqt),
        in_specs=[pl.BlockSpec((None, tq, QA_W), lambda i, t: (i, t, 0)),
                  _tile_spec(kf_own, tko), _tile_spec(v_own, tko),
                  _full_spec(kf_past), _full_spec(v_past)],
        out_specs=pl.BlockSpec((None, tq, BRANCH_W), lambda i, t: (i, t, 0)),
        out_shape=jax.ShapeDtypeStruct((b, lq, BRANCH_W), BF16),
        compiler_params=_params(("parallel", "arbitrary")),
        name="mla_attn",
    )(q, kf_own[0], v_own[0], kf_past[0], v_past[0])


def _mla_kernel_t(qt_ref, kfo_ref, vto_ref, kfp_ref, vtp_ref, o_ref, *, tq, tk, nseq):
    qi = pl.program_id(1)
    cols = N_HEADS * tq
    qmts = []
    for s in range(nseq):
        qt = qt_ref[s]
        feat = lax.broadcasted_iota(jnp.int32, qt.shape, 0)
        parts = []
        for h in range(N_HEADS):
            keep = None
            for lo, hi in _mla_lanes(h):
                m = jnp.logical_and(feat >= lo, feat < hi)
                keep = m if keep is None else jnp.logical_or(keep, m)
            parts.append(jnp.where(keep, qt, jnp.zeros_like(qt)))
        qmts.append(jnp.concatenate(parts, axis=1))

    def softmax_pv(st, vt, carry):
        m, l, accs = carry
        m_new = jnp.maximum(m, jnp.max(st, axis=0, keepdims=True))
        pt = jnp.exp2(st - m_new)
        a = jnp.exp2(m - m_new)
        l = a * l + jnp.sum(pt, axis=0, keepdims=True)
        pb = pt.astype(BF16)
        accs = tuple(a[:, h * tq:(h + 1) * tq] * accs[h]
                     + _dot(vt[_head_slice(h), :], pb[:, h * tq:(h + 1) * tq]) for h in range(N_HEADS))
        return m_new, l, accs

    def past_scores(s, j):
        start = pl.multiple_of(jnp.minimum(j, jnp.maximum(qi - 1, 0)) * tk, tk)
        return _dot(kfp_ref[s, pl.ds(start, tk), :], qmts[s])

    key = lax.broadcasted_iota(jnp.int32, (tq, cols), 0)
    qry = lax.broadcasted_iota(jnp.int32, (tq, cols), 1) % tq
    own_mask = key // CHUNK <= qry // CHUNK
    s_next, carries = [], []
    for s in range(nseq):
        empty = (jnp.full((1, cols), NEG_INF, F32), jnp.zeros((1, cols), F32),
                 tuple(jnp.zeros((D_HEAD, tq), F32) for _ in range(N_HEADS)))
        s_next.append(past_scores(s, 0))
        carries.append(softmax_pv(jnp.where(own_mask, _dot(kfo_ref[s], qmts[s]), NEG_INF), vto_ref[s], empty))

    def body(j, c):
        s_cur, rest = c
        start = pl.multiple_of(j * tk, tk)
        s_after = tuple(past_scores(s, j + 1) for s in range(nseq))
        return s_after, tuple(softmax_pv(s_cur[s], vtp_ref[s, :, pl.ds(start, tk)], rest[s])
                              for s in range(nseq))

    _, done = lax.fori_loop(0, qi, body, (tuple(s_next), tuple(carries)))
    for s in range(nseq):
        _, l, accs = done[s]
        out_t = jnp.concatenate([accs[h] / l[:, h * tq:(h + 1) * tq] for h in range(N_HEADS)], axis=0)
        o_ref[s] = out_t.T.astype(o_ref.dtype)


def _mla_attn_t(qt, kf, vt, *, tq, nseq):
    b, _, length = qt.shape
    assert length % tq == 0 and b % nseq == 0
    return pl.pallas_call(
        functools.partial(_mla_kernel_t, tq=tq, tk=tq, nseq=nseq),
        grid=(b // nseq, length // tq),
        in_specs=[pl.BlockSpec((nseq, QA_W, tq), lambda i, t: (i, 0, t)),
                  pl.BlockSpec((nseq, tq, QA_W), lambda i, t: (i, t, 0)),
                  pl.BlockSpec((nseq, BRANCH_W, tq), lambda i, t: (i, 0, t)),
                  pl.BlockSpec((nseq, length, QA_W), lambda i, t: (i, 0, 0)),
                  pl.BlockSpec((nseq, BRANCH_W, length), lambda i, t: (i, 0, 0))],
        out_specs=pl.BlockSpec((nseq, tq, BRANCH_W), lambda i, t: (i, t, 0)),
        out_shape=jax.ShapeDtypeStruct((b, length, BRANCH_W), BF16),
        compiler_params=_params(("parallel", "arbitrary")),
        name="mla_attn_t",
    )(qt, kf, vt, kf, vt)


def _sb_kernel(q_ref, ko_ref, vo_ref, kp_ref, vp_ref, o_ref, *, tq, tk, n_own, n_past):
    qi = pl.program_id(1)
    qm = _stack_heads(q_ref[...], _own_lanes)
    rows = N_HEADS * tq
    tko = ko_ref.shape[0]
    n_loop = qi if n_past is None else n_past

    def tri2(n):
        r = lax.broadcasted_iota(jnp.int32, (2 * n, n), 0) % n
        c = lax.broadcasted_iota(jnp.int32, (2 * n, n), 1)
        return jnp.where(r > c, 1.0, 0.0).astype(BF16)

    def weigh(z, v, carry, mask, tri):
        run, accs = carry
        neg_abs = lax.bitcast_convert_type(
            lax.bitcast_convert_type(z, jnp.uint32) | jnp.uint32(0x80000000), F32)
        t = jnp.log2(1.0 + jnp.exp2(neg_abs))
        log_beta = jnp.minimum(z, 0.0) - t
        log_stay = log_beta - z
        if mask is not None:
            log_stay = jnp.where(mask, log_stay, 0.0)
        hi = log_stay.astype(BF16)
        lo = (log_stay - hi.astype(F32)).astype(BF16)
        later = _dot(jnp.concatenate([hi, lo], axis=1), tri) + run
        w = jnp.exp2(log_beta + later)
        if mask is not None:
            w = jnp.where(mask, w, 0.0)
        wb = w.astype(BF16)
        vb = v.astype(BF16)
        accs = tuple(accs[h] + _dot(wb[h * tq:(h + 1) * tq], vb[:, _head_slice(h)])
                     for h in range(N_HEADS))
        return run + jnp.sum(log_stay, axis=1, keepdims=True), accs

    row_q = lax.broadcasted_iota(jnp.int32, (rows, tko), 0) % tq
    col = lax.broadcasted_iota(jnp.int32, (rows, tko), 1)
    own_mask = jnp.logical_and(col < n_own, col < row_q)

    def past_start(jj):
        return pl.multiple_of(jnp.clip(n_loop - 1 - jj, 0, kp_ref.shape[0] // tk - 1) * tk, tk)

    def past_scores(jj):
        return _nt_dot(qm, kp_ref[pl.ds(past_start(jj), tk), :].astype(BF16))

    def alive(run):
        return (jnp.max(run) > SB_DEAD_LOG2).astype(jnp.int32)

    z_next = past_scores(0)
    carry = (jnp.zeros((rows, 1), F32), tuple(jnp.zeros((tq, D_HEAD), F32) for _ in range(N_HEADS)))
    run, accs = weigh(_nt_dot(qm, ko_ref[...].astype(BF16)), vo_ref[...], carry, own_mask, tri2(tko))
    tri_past = tri2(tk)

    def cond(c):
        return jnp.logical_and(c[0] < n_loop, c[1] > 0)

    def body(c):
        jj, _, z_cur, run, accs = c
        z_after = past_scores(jj + 1)
        run, accs = weigh(z_cur, vp_ref[pl.ds(past_start(jj), tk), :], (run, accs), None, tri_past)
        return jj + 1, alive(run), z_after, run, accs

    out = lax.while_loop(cond, body, (jnp.int32(0), alive(run), z_next, run, accs))
    _store_heads(o_ref, out[4])


def _sb_attn(q, k_own, v_own, k_past, v_past, *, tq, tk, n_own, causal_tiles):
    b, lq, _ = q.shape
    nqt = lq // tq
    tko = k_own[0].shape[2] // nqt
    lp = k_past[0].shape[2]
    assert lq % tq == 0 and lp % tk == 0
    return pl.pallas_call(
        functools.partial(_sb_kernel, tq=tq, tk=tk, n_own=n_own,
                          n_past=None if causal_tiles else lp // tk),
        grid=(b, nqt),
        in_specs=[pl.BlockSpec((None, tq, BRANCH_W), lambda i, t: (i, t, 0)),
                  _tile_spec(k_own, tko), _tile_spec(v_own, tko),
                  _full_spec(k_past), _full_spec(v_past)],
        out_specs=pl.BlockSpec((None, tq, BRANCH_W), lambda i, t: (i, t, 0)),
        out_shape=jax.ShapeDtypeStruct((b, lq, BRANCH_W), BF16),
        compiler_params=_params(("parallel", "arbitrary")),
        name="sb_attn",
    )(q, k_own[0], v_own[0], k_past[0], v_past[0])


def _sb_kernel_t(qt_ref, ko_ref, vto_ref, kp_ref, vtp_ref, o_ref, *, tq, tk, nseq):
    qi = pl.program_id(1)
    qmts = []
    for s in range(nseq):
        qt = qt_ref[s]
        feat = lax.broadcasted_iota(jnp.int32, qt.shape, 0)
        qmts.append(jnp.concatenate(
            [jnp.where(jnp.logical_and(feat >= h * D_HEAD, feat < (h + 1) * D_HEAD), qt, jnp.zeros_like(qt))
             for h in range(N_HEADS)], axis=1))
    cols = N_HEADS * tq

    def tri2(n):
        r = lax.broadcasted_iota(jnp.int32, (n, 2 * n), 0)
        c = lax.broadcasted_iota(jnp.int32, (n, 2 * n), 1) % n
        return jnp.where(c > r, 1.0, 0.0).astype(BF16)

    def weigh(zt, vt, carry, mask, tri):
        run, accs = carry
        neg_abs = lax.bitcast_convert_type(
            lax.bitcast_convert_type(zt, jnp.uint32) | jnp.uint32(0x80000000), F32)
        t = jnp.log2(1.0 + jnp.exp2(neg_abs))
        log_beta = jnp.minimum(zt, 0.0) - t
        log_stay = log_beta - zt
        if mask is not None:
            log_stay = jnp.where(mask, log_stay, 0.0)
        hi = log_stay.astype(BF16)
        lo = (log_stay - hi.astype(F32)).astype(BF16)
        later = _dot(tri, jnp.concatenate([hi, lo], axis=0)) + run
        w = jnp.exp2(log_beta + later)
        if mask is not None:
            w = jnp.where(mask, w, 0.0)
        wb = w.astype(BF16)
        accs = tuple(accs[h] + _dot(vt[_head_slice(h), :], wb[:, h * tq:(h + 1) * tq])
                     for h in range(N_HEADS))
        return run + jnp.sum(log_stay, axis=0, keepdims=True), accs

    def past_start(jj):
        return pl.multiple_of(jnp.clip(qi - 1 - jj, 0, kp_ref.shape[1] // tk - 1) * tk, tk)

    def past_scores(s, jj):
        return _dot(kp_ref[s, pl.ds(past_start(jj), tk), :].astype(BF16), qmts[s])

    def alive(runs):
        top = jnp.max(runs[0])
        for r in runs[1:]:
            top = jnp.maximum(top, jnp.max(r))
        return (top > SB_DEAD_LOG2).astype(jnp.int32)

    key = lax.broadcasted_iota(jnp.int32, (tq, cols), 0)
    qry = lax.broadcasted_iota(jnp.int32, (tq, cols), 1) % tq
    own_mask = key < qry
    tri_own = tri2(tq)
    z_next, runs, accss = [], [], []
    for s in range(nseq):
        z_next.append(past_scores(s, 0))
        empty = (jnp.zeros((1, cols), F32), tuple(jnp.zeros((D_HEAD, tq), F32) for _ in range(N_HEADS)))
        run, accs = weigh(_dot(ko_ref[s].astype(BF16), qmts[s]), vto_ref[s], empty, own_mask, tri_own)
        runs.append(run)
        accss.append(accs)
    tri_past = tri2(tk)

    def cond(c):
        return jnp.logical_and(c[0] < qi, c[1] > 0)

    def body(c):
        jj, _, z_cur, runs, accss = c
        z_after = tuple(past_scores(s, jj + 1) for s in range(nseq))
        new = [weigh(z_cur[s], vtp_ref[s, :, pl.ds(past_start(jj), tk)], (runs[s], accss[s]), None, tri_past)
               for s in range(nseq)]
        runs = tuple(n[0] for n in new)
        return jj + 1, alive(runs), z_after, runs, tuple(n[1] for n in new)

    out = lax.while_loop(cond, body, (jnp.int32(0), alive(runs), tuple(z_next), tuple(runs), tuple(accss)))
    for s in range(nseq):
        o_ref[s] = jnp.concatenate(out[4][s], axis=0).T.astype(o_ref.dtype)


def _sb_attn_t(qt, k, vt, *, tq, nseq):
    b, _, length = qt.shape
    k_arr, layer = k
    assert length % tq == 0 and b % nseq == 0
    return pl.pallas_call(
        functools.partial(_sb_kernel_t, tq=tq, tk=tq, nseq=nseq),
        grid=(b // nseq, length // tq),
        in_specs=[pl.BlockSpec((nseq, BRANCH_W, tq), lambda i, t: (i, 0, t)),
                  pl.BlockSpec((None, nseq, tq, BRANCH_W), lambda i, t: (layer, i, t, 0)),
                  pl.BlockSpec((nseq, BRANCH_W, tq), lambda i, t: (i, 0, t)),
                  pl.BlockSpec((None, nseq, length, BRANCH_W), lambda i, t: (layer, i, 0, 0)),
                  pl.BlockSpec((nseq, BRANCH_W, length), lambda i, t: (i, 0, 0))],
        out_specs=pl.BlockSpec((nseq, tq, BRANCH_W), lambda i, t: (i, t, 0)),
        out_shape=jax.ShapeDtypeStruct((b, length, BRANCH_W), BF16),
        compiler_params=_params(("parallel", "arbitrary")),
        name="sb_attn_t",
    )(qt, k_arr, vt, k_arr, vt)


def _ret_kernel(q_ref, k_ref, kd_ref, v_ref, rg_ref, s0_ref, dec_ref, qdec_ref, gl_ref,
                gng_ref, gnb_ref, o_ref, sout_ref, state_ref, *, lc, nseq):
    c = pl.program_id(1)

    @pl.when(c == 0)
    def _():
        state_ref[...] = s0_ref[...]

    for s in range(nseq):
        qm = _stack_heads(q_ref[s], _own_lanes)
        v = v_ref[s]
        state = state_ref[s]
        scores = (_nt_dot(qm, k_ref[s]) * dec_ref[...]).astype(BF16)
        cross = _dot(qm, state.astype(BF16)) * qdec_ref[...]
        kv_full = _tn_dot(kd_ref[s], v)
        new_state = gl_ref[...] * state + jnp.concatenate(
            [kv_full[_head_slice(h), _head_slice(h)] for h in range(N_HEADS)], axis=0)
        state_ref[s] = new_state

        rg = rg_ref[s]
        for h in range(N_HEADS):
            o = _dot(scores[h * lc:(h + 1) * lc], v[:, _head_slice(h)]) + cross[h * lc:(h + 1) * lc]
            mu = jnp.mean(o, axis=-1, keepdims=True)
            d = o - mu
            var = jnp.mean(d * d, axis=-1, keepdims=True)
            y = d * lax.rsqrt(var + EPS) * gng_ref[h] + gnb_ref[h]
            g = rg[:, _head_slice(h)]
            o_ref[s, :, _head_slice(h)] = (y * (g * _sigmoid(g))).astype(o_ref.dtype)

    @pl.when(c == pl.num_programs(1) - 1)
    def _():
        sout_ref[...] = state_ref[...]


def _retention(q, k, kd, v, rg, s0, dec, qdec, gl, gng, gnb, *, lc, nseq):
    b, length, _ = q.shape
    assert length % lc == 0 and b % nseq == 0
    seq = lambda i, t: (i, t, 0)
    st = lambda i, t: (i, 0, 0)
    c2 = lambda i, t: (0, 0)
    c3 = lambda i, t: (0, 0, 0)
    return pl.pallas_call(
        functools.partial(_ret_kernel, lc=lc, nseq=nseq),
        grid=(b // nseq, length // lc),
        in_specs=[pl.BlockSpec((nseq, lc, BRANCH_W), seq)] * 5
        + [pl.BlockSpec((nseq, BRANCH_W, D_HEAD), st),
           pl.BlockSpec(dec.shape, c2), pl.BlockSpec(qdec.shape, c2), pl.BlockSpec(gl.shape, c2),
           pl.BlockSpec(gng.shape, c3), pl.BlockSpec(gnb.shape, c3)],
        out_specs=[pl.BlockSpec((nseq, lc, BRANCH_W), seq),
                   pl.BlockSpec((nseq, BRANCH_W, D_HEAD), st)],
        out_shape=[jax.ShapeDtypeStruct((b, length, BRANCH_W), BF16),
                   jax.ShapeDtypeStruct((b, BRANCH_W, D_HEAD), F32)],
        scratch_shapes=[pltpu.VMEM((nseq, BRANCH_W, D_HEAD), F32)],
        compiler_params=_params(("parallel", "arbitrary")),
        name="retention",
    )(q, k, kd, v, rg, s0, dec, qdec, gl, gng, gnb)


def _band_kernel(q_ref, k_ref, v_ref, bias_ref, o_ref, *, tq, win, back):
    qi = pl.program_id(1)
    start = pl.multiple_of(jnp.maximum(qi - back, 0) * tq, tq)
    k = k_ref[pl.ds(start, win), :].astype(BF16)
    v = v_ref[pl.ds(start, win), :].astype(BF16)
    qm = _stack_heads(q_ref[...], _own_lanes)
    s = _nt_dot(qm, k) + bias_ref[...]
    pb = jnp.exp2(s - jnp.max(s, axis=1, keepdims=True)).astype(BF16)
    ones = jnp.ones((win, D_HEAD), BF16)
    outs = []
    for h in range(N_HEADS):
        o = _dot(pb[h * tq:(h + 1) * tq], jnp.concatenate([v[:, _head_slice(h)], ones], axis=1))
        outs.append(o[:, :D_HEAD] / o[:, D_HEAD:D_HEAD + 1])
    _store_heads(o_ref, outs)


def _band_attn(q, k, v, bias, *, tq, win, back):
    b, lq, _ = q.shape
    nvar = bias.shape[0]
    assert lq % tq == 0
    return pl.pallas_call(
        functools.partial(_band_kernel, tq=tq, win=win, back=back),
        grid=(b, lq // tq),
        in_specs=[pl.BlockSpec((None, tq, BRANCH_W), lambda i, t: (i, t, 0)),
                  _full_spec(k), _full_spec(v),
                  pl.BlockSpec((None, N_HEADS * tq, win), lambda i, t: (jnp.minimum(t, nvar - 1), 0, 0))],
        out_specs=pl.BlockSpec((None, tq, BRANCH_W), lambda i, t: (i, t, 0)),
        out_shape=jax.ShapeDtypeStruct((b, lq, BRANCH_W), BF16),
        compiler_params=_params(("parallel", "arbitrary")),
        name="band_attn",
    )(q, k[0], v[0], bias)


def _band_kernel_t(qt_ref, k_ref, vt_ref, bias_ref, o_ref, *, tq, win, back, nseq):
    qi = pl.program_id(1)
    start = pl.multiple_of(jnp.maximum(qi - back, 0) * tq, tq)
    for s in range(nseq):
        k = k_ref[s, pl.ds(start, win), :].astype(BF16)
        vt = vt_ref[s, :, pl.ds(start, win)]
        qt = qt_ref[s]
        feat = lax.broadcasted_iota(jnp.int32, qt.shape, 0)
        qmt = jnp.concatenate(
            [jnp.where(jnp.logical_and(feat >= h * D_HEAD, feat < (h + 1) * D_HEAD), qt, jnp.zeros_like(qt))
             for h in range(N_HEADS)], axis=1)
        st = _dot(k, qmt) + bias_ref[...]
        pt = jnp.exp2(st - jnp.max(st, axis=0, keepdims=True))
        l = jnp.sum(pt, axis=0, keepdims=True)
        pb = pt.astype(BF16)
        out_t = jnp.concatenate(
            [_dot(vt[_head_slice(h), :], pb[:, h * tq:(h + 1) * tq]) / l[:, h * tq:(h + 1) * tq]
             for h in range(N_HEADS)], axis=0)
        o_ref[s] = out_t.T.astype(o_ref.dtype)


def _band_attn_t(qt, k, vt, bias_t, *, tq, win, back, nseq):
    b, _, length = qt.shape
    nvar = bias_t.shape[0]
    k_arr, layer = k
    assert length % tq == 0 and b % nseq == 0
    return pl.pallas_call(
        functools.partial(_band_kernel_t, tq=tq, win=win, back=back, nseq=nseq),
        grid=(b // nseq, length // tq),
        in_specs=[pl.BlockSpec((nseq, BRANCH_W, tq), lambda i, t: (i, 0, t)),
                  pl.BlockSpec((None, nseq, k_arr.shape[2], BRANCH_W), lambda i, t: (layer, i, 0, 0)),
                  pl.BlockSpec((nseq, BRANCH_W, length), lambda i, t: (i, 0, 0)),
                  pl.BlockSpec((None, win, N_HEADS * tq), lambda i, t: (jnp.minimum(t, nvar - 1), 0, 0))],
        out_specs=pl.BlockSpec((nseq, tq, BRANCH_W), lambda i, t: (i, t, 0)),
        out_shape=jax.ShapeDtypeStruct((b, length, BRANCH_W), BF16),
        compiler_params=_params(("parallel", "arbitrary")),
        name="band_attn_t",
    )(qt, k_arr, vt, bias_t)


def _merge_kernel(x_ref, ba_ref, bb_ref, bc_ref, bd_ref, wg_ref, wb_ref, wo_ref, g_ref, b_ref, o_ref):
    x = x_ref[...]
    xb = x.astype(BF16)
    merged = None
    for n, br_ref in enumerate((ba_ref, bb_ref, bc_ref, bd_ref)):
        logits = _dot(xb, wg_ref[:, n * D_MODEL:(n + 1) * D_MODEL])
        term = _dot(br_ref[...], wb_ref[n]) * _sigmoid(logits)
        merged = term if merged is None else merged + term
    mix = _dot(merged.astype(BF16), wo_ref[...])
    o_ref[...] = _layer_norm(ALPHA * x + mix, g_ref[...], b_ref[...])


def _merge(x, branches, wg, wb, wo, g, b, *, tm):
    m = x.shape[0]
    tm = min(tm, m)
    assert m % tm == 0
    const2 = lambda i: (0, 0)
    row = lambda i: (i, 0)
    return pl.pallas_call(
        _merge_kernel,
        grid=(m // tm,),
        in_specs=[pl.BlockSpec((tm, D_MODEL), row)]
        + [pl.BlockSpec((tm, BRANCH_W), row)] * N_BRANCH
        + [pl.BlockSpec((D_MODEL, N_BRANCH * D_MODEL), const2),
           pl.BlockSpec((N_BRANCH, BRANCH_W, D_MODEL), lambda i: (0, 0, 0)),
           pl.BlockSpec((D_MODEL, D_MODEL), const2),
           pl.BlockSpec((1, D_MODEL), const2),
           pl.BlockSpec((1, D_MODEL), const2)],
        out_specs=pl.BlockSpec((tm, D_MODEL), row),
        out_shape=jax.ShapeDtypeStruct((m, D_MODEL), F32),
        compiler_params=_params(("parallel",)),
        name="merge",
    )(x, *branches, wg, wb, wo, g, b)


def _route(aff_t, sel_t):
    def top2_sum(a, b, c, d):
        hi1, lo1 = jnp.maximum(a, b), jnp.minimum(a, b)
        hi2, lo2 = jnp.maximum(c, d), jnp.minimum(c, d)
        return jnp.maximum(hi1, hi2) + jnp.maximum(jnp.minimum(hi1, hi2), jnp.maximum(lo1, lo2))

    score = [top2_sum(*sel_t[g * EXPERTS_PER_GROUP:(g + 1) * EXPERTS_PER_GROUP])
             for g in range(N_GROUPS)]
    best_here = []
    for g in range(N_GROUPS):
        ok = None
        for o in range(N_GROUPS):
            if o == g:
                continue
            c = (score[g] > score[o]) if o < g else (score[g] >= score[o])
            ok = c if ok is None else jnp.logical_and(ok, c)
        best_here.append(ok)
    picked = []
    for e in range(N_EXPERTS):
        g = e // EXPERTS_PER_GROUP
        rank = jnp.zeros_like(sel_t[e])
        for o in range(g * EXPERTS_PER_GROUP, (g + 1) * EXPERTS_PER_GROUP):
            if o == e:
                continue
            ahead = (sel_t[o] >= sel_t[e]) if o < e else (sel_t[o] > sel_t[e])
            rank = rank + jnp.where(ahead, 1.0, 0.0)
        picked.append(jnp.where(jnp.logical_and(best_here[g], rank < TOP_K), aff_t[e], 0.0))
    total = picked[0]
    for e in range(1, N_EXPERTS):
        total = total + picked[e]
    return [p / total for p in picked]


def _moe_kernel(x_ref, wrh_ref, wrl_ref, br_ref, wg_ref, wu_ref, wd_ref, g_ref, b_ref, o_ref, acc_ref):
    x = x_ref[...]
    xh = x.astype(BF16)
    xl = (x - xh.astype(F32)).astype(BF16)
    wrh = wrh_ref[...]
    logits = _dot(xh, wrh) + _dot(xl, wrh) + _dot(xh, wrl_ref[...])
    aff = _sigmoid(logits).T
    bias = br_ref[...]
    aff_t = [aff[e:e + 1, :] for e in range(N_EXPERTS)]
    sel_t = [aff_t[e] + bias[e:e + 1, :] for e in range(N_EXPERTS)]
    gate_rows = _route(aff_t, sel_t)
    tm = x.shape[0]
    gate_t = jnp.concatenate(gate_rows + [jnp.zeros((LANE - N_EXPERTS, tm), F32)], axis=0)
    gate = gate_t.T

    for e in range(N_EXPERTS):
        gt = _dot(xh, wg_ref[e])
        hmid = gt * _sigmoid(gt) * _dot(xh, wu_ref[e])
        y = _dot(hmid.astype(BF16), wd_ref[e]) * gate[:, e:e + 1]
        if e == 0:
            acc_ref[...] = y
        else:
            acc_ref[...] += y
    o_ref[...] = _layer_norm(ALPHA * x + acc_ref[...], g_ref[...], b_ref[...])


def _moe(x, wrh, wrl, br, wg, wu, wd, g, b, *, tm, layer):
    m = x.shape[0]
    tm = min(tm, m)
    assert m % tm == 0
    const2 = lambda i: (0, 0)
    this_layer = lambda i: (layer, 0, 0, 0)
    return pl.pallas_call(
        _moe_kernel,
        grid=(m // tm,),
        in_specs=[pl.BlockSpec((tm, D_MODEL), lambda i: (i, 0)),
                  pl.BlockSpec((D_MODEL, LANE), const2),
                  pl.BlockSpec((D_MODEL, LANE), const2),
                  pl.BlockSpec((LANE, 1), const2),
                  pl.BlockSpec((None, N_EXPERTS, D_MODEL, D_EXPERT), this_layer, pipeline_mode=pl.Buffered(1)),
                  pl.BlockSpec((None, N_EXPERTS, D_MODEL, D_EXPERT), this_layer, pipeline_mode=pl.Buffered(1)),
                  pl.BlockSpec((None, N_EXPERTS, D_EXPERT, D_MODEL), this_layer, pipeline_mode=pl.Buffered(1)),
                  pl.BlockSpec((1, D_MODEL), const2),
                  pl.BlockSpec((1, D_MODEL), const2)],
        out_specs=pl.BlockSpec((tm, D_MODEL), lambda i: (i, 0)),
        out_shape=jax.ShapeDtypeStruct((m, D_MODEL), F32),
        scratch_shapes=[pltpu.VMEM((tm, D_MODEL), F32)],
        compiler_params=_params(("parallel",)),
        name="moe",
    )(x, wrh, wrl, br, wg, wu, wd, g, b)


def _rope_tables(pos, d):
    half = d // 2
    inv = jnp.power(ROPE_BASE, -jnp.arange(half, dtype=F32) / half)
    ang = pos.astype(F32)[:, None] * inv[None, :]
    cos, sin = jnp.cos(ang), jnp.sin(ang)
    zero = jnp.zeros_like(sin)
    rep = LANE // d
    cos_t = jnp.tile(jnp.concatenate([cos, cos], axis=1), (1, rep))
    sin_a = jnp.tile(jnp.concatenate([-sin, zero], axis=1), (1, rep))
    sin_b = jnp.tile(jnp.concatenate([zero, sin], axis=1), (1, rep))
    return cos_t, sin_a, sin_b


def _retention_tables(lc):
    log_g = jnp.log1p(-jnp.exp2(-5.0 - jnp.arange(N_HEADS, dtype=F32)))
    i = jnp.arange(lc, dtype=F32)
    diff = i[:, None] - i[None, :]
    dec = jnp.where(diff >= 0, jnp.exp(jnp.maximum(diff, 0.0)[None] * log_g[:, None, None]), 0.0)
    qdec = jnp.exp((i[None, :] + 1.0) * log_g[:, None])
    kdec = jnp.exp((lc - 1.0 - i)[None, :] * log_g[:, None])
    gl = jnp.exp(lc * log_g)
    dec = dec.reshape(N_HEADS * lc, lc)
    qdec = jnp.broadcast_to(qdec[:, :, None], (N_HEADS, lc, D_HEAD)).reshape(N_HEADS * lc, D_HEAD)
    gl = jnp.broadcast_to(gl[:, None, None], (N_HEADS, D_HEAD, D_HEAD)).reshape(BRANCH_W, D_HEAD)
    kdec = jnp.repeat(kdec.T, D_HEAD, axis=1)
    return dec, qdec, kdec, gl


def _band_bias(rel_bias, tq, win, q_minus_k0, valid):
    length = tq + win - 1
    d = np.arange(length) - (tq - 1) - q_minus_k0
    idx = np.clip(d, -REL_CLIP, REL_CLIP) + REL_CLIP
    g = rel_bias[:, idx].astype(F32) * LOG2E
    gp = jnp.concatenate([g, jnp.zeros((N_HEADS, 1), F32)], axis=1)
    m = jnp.tile(gp, (1, tq))[:, :tq * length].reshape(N_HEADS, tq, length)
    tile = m[:, :, tq - 1:tq - 1 + win]
    return jnp.where(valid[None], tile, NEG_INF).reshape(N_HEADS * tq, win)


def _pack_w_in(w_in_l):
    cols = []
    src = 0
    for w in _IN_WIDTH:
        seg = w_in_l[:, src:src + w]
        cols.append(jnp.pad(seg, ((0, 0), (0, _round_up(w, LANE) - w))))
        src += w
    return jnp.concatenate(cols, axis=1).astype(BF16)


def _pad_rows(t, n):
    return jnp.pad(t, ((0, 0), (0, n - t.shape[1]), (0, 0)))


def _token_mixers(x, pos0, past, lw, *, prompt, layer, stacked):
    (w_in_p, qn, wuq, kvn, wukv, gn_g, gn_b, rel_bias, b, length) = lw
    assert pos0 % CHUNK == 0
    pos = pos0 + jnp.arange(length)
    lc = 256 if prompt else length
    dec, qdec, kdec, gl = _retention_tables(lc)
    tables = list(_rope_tables(pos, DR_A)) + list(_rope_tables(pos, D_HEAD)) + [jnp.tile(kdec, (length // lc, 1))]
    if not prompt:
        tables = [jnp.tile(t, (b, 1)) for t in tables]
    keep = min(PREV_CHUNKS * CHUNK, length)
    outs, stacked = _inproj(x, w_in_p, qn, wuq, kvn, wukv, tables, tm=TOKEN_TILE if prompt else b * length,
                            layer=layer, stacked=stacked, seq_len=length if prompt else None,
                            keep_rows=keep if prompt and keep < length else None)
    per_batch = lambda o: o.reshape(o.shape[:-2] + (b, length, o.shape[-1]))
    (q_a, ckv, kpe, kf, v_a, rq, rk, rkd, rv, rg, sq, sk, sv, bq, bk, bv) = [
        per_batch(o) for o in outs[:len(_INPROJ_OUT)]]
    here = lambda t: (t, layer)
    only = lambda t: (t[None], 0)

    if prompt:
        tq = 256
        qa_t, va_t, sq_t, sv_t, bq_t, bv_t = outs[len(_INPROJ_OUT):len(_INPROJ_OUT) + 6]
        nseq = SEQS_PER_STEP if b % SEQS_PER_STEP == 0 else 1
        o_a = _mla_attn_t(qa_t, kf, va_t, tq=tq, nseq=nseq)
        o_c = _sb_attn_t(sq_t, here(sk), sv_t, tq=tq, nseq=nseq)
        s0 = jnp.zeros((b, BRANCH_W, D_HEAD), F32)
        win = 3 * tq
        i = np.arange(tq)[:, None]
        c = np.arange(win)[None, :]
        variants = []
        for t in range(3):
            qc, kc = i // CHUNK + t * (tq // CHUNK), c // CHUNK
            variants.append(_band_bias(rel_bias, tq, win, t * tq, (kc <= qc) & (kc >= qc - PREV_CHUNKS)))
        bias_t = jnp.stack(variants).transpose(0, 2, 1)
        o_d = _band_attn_t(bq_t, here(bk), bv_t, bias_t, tq=tq, win=win, back=2,
                           nseq=2 * nseq if b % (2 * nseq) == 0 else nseq)
    else:
        c_ckv, c_kpe, s_prev, c_sk, c_sv, c_bk, c_bv = past
        n_past = c_ckv.shape[2]
        tko = LANE
        kf_c, v_c = _expand_latent(c_ckv.reshape(DEPTH, b * n_past, KV_RANK),
                                   c_kpe.reshape(DEPTH, b * n_past, DR_A), wukv, layer, TOKEN_TILE)
        o_a = _mla_attn(q_a, only(_pad_rows(kf, tko)), only(_pad_rows(v_a, tko)),
                        only(kf_c.reshape(b, n_past, QA_W)), only(v_c.reshape(b, n_past, BRANCH_W)),
                        tq=length, tk=256, n_own=length, causal_tiles=False)
        o_c = _sb_attn(sq, only(_pad_rows(sk[layer], tko)), only(_pad_rows(sv[layer], tko)),
                       (c_sk.reshape(DEPTH, b, n_past, BRANCH_W), layer),
                       (c_sv.reshape(DEPTH, b, n_past, BRANCH_W), layer),
                       tq=length, tk=256, n_own=length, causal_tiles=False)
        s0 = s_prev[layer].reshape(b, BRANCH_W, D_HEAD)
        n_band = c_bk.shape[2]
        n_keys = n_band + length
        win = _round_up(n_keys, LANE)
        bk_all = _pad_rows(jnp.concatenate([c_bk[layer].reshape(b, n_band, BRANCH_W), bk[layer]], axis=1), win)
        bv_all = _pad_rows(jnp.concatenate([c_bv[layer].reshape(b, n_band, BRANCH_W), bv[layer]], axis=1), win)
        k_pos = pos0 - n_band + np.arange(win)
        q_pos = pos0 + np.arange(length)
        qc, kc = q_pos[:, None] // CHUNK, k_pos[None, :] // CHUNK
        valid = (np.arange(win)[None, :] < n_keys) & (k_pos[None, :] >= 0) & (kc <= qc) & (kc >= qc - PREV_CHUNKS)
        bias = _band_bias(rel_bias, length, win, n_band, valid)[None]
        o_d = _band_attn(bq, only(bk_all), only(bv_all), bias, tq=length, win=win, back=0)

    o_r, s_ret = _retention(rq, rk, rkd, rv, rg, s0, dec, qdec, gl,
                            gn_g.reshape(N_HEADS, 1, D_HEAD), gn_b.reshape(N_HEADS, 1, D_HEAD), lc=lc,
                            nseq=SEQS_PER_STEP if b % SEQS_PER_STEP == 0 else 1)
    s_ret = s_ret.reshape(b, N_HEADS, D_HEAD, D_HEAD)
    flat = lambda t: t.reshape(b * length, BRANCH_W)
    return (flat(o_a), flat(o_r), flat(o_c), flat(o_d)), s_ret, stacked


def _state_outputs(stacked, s_ret, b, length):
    per_seq = lambda t: t.reshape(DEPTH, b, t.shape[1] // b, t.shape[-1])
    heads4 = lambda t: per_seq(t).reshape(DEPTH, b, t.shape[1] // b, N_HEADS, D_HEAD)
    ckv, kpe, sk, sv, bk, bv = stacked[:6]
    if len(stacked) > 6:
        bk, bv = stacked[6:]
    return (per_seq(ckv), per_seq(kpe), jnp.stack(s_ret, axis=0), heads4(sk), heads4(sv), heads4(bk), heads4(bv))


def kernel(x_prompt, x_sample, cache_mla_ckv, cache_mla_kpe, state_ret, cache_sb_k, cache_sb_v, cache_band_k, cache_band_v, w_in, mla_q_norm, mla_w_uq, mla_kv_norm, mla_w_ukv, ret_gn_g, ret_gn_b, band_rel_bias, w_branch, w_o, ln1_g, ln1_b, w_router, b_router, w_exp_gate, w_exp_up, w_exp_down, ln2_g, ln2_b):
    bp, lp, _ = x_prompt.shape
    bs, ls, _ = x_sample.shape
    past_len = cache_mla_ckv.shape[2]
    xp = x_prompt.reshape(bp * lp, D_MODEL)
    xs = x_sample.reshape(bs * ls, D_MODEL)

    wr = jnp.pad(w_router, ((0, 0), (0, LANE - N_EXPERTS)))
    wrh = wr.astype(BF16)
    wrl = (wr - wrh.astype(F32)).astype(BF16)
    br = jnp.pad(b_router, (0, LANE - N_EXPERTS)).reshape(LANE, 1)

    we_gate, we_up, we_down = w_exp_gate.astype(BF16), w_exp_up.astype(BF16), w_exp_down.astype(BF16)
    past = (cache_mla_ckv, cache_mla_kpe, state_ret, cache_sb_k, cache_sb_v, cache_band_k, cache_band_v)
    ret_p, ret_s = [], []
    stacked_p = stacked_s = None
    for l in range(DEPTH):
        wuq = mla_w_uq[l]
        wuq = jnp.concatenate([wuq[:, :, :DN_A].reshape(Q_RANK, -1), wuq[:, :, DN_A:].reshape(Q_RANK, -1)], axis=1)
        wuq = jnp.pad(wuq, ((0, _round_up(Q_RANK, LANE) - Q_RANK), (0, 0))).astype(BF16)
        wukv = mla_w_ukv[l]
        wukv = jnp.concatenate([wukv[:, :, :DN_A].reshape(KV_RANK, -1), wukv[:, :, DN_A:].reshape(KV_RANK, -1)],
                               axis=1).astype(BF16)
        qn = jnp.pad(mla_q_norm[l], (0, _round_up(Q_RANK, LANE) - Q_RANK)).reshape(1, -1)
        kvn = mla_kv_norm[l].reshape(1, KV_RANK)
        w_in_p = _pack_w_in(w_in[l])
        wg = w_in[l][:, GATE_COL0:].astype(BF16)
        wb = w_branch[l].astype(BF16)
        wo = w_o[l].astype(BF16)
        g1, b1 = ln1_g[l].reshape(1, D_MODEL), ln1_b[l].reshape(1, D_MODEL)
        g2, b2 = ln2_g[l].reshape(1, D_MODEL), ln2_b[l].reshape(1, D_MODEL)
        lw =(w_in_p, qn, wuq, kvn, wukv, ret_gn_g[l], ret_gn_b[l], band_rel_bias[l])

        br_p, s_ret_p, stacked_p = _token_mixers(xp, 0, None, lw + (bp, lp), prompt=True,
                                                 layer=l, stacked=stacked_p)
        br_s, s_ret_s, stacked_s = _token_mixers(xs, past_len, past, lw + (bs, ls), prompt=False,
                                                 layer=l, stacked=stacked_s)
        xp = _merge(xp, br_p, wg, wb, wo, g1, b1, tm=TOKEN_TILE)
        xs = _merge(xs, br_s, wg, wb, wo, g1, b1, tm=TOKEN_TILE)
        xp = _moe(xp, wrh, wrl, br, we_gate, we_up, we_down, g2, b2, tm=TOKEN_TILE, layer=l)
        xs = _moe(xs, wrh, wrl, br, we_gate, we_up, we_down, g2, b2, tm=TOKEN_TILE, layer=l)
        ret_p.append(s_ret_p)
        ret_s.append(s_ret_s)

    return ((xp.reshape(bp, lp, D_MODEL), xs.reshape(bs, ls, D_MODEL))
            + _state_outputs(stacked_p, ret_p, bp, lp)
            + _state_outputs(stacked_s, ret_s, bs, ls))
```

```python
import functools

import jax
import jax.numpy as jnp
import numpy as np
from jax import lax
from jax.experimental import pallas as pl
from jax.experimental.pallas import tpu as pltpu

D_MODEL = 1024
DEPTH = 2
CHUNK = 64
N_BRANCH = 4
BRANCH_W = D_MODEL // 4
N_HEADS = 4
D_HEAD = BRANCH_W // N_HEADS
DN_A = 64
DR_A = 32
DQK_A = DN_A + DR_A
Q_RANK = (3 * D_MODEL) // 16
KV_RANK = D_MODEL // 8
PREV_CHUNKS = 8
REL_CLIP = 128
ROPE_BASE = 10000.0
N_EXPERTS = 16
N_GROUPS = 4
EXPERTS_PER_GROUP = N_EXPERTS // N_GROUPS
TOP_K = 2
D_EXPERT = D_MODEL // 4
ALPHA = (2.0 * DEPTH) ** 0.25
EPS = 1e-5
NEG_INF = -1e30
LOG2E = 1.4426950408889634
SB_DEAD_LOG2 = -150.0

F32 = jnp.float32
BF16 = jnp.bfloat16

V7X_VMEM_LIMIT = 56 * 1024 * 1024
LANE = 128
TOKEN_TILE = 512
SEQS_PER_STEP = 2

_IN_NAMES = ("c_q", "c_kv", "k_pe", "rq", "rk", "rv", "rg", "sq", "sk", "sv", "bq", "bk", "bv")
_IN_WIDTH = (Q_RANK, KV_RANK, DR_A) + (BRANCH_W,) * 10
QA_W = N_HEADS * LANE


def _round_up(n, m):
    return (n + m - 1) // m * m


_IN_OFF = {}
_off = 0
for _n, _w in zip(_IN_NAMES, _IN_WIDTH):
    _IN_OFF[_n] = (_off, _round_up(_w, LANE))
    _off += _round_up(_w, LANE)
IN_PACKED = _off
GATE_COL0 = sum(_IN_WIDTH)


def _params(sem):
    return pltpu.CompilerParams(dimension_semantics=sem, vmem_limit_bytes=V7X_VMEM_LIMIT)


def _nt_dot(a, b):
    return lax.dot_general(a, b, (((1,), (1,)), ((), ())), preferred_element_type=F32)


def _tn_dot(a, b):
    return lax.dot_general(a, b, (((0,), (0,)), ((), ())), preferred_element_type=F32)


def _dot(a, b):
    return jnp.dot(a, b, preferred_element_type=F32)


def _layer_norm(v, g, b):
    mu = jnp.mean(v, axis=-1, keepdims=True)
    d = v - mu
    var = jnp.mean(d * d, axis=-1, keepdims=True)
    return d * lax.rsqrt(var + EPS) * g + b


def _sigmoid(v):
    return 0.5 * jnp.tanh(0.5 * v) + 0.5


def _head_slice(h):
    return slice(h * D_HEAD, (h + 1) * D_HEAD)


def _stack_heads(q, lane_sets):
    lane = lax.broadcasted_iota(jnp.int32, q.shape, 1)
    zero = jnp.zeros_like(q)
    parts = []
    for h in range(N_HEADS):
        keep = None
        for lo, hi in lane_sets(h):
            m = jnp.logical_and(lane >= lo, lane < hi)
            keep = m if keep is None else jnp.logical_or(keep, m)
        parts.append(jnp.where(keep, q, zero))
    return jnp.concatenate(parts, axis=0)


def _own_lanes(h):
    return ((h * D_HEAD, (h + 1) * D_HEAD),)


def _mla_lanes(h):
    return ((h * LANE, (h + 1) * LANE),)


def _store_heads(o_ref, parts):
    for h, p in enumerate(parts):
        o_ref[:, _head_slice(h)] = p.astype(o_ref.dtype)


def _expand_kernel(ckv_ref, kpe_ref, w_ref, kf_ref, v_ref):
    kvx = _dot(ckv_ref[...].astype(BF16), w_ref[...])
    kp = kpe_ref[...].astype(BF16)
    tail = jnp.zeros((kp.shape[0], LANE - DN_A - DR_A), BF16)
    for h in range(N_HEADS):
        kf_ref[:, h * LANE:(h + 1) * LANE] = jnp.concatenate(
            [kvx[:, h * LANE:h * LANE + DN_A].astype(BF16), kp, tail], axis=1)
    v_ref[...] = kvx[:, QA_W:].astype(BF16)


def _expand_latent(ckv, kpe, w_ukv, layer, tm):
    m = ckv.shape[1]
    assert m % tm == 0
    return pl.pallas_call(
        _expand_kernel,
        grid=(m // tm,),
        in_specs=[pl.BlockSpec((None, tm, KV_RANK), lambda i: (layer, i, 0)),
                  pl.BlockSpec((None, tm, DR_A), lambda i: (layer, i, 0)),
                  pl.BlockSpec(w_ukv.shape, lambda i: (0, 0))],
        out_specs=[pl.BlockSpec((tm, QA_W), lambda i: (i, 0)),
                   pl.BlockSpec((tm, BRANCH_W), lambda i: (i, 0))],
        out_shape=[jax.ShapeDtypeStruct((m, QA_W), BF16),
                   jax.ShapeDtypeStruct((m, BRANCH_W), BF16)],
        compiler_params=_params(("parallel",)),
        name="expand_latent",
    )(ckv, kpe, w_ukv)


def _rope_block(x, cos, sin_a, sin_b, half):
    return x * cos + pltpu.roll(x, LANE - half, 1) * sin_a + pltpu.roll(x, half, 1) * sin_b


def _inproj_kernel(x_ref, w_ref, qn_ref, wuq_ref, kvn_ref, wukv_ref,
                   c32_ref, a32_ref, b32_ref, c64_ref, a64_ref, b64_ref, kdec_ref, cqa_ref, aqa_ref, bqa_ref,
                   qa_ref, ckv_ref, kpe_ref, kf_ref, va_ref,
                   rq_ref, rk_ref, rkd_ref, rv_ref, rg_ref,
                   sq_ref, sk_ref, sv_ref, bq_ref, bk_ref, bv_ref, t_refs=(), keep_refs=()):
    qat_ref, vat_ref, sqt_ref, svt_ref, bqt_ref, bvt_ref = t_refs if t_refs else (None,) * 6
    z = _dot(x_ref[...].astype(BF16), w_ref[...])

    def seg(name):
        o, w = _IN_OFF[name]
        return z[:, o:o + w]

    cq = seg("c_q")
    cqn = cq * lax.rsqrt(jnp.sum(cq * cq, axis=1, keepdims=True) * (1.0 / Q_RANK) + EPS) * qn_ref[...]
    qa = _dot(cqn.astype(BF16), wuq_ref[...])
    scale_a = DQK_A ** -0.5 * LOG2E
    qa_s = jnp.concatenate(
        [_rope_block(qa[:, h * LANE:(h + 1) * LANE], cqa_ref[...], aqa_ref[...], bqa_ref[...], DR_A // 2)
         for h in range(N_HEADS)], axis=1) * scale_a
    qa_ref[...] = qa_s.astype(BF16)
    if qat_ref is not None:
        qat_ref[...] = qa_s.T.astype(BF16)

    ckv_raw = seg("c_kv")
    ckv = ckv_raw * lax.rsqrt(jnp.mean(ckv_raw * ckv_raw, axis=1, keepdims=True) + EPS) * kvn_ref[...]
    ckv_ref[...] = ckv
    kvx = _dot(ckv.astype(BF16), wukv_ref[...])
    kp = _rope_block(seg("k_pe"), c32_ref[...], a32_ref[...], b32_ref[...], DR_A // 2)
    kpe_ref[...] = kp[:, :DR_A]
    kp_at = pltpu.roll(kp, DN_A, 1)
    for h in range(N_HEADS):
        kf_ref[:, h * LANE:(h + 1) * LANE] = (kvx[:, h * LANE:(h + 1) * LANE] + kp_at).astype(BF16)
    va_ref[...] = kvx[:, QA_W:].astype(BF16)
    if vat_ref is not None:
        vat_ref[...] = kvx[:, QA_W:].T.astype(BF16)

    rq, rk = seg("rq"), seg("rk")
    kdec = kdec_ref[...]
    for blk in range(BRANCH_W // LANE):
        cols = slice(blk * LANE, (blk + 1) * LANE)
        rq_ref[:, cols] = _rope_block(rq[:, cols], c64_ref[...], a64_ref[...], b64_ref[...],
                                      D_HEAD // 2).astype(BF16)
        rkb = _rope_block(rk[:, cols], c64_ref[...], a64_ref[...], b64_ref[...], D_HEAD // 2) * (D_HEAD ** -0.5)
        rk_ref[:, cols] = rkb.astype(BF16)
        rkd_ref[:, cols] = (rkb * kdec[:, cols]).astype(BF16)
    rv_ref[...] = seg("rv").astype(BF16)
    rg_ref[...] = seg("rg")

    scale_h = D_HEAD ** -0.5 * LOG2E
    sq = seg("sq") * scale_h
    sq_ref[...] = sq.astype(BF16)
    sk_ref[...] = seg("sk")
    sv_ref[...] = seg("sv")
    if sqt_ref is not None:
        sqt_ref[...] = sq.T.astype(BF16)
        svt_ref[...] = seg("sv").T.astype(BF16)
    bq = seg("bq") * scale_h
    bq_ref[...] = bq.astype(BF16)
    bk_ref[...] = seg("bk")
    bv_ref[...] = seg("bv")
    if bqt_ref is not None:
        bqt_ref[...] = bq.T.astype(BF16)
        bvt_ref[...] = seg("bv").T.astype(BF16)
    if keep_refs:
        keep_refs[0][...] = seg("bk")
        keep_refs[1][...] = seg("bv")


_INPROJ_OUT = (
    (QA_W, BF16), (KV_RANK, F32), (DR_A, F32), (QA_W, BF16), (BRANCH_W, BF16),
    (BRANCH_W, BF16), (BRANCH_W, BF16), (BRANCH_W, BF16), (BRANCH_W, BF16), (BRANCH_W, F32),
    (BRANCH_W, BF16), (BRANCH_W, F32), (BRANCH_W, F32), (BRANCH_W, BF16), (BRANCH_W, F32), (BRANCH_W, F32))


_STATE_OUTS = (1, 2, 11, 12, 14, 15)


def _inproj_entry(*refs, n_in, n_alias, n_t, n_keep):
    outs = refs[n_in + n_alias:]
    n_main = len(_INPROJ_OUT)
    _inproj_kernel(*refs[:n_in], *outs[:n_main], t_refs=outs[n_main:n_main + n_t],
                   keep_refs=outs[n_main + n_t:n_main + n_t + n_keep])


def _inproj(x, w_in_p, qn, wuq, kvn, wukv, tables, *, tm, layer, stacked, seq_len=None, keep_rows=None):
    m = x.shape[0]
    tm = min(tm, m)
    assert m % tm == 0
    n_pos_tiles = tables[0].shape[0] // tm
    assert tables[0].shape[0] % tm == 0
    row = lambda i: (i, 0)
    const = lambda i: (0, 0)
    pos = lambda i: (i % n_pos_tiles, 0)
    in_specs = [pl.BlockSpec((tm, D_MODEL), row),
                pl.BlockSpec(w_in_p.shape, const),
                pl.BlockSpec(qn.shape, const),
                pl.BlockSpec(wuq.shape, const),
                pl.BlockSpec(kvn.shape, const),
                pl.BlockSpec(wukv.shape, const)]
    in_specs += [pl.BlockSpec((tm, t.shape[1]), pos) for t in tables]
    n_in = len(in_specs)
    out_specs, out_shape = [], []
    for k, (w, dt) in enumerate(_INPROJ_OUT):
        if k in _STATE_OUTS:
            out_specs.append(pl.BlockSpec((None, tm, w), lambda i: (layer, i, 0)))
            out_shape.append(jax.ShapeDtypeStruct((DEPTH, m, w), dt))
        else:
            out_specs.append(pl.BlockSpec((tm, w), row))
            out_shape.append(jax.ShapeDtypeStruct((m, w), dt))
    if seq_len is not None:
        assert seq_len % tm == 0 and m % seq_len == 0
        tps = seq_len // tm
        for w in (QA_W,) + (BRANCH_W,) * 5:
            out_specs.append(pl.BlockSpec((None, w, tm), lambda i: (i // tps, 0, i % tps)))
            out_shape.append(jax.ShapeDtypeStruct((m // seq_len, w, seq_len), BF16))
    n_t = len(out_specs) - len(_INPROJ_OUT)
    state_outs = list(_STATE_OUTS)
    if keep_rows is not None:
        assert seq_len is not None and keep_rows % tm == 0 and keep_rows <= seq_len
        tps, ktiles = seq_len // tm, keep_rows // tm
        kept = lambda i: (layer, (i // tps) * ktiles + jnp.maximum(i % tps - (tps - ktiles), 0), 0)
        for _ in range(2):
            state_outs.append(len(out_specs))
            out_specs.append(pl.BlockSpec((None, tm, BRANCH_W), kept))
            out_shape.append(jax.ShapeDtypeStruct((DEPTH, m // seq_len * keep_rows, BRANCH_W), F32))
    aliases = {}
    extra = ()
    if stacked is not None:
        extra = tuple(stacked)
        assert len(extra) == len(state_outs)
        in_specs += [pl.BlockSpec(memory_space=pl.ANY)] * len(extra)
        aliases = {n_in + k: o for k, o in enumerate(state_outs)}
    outs = pl.pallas_call(
        functools.partial(_inproj_entry, n_in=n_in, n_alias=len(extra), n_t=n_t,
                          n_keep=len(state_outs) - len(_STATE_OUTS)),
        grid=(m // tm,),
        in_specs=in_specs,
        out_specs=out_specs,
        out_shape=out_shape,
        input_output_aliases=aliases,
        compiler_params=_params(("arbitrary",)),
        name="inproj",
    )(x, w_in_p, qn, wuq, kvn, wukv, *tables, *extra)
    return outs, tuple(outs[k] for k in state_outs)


def _tile_spec(src, rows):
    arr, layer = src
    return pl.BlockSpec((None, None, rows, arr.shape[3]), lambda i, t: (layer, i, t, 0))


def _full_spec(src):
    arr, layer = src
    return pl.BlockSpec((None, None, arr.shape[2], arr.shape[3]), lambda i, t: (layer, i, 0, 0))


def _mla_kernel(q_ref, kfo_ref, vo_ref, kfp_ref, vp_ref, o_ref, *, tq, tk, n_own, n_past):
    qi = pl.program_id(1)
    qm = _stack_heads(q_ref[...], _mla_lanes)
    rows = N_HEADS * tq
    tko = kfo_ref.shape[0]
    n_loop = qi if n_past is None else n_past

    def softmax_pv(s, v, carry):
        m, accs = carry
        m_new = jnp.maximum(m, jnp.max(s, axis=1, keepdims=True))
        pb = jnp.exp2(s - m_new).astype(BF16)
        a = jnp.exp2(m - m_new)
        ones = jnp.ones((v.shape[0], D_HEAD), BF16)
        accs = tuple(a[h * tq:(h + 1) * tq] * accs[h]
                     + _dot(pb[h * tq:(h + 1) * tq], jnp.concatenate([v[:, _head_slice(h)], ones], axis=1))
                     for h in range(N_HEADS))
        return m_new, accs

    def past_scores(j):
        start = pl.multiple_of(jnp.minimum(j, jnp.maximum(n_loop - 1, 0)) * tk, tk)
        return _nt_dot(qm, kfp_ref[pl.ds(start, tk), :])

    row_q = lax.broadcasted_iota(jnp.int32, (rows, tko), 0) % tq
    col = lax.broadcasted_iota(jnp.int32, (rows, tko), 1)
    own_mask = jnp.logical_and(col < n_own, col // CHUNK <= row_q // CHUNK)
    carry = (jnp.full((rows, 1), NEG_INF, F32),
             tuple(jnp.zeros((tq, 2 * D_HEAD), F32) for _ in range(N_HEADS)))
    s_next = past_scores(0)
    carry = softmax_pv(jnp.where(own_mask, _nt_dot(qm, kfo_ref[...]), NEG_INF), vo_ref[...], carry)

    def body(j, c):
        s_cur, rest = c
        s_after = past_scores(j + 1)
        start = pl.multiple_of(j * tk, tk)
        return s_after, softmax_pv(s_cur, vp_ref[pl.ds(start, tk), :], rest)

    _, (_, accs) = lax.fori_loop(0, n_loop, body, (s_next, carry))
    _store_heads(o_ref, [accs[h][:, :D_HEAD] / accs[h][:, D_HEAD:D_HEAD + 1] for h in range(N_HEADS)])


def _mla_attn(q, kf_own, v_own, kf_past, v_past, *, tq, tk, n_own, causal_tiles):
    b, lq, _ = q.shape
    nqt = lq // tq
    tko = kf_own[0].shape[2] // nqt
    lp = kf_past[0].shape[2]
    assert lq % tq == 0 and lp % tk == 0
    return pl.pallas_call(
        functools.partial(_mla_kernel, tq=tq, tk=tk, n_own=n_own,
                          n_past=None if causal_tiles else lp // tk),
        grid=(b, nqt),
        in_specs=[pl.BlockSpec((None, tq, QA_W), lambda i, t: (i, t, 0)),
                  _tile_spec(kf_own, tko), _tile_spec(v_own, tko),
                  _full_spec(kf_past), _full_spec(v_past)],
        out_specs=pl.BlockSpec((None, tq, BRANCH_W), lambda i, t: (i, t, 0)),
        out_shape=jax.ShapeDtypeStruct((b, lq, BRANCH_W), BF16),
        compiler_params=_params(("parallel", "arbitrary")),
        name="mla_attn",
    )(q, kf_own[0], v_own[0], kf_past[0], v_past[0])


def _mla_kernel_t(qt_ref, kfo_ref, vto_ref, kfp_ref, vtp_ref, o_ref, *, tq, tk, nseq):
    qi = pl.program_id(1)
    cols = N_HEADS * tq

    def scores(s, kf):
        qt = qt_ref[s]
        return jnp.concatenate([_dot(kf[:, h * LANE:(h + 1) * LANE], qt[h * LANE:(h + 1) * LANE, :])
                                for h in range(N_HEADS)], axis=1)

    def softmax_pv(st, vt, carry):
        m, l, accs = carry
        m_new = jnp.maximum(m, jnp.max(st, axis=0, keepdims=True))
        pt = jnp.exp2(st - m_new)
        a = jnp.exp2(m - m_new)
        l = a * l + jnp.sum(pt, axis=0, keepdims=True)
        pb = pt.astype(BF16)
        accs = tuple(a[:, h * tq:(h + 1) * tq] * accs[h]
                     + _dot(vt[_head_slice(h), :], pb[:, h * tq:(h + 1) * tq]) for h in range(N_HEADS))
        return m_new, l, accs

    def past_scores(s, j):
        start = pl.multiple_of(jnp.minimum(j, jnp.maximum(qi - 1, 0)) * tk, tk)
        return scores(s, kfp_ref[s, pl.ds(start, tk), :])

    key = lax.broadcasted_iota(jnp.int32, (tq, cols), 0)
    qry = lax.broadcasted_iota(jnp.int32, (tq, cols), 1) % tq
    own_mask = key // CHUNK <= qry // CHUNK
    s_next, carries = [], []
    for s in range(nseq):
        empty = (jnp.full((1, cols), NEG_INF, F32), jnp.zeros((1, cols), F32),
                 tuple(jnp.zeros((D_HEAD, tq), F32) for _ in range(N_HEADS)))
        s_next.append(past_scores(s, 0))
        carries.append(softmax_pv(jnp.where(own_mask, scores(s, kfo_ref[s]), NEG_INF), vto_ref[s], empty))

    def body(j, c):
        s_cur, rest = c
        start = pl.multiple_of(j * tk, tk)
        s_after = tuple(past_scores(s, j + 1) for s in range(nseq))
        return s_after, tuple(softmax_pv(s_cur[s], vtp_ref[s, :, pl.ds(start, tk)], rest[s])
                              for s in range(nseq))

    _, done = lax.fori_loop(0, qi, body, (tuple(s_next), tuple(carries)))
    for s in range(nseq):
        _, l, accs = done[s]
        out_t = jnp.concatenate([accs[h] / l[:, h * tq:(h + 1) * tq] for h in range(N_HEADS)], axis=0)
        o_ref[s] = out_t.T.astype(o_ref.dtype)


def _mla_attn_t(qt, kf, vt, *, tq, nseq):
    b, _, length = qt.shape
    assert length % tq == 0 and b % nseq == 0
    return pl.pallas_call(
        functools.partial(_mla_kernel_t, tq=tq, tk=tq, nseq=nseq),
        grid=(b // nseq, length // tq),
        in_specs=[pl.BlockSpec((nseq, QA_W, tq), lambda i, t: (i, 0, t)),
                  pl.BlockSpec((nseq, tq, QA_W), lambda i, t: (i, t, 0)),
                  pl.BlockSpec((nseq, BRANCH_W, tq), lambda i, t: (i, 0, t)),
                  pl.BlockSpec((nseq, length, QA_W), lambda i, t: (i, 0, 0)),
                  pl.BlockSpec((nseq, BRANCH_W, length), lambda i, t: (i, 0, 0))],
        out_specs=pl.BlockSpec((nseq, tq, BRANCH_W), lambda i, t: (i, t, 0)),
        out_shape=jax.ShapeDtypeStruct((b, length, BRANCH_W), BF16),
        compiler_params=_params(("parallel", "arbitrary")),
        name="mla_attn_t",
    )(qt, kf, vt, kf, vt)


def _sb_kernel(q_ref, ko_ref, vo_ref, kp_ref, vp_ref, o_ref, *, tq, tk, n_own, n_past):
    qi = pl.program_id(1)
    qm = _stack_heads(q_ref[...], _own_lanes)
    rows = N_HEADS * tq
    tko = ko_ref.shape[0]
    n_loop = qi if n_past is None else n_past

    def tri2(n):
        r = lax.broadcasted_iota(jnp.int32, (2 * n, n), 0) % n
        c = lax.broadcasted_iota(jnp.int32, (2 * n, n), 1)
        return jnp.where(r > c, 1.0, 0.0).astype(BF16)

    def weigh(z, v, carry, mask, tri):
        run, accs = carry
        neg_abs = lax.bitcast_convert_type(
            lax.bitcast_convert_type(z, jnp.uint32) | jnp.uint32(0x80000000), F32)
        t = jnp.log2(1.0 + jnp.exp2(neg_abs))
        log_beta = jnp.minimum(z, 0.0) - t
        log_stay = log_beta - z
        if mask is not None:
            log_stay = jnp.where(mask, log_stay, 0.0)
        hi = log_stay.astype(BF16)
        lo = (log_stay - hi.astype(F32)).astype(BF16)
        later = _dot(jnp.concatenate([hi, lo], axis=1), tri) + run
        w = jnp.exp2(log_beta + later)
        if mask is not None:
            w = jnp.where(mask, w, 0.0)
        wb = w.astype(BF16)
        vb = v.astype(BF16)
        accs = tuple(accs[h] + _dot(wb[h * tq:(h + 1) * tq], vb[:, _head_slice(h)])
                     for h in range(N_HEADS))
        return run + jnp.sum(log_stay, axis=1, keepdims=True), accs

    row_q = lax.broadcasted_iota(jnp.int32, (rows, tko), 0) % tq
    col = lax.broadcasted_iota(jnp.int32, (rows, tko), 1)
    own_mask = jnp.logical_and(col < n_own, col < row_q)

    def past_start(jj):
        return pl.multiple_of(jnp.clip(n_loop - 1 - jj, 0, kp_ref.shape[0] // tk - 1) * tk, tk)

    def past_scores(jj):
        return _nt_dot(qm, kp_ref[pl.ds(past_start(jj), tk), :].astype(BF16))

    def alive(run):
        return (jnp.max(run) > SB_DEAD_LOG2).astype(jnp.int32)

    z_next = past_scores(0)
    carry = (jnp.zeros((rows, 1), F32), tuple(jnp.zeros((tq, D_HEAD), F32) for _ in range(N_HEADS)))
    run, accs = weigh(_nt_dot(qm, ko_ref[...].astype(BF16)), vo_ref[...], carry, own_mask, tri2(tko))
    tri_past = tri2(tk)

    def cond(c):
        return jnp.logical_and(c[0] < n_loop, c[1] > 0)

    def body(c):
        jj, _, z_cur, run, accs = c
        z_after = past_scores(jj + 1)
        run, accs = weigh(z_cur, vp_ref[pl.ds(past_start(jj), tk), :], (run, accs), None, tri_past)
        return jj + 1, alive(run), z_after, run, accs

    out = lax.while_loop(cond, body, (jnp.int32(0), alive(run), z_next, run, accs))
    _store_heads(o_ref, out[4])


def _sb_attn(q, k_own, v_own, k_past, v_past, *, tq, tk, n_own, causal_tiles):
    b, lq, _ = q.shape
    nqt = lq // tq
    tko = k_own[0].shape[2] // nqt
    lp = k_past[0].shape[2]
    assert lq % tq == 0 and lp % tk == 0
    return pl.pallas_call(
        functools.partial(_sb_kernel, tq=tq, tk=tk, n_own=n_own,
                          n_past=None if causal_tiles else lp // tk),
        grid=(b, nqt),
        in_specs=[pl.BlockSpec((None, tq, BRANCH_W), lambda i, t: (i, t, 0)),
                  _tile_spec(k_own, tko), _tile_spec(v_own, tko),
                  _full_spec(k_past), _full_spec(v_past)],
        out_specs=pl.BlockSpec((None, tq, BRANCH_W), lambda i, t: (i, t, 0)),
        out_shape=jax.ShapeDtypeStruct((b, lq, BRANCH_W), BF16),
        compiler_params=_params(("parallel", "arbitrary")),
        name="sb_attn",
    )(q, k_own[0], v_own[0], k_past[0], v_past[0])


def _sb_kernel_t(qt_ref, ko_ref, vto_ref, kp_ref, vtp_ref, o_ref, *, tq, tk, nseq):
    qi = pl.program_id(1)
    qmts = []
    for s in range(nseq):
        qt = qt_ref[s]
        feat = lax.broadcasted_iota(jnp.int32, qt.shape, 0)
        qmts.append(jnp.concatenate(
            [jnp.where(jnp.logical_and(feat >= h * D_HEAD, feat < (h + 1) * D_HEAD), qt, jnp.zeros_like(qt))
             for h in range(N_HEADS)], axis=1))
    cols = N_HEADS * tq

    def tri2(n):
        r = lax.broadcasted_iota(jnp.int32, (n, 2 * n), 0)
        c = lax.broadcasted_iota(jnp.int32, (n, 2 * n), 1) % n
        return jnp.where(c > r, 1.0, 0.0).astype(BF16)

    def weigh(zt, vt, carry, mask, tri):
        run, accs = carry
        neg_abs = lax.bitcast_convert_type(
            lax.bitcast_convert_type(zt, jnp.uint32) | jnp.uint32(0x80000000), F32)
        t = jnp.log2(1.0 + jnp.exp2(neg_abs))
        log_beta = jnp.minimum(zt, 0.0) - t
        log_stay = log_beta - zt
        if mask is not None:
            log_stay = jnp.where(mask, log_stay, 0.0)
        hi = log_stay.astype(BF16)
        lo = (log_stay - hi.astype(F32)).astype(BF16)
        later = _dot(tri, jnp.concatenate([hi, lo], axis=0)) + run
        w = jnp.exp2(log_beta + later)
        if mask is not None:
            w = jnp.where(mask, w, 0.0)
        wb = w.astype(BF16)
        accs = tuple(accs[h] + _dot(vt[_head_slice(h), :], wb[:, h * tq:(h + 1) * tq])
                     for h in range(N_HEADS))
        return run + jnp.sum(log_stay, axis=0, keepdims=True), accs

    def past_start(jj):
        return pl.multiple_of(jnp.clip(qi - 1 - jj, 0, kp_ref.shape[1] // tk - 1) * tk, tk)

    def past_scores(s, jj):
        return _dot(kp_ref[s, pl.ds(past_start(jj), tk), :].astype(BF16), qmts[s])

    def alive(runs):
        top = jnp.max(runs[0])
        for r in runs[1:]:
            top = jnp.maximum(top, jnp.max(r))
        return (top > SB_DEAD_LOG2).astype(jnp.int32)

    key = lax.broadcasted_iota(jnp.int32, (tq, cols), 0)
    qry = lax.broadcasted_iota(jnp.int32, (tq, cols), 1) % tq
    own_mask = key < qry
    tri_own = tri2(tq)
    z_next, runs, accss = [], [], []
    for s in range(nseq):
        z_next.append(past_scores(s, 0))
        empty = (jnp.zeros((1, cols), F32), tuple(jnp.zeros((D_HEAD, tq), F32) for _ in range(N_HEADS)))
        run, accs = weigh(_dot(ko_ref[s].astype(BF16), qmts[s]), vto_ref[s], empty, own_mask, tri_own)
        runs.append(run)
        accss.append(accs)
    tri_past = tri2(tk)

    def cond(c):
        return jnp.logical_and(c[0] < qi, c[1] > 0)

    def body(c):
        jj, _, z_cur, runs, accss = c
        z_after = tuple(past_scores(s, jj + 1) for s in range(nseq))
        new = [weigh(z_cur[s], vtp_ref[s, :, pl.ds(past_start(jj), tk)], (runs[s], accss[s]), None, tri_past)
               for s in range(nseq)]
        runs = tuple(n[0] for n in new)
        return jj + 1, alive(runs), z_after, runs, tuple(n[1] for n in new)

    out = lax.while_loop(cond, body, (jnp.int32(0), alive(runs), tuple(z_next), tuple(runs), tuple(accss)))
    for s in range(nseq):
        o_ref[s] = jnp.concatenate(out[4][s], axis=0).T.astype(o_ref.dtype)


def _sb_attn_t(qt, k, vt, *, tq, nseq):
    b, _, length = qt.shape
    k_arr, layer = k
    assert length % tq == 0 and b % nseq == 0
    return pl.pallas_call(
        functools.partial(_sb_kernel_t, tq=tq, tk=tq, nseq=nseq),
        grid=(b // nseq, length // tq),
        in_specs=[pl.BlockSpec((nseq, BRANCH_W, tq), lambda i, t: (i, 0, t)),
                  pl.BlockSpec((None, nseq, tq, BRANCH_W), lambda i, t: (layer, i, t, 0)),
                  pl.BlockSpec((nseq, BRANCH_W, tq), lambda i, t: (i, 0, t)),
                  pl.BlockSpec((None, nseq, length, BRANCH_W), lambda i, t: (layer, i, 0, 0)),
                  pl.BlockSpec((nseq, BRANCH_W, length), lambda i, t: (i, 0, 0))],
        out_specs=pl.BlockSpec((nseq, tq, BRANCH_W), lambda i, t: (i, t, 0)),
        out_shape=jax.ShapeDtypeStruct((b, length, BRANCH_W), BF16),
        compiler_params=_params(("parallel", "arbitrary")),
        name="sb_attn_t",
    )(qt, k_arr, vt, k_arr, vt)


def _ret_kernel(q_ref, k_ref, kd_ref, v_ref, rg_ref, s0_ref, dec_ref, qdec_ref, gl_ref,
                gng_ref, gnb_ref, o_ref, sout_ref, state_ref, *, lc, nseq):
    c = pl.program_id(1)

    @pl.when(c == 0)
    def _():
        state_ref[...] = s0_ref[...]

    for s in range(nseq):
        qm = _stack_heads(q_ref[s], _own_lanes)
        v = v_ref[s]
        state = state_ref[s]
        scores = (_nt_dot(qm, k_ref[s]) * dec_ref[...]).astype(BF16)
        cross = _dot(qm, state.astype(BF16)) * qdec_ref[...]
        kv_full = _tn_dot(kd_ref[s], v)
        new_state = gl_ref[...] * state + jnp.concatenate(
            [kv_full[_head_slice(h), _head_slice(h)] for h in range(N_HEADS)], axis=0)
        state_ref[s] = new_state

        rg = rg_ref[s]
        for h in range(N_HEADS):
            o = _dot(scores[h * lc:(h + 1) * lc], v[:, _head_slice(h)]) + cross[h * lc:(h + 1) * lc]
            mu = jnp.mean(o, axis=-1, keepdims=True)
            d = o - mu
            var = jnp.mean(d * d, axis=-1, keepdims=True)
            y = d * lax.rsqrt(var + EPS) * gng_ref[h] + gnb_ref[h]
            g = rg[:, _head_slice(h)]
            o_ref[s, :, _head_slice(h)] = (y * (g * _sigmoid(g))).astype(o_ref.dtype)

    @pl.when(c == pl.num_programs(1) - 1)
    def _():
        sout_ref[...] = state_ref[...]


def _retention(q, k, kd, v, rg, s0, dec, qdec, gl, gng, gnb, *, lc, nseq):
    b, length, _ = q.shape
    assert length % lc == 0 and b % nseq == 0
    seq = lambda i, t: (i, t, 0)
    st = lambda i, t: (i, 0, 0)
    c2 = lambda i, t: (0, 0)
    c3 = lambda i, t: (0, 0, 0)
    return pl.pallas_call(
        functools.partial(_ret_kernel, lc=lc, nseq=nseq),
        grid=(b // nseq, length // lc),
        in_specs=[pl.BlockSpec((nseq, lc, BRANCH_W), seq)] * 5
        + [pl.BlockSpec((nseq, BRANCH_W, D_HEAD), st),
           pl.BlockSpec(dec.shape, c2), pl.BlockSpec(qdec.shape, c2), pl.BlockSpec(gl.shape, c2),
           pl.BlockSpec(gng.shape, c3), pl.BlockSpec(gnb.shape, c3)],
        out_specs=[pl.BlockSpec((nseq, lc, BRANCH_W), seq),
                   pl.BlockSpec((nseq, BRANCH_W, D_HEAD), st)],
        out_shape=[jax.ShapeDtypeStruct((b, length, BRANCH_W), BF16),
                   jax.ShapeDtypeStruct((b, BRANCH_W, D_HEAD), F32)],
        scratch_shapes=[pltpu.VMEM((nseq, BRANCH_W, D_HEAD), F32)],
        compiler_params=_params(("parallel", "arbitrary")),
        name="retention",
    )(q, k, kd, v, rg, s0, dec, qdec, gl, gng, gnb)


def _band_kernel(q_ref, k_ref, v_ref, bias_ref, o_ref, *, tq, win, back):
    qi = pl.program_id(1)
    start = pl.multiple_of(jnp.maximum(qi - back, 0) * tq, tq)
    k = k_ref[pl.ds(start, win), :].astype(BF16)
    v = v_ref[pl.ds(start, win), :].astype(BF16)
    qm = _stack_heads(q_ref[...], _own_lanes)
    s = _nt_dot(qm, k) + bias_ref[...]
    pb = jnp.exp2(s - jnp.max(s, axis=1, keepdims=True)).astype(BF16)
    ones = jnp.ones((win, D_HEAD), BF16)
    outs = []
    for h in range(N_HEADS):
        o = _dot(pb[h * tq:(h + 1) * tq], jnp.concatenate([v[:, _head_slice(h)], ones], axis=1))
        outs.append(o[:, :D_HEAD] / o[:, D_HEAD:D_HEAD + 1])
    _store_heads(o_ref, outs)


def _band_attn(q, k, v, bias, *, tq, win, back):
    b, lq, _ = q.shape
    nvar = bias.shape[0]
    assert lq % tq == 0
    return pl.pallas_call(
        functools.partial(_band_kernel, tq=tq, win=win, back=back),
        grid=(b, lq // tq),
        in_specs=[pl.BlockSpec((None, tq, BRANCH_W), lambda i, t: (i, t, 0)),
                  _full_spec(k), _full_spec(v),
                  pl.BlockSpec((None, N_HEADS * tq, win), lambda i, t: (jnp.minimum(t, nvar - 1), 0, 0))],
        out_specs=pl.BlockSpec((None, tq, BRANCH_W), lambda i, t: (i, t, 0)),
        out_shape=jax.ShapeDtypeStruct((b, lq, BRANCH_W), BF16),
        compiler_params=_params(("parallel", "arbitrary")),
        name="band_attn",
    )(q, k[0], v[0], bias)


def _band_kernel_t(qt_ref, k_ref, vt_ref, bias_ref, o_ref, *, tq, win, back, nseq):
    qi = pl.program_id(1)
    start = pl.multiple_of(jnp.maximum(qi - back, 0) * tq, tq)
    for s in range(nseq):
        k = k_ref[s, pl.ds(start, win), :].astype(BF16)
        vt = vt_ref[s, :, pl.ds(start, win)]
        qt = qt_ref[s]
        feat = lax.broadcasted_iota(jnp.int32, qt.shape, 0)
        qmt = jnp.concatenate(
            [jnp.where(jnp.logical_and(feat >= h * D_HEAD, feat < (h + 1) * D_HEAD), qt, jnp.zeros_like(qt))
             for h in range(N_HEADS)], axis=1)
        st = _dot(k, qmt) + bias_ref[...]
        pt = jnp.exp2(st - jnp.max(st, axis=0, keepdims=True))
        l = jnp.sum(pt, axis=0, keepdims=True)
        pb = pt.astype(BF16)
        out_t = jnp.concatenate(
            [_dot(vt[_head_slice(h), :], pb[:, h * tq:(h + 1) * tq]) / l[:, h * tq:(h + 1) * tq]
             for h in range(N_HEADS)], axis=0)
        o_ref[s] = out_t.T.astype(o_ref.dtype)


def _band_attn_t(qt, k, vt, bias_t, *, tq, win, back, nseq):
    b, _, length = qt.shape
    nvar = bias_t.shape[0]
    k_arr, layer = k
    assert length % tq == 0 and b % nseq == 0
    return pl.pallas_call(
        functools.partial(_band_kernel_t, tq=tq, win=win, back=back, nseq=nseq),
        grid=(b // nseq, length // tq),
        in_specs=[pl.BlockSpec((nseq, BRANCH_W, tq), lambda i, t: (i, 0, t)),
                  pl.BlockSpec((None, nseq, k_arr.shape[2], BRANCH_W), lambda i, t: (layer, i, 0, 0)),
                  pl.BlockSpec((nseq, BRANCH_W, length), lambda i, t: (i, 0, 0)),
                  pl.BlockSpec((None, win, N_HEADS * tq), lambda i, t: (jnp.minimum(t, nvar - 1), 0, 0))],
        out_specs=pl.BlockSpec((nseq, tq, BRANCH_W), lambda i, t: (i, t, 0)),
        out_shape=jax.ShapeDtypeStruct((b, length, BRANCH_W), BF16),
        compiler_params=_params(("parallel", "arbitrary")),
        name="band_attn_t",
    )(qt, k_arr, vt, bias_t)


def _merge_kernel(x_ref, ba_ref, bb_ref, bc_ref, bd_ref, wg_ref, wb_ref, wo_ref, g_ref, b_ref, o_ref):
    x = x_ref[...]
    xb = x.astype(BF16)
    merged = None
    for n, br_ref in enumerate((ba_ref, bb_ref, bc_ref, bd_ref)):
        logits = _dot(xb, wg_ref[:, n * D_MODEL:(n + 1) * D_MODEL])
        term = _dot(br_ref[...], wb_ref[n]) * _sigmoid(logits)
        merged = term if merged is None else merged + term
    mix = _dot(merged.astype(BF16), wo_ref[...])
    o_ref[...] = _layer_norm(ALPHA * x + mix, g_ref[...], b_ref[...])


def _merge(x, branches, wg, wb, wo, g, b, *, tm):
    m = x.shape[0]
    tm = min(tm, m)
    assert m % tm == 0
    const2 = lambda i: (0, 0)
    row = lambda i: (i, 0)
    return pl.pallas_call(
        _merge_kernel,
        grid=(m // tm,),
        in_specs=[pl.BlockSpec((tm, D_MODEL), row)]
        + [pl.BlockSpec((tm, BRANCH_W), row)] * N_BRANCH
        + [pl.BlockSpec((D_MODEL, N_BRANCH * D_MODEL), const2),
           pl.BlockSpec((N_BRANCH, BRANCH_W, D_MODEL), lambda i: (0, 0, 0)),
           pl.BlockSpec((D_MODEL, D_MODEL), const2),
           pl.BlockSpec((1, D_MODEL), const2),
           pl.BlockSpec((1, D_MODEL), const2)],
        out_specs=pl.BlockSpec((tm, D_MODEL), row),
        out_shape=jax.ShapeDtypeStruct((m, D_MODEL), F32),
        compiler_params=_params(("parallel",)),
        name="merge",
    )(x, *branches, wg, wb, wo, g, b)


def _route(aff_t, sel_t):
    def top2_sum(a, b, c, d):
        hi1, lo1 = jnp.maximum(a, b), jnp.minimum(a, b)
        hi2, lo2 = jnp.maximum(c, d), jnp.minimum(c, d)
        return jnp.maximum(hi1, hi2) + jnp.maximum(jnp.minimum(hi1, hi2), jnp.maximum(lo1, lo2))

    score = [top2_sum(*sel_t[g * EXPERTS_PER_GROUP:(g + 1) * EXPERTS_PER_GROUP])
             for g in range(N_GROUPS)]
    best_here = []
    for g in range(N_GROUPS):
        ok = None
        for o in range(N_GROUPS):
            if o == g:
                continue
            c = (score[g] > score[o]) if o < g else (score[g] >= score[o])
            ok = c if ok is None else jnp.logical_and(ok, c)
        best_here.append(ok)
    picked = []
    for e in range(N_EXPERTS):
        g = e // EXPERTS_PER_GROUP
        rank = jnp.zeros_like(sel_t[e])
        for o in range(g * EXPERTS_PER_GROUP, (g + 1) * EXPERTS_PER_GROUP):
            if o == e:
                continue
            ahead = (sel_t[o] >= sel_t[e]) if o < e else (sel_t[o] > sel_t[e])
            rank = rank + jnp.where(ahead, 1.0, 0.0)
        picked.append(jnp.where(jnp.logical_and(best_here[g], rank < TOP_K), aff_t[e], 0.0))
    total = picked[0]
    for e in range(1, N_EXPERTS):
        total = total + picked[e]
    return [p / total for p in picked]


def _moe_kernel(x_ref, wr_ref, br_ref, wg_ref, wu_ref, wd_ref, g_ref, b_ref, o_ref, acc_ref):
    x = x_ref[...]
    xh = x.astype(BF16)
    xl = (x - xh.astype(F32)).astype(BF16)
    both = _dot(xh, wr_ref[...])
    logits = both[:, :LANE] + both[:, LANE:] + _dot(xl, wr_ref[:, :LANE])
    aff = _sigmoid(logits).T
    bias = br_ref[...]
    aff_t = [aff[e:e + 1, :] for e in range(N_EXPERTS)]
    sel_t = [aff_t[e] + bias[e:e + 1, :] for e in range(N_EXPERTS)]
    gate_rows = _route(aff_t, sel_t)
    tm = x.shape[0]
    gate_t = jnp.concatenate(gate_rows + [jnp.zeros((LANE - N_EXPERTS, tm), F32)], axis=0)
    gate = gate_t.T

    for e in range(N_EXPERTS):
        gt = _dot(xh, wg_ref[e])
        hmid = gt * _sigmoid(gt) * _dot(xh, wu_ref[e])
        y = _dot(hmid.astype(BF16), wd_ref[e]) * gate[:, e:e + 1]
        if e == 0:
            acc_ref[...] = y
        else:
            acc_ref[...] += y
    o_ref[...] = _layer_norm(ALPHA * x + acc_ref[...], g_ref[...], b_ref[...])


def _moe(x, wr, br, wg, wu, wd, g, b, *, tm, layer):
    m = x.shape[0]
    tm = min(tm, m)
    assert m % tm == 0
    const2 = lambda i: (0, 0)
    this_layer = lambda i: (layer, 0, 0, 0)
    return pl.pallas_call(
        _moe_kernel,
        grid=(m // tm,),
        in_specs=[pl.BlockSpec((tm, D_MODEL), lambda i: (i, 0)),
                  pl.BlockSpec((D_MODEL, 2 * LANE), const2),
                  pl.BlockSpec((LANE, 1), const2),
                  pl.BlockSpec((None, N_EXPERTS, D_MODEL, D_EXPERT), this_layer, pipeline_mode=pl.Buffered(1)),
                  pl.BlockSpec((None, N_EXPERTS, D_MODEL, D_EXPERT), this_layer, pipeline_mode=pl.Buffered(1)),
                  pl.BlockSpec((None, N_EXPERTS, D_EXPERT, D_MODEL), this_layer, pipeline_mode=pl.Buffered(1)),
                  pl.BlockSpec((1, D_MODEL), const2),
                  pl.BlockSpec((1, D_MODEL), const2)],
        out_specs=pl.BlockSpec((tm, D_MODEL), lambda i: (i, 0)),
        out_shape=jax.ShapeDtypeStruct((m, D_MODEL), F32),
        scratch_shapes=[pltpu.VMEM((tm, D_MODEL), F32)],
        compiler_params=_params(("parallel",)),
        name="moe",
    )(x, wr, br, wg, wu, wd, g, b)


def _rope_tables(pos, d):
    half = d // 2
    inv = jnp.power(ROPE_BASE, -jnp.arange(half, dtype=F32) / half)
    ang = pos.astype(F32)[:, None] * inv[None, :]
    cos, sin = jnp.cos(ang), jnp.sin(ang)
    zero = jnp.zeros_like(sin)
    rep = LANE // d
    cos_t = jnp.tile(jnp.concatenate([cos, cos], axis=1), (1, rep))
    sin_a = jnp.tile(jnp.concatenate([-sin, zero], axis=1), (1, rep))
    sin_b = jnp.tile(jnp.concatenate([zero, sin], axis=1), (1, rep))
    return cos_t, sin_a, sin_b


def _retention_tables(lc):
    log_g = jnp.log1p(-jnp.exp2(-5.0 - jnp.arange(N_HEADS, dtype=F32)))
    i = jnp.arange(lc, dtype=F32)
    diff = i[:, None] - i[None, :]
    dec = jnp.where(diff >= 0, jnp.exp(jnp.maximum(diff, 0.0)[None] * log_g[:, None, None]), 0.0)
    qdec = jnp.exp((i[None, :] + 1.0) * log_g[:, None])
    kdec = jnp.exp((lc - 1.0 - i)[None, :] * log_g[:, None])
    gl = jnp.exp(lc * log_g)
    dec = dec.reshape(N_HEADS * lc, lc)
    qdec = jnp.broadcast_to(qdec[:, :, None], (N_HEADS, lc, D_HEAD)).reshape(N_HEADS * lc, D_HEAD)
    gl = jnp.broadcast_to(gl[:, None, None], (N_HEADS, D_HEAD, D_HEAD)).reshape(BRANCH_W, D_HEAD)
    kdec = jnp.repeat(kdec.T, D_HEAD, axis=1)
    return dec, qdec, kdec, gl


def _band_bias(rel_bias, tq, win, q_minus_k0, valid):
    length = tq + win - 1
    d = np.arange(length) - (tq - 1) - q_minus_k0
    idx = np.clip(d, -REL_CLIP, REL_CLIP) + REL_CLIP
    g = rel_bias[:, idx].astype(F32) * LOG2E
    gp = jnp.concatenate([g, jnp.zeros((N_HEADS, 1), F32)], axis=1)
    m = jnp.tile(gp, (1, tq))[:, :tq * length].reshape(N_HEADS, tq, length)
    tile = m[:, :, tq - 1:tq - 1 + win]
    return jnp.where(valid[None], tile, NEG_INF).reshape(N_HEADS * tq, win)


def _pack_w_in(w_in_l):
    cols = []
    src = 0
    for w in _IN_WIDTH:
        seg = w_in_l[:, src:src + w]
        cols.append(jnp.pad(seg, ((0, 0), (0, _round_up(w, LANE) - w))))
        src += w
    return jnp.concatenate(cols, axis=1).astype(BF16)


def _pad_rows(t, n):
    return jnp.pad(t, ((0, 0), (0, n - t.shape[1]), (0, 0)))


def _token_mixers(x, pos0, past, lw, *, prompt, layer, stacked):
    (w_in_p, qn, wuq, kvn, wukv, gn_g, gn_b, rel_bias, b, length) = lw
    assert pos0 % CHUNK == 0
    pos = pos0 + jnp.arange(length)
    lc = 256 if prompt else length
    dec, qdec, kdec, gl = _retention_tables(lc)
    rope32 = _rope_tables(pos, DR_A)
    lane = np.arange(LANE)
    rotary = (lane >= DN_A) & (lane < DQK_A)
    rope_q = [jnp.where(rotary[None, :], t, fill) for t, fill in zip(rope32, (1.0, 0.0, 0.0))]
    tables = (list(rope32) + list(_rope_tables(pos, D_HEAD)) + [jnp.tile(kdec, (length // lc, 1))] + rope_q)
    if not prompt:
        tables = [jnp.tile(t, (b, 1)) for t in tables]
    keep = min(PREV_CHUNKS * CHUNK, length)
    outs, stacked = _inproj(x, w_in_p, qn, wuq, kvn, wukv, tables, tm=TOKEN_TILE if prompt else b * length,
                            layer=layer, stacked=stacked, seq_len=length if prompt else None,
                            keep_rows=keep if prompt and keep < length else None)
    per_batch = lambda o: o.reshape(o.shape[:-2] + (b, length, o.shape[-1]))
    (q_a, ckv, kpe, kf, v_a, rq, rk, rkd, rv, rg, sq, sk, sv, bq, bk, bv) = [
        per_batch(o) for o in outs[:len(_INPROJ_OUT)]]
    here = lambda t: (t, layer)
    only = lambda t: (t[None], 0)

    if prompt:
        tq = 256
        qa_t, va_t, sq_t, sv_t, bq_t, bv_t = outs[len(_INPROJ_OUT):len(_INPROJ_OUT) + 6]
        nseq = SEQS_PER_STEP if b % SEQS_PER_STEP == 0 else 1
        o_a = _mla_attn_t(qa_t, kf, va_t, tq=tq, nseq=nseq)
        o_c = _sb_attn_t(sq_t, here(sk), sv_t, tq=tq, nseq=nseq)
        s0 = jnp.zeros((b, BRANCH_W, D_HEAD), F32)
        win = 3 * tq
        i = np.arange(tq)[:, None]
        c = np.arange(win)[None, :]
        variants = []
        for t in range(3):
            qc, kc = i // CHUNK + t * (tq // CHUNK), c // CHUNK
            variants.append(_band_bias(rel_bias, tq, win, t * tq, (kc <= qc) & (kc >= qc - PREV_CHUNKS)))
        bias_t = jnp.stack(variants).transpose(0, 2, 1)
        o_d = _band_attn_t(bq_t, here(bk), bv_t, bias_t, tq=tq, win=win, back=2,
                           nseq=2 * nseq if b % (2 * nseq) == 0 else nseq)
    else:
        c_ckv, c_kpe, s_prev, c_sk, c_sv, c_bk, c_bv = past
        n_past = c_ckv.shape[2]
        tko = LANE
        kf_c, v_c = _expand_latent(c_ckv.reshape(DEPTH, b * n_past, KV_RANK),
                                   c_kpe.reshape(DEPTH, b * n_past, DR_A), wukv, layer, TOKEN_TILE)
        o_a = _mla_attn(q_a, only(_pad_rows(kf, tko)), only(_pad_rows(v_a, tko)),
                        only(kf_c.reshape(b, n_past, QA_W)), only(v_c.reshape(b, n_past, BRANCH_W)),
                        tq=length, tk=256, n_own=length, causal_tiles=False)
        o_c = _sb_attn(sq, only(_pad_rows(sk[layer], tko)), only(_pad_rows(sv[layer], tko)),
                       (c_sk.reshape(DEPTH, b, n_past, BRANCH_W), layer),
                       (c_sv.reshape(DEPTH, b, n_past, BRANCH_W), layer),
                       tq=length, tk=256, n_own=length, causal_tiles=False)
        s0 = s_prev[layer].reshape(b, BRANCH_W, D_HEAD)
        n_band = c_bk.shape[2]
        n_keys = n_band + length
        win = _round_up(n_keys, LANE)
        bk_all = _pad_rows(jnp.concatenate([c_bk[layer].reshape(b, n_band, BRANCH_W), bk[layer]], axis=1), win)
        bv_all = _pad_rows(jnp.concatenate([c_bv[layer].reshape(b, n_band, BRANCH_W), bv[layer]], axis=1), win)
        k_pos = pos0 - n_band + np.arange(win)
        q_pos = pos0 + np.arange(length)
        qc, kc = q_pos[:, None] // CHUNK, k_pos[None, :] // CHUNK
        valid = (np.arange(win)[None, :] < n_keys) & (k_pos[None, :] >= 0) & (kc <= qc) & (kc >= qc - PREV_CHUNKS)
        bias = _band_bias(rel_bias, length, win, n_band, valid)[None]
        o_d = _band_attn(bq, only(bk_all), only(bv_all), bias, tq=length, win=win, back=0)

    o_r, s_ret = _retention(rq, rk, rkd, rv, rg, s0, dec, qdec, gl,
                            gn_g.reshape(N_HEADS, 1, D_HEAD), gn_b.reshape(N_HEADS, 1, D_HEAD), lc=lc,
                            nseq=SEQS_PER_STEP if b % SEQS_PER_STEP == 0 else 1)
    s_ret = s_ret.reshape(b, N_HEADS, D_HEAD, D_HEAD)
    flat = lambda t: t.reshape(b * length, BRANCH_W)
    return (flat(o_a), flat(o_r), flat(o_c), flat(o_d)), s_ret, stacked


def _state_outputs(stacked, s_ret, b, length):
    per_seq = lambda t: t.reshape(DEPTH, b, t.shape[1] // b, t.shape[-1])
    heads4 = lambda t: per_seq(t).reshape(DEPTH, b, t.shape[1] // b, N_HEADS, D_HEAD)
    ckv, kpe, sk, sv, bk, bv = stacked[:6]
    if len(stacked) > 6:
        bk, bv = stacked[6:]
    return (per_seq(ckv), per_seq(kpe), jnp.stack(s_ret, axis=0), heads4(sk), heads4(sv), heads4(bk), heads4(bv))


def kernel(x_prompt, x_sample, cache_mla_ckv, cache_mla_kpe, state_ret, cache_sb_k, cache_sb_v, cache_band_k, cache_band_v, w_in, mla_q_norm, mla_w_uq, mla_kv_norm, mla_w_ukv, ret_gn_g, ret_gn_b, band_rel_bias, w_branch, w_o, ln1_g, ln1_b, w_router, b_router, w_exp_gate, w_exp_up, w_exp_down, ln2_g, ln2_b):
    bp, lp, _ = x_prompt.shape
    bs, ls, _ = x_sample.shape
    past_len = cache_mla_ckv.shape[2]
    xp = x_prompt.reshape(bp * lp, D_MODEL)
    xs = x_sample.reshape(bs * ls, D_MODEL)

    wr = jnp.pad(w_router, ((0, 0), (0, LANE - N_EXPERTS)))
    wrh = wr.astype(BF16)
    wr2 = jnp.concatenate([wrh, (wr - wrh.astype(F32)).astype(BF16)], axis=1)
    br = jnp.pad(b_router, (0, LANE - N_EXPERTS)).reshape(LANE, 1)

    we_gate, we_up, we_down = w_exp_gate.astype(BF16), w_exp_up.astype(BF16), w_exp_down.astype(BF16)
    past = (cache_mla_ckv, cache_mla_kpe, state_ret, cache_sb_k, cache_sb_v, cache_band_k, cache_band_v)
    ret_p, ret_s = [], []
    stacked_p = stacked_s = None
    for l in range(DEPTH):
        wuq = mla_w_uq[l]
        wuq = jnp.pad(wuq, ((0, _round_up(Q_RANK, LANE) - Q_RANK), (0, 0), (0, LANE - DQK_A)))
        wuq = wuq.reshape(-1, QA_W).astype(BF16)
        wukv = mla_w_ukv[l]
        wukv = jnp.concatenate(
            [jnp.pad(wukv[:, :, :DN_A], ((0, 0), (0, 0), (0, LANE - DN_A))).reshape(KV_RANK, QA_W),
             wukv[:, :, DN_A:].reshape(KV_RANK, -1)], axis=1).astype(BF16)
        qn = jnp.pad(mla_q_norm[l], (0, _round_up(Q_RANK, LANE) - Q_RANK)).reshape(1, -1)
        kvn = mla_kv_norm[l].reshape(1, KV_RANK)
        w_in_p = _pack_w_in(w_in[l])
        wg = w_in[l][:, GATE_COL0:].astype(BF16)
        wb = w_branch[l].astype(BF16)
        wo = w_o[l].astype(BF16)
        g1, b1 = ln1_g[l].reshape(1, D_MODEL), ln1_b[l].reshape(1, D_MODEL)
        g2, b2 = ln2_g[l].reshape(1, D_MODEL), ln2_b[l].reshape(1, D_MODEL)
        lw =(w_in_p, qn, wuq, kvn, wukv, ret_gn_g[l], ret_gn_b[l], band_rel_bias[l])

        br_p, s_ret_p, stacked_p = _token_mixers(xp, 0, None, lw + (bp, lp), prompt=True,
                                                 layer=l, stacked=stacked_p)
        br_s, s_ret_s, stacked_s = _token_mixers(xs, past_len, past, lw + (bs, ls), prompt=False,
                                                 layer=l, stacked=stacked_s)
        xp = _merge(xp, br_p, wg, wb, wo, g1, b1, tm=TOKEN_TILE)
        xs = _merge(xs, br_s, wg, wb, wo, g1, b1, tm=TOKEN_TILE)
        xp = _moe(xp, wr2, br, we_gate, we_up, we_down, g2, b2, tm=TOKEN_TILE, layer=l)
        xs = _moe(xs, wr2, br, we_gate, we_up, we_down, g2, b2, tm=TOKEN_TILE, layer=l)
        ret_p.append(s_ret_p)
        ret_s.append(s_ret_s)

    return ((xp.reshape(bp, lp, D_MODEL), xs.reshape(bs, ls, D_MODEL))
            + _state_outputs(stacked_p, ret_p, bp, lp)
            + _state_outputs(stacked_s, ret_s, bs, ls))
```

```python
import functools
import math

import jax
import jax.numpy as jnp
import numpy as np
from jax import lax
from jax.experimental import pallas as pl
from jax.experimental.pallas import tpu as pltpu

D_MODEL = 1024
DEPTH = 2
CHUNK = 64
N_BRANCH = 4
BRANCH_W = D_MODEL // 4
N_HEADS = 4
D_HEAD = BRANCH_W // N_HEADS
DN_A = 64
DR_A = 32
DQK_A = DN_A + DR_A
Q_RANK = (3 * D_MODEL) // 16
KV_RANK = D_MODEL // 8
PREV_CHUNKS = 8
REL_CLIP = 128
ROPE_BASE = 10000.0
N_EXPERTS = 16
N_GROUPS = 4
EXPERTS_PER_GROUP = N_EXPERTS // N_GROUPS
TOP_K = 2
D_EXPERT = D_MODEL // 4
ALPHA = (2.0 * DEPTH) ** 0.25
EPS = 1e-5
NEG_INF = -1e30
LOG2E = 1.4426950408889634
SB_DEAD_LOG2 = -150.0

F32 = jnp.float32
BF16 = jnp.bfloat16

V7X_VMEM_LIMIT = 56 * 1024 * 1024
LANE = 128
TOKEN_TILE = 512
SEQS_PER_STEP = 2
CACHE_TILE = 2048
SAMPLE_KEY_TILE = 512

_IN_NAMES = ("c_q", "c_kv", "k_pe", "rq", "rk", "rv", "rg", "sq", "sk", "sv", "bq", "bk", "bv")
_IN_WIDTH = (Q_RANK, KV_RANK, DR_A) + (BRANCH_W,) * 10
QA_W = N_HEADS * LANE


def _round_up(n, m):
    return (n + m - 1) // m * m


_IN_OFF = {}
_off = 0
for _n, _w in zip(_IN_NAMES, _IN_WIDTH):
    _IN_OFF[_n] = (_off, _round_up(_w, LANE))
    _off += _round_up(_w, LANE)
IN_PACKED = _off
GATE_COL0 = sum(_IN_WIDTH)


def _params(sem):
    return pltpu.CompilerParams(dimension_semantics=sem, vmem_limit_bytes=V7X_VMEM_LIMIT)


def _nt_dot(a, b):
    return lax.dot_general(a, b, (((1,), (1,)), ((), ())), preferred_element_type=F32)


def _tn_dot(a, b):
    return lax.dot_general(a, b, (((0,), (0,)), ((), ())), preferred_element_type=F32)


def _dot(a, b):
    return jnp.dot(a, b, preferred_element_type=F32)


def _layer_norm(v, g, b):
    mu = jnp.mean(v, axis=-1, keepdims=True)
    d = v - mu
    var = jnp.mean(d * d, axis=-1, keepdims=True)
    return d * lax.rsqrt(var + EPS) * g + b


def _sigmoid(v):
    return 0.5 * jnp.tanh(0.5 * v) + 0.5


def _head_slice(h):
    return slice(h * D_HEAD, (h + 1) * D_HEAD)


def _stack_heads(q, lane_sets):
    lane = lax.broadcasted_iota(jnp.int32, q.shape, 1)
    zero = jnp.zeros_like(q)
    parts = []
    for h in range(N_HEADS):
        keep = None
        for lo, hi in lane_sets(h):
            m = jnp.logical_and(lane >= lo, lane < hi)
            keep = m if keep is None else jnp.logical_or(keep, m)
        parts.append(jnp.where(keep, q, zero))
    return jnp.concatenate(parts, axis=0)


def _own_lanes(h):
    return ((h * D_HEAD, (h + 1) * D_HEAD),)


def _mla_lanes(h):
    return ((h * LANE, (h + 1) * LANE),)


def _store_heads(o_ref, parts):
    for h, p in enumerate(parts):
        o_ref[:, _head_slice(h)] = p.astype(o_ref.dtype)


def _expand_kernel(ckv_ref, kpe_ref, w_ref, kf_ref, v_ref):
    kvx = _dot(ckv_ref[...].astype(BF16), w_ref[...])
    kp = kpe_ref[...].astype(BF16)
    tail = jnp.zeros((kp.shape[0], LANE - DN_A - DR_A), BF16)
    for h in range(N_HEADS):
        kf_ref[:, h * LANE:(h + 1) * LANE] = jnp.concatenate(
            [kvx[:, h * LANE:h * LANE + DN_A].astype(BF16), kp, tail], axis=1)
    v_ref[...] = kvx[:, QA_W:].astype(BF16)


def _expand_latent(ckv, kpe, w_ukv, layer, tm):
    m = ckv.shape[1]
    assert m % tm == 0
    return pl.pallas_call(
        _expand_kernel,
        grid=(m // tm,),
        in_specs=[pl.BlockSpec((None, tm, KV_RANK), lambda i: (layer, i, 0)),
                  pl.BlockSpec((None, tm, DR_A), lambda i: (layer, i, 0)),
                  pl.BlockSpec(w_ukv.shape, lambda i: (0, 0))],
        out_specs=[pl.BlockSpec((tm, QA_W), lambda i: (i, 0)),
                   pl.BlockSpec((tm, BRANCH_W), lambda i: (i, 0))],
        out_shape=[jax.ShapeDtypeStruct((m, QA_W), BF16),
                   jax.ShapeDtypeStruct((m, BRANCH_W), BF16)],
        compiler_params=_params(("parallel",)),
        name="expand_latent",
    )(ckv, kpe, w_ukv)


def _rope_block(x, cos, sin_a, sin_b, half):
    return x * cos + pltpu.roll(x, LANE - half, 1) * sin_a + pltpu.roll(x, half, 1) * sin_b


def _inproj_kernel(x_ref, w_ref, qn_ref, wuq_ref, kvn_ref, wukv_ref,
                   c32_ref, a32_ref, b32_ref, c64_ref, a64_ref, b64_ref, kdec_ref, cqa_ref, aqa_ref, bqa_ref,
                   qa_ref, ckv_ref, kpe_ref, kf_ref, va_ref,
                   rq_ref, rk_ref, rkd_ref, rv_ref, rg_ref,
                   sq_ref, sk_ref, sv_ref, bq_ref, bk_ref, bv_ref, t_refs=(), keep_refs=()):
    qat_ref, vat_ref, sqt_ref, svt_ref, bqt_ref, bvt_ref = t_refs if t_refs else (None,) * 6
    z = _dot(x_ref[...].astype(BF16), w_ref[...])

    def seg(name):
        o, w = _IN_OFF[name]
        return z[:, o:o + w]

    cq = seg("c_q")
    cqn = cq * lax.rsqrt(jnp.sum(cq * cq, axis=1, keepdims=True) * (1.0 / Q_RANK) + EPS) * qn_ref[...]
    qa = _dot(cqn.astype(BF16), wuq_ref[...])
    scale_a = DQK_A ** -0.5 * LOG2E
    qa_s = jnp.concatenate(
        [_rope_block(qa[:, h * LANE:(h + 1) * LANE], cqa_ref[...], aqa_ref[...], bqa_ref[...], DR_A // 2)
         for h in range(N_HEADS)], axis=1) * scale_a
    qa_ref[...] = qa_s.astype(BF16)
    if qat_ref is not None:
        qat_ref[...] = qa_s.T.astype(BF16)

    ckv_raw = seg("c_kv")
    ckv = ckv_raw * lax.rsqrt(jnp.mean(ckv_raw * ckv_raw, axis=1, keepdims=True) + EPS) * kvn_ref[...]
    ckv_ref[...] = ckv
    kvx = _dot(ckv.astype(BF16), wukv_ref[...])
    kp = _rope_block(seg("k_pe"), c32_ref[...], a32_ref[...], b32_ref[...], DR_A // 2)
    kpe_ref[...] = kp[:, :DR_A]
    kp_at = pltpu.roll(kp, DN_A, 1)
    for h in range(N_HEADS):
        kf_ref[:, h * LANE:(h + 1) * LANE] = (kvx[:, h * LANE:(h + 1) * LANE] + kp_at).astype(BF16)
    va_ref[...] = kvx[:, QA_W:].astype(BF16)
    if vat_ref is not None:
        vat_ref[...] = kvx[:, QA_W:].T.astype(BF16)

    rq, rk = seg("rq"), seg("rk")
    kdec = kdec_ref[...]
    for blk in range(BRANCH_W // LANE):
        cols = slice(blk * LANE, (blk + 1) * LANE)
        rq_ref[:, cols] = _rope_block(rq[:, cols], c64_ref[...], a64_ref[...], b64_ref[...],
                                      D_HEAD // 2).astype(BF16)
        rkb = _rope_block(rk[:, cols], c64_ref[...], a64_ref[...], b64_ref[...], D_HEAD // 2) * (D_HEAD ** -0.5)
        rk_ref[:, cols] = rkb.astype(BF16)
        rkd_ref[:, cols] = (rkb * kdec[:, cols]).astype(BF16)
    rv_ref[...] = seg("rv").astype(BF16)
    rg_ref[...] = seg("rg")

    scale_h = D_HEAD ** -0.5 * LOG2E
    sq = seg("sq") * scale_h
    sq_ref[...] = sq.astype(BF16)
    sk_ref[...] = seg("sk")
    sv_ref[...] = seg("sv")
    if sqt_ref is not None:
        sqt_ref[...] = sq.T.astype(BF16)
        svt_ref[...] = seg("sv").T.astype(BF16)
    bq = seg("bq") * scale_h
    bq_ref[...] = bq.astype(BF16)
    bk_ref[...] = seg("bk")
    bv_ref[...] = seg("bv")
    if bqt_ref is not None:
        bqt_ref[...] = bq.T.astype(BF16)
        bvt_ref[...] = seg("bv").T.astype(BF16)
    if keep_refs:
        keep_refs[0][...] = seg("bk")
        keep_refs[1][...] = seg("bv")


_INPROJ_OUT = (
    (QA_W, BF16), (KV_RANK, F32), (DR_A, F32), (QA_W, BF16), (BRANCH_W, BF16),
    (BRANCH_W, BF16), (BRANCH_W, BF16), (BRANCH_W, BF16), (BRANCH_W, BF16), (BRANCH_W, F32),
    (BRANCH_W, BF16), (BRANCH_W, F32), (BRANCH_W, F32), (BRANCH_W, BF16), (BRANCH_W, F32), (BRANCH_W, F32))


_STATE_OUTS = (1, 2, 11, 12, 14, 15)


def _inproj_entry(*refs, n_in, n_alias, n_t, n_keep):
    outs = refs[n_in + n_alias:]
    n_main = len(_INPROJ_OUT)
    _inproj_kernel(*refs[:n_in], *outs[:n_main], t_refs=outs[n_main:n_main + n_t],
                   keep_refs=outs[n_main + n_t:n_main + n_t + n_keep])


def _inproj(x, w_in_p, qn, wuq, kvn, wukv, tables, *, tm, layer, stacked, seq_len=None, keep_rows=None):
    m = x.shape[0]
    tm = min(tm, m)
    assert m % tm == 0
    n_pos_tiles = tables[0].shape[0] // tm
    assert tables[0].shape[0] % tm == 0
    row = lambda i: (i, 0)
    const = lambda i: (0, 0)
    pos = lambda i: (i % n_pos_tiles, 0)
    in_specs = [pl.BlockSpec((tm, D_MODEL), row),
                pl.BlockSpec(w_in_p.shape, const),
                pl.BlockSpec(qn.shape, const),
                pl.BlockSpec(wuq.shape, const),
                pl.BlockSpec(kvn.shape, const),
                pl.BlockSpec(wukv.shape, const)]
    in_specs += [pl.BlockSpec((tm, t.shape[1]), pos) for t in tables]
    n_in = len(in_specs)
    out_specs, out_shape = [], []
    for k, (w, dt) in enumerate(_INPROJ_OUT):
        if k in _STATE_OUTS:
            out_specs.append(pl.BlockSpec((None, tm, w), lambda i: (layer, i, 0)))
            out_shape.append(jax.ShapeDtypeStruct((DEPTH, m, w), dt))
        else:
            out_specs.append(pl.BlockSpec((tm, w), row))
            out_shape.append(jax.ShapeDtypeStruct((m, w), dt))
    if seq_len is not None:
        assert seq_len % tm == 0 and m % seq_len == 0
        tps = seq_len // tm
        for w in (QA_W,) + (BRANCH_W,) * 5:
            out_specs.append(pl.BlockSpec((None, w, tm), lambda i: (i // tps, 0, i % tps)))
            out_shape.append(jax.ShapeDtypeStruct((m // seq_len, w, seq_len), BF16))
    n_t = len(out_specs) - len(_INPROJ_OUT)
    state_outs = list(_STATE_OUTS)
    if keep_rows is not None:
        assert seq_len is not None and keep_rows % tm == 0 and keep_rows <= seq_len
        tps, ktiles = seq_len // tm, keep_rows // tm
        kept = lambda i: (layer, (i // tps) * ktiles + jnp.maximum(i % tps - (tps - ktiles), 0), 0)
        for _ in range(2):
            state_outs.append(len(out_specs))
            out_specs.append(pl.BlockSpec((None, tm, BRANCH_W), kept))
            out_shape.append(jax.ShapeDtypeStruct((DEPTH, m // seq_len * keep_rows, BRANCH_W), F32))
    aliases = {}
    extra = ()
    if stacked is not None:
        extra = tuple(stacked)
        assert len(extra) == len(state_outs)
        in_specs += [pl.BlockSpec(memory_space=pl.ANY)] * len(extra)
        aliases = {n_in + k: o for k, o in enumerate(state_outs)}
    outs = pl.pallas_call(
        functools.partial(_inproj_entry, n_in=n_in, n_alias=len(extra), n_t=n_t,
                          n_keep=len(state_outs) - len(_STATE_OUTS)),
        grid=(m // tm,),
        in_specs=in_specs,
        out_specs=out_specs,
        out_shape=out_shape,
        input_output_aliases=aliases,
        compiler_params=_params(("arbitrary",)),
        name="inproj",
    )(x, w_in_p, qn, wuq, kvn, wukv, *tables, *extra)
    return outs, tuple(outs[k] for k in state_outs)


def _tile_spec(src, rows):
    arr, layer = src
    return pl.BlockSpec((None, None, rows, arr.shape[3]), lambda i, t: (layer, i, t, 0))


def _full_spec(src):
    arr, layer = src
    return pl.BlockSpec((None, None, arr.shape[2], arr.shape[3]), lambda i, t: (layer, i, 0, 0))


def _mla_kernel(q_ref, kfo_ref, vo_ref, kfp_ref, vp_ref, o_ref, *, tq, tk, n_own, n_past):
    qi = pl.program_id(1)
    qm = _stack_heads(q_ref[...], _mla_lanes)
    rows = N_HEADS * tq
    tko = kfo_ref.shape[0]
    n_loop = qi if n_past is None else n_past

    def softmax_pv(s, v, carry):
        m, accs = carry
        m_new = jnp.maximum(m, jnp.max(s, axis=1, keepdims=True))
        pb = jnp.exp2(s - m_new).astype(BF16)
        a = jnp.exp2(m - m_new)
        ones = jnp.ones((v.shape[0], D_HEAD), BF16)
        accs = tuple(a[h * tq:(h + 1) * tq] * accs[h]
                     + _dot(pb[h * tq:(h + 1) * tq], jnp.concatenate([v[:, _head_slice(h)], ones], axis=1))
                     for h in range(N_HEADS))
        return m_new, accs

    def past_scores(j):
        start = pl.multiple_of(jnp.minimum(j, jnp.maximum(n_loop - 1, 0)) * tk, tk)
        return _nt_dot(qm, kfp_ref[pl.ds(start, tk), :])

    row_q = lax.broadcasted_iota(jnp.int32, (rows, tko), 0) % tq
    col = lax.broadcasted_iota(jnp.int32, (rows, tko), 1)
    own_mask = jnp.logical_and(col < n_own, col // CHUNK <= row_q // CHUNK)
    carry = (jnp.full((rows, 1), NEG_INF, F32),
             tuple(jnp.zeros((tq, 2 * D_HEAD), F32) for _ in range(N_HEADS)))
    s_next = past_scores(0)
    carry = softmax_pv(jnp.where(own_mask, _nt_dot(qm, kfo_ref[...]), NEG_INF), vo_ref[...], carry)

    def body(j, c):
        s_cur, rest = c
        s_after = past_scores(j + 1)
        start = pl.multiple_of(j * tk, tk)
        return s_after, softmax_pv(s_cur, vp_ref[pl.ds(start, tk), :], rest)

    _, (_, accs) = lax.fori_loop(0, n_loop, body, (s_next, carry))
    _store_heads(o_ref, [accs[h][:, :D_HEAD] / accs[h][:, D_HEAD:D_HEAD + 1] for h in range(N_HEADS)])


def _mla_attn(q, kf_own, v_own, kf_past, v_past, *, tq, tk, n_own, causal_tiles):
    b, lq, _ = q.shape
    nqt = lq // tq
    tko = kf_own[0].shape[2] // nqt
    lp = kf_past[0].shape[2]
    assert lq % tq == 0 and lp % tk == 0
    return pl.pallas_call(
        functools.partial(_mla_kernel, tq=tq, tk=tk, n_own=n_own,
                          n_past=None if causal_tiles else lp // tk),
        grid=(b, nqt),
        in_specs=[pl.BlockSpec((None, tq, QA_W), lambda i, t: (i, t, 0)),
                  _tile_spec(kf_own, tko), _tile_spec(v_own, tko),
                  _full_spec(kf_past), _full_spec(v_past)],
        out_specs=pl.BlockSpec((None, tq, BRANCH_W), lambda i, t: (i, t, 0)),
        out_shape=jax.ShapeDtypeStruct((b, lq, BRANCH_W), BF16),
        compiler_params=_params(("parallel", "arbitrary")),
        name="mla_attn",
    )(q, kf_own[0], v_own[0], kf_past[0], v_past[0])


def _mla_kernel_t(qt_ref, kfo_ref, vto_ref, kfp_ref, vtp_ref, o_ref, *, tq, tk, nseq):
    qi = pl.program_id(1)
    cols = N_HEADS * tq

    def scores(s, kf):
        qt = qt_ref[s]
        return jnp.concatenate([_dot(kf[:, h * LANE:(h + 1) * LANE], qt[h * LANE:(h + 1) * LANE, :])
                                for h in range(N_HEADS)], axis=1)

    def softmax_pv(st, vt, carry):
        m, l, accs = carry
        m_new = jnp.maximum(m, jnp.max(st, axis=0, keepdims=True))
        pt = jnp.exp2(st - m_new)
        a = jnp.exp2(m - m_new)
        l = a * l + jnp.sum(pt, axis=0, keepdims=True)
        pb = pt.astype(BF16)
        accs = tuple(a[:, h * tq:(h + 1) * tq] * accs[h]
                     + _dot(vt[_head_slice(h), :], pb[:, h * tq:(h + 1) * tq]) for h in range(N_HEADS))
        return m_new, l, accs

    def past_scores(s, j):
        start = pl.multiple_of(jnp.minimum(j, jnp.maximum(qi - 1, 0)) * tk, tk)
        return scores(s, kfp_ref[s, pl.ds(start, tk), :])

    key = lax.broadcasted_iota(jnp.int32, (tq, cols), 0)
    qry = lax.broadcasted_iota(jnp.int32, (tq, cols), 1) % tq
    own_mask = key // CHUNK <= qry // CHUNK
    s_next, carries = [], []
    for s in range(nseq):
        empty = (jnp.full((1, cols), NEG_INF, F32), jnp.zeros((1, cols), F32),
                 tuple(jnp.zeros((D_HEAD, tq), F32) for _ in range(N_HEADS)))
        s_next.append(past_scores(s, 0))
        carries.append(softmax_pv(jnp.where(own_mask, scores(s, kfo_ref[s]), NEG_INF), vto_ref[s], empty))

    def body(j, c):
        s_cur, rest = c
        start = pl.multiple_of(j * tk, tk)
        s_after = tuple(past_scores(s, j + 1) for s in range(nseq))
        return s_after, tuple(softmax_pv(s_cur[s], vtp_ref[s, :, pl.ds(start, tk)], rest[s])
                              for s in range(nseq))

    _, done = lax.fori_loop(0, qi, body, (tuple(s_next), tuple(carries)))
    for s in range(nseq):
        _, l, accs = done[s]
        out_t = jnp.concatenate([accs[h] / l[:, h * tq:(h + 1) * tq] for h in range(N_HEADS)], axis=0)
        o_ref[s] = out_t.T.astype(o_ref.dtype)


def _mla_attn_t(qt, kf, vt, *, tq, nseq):
    b, _, length = qt.shape
    assert length % tq == 0 and b % nseq == 0
    return pl.pallas_call(
        functools.partial(_mla_kernel_t, tq=tq, tk=tq, nseq=nseq),
        grid=(b // nseq, length // tq),
        in_specs=[pl.BlockSpec((nseq, QA_W, tq), lambda i, t: (i, 0, t)),
                  pl.BlockSpec((nseq, tq, QA_W), lambda i, t: (i, t, 0)),
                  pl.BlockSpec((nseq, BRANCH_W, tq), lambda i, t: (i, 0, t)),
                  pl.BlockSpec((nseq, length, QA_W), lambda i, t: (i, 0, 0)),
                  pl.BlockSpec((nseq, BRANCH_W, length), lambda i, t: (i, 0, 0))],
        out_specs=pl.BlockSpec((nseq, tq, BRANCH_W), lambda i, t: (i, t, 0)),
        out_shape=jax.ShapeDtypeStruct((b, length, BRANCH_W), BF16),
        compiler_params=_params(("parallel", "arbitrary")),
        name="mla_attn_t",
    )(qt, kf, vt, kf, vt)


def _sb_kernel(q_ref, ko_ref, vo_ref, kp_ref, vp_ref, o_ref, *, tq, tk, n_own, n_past):
    qi = pl.program_id(1)
    qm = _stack_heads(q_ref[...], _own_lanes)
    rows = N_HEADS * tq
    tko = ko_ref.shape[0]
    n_loop = qi if n_past is None else n_past

    def tri2(n):
        r = lax.broadcasted_iota(jnp.int32, (2 * n, n), 0) % n
        c = lax.broadcasted_iota(jnp.int32, (2 * n, n), 1)
        return jnp.where(r > c, 1.0, 0.0).astype(BF16)

    def weigh(z, v, carry, mask, tri):
        run, accs = carry
        neg_abs = lax.bitcast_convert_type(
            lax.bitcast_convert_type(z, jnp.uint32) | jnp.uint32(0x80000000), F32)
        t = jnp.log2(1.0 + jnp.exp2(neg_abs))
        log_beta = jnp.minimum(z, 0.0) - t
        log_stay = log_beta - z
        if mask is not None:
            log_stay = jnp.where(mask, log_stay, 0.0)
        hi = log_stay.astype(BF16)
        lo = (log_stay - hi.astype(F32)).astype(BF16)
        later = _dot(jnp.concatenate([hi, lo], axis=1), tri) + run
        w = jnp.exp2(log_beta + later)
        if mask is not None:
            w = jnp.where(mask, w, 0.0)
        wb = w.astype(BF16)
        vb = v.astype(BF16)
        accs = tuple(accs[h] + _dot(wb[h * tq:(h + 1) * tq], vb[:, _head_slice(h)])
                     for h in range(N_HEADS))
        return run + jnp.sum(log_stay, axis=1, keepdims=True), accs

    row_q = lax.broadcasted_iota(jnp.int32, (rows, tko), 0) % tq
    col = lax.broadcasted_iota(jnp.int32, (rows, tko), 1)
    own_mask = jnp.logical_and(col < n_own, col < row_q)

    def past_start(jj):
        return pl.multiple_of(jnp.clip(n_loop - 1 - jj, 0, kp_ref.shape[0] // tk - 1) * tk, tk)

    def past_scores(jj):
        return _nt_dot(qm, kp_ref[pl.ds(past_start(jj), tk), :].astype(BF16))

    def alive(run):
        return (jnp.max(run) > SB_DEAD_LOG2).astype(jnp.int32)

    z_next = past_scores(0)
    carry = (jnp.zeros((rows, 1), F32), tuple(jnp.zeros((tq, D_HEAD), F32) for _ in range(N_HEADS)))
    run, accs = weigh(_nt_dot(qm, ko_ref[...].astype(BF16)), vo_ref[...], carry, own_mask, tri2(tko))
    tri_past = tri2(tk)

    def cond(c):
        return jnp.logical_and(c[0] < n_loop, c[1] > 0)

    def body(c):
        jj, _, z_cur, run, accs = c
        z_after = past_scores(jj + 1)
        run, accs = weigh(z_cur, vp_ref[pl.ds(past_start(jj), tk), :], (run, accs), None, tri_past)
        return jj + 1, alive(run), z_after, run, accs

    out = lax.while_loop(cond, body, (jnp.int32(0), alive(run), z_next, run, accs))
    _store_heads(o_ref, out[4])


def _sb_attn(q, k_own, v_own, k_past, v_past, *, tq, tk, n_own, causal_tiles):
    b, lq, _ = q.shape
    nqt = lq // tq
    tko = k_own[0].shape[2] // nqt
    lp = k_past[0].shape[2]
    assert lq % tq == 0 and lp % tk == 0
    return pl.pallas_call(
        functools.partial(_sb_kernel, tq=tq, tk=tk, n_own=n_own,
                          n_past=None if causal_tiles else lp // tk),
        grid=(b, nqt),
        in_specs=[pl.BlockSpec((None, tq, BRANCH_W), lambda i, t: (i, t, 0)),
                  _tile_spec(k_own, tko), _tile_spec(v_own, tko),
                  _full_spec(k_past), _full_spec(v_past)],
        out_specs=pl.BlockSpec((None, tq, BRANCH_W), lambda i, t: (i, t, 0)),
        out_shape=jax.ShapeDtypeStruct((b, lq, BRANCH_W), BF16),
        compiler_params=_params(("parallel", "arbitrary")),
        name="sb_attn",
    )(q, k_own[0], v_own[0], k_past[0], v_past[0])


def _sb_kernel_t(qt_ref, ko_ref, vto_ref, kp_ref, vtp_ref, o_ref, *, tq, tk, nseq):
    qi = pl.program_id(1)
    qmts = []
    for s in range(nseq):
        qt = qt_ref[s]
        feat = lax.broadcasted_iota(jnp.int32, qt.shape, 0)
        qmts.append(jnp.concatenate(
            [jnp.where(jnp.logical_and(feat >= h * D_HEAD, feat < (h + 1) * D_HEAD), qt, jnp.zeros_like(qt))
             for h in range(N_HEADS)], axis=1))
    cols = N_HEADS * tq

    def tri2(n):
        r = lax.broadcasted_iota(jnp.int32, (n, 2 * n), 0)
        c = lax.broadcasted_iota(jnp.int32, (n, 2 * n), 1) % n
        return jnp.where(c > r, 1.0, 0.0).astype(BF16)

    def weigh(zt, vt, carry, mask, tri):
        run, accs = carry
        neg_abs = lax.bitcast_convert_type(
            lax.bitcast_convert_type(zt, jnp.uint32) | jnp.uint32(0x80000000), F32)
        t = jnp.log2(1.0 + jnp.exp2(neg_abs))
        log_beta = jnp.minimum(zt, 0.0) - t
        log_stay = log_beta - zt
        if mask is not None:
            log_stay = jnp.where(mask, log_stay, 0.0)
        hi = log_stay.astype(BF16)
        lo = (log_stay - hi.astype(F32)).astype(BF16)
        later = _dot(tri, jnp.concatenate([hi, lo], axis=0)) + run
        w = jnp.exp2(log_beta + later)
        if mask is not None:
            w = jnp.where(mask, w, 0.0)
        wb = w.astype(BF16)
        accs = tuple(accs[h] + _dot(vt[_head_slice(h), :], wb[:, h * tq:(h + 1) * tq])
                     for h in range(N_HEADS))
        return run + jnp.sum(log_stay, axis=0, keepdims=True), accs

    def past_start(jj):
        return pl.multiple_of(jnp.clip(qi - 1 - jj, 0, kp_ref.shape[1] // tk - 1) * tk, tk)

    def past_scores(s, jj):
        return _dot(kp_ref[s, pl.ds(past_start(jj), tk), :].astype(BF16), qmts[s])

    def alive(runs):
        top = jnp.max(runs[0])
        for r in runs[1:]:
            top = jnp.maximum(top, jnp.max(r))
        return (top > SB_DEAD_LOG2).astype(jnp.int32)

    key = lax.broadcasted_iota(jnp.int32, (tq, cols), 0)
    qry = lax.broadcasted_iota(jnp.int32, (tq, cols), 1) % tq
    own_mask = key < qry
    tri_own = tri2(tq)
    z_next, runs, accss = [], [], []
    for s in range(nseq):
        z_next.append(past_scores(s, 0))
        empty = (jnp.zeros((1, cols), F32), tuple(jnp.zeros((D_HEAD, tq), F32) for _ in range(N_HEADS)))
        run, accs = weigh(_dot(ko_ref[s].astype(BF16), qmts[s]), vto_ref[s], empty, own_mask, tri_own)
        runs.append(run)
        accss.append(accs)
    tri_past = tri2(tk)

    def cond(c):
        return jnp.logical_and(c[0] < qi, c[1] > 0)

    def body(c):
        jj, _, z_cur, runs, accss = c
        z_after = tuple(past_scores(s, jj + 1) for s in range(nseq))
        new = [weigh(z_cur[s], vtp_ref[s, :, pl.ds(past_start(jj), tk)], (runs[s], accss[s]), None, tri_past)
               for s in range(nseq)]
        runs = tuple(n[0] for n in new)
        return jj + 1, alive(runs), z_after, runs, tuple(n[1] for n in new)

    out = lax.while_loop(cond, body, (jnp.int32(0), alive(runs), tuple(z_next), tuple(runs), tuple(accss)))
    for s in range(nseq):
        o_ref[s] = jnp.concatenate(out[4][s], axis=0).T.astype(o_ref.dtype)


def _sb_attn_t(qt, k, vt, *, tq, nseq):
    b, _, length = qt.shape
    k_arr, layer = k
    assert length % tq == 0 and b % nseq == 0
    return pl.pallas_call(
        functools.partial(_sb_kernel_t, tq=tq, tk=tq, nseq=nseq),
        grid=(b // nseq, length // tq),
        in_specs=[pl.BlockSpec((nseq, BRANCH_W, tq), lambda i, t: (i, 0, t)),
                  pl.BlockSpec((None, nseq, tq, BRANCH_W), lambda i, t: (layer, i, t, 0)),
                  pl.BlockSpec((nseq, BRANCH_W, tq), lambda i, t: (i, 0, t)),
                  pl.BlockSpec((None, nseq, length, BRANCH_W), lambda i, t: (layer, i, 0, 0)),
                  pl.BlockSpec((nseq, BRANCH_W, length), lambda i, t: (i, 0, 0))],
        out_specs=pl.BlockSpec((nseq, tq, BRANCH_W), lambda i, t: (i, t, 0)),
        out_shape=jax.ShapeDtypeStruct((b, length, BRANCH_W), BF16),
        compiler_params=_params(("parallel", "arbitrary")),
        name="sb_attn_t",
    )(qt, k_arr, vt, k_arr, vt)


def _ret_kernel(q_ref, k_ref, kd_ref, v_ref, rg_ref, s0_ref, dec_ref, qdec_ref, gl_ref,
                gng_ref, gnb_ref, o_ref, sout_ref, state_ref, *, lc, nseq):
    c = pl.program_id(1)

    @pl.when(c == 0)
    def _():
        state_ref[...] = s0_ref[...]

    for s in range(nseq):
        qm = _stack_heads(q_ref[s], _own_lanes)
        v = v_ref[s]
        state = state_ref[s]
        scores = (_nt_dot(qm, k_ref[s]) * dec_ref[...]).astype(BF16)
        cross = _dot(qm, state.astype(BF16)) * qdec_ref[...]
        kv_full = _tn_dot(kd_ref[s], v)
        new_state = gl_ref[...] * state + jnp.concatenate(
            [kv_full[_head_slice(h), _head_slice(h)] for h in range(N_HEADS)], axis=0)
        state_ref[s] = new_state

        rg = rg_ref[s]
        for h in range(N_HEADS):
            o = _dot(scores[h * lc:(h + 1) * lc], v[:, _head_slice(h)]) + cross[h * lc:(h + 1) * lc]
            mu = jnp.mean(o, axis=-1, keepdims=True)
            d = o - mu
            var = jnp.mean(d * d, axis=-1, keepdims=True)
            y = d * lax.rsqrt(var + EPS) * gng_ref[h] + gnb_ref[h]
            g = rg[:, _head_slice(h)]
            o_ref[s, :, _head_slice(h)] = (y * (g * _sigmoid(g))).astype(o_ref.dtype)

    @pl.when(c == pl.num_programs(1) - 1)
    def _():
        sout_ref[...] = state_ref[...]


def _retention(q, k, kd, v, rg, s0, dec, qdec, gl, gng, gnb, *, lc, nseq):
    b, length, _ = q.shape
    assert length % lc == 0 and b % nseq == 0
    seq = lambda i, t: (i, t, 0)
    st = lambda i, t: (i, 0, 0)
    c2 = lambda i, t: (0, 0)
    c3 = lambda i, t: (0, 0, 0)
    return pl.pallas_call(
        functools.partial(_ret_kernel, lc=lc, nseq=nseq),
        grid=(b // nseq, length // lc),
        in_specs=[pl.BlockSpec((nseq, lc, BRANCH_W), seq)] * 5
        + [pl.BlockSpec((nseq, BRANCH_W, D_HEAD), st),
           pl.BlockSpec(dec.shape, c2), pl.BlockSpec(qdec.shape, c2), pl.BlockSpec(gl.shape, c2),
           pl.BlockSpec(gng.shape, c3), pl.BlockSpec(gnb.shape, c3)],
        out_specs=[pl.BlockSpec((nseq, lc, BRANCH_W), seq),
                   pl.BlockSpec((nseq, BRANCH_W, D_HEAD), st)],
        out_shape=[jax.ShapeDtypeStruct((b, length, BRANCH_W), BF16),
                   jax.ShapeDtypeStruct((b, BRANCH_W, D_HEAD), F32)],
        scratch_shapes=[pltpu.VMEM((nseq, BRANCH_W, D_HEAD), F32)],
        compiler_params=_params(("parallel", "arbitrary")),
        name="retention",
    )(q, k, kd, v, rg, s0, dec, qdec, gl, gng, gnb)


def _band_kernel(q_ref, k_ref, v_ref, bias_ref, o_ref, *, tq, win, back):
    qi = pl.program_id(1)
    start = pl.multiple_of(jnp.maximum(qi - back, 0) * tq, tq)
    k = k_ref[pl.ds(start, win), :].astype(BF16)
    v = v_ref[pl.ds(start, win), :].astype(BF16)
    qm = _stack_heads(q_ref[...], _own_lanes)
    s = _nt_dot(qm, k) + bias_ref[...]
    pb = jnp.exp2(s - jnp.max(s, axis=1, keepdims=True)).astype(BF16)
    ones = jnp.ones((win, D_HEAD), BF16)
    outs = []
    for h in range(N_HEADS):
        o = _dot(pb[h * tq:(h + 1) * tq], jnp.concatenate([v[:, _head_slice(h)], ones], axis=1))
        outs.append(o[:, :D_HEAD] / o[:, D_HEAD:D_HEAD + 1])
    _store_heads(o_ref, outs)


def _band_attn(q, k, v, bias, *, tq, win, back):
    b, lq, _ = q.shape
    nvar = bias.shape[0]
    assert lq % tq == 0
    return pl.pallas_call(
        functools.partial(_band_kernel, tq=tq, win=win, back=back),
        grid=(b, lq // tq),
        in_specs=[pl.BlockSpec((None, tq, BRANCH_W), lambda i, t: (i, t, 0)),
                  _full_spec(k), _full_spec(v),
                  pl.BlockSpec((None, N_HEADS * tq, win), lambda i, t: (jnp.minimum(t, nvar - 1), 0, 0))],
        out_specs=pl.BlockSpec((None, tq, BRANCH_W), lambda i, t: (i, t, 0)),
        out_shape=jax.ShapeDtypeStruct((b, lq, BRANCH_W), BF16),
        compiler_params=_params(("parallel", "arbitrary")),
        name="band_attn",
    )(q, k[0], v[0], bias)


def _band_kernel_t(qt_ref, k_ref, vt_ref, bias_ref, o_ref, *, tq, win, back, nseq):
    qi = pl.program_id(1)
    start = pl.multiple_of(jnp.maximum(qi - back, 0) * tq, tq)
    for s in range(nseq):
        k = k_ref[s, pl.ds(start, win), :].astype(BF16)
        vt = vt_ref[s, :, pl.ds(start, win)]
        qt = qt_ref[s]
        feat = lax.broadcasted_iota(jnp.int32, qt.shape, 0)
        qmt = jnp.concatenate(
            [jnp.where(jnp.logical_and(feat >= h * D_HEAD, feat < (h + 1) * D_HEAD), qt, jnp.zeros_like(qt))
             for h in range(N_HEADS)], axis=1)
        st = _dot(k, qmt) + bias_ref[...]
        pt = jnp.exp2(st - jnp.max(st, axis=0, keepdims=True))
        l = jnp.sum(pt, axis=0, keepdims=True)
        pb = pt.astype(BF16)
        out_t = jnp.concatenate(
            [_dot(vt[_head_slice(h), :], pb[:, h * tq:(h + 1) * tq]) / l[:, h * tq:(h + 1) * tq]
             for h in range(N_HEADS)], axis=0)
        o_ref[s] = out_t.T.astype(o_ref.dtype)


def _band_attn_t(qt, k, vt, bias_t, *, tq, win, back, nseq):
    b, _, length = qt.shape
    nvar = bias_t.shape[0]
    k_arr, layer = k
    assert length % tq == 0 and b % nseq == 0
    return pl.pallas_call(
        functools.partial(_band_kernel_t, tq=tq, win=win, back=back, nseq=nseq),
        grid=(b // nseq, length // tq),
        in_specs=[pl.BlockSpec((nseq, BRANCH_W, tq), lambda i, t: (i, 0, t)),
                  pl.BlockSpec((None, nseq, k_arr.shape[2], BRANCH_W), lambda i, t: (layer, i, 0, 0)),
                  pl.BlockSpec((nseq, BRANCH_W, length), lambda i, t: (i, 0, 0)),
                  pl.BlockSpec((None, win, N_HEADS * tq), lambda i, t: (jnp.minimum(t, nvar - 1), 0, 0))],
        out_specs=pl.BlockSpec((nseq, tq, BRANCH_W), lambda i, t: (i, t, 0)),
        out_shape=jax.ShapeDtypeStruct((b, length, BRANCH_W), BF16),
        compiler_params=_params(("parallel", "arbitrary")),
        name="band_attn_t",
    )(qt, k_arr, vt, bias_t)


def _merge_kernel(x_ref, ba_ref, bb_ref, bc_ref, bd_ref, wg_ref, wb_ref, wo_ref, g_ref, b_ref, o_ref):
    x = x_ref[...]
    xb = x.astype(BF16)
    merged = None
    for n, br_ref in enumerate((ba_ref, bb_ref, bc_ref, bd_ref)):
        logits = _dot(xb, wg_ref[:, n * D_MODEL:(n + 1) * D_MODEL])
        term = _dot(br_ref[...], wb_ref[n]) * _sigmoid(logits)
        merged = term if merged is None else merged + term
    mix = _dot(merged.astype(BF16), wo_ref[...])
    o_ref[...] = _layer_norm(ALPHA * x + mix, g_ref[...], b_ref[...])


def _merge(x, branches, wg, wb, wo, g, b, *, tm):
    m = x.shape[0]
    tm = min(tm, m)
    assert m % tm == 0
    const2 = lambda i: (0, 0)
    row = lambda i: (i, 0)
    return pl.pallas_call(
        _merge_kernel,
        grid=(m // tm,),
        in_specs=[pl.BlockSpec((tm, D_MODEL), row)]
        + [pl.BlockSpec((tm, BRANCH_W), row)] * N_BRANCH
        + [pl.BlockSpec((D_MODEL, N_BRANCH * D_MODEL), const2),
           pl.BlockSpec((N_BRANCH, BRANCH_W, D_MODEL), lambda i: (0, 0, 0)),
           pl.BlockSpec((D_MODEL, D_MODEL), const2),
           pl.BlockSpec((1, D_MODEL), const2),
           pl.BlockSpec((1, D_MODEL), const2)],
        out_specs=pl.BlockSpec((tm, D_MODEL), row),
        out_shape=jax.ShapeDtypeStruct((m, D_MODEL), F32),
        compiler_params=_params(("parallel",)),
        name="merge",
    )(x, *branches, wg, wb, wo, g, b)


def _route(aff_t, sel_t):
    def top2_sum(a, b, c, d):
        hi1, lo1 = jnp.maximum(a, b), jnp.minimum(a, b)
        hi2, lo2 = jnp.maximum(c, d), jnp.minimum(c, d)
        return jnp.maximum(hi1, hi2) + jnp.maximum(jnp.minimum(hi1, hi2), jnp.maximum(lo1, lo2))

    score = [top2_sum(*sel_t[g * EXPERTS_PER_GROUP:(g + 1) * EXPERTS_PER_GROUP])
             for g in range(N_GROUPS)]
    best_here = []
    for g in range(N_GROUPS):
        ok = None
        for o in range(N_GROUPS):
            if o == g:
                continue
            c = (score[g] > score[o]) if o < g else (score[g] >= score[o])
            ok = c if ok is None else jnp.logical_and(ok, c)
        best_here.append(ok)
    picked = []
    for e in range(N_EXPERTS):
        g = e // EXPERTS_PER_GROUP
        rank = jnp.zeros_like(sel_t[e])
        for o in range(g * EXPERTS_PER_GROUP, (g + 1) * EXPERTS_PER_GROUP):
            if o == e:
                continue
            ahead = (sel_t[o] >= sel_t[e]) if o < e else (sel_t[o] > sel_t[e])
            rank = rank + jnp.where(ahead, 1.0, 0.0)
        picked.append(jnp.where(jnp.logical_and(best_here[g], rank < TOP_K), aff_t[e], 0.0))
    total = picked[0]
    for e in range(1, N_EXPERTS):
        total = total + picked[e]
    return [p / total for p in picked]


def _moe_kernel(x_ref, wr_ref, br_ref, wg_ref, wu_ref, wd_ref, g_ref, b_ref, o_ref, acc_ref):
    x = x_ref[...]
    xh = x.astype(BF16)
    xl = (x - xh.astype(F32)).astype(BF16)
    both = _dot(xh, wr_ref[...])
    logits = both[:, :LANE] + both[:, LANE:] + _dot(xl, wr_ref[:, :LANE])
    aff = _sigmoid(logits).T
    bias = br_ref[...]
    aff_t = [aff[e:e + 1, :] for e in range(N_EXPERTS)]
    sel_t = [aff_t[e] + bias[e:e + 1, :] for e in range(N_EXPERTS)]
    gate_rows = _route(aff_t, sel_t)
    tm = x.shape[0]
    gate_t = jnp.concatenate(gate_rows + [jnp.zeros((LANE - N_EXPERTS, tm), F32)], axis=0)
    gate = gate_t.T

    for e in range(N_EXPERTS):
        gt = _dot(xh, wg_ref[e])
        hmid = gt * _sigmoid(gt) * _dot(xh, wu_ref[e])
        y = _dot(hmid.astype(BF16), wd_ref[e]) * gate[:, e:e + 1]
        if e == 0:
            acc_ref[...] = y
        else:
            acc_ref[...] += y
    o_ref[...] = _layer_norm(ALPHA * x + acc_ref[...], g_ref[...], b_ref[...])


def _moe(x, wr, br, wg, wu, wd, g, b, *, tm, layer):
    m = x.shape[0]
    tm = min(tm, m)
    assert m % tm == 0
    const2 = lambda i: (0, 0)
    this_layer = lambda i: (layer, 0, 0, 0)
    return pl.pallas_call(
        _moe_kernel,
        grid=(m // tm,),
        in_specs=[pl.BlockSpec((tm, D_MODEL), lambda i: (i, 0)),
                  pl.BlockSpec((D_MODEL, 2 * LANE), const2),
                  pl.BlockSpec((LANE, 1), const2),
                  pl.BlockSpec((None, N_EXPERTS, D_MODEL, D_EXPERT), this_layer, pipeline_mode=pl.Buffered(1)),
                  pl.BlockSpec((None, N_EXPERTS, D_MODEL, D_EXPERT), this_layer, pipeline_mode=pl.Buffered(1)),
                  pl.BlockSpec((None, N_EXPERTS, D_EXPERT, D_MODEL), this_layer, pipeline_mode=pl.Buffered(1)),
                  pl.BlockSpec((1, D_MODEL), const2),
                  pl.BlockSpec((1, D_MODEL), const2)],
        out_specs=pl.BlockSpec((tm, D_MODEL), lambda i: (i, 0)),
        out_shape=jax.ShapeDtypeStruct((m, D_MODEL), F32),
        scratch_shapes=[pltpu.VMEM((tm, D_MODEL), F32)],
        compiler_params=_params(("parallel",)),
        name="moe",
    )(x, wr, br, wg, wu, wd, g, b)


def _rope_tables(pos, d):
    half = d // 2
    inv = jnp.power(ROPE_BASE, -jnp.arange(half, dtype=F32) / half)
    ang = pos.astype(F32)[:, None] * inv[None, :]
    cos, sin = jnp.cos(ang), jnp.sin(ang)
    zero = jnp.zeros_like(sin)
    rep = LANE // d
    cos_t = jnp.tile(jnp.concatenate([cos, cos], axis=1), (1, rep))
    sin_a = jnp.tile(jnp.concatenate([-sin, zero], axis=1), (1, rep))
    sin_b = jnp.tile(jnp.concatenate([zero, sin], axis=1), (1, rep))
    return cos_t, sin_a, sin_b


def _retention_tables(lc):
    log_g = jnp.log1p(-jnp.exp2(-5.0 - jnp.arange(N_HEADS, dtype=F32)))
    i = jnp.arange(lc, dtype=F32)
    diff = i[:, None] - i[None, :]
    dec = jnp.where(diff >= 0, jnp.exp(jnp.maximum(diff, 0.0)[None] * log_g[:, None, None]), 0.0)
    qdec = jnp.exp((i[None, :] + 1.0) * log_g[:, None])
    kdec = jnp.exp((lc - 1.0 - i)[None, :] * log_g[:, None])
    gl = jnp.exp(lc * log_g)
    dec = dec.reshape(N_HEADS * lc, lc)
    qdec = jnp.broadcast_to(qdec[:, :, None], (N_HEADS, lc, D_HEAD)).reshape(N_HEADS * lc, D_HEAD)
    gl = jnp.broadcast_to(gl[:, None, None], (N_HEADS, D_HEAD, D_HEAD)).reshape(BRANCH_W, D_HEAD)
    kdec = jnp.repeat(kdec.T, D_HEAD, axis=1)
    return dec, qdec, kdec, gl


def _band_bias(rel_bias, tq, win, q_minus_k0, valid):
    length = tq + win - 1
    d = np.arange(length) - (tq - 1) - q_minus_k0
    idx = np.clip(d, -REL_CLIP, REL_CLIP) + REL_CLIP
    g = rel_bias[:, idx].astype(F32) * LOG2E
    gp = jnp.concatenate([g, jnp.zeros((N_HEADS, 1), F32)], axis=1)
    m = jnp.tile(gp, (1, tq))[:, :tq * length].reshape(N_HEADS, tq, length)
    tile = m[:, :, tq - 1:tq - 1 + win]
    return jnp.where(valid[None], tile, NEG_INF).reshape(N_HEADS * tq, win)


def _pack_w_in(w_in_l):
    cols = []
    src = 0
    for w in _IN_WIDTH:
        seg = w_in_l[:, src:src + w]
        cols.append(jnp.pad(seg, ((0, 0), (0, _round_up(w, LANE) - w))))
        src += w
    return jnp.concatenate(cols, axis=1).astype(BF16)


def _pad_rows(t, n):
    return jnp.pad(t, ((0, 0), (0, n - t.shape[1]), (0, 0)))


def _token_mixers(x, pos0, past, lw, *, prompt, layer, stacked):
    (w_in_p, qn, wuq, kvn, wukv, gn_g, gn_b, rel_bias, b, length) = lw
    assert pos0 % CHUNK == 0
    pos = pos0 + jnp.arange(length)
    lc = 256 if prompt else length
    dec, qdec, kdec, gl = _retention_tables(lc)
    rope32 = _rope_tables(pos, DR_A)
    lane = np.arange(LANE)
    rotary = (lane >= DN_A) & (lane < DQK_A)
    rope_q = [jnp.where(rotary[None, :], t, fill) for t, fill in zip(rope32, (1.0, 0.0, 0.0))]
    tables = (list(rope32) + list(_rope_tables(pos, D_HEAD)) + [jnp.tile(kdec, (length // lc, 1))] + rope_q)
    if not prompt:
        tables = [jnp.tile(t, (b, 1)) for t in tables]
    keep = min(PREV_CHUNKS * CHUNK, length)
    outs, stacked = _inproj(x, w_in_p, qn, wuq, kvn, wukv, tables, tm=TOKEN_TILE if prompt else b * length,
                            layer=layer, stacked=stacked, seq_len=length if prompt else None,
                            keep_rows=keep if prompt and keep < length else None)
    per_batch = lambda o: o.reshape(o.shape[:-2] + (b, length, o.shape[-1]))
    (q_a, ckv, kpe, kf, v_a, rq, rk, rkd, rv, rg, sq, sk, sv, bq, bk, bv) = [
        per_batch(o) for o in outs[:len(_INPROJ_OUT)]]
    here = lambda t: (t, layer)
    only = lambda t: (t[None], 0)

    if prompt:
        tq = 256
        qa_t, va_t, sq_t, sv_t, bq_t, bv_t = outs[len(_INPROJ_OUT):len(_INPROJ_OUT) + 6]
        nseq = SEQS_PER_STEP if b % SEQS_PER_STEP == 0 else 1
        o_a = _mla_attn_t(qa_t, kf, va_t, tq=tq, nseq=nseq)
        o_c = _sb_attn_t(sq_t, here(sk), sv_t, tq=tq, nseq=nseq)
        s0 = jnp.zeros((b, BRANCH_W, D_HEAD), F32)
        win = 3 * tq
        i = np.arange(tq)[:, None]
        c = np.arange(win)[None, :]
        variants = []
        for t in range(3):
            qc, kc = i // CHUNK + t * (tq // CHUNK), c // CHUNK
            variants.append(_band_bias(rel_bias, tq, win, t * tq, (kc <= qc) & (kc >= qc - PREV_CHUNKS)))
        bias_t = jnp.stack(variants).transpose(0, 2, 1)
        o_d = _band_attn_t(bq_t, here(bk), bv_t, bias_t, tq=tq, win=win, back=2,
                           nseq=2 * nseq if b % (2 * nseq) == 0 else nseq)
    else:
        c_ckv, c_kpe, s_prev, c_sk, c_sv, c_bk, c_bv = past
        n_past = c_ckv.shape[2]
        tko = LANE
        kf_c, v_c = _expand_latent(c_ckv.reshape(DEPTH, b * n_past, KV_RANK),
                                   c_kpe.reshape(DEPTH, b * n_past, DR_A), wukv, layer,
                                   math.gcd(b * n_past, CACHE_TILE))
        o_a = _mla_attn(q_a, only(_pad_rows(kf, tko)), only(_pad_rows(v_a, tko)),
                        only(kf_c.reshape(b, n_past, QA_W)), only(v_c.reshape(b, n_past, BRANCH_W)),
                        tq=length, tk=math.gcd(n_past, SAMPLE_KEY_TILE), n_own=length, causal_tiles=False)
        o_c = _sb_attn(sq, only(_pad_rows(sk[layer], tko)), only(_pad_rows(sv[layer], tko)),
                       (c_sk.reshape(DEPTH, b, n_past, BRANCH_W), layer),
                       (c_sv.reshape(DEPTH, b, n_past, BRANCH_W), layer),
                       tq=length, tk=256, n_own=length, causal_tiles=False)
        s0 = s_prev[layer].reshape(b, BRANCH_W, D_HEAD)
        n_band = c_bk.shape[2]
        n_keys = n_band + length
        win = _round_up(n_keys, LANE)
        bk_all = _pad_rows(jnp.concatenate([c_bk[layer].reshape(b, n_band, BRANCH_W), bk[layer]], axis=1), win)
        bv_all = _pad_rows(jnp.concatenate([c_bv[layer].reshape(b, n_band, BRANCH_W), bv[layer]], axis=1), win)
        k_pos = pos0 - n_band + np.arange(win)
        q_pos = pos0 + np.arange(length)
        qc, kc = q_pos[:, None] // CHUNK, k_pos[None, :] // CHUNK
        valid = (np.arange(win)[None, :] < n_keys) & (k_pos[None, :] >= 0) & (kc <= qc) & (kc >= qc - PREV_CHUNKS)
        bias = _band_bias(rel_bias, length, win, n_band, valid)[None]
        o_d = _band_attn(bq, only(bk_all), only(bv_all), bias, tq=length, win=win, back=0)

    o_r, s_ret = _retention(rq, rk, rkd, rv, rg, s0, dec, qdec, gl,
                            gn_g.reshape(N_HEADS, 1, D_HEAD), gn_b.reshape(N_HEADS, 1, D_HEAD), lc=lc,
                            nseq=SEQS_PER_STEP if b % SEQS_PER_STEP == 0 else 1)
    s_ret = s_ret.reshape(b, N_HEADS, D_HEAD, D_HEAD)
    flat = lambda t: t.reshape(b * length, BRANCH_W)
    return (flat(o_a), flat(o_r), flat(o_c), flat(o_d)), s_ret, stacked


def _state_outputs(stacked, s_ret, b, length):
    per_seq = lambda t: t.reshape(DEPTH, b, t.shape[1] // b, t.shape[-1])
    heads4 = lambda t: per_seq(t).reshape(DEPTH, b, t.shape[1] // b, N_HEADS, D_HEAD)
    ckv, kpe, sk, sv, bk, bv = stacked[:6]
    if len(stacked) > 6:
        bk, bv = stacked[6:]
    return (per_seq(ckv), per_seq(kpe), jnp.stack(s_ret, axis=0), heads4(sk), heads4(sv), heads4(bk), heads4(bv))


def kernel(x_prompt, x_sample, cache_mla_ckv, cache_mla_kpe, state_ret, cache_sb_k, cache_sb_v, cache_band_k, cache_band_v, w_in, mla_q_norm, mla_w_uq, mla_kv_norm, mla_w_ukv, ret_gn_g, ret_gn_b, band_rel_bias, w_branch, w_o, ln1_g, ln1_b, w_router, b_router, w_exp_gate, w_exp_up, w_exp_down, ln2_g, ln2_b):
    bp, lp, _ = x_prompt.shape
    bs, ls, _ = x_sample.shape
    past_len = cache_mla_ckv.shape[2]
    xp = x_prompt.reshape(bp * lp, D_MODEL)
    xs = x_sample.reshape(bs * ls, D_MODEL)

    wr = jnp.pad(w_router, ((0, 0), (0, LANE - N_EXPERTS)))
    wrh = wr.astype(BF16)
    wr2 = jnp.concatenate([wrh, (wr - wrh.astype(F32)).astype(BF16)], axis=1)
    br = jnp.pad(b_router, (0, LANE - N_EXPERTS)).reshape(LANE, 1)

    we_gate, we_up, we_down = w_exp_gate.astype(BF16), w_exp_up.astype(BF16), w_exp_down.astype(BF16)
    past = (cache_mla_ckv, cache_mla_kpe, state_ret, cache_sb_k, cache_sb_v, cache_band_k, cache_band_v)
    ret_p, ret_s = [], []
    stacked_p = stacked_s = None
    for l in range(DEPTH):
        wuq = mla_w_uq[l]
        wuq = jnp.pad(wuq, ((0, _round_up(Q_RANK, LANE) - Q_RANK), (0, 0), (0, LANE - DQK_A)))
        wuq = wuq.reshape(-1, QA_W).astype(BF16)
        wukv = mla_w_ukv[l]
        wukv = jnp.concatenate(
            [jnp.pad(wukv[:, :, :DN_A], ((0, 0), (0, 0), (0, LANE - DN_A))).reshape(KV_RANK, QA_W),
             wukv[:, :, DN_A:].reshape(KV_RANK, -1)], axis=1).astype(BF16)
        qn = jnp.pad(mla_q_norm[l], (0, _round_up(Q_RANK, LANE) - Q_RANK)).reshape(1, -1)
        kvn = mla_kv_norm[l].reshape(1, KV_RANK)
        w_in_p = _pack_w_in(w_in[l])
        wg = w_in[l][:, GATE_COL0:].astype(BF16)
        wb = w_branch[l].astype(BF16)
        wo = w_o[l].astype(BF16)
        g1, b1 = ln1_g[l].reshape(1, D_MODEL), ln1_b[l].reshape(1, D_MODEL)
        g2, b2 = ln2_g[l].reshape(1, D_MODEL), ln2_b[l].reshape(1, D_MODEL)
        lw =(w_in_p, qn, wuq, kvn, wukv, ret_gn_g[l], ret_gn_b[l], band_rel_bias[l])

        br_p, s_ret_p, stacked_p = _token_mixers(xp, 0, None, lw + (bp, lp), prompt=True,
                                                 layer=l, stacked=stacked_p)
        br_s, s_ret_s, stacked_s = _token_mixers(xs, past_len, past, lw + (bs, ls), prompt=False,
                                                 layer=l, stacked=stacked_s)
        xp = _merge(xp, br_p, wg, wb, wo, g1, b1, tm=TOKEN_TILE)
        xs = _merge(xs, br_s, wg, wb, wo, g1, b1, tm=TOKEN_TILE)
        xp = _moe(xp, wr2, br, we_gate, we_up, we_down, g2, b2, tm=TOKEN_TILE, layer=l)
        xs = _moe(xs, wr2, br, we_gate, we_up, we_down, g2, b2, tm=TOKEN_TILE, layer=l)
        ret_p.append(s_ret_p)
        ret_s.append(s_ret_s)

    return ((xp.reshape(bp, lp, D_MODEL), xs.reshape(bs, ls, D_MODEL))
            + _state_outputs(stacked_p, ret_p, bp, lp)
            + _state_outputs(stacked_s, ret_s, bs, ls))
```

```python
import functools
import math

import jax
import jax.numpy as jnp
import numpy as np
from jax import lax
from jax.experimental import pallas as pl
from jax.experimental.pallas import tpu as pltpu

D_MODEL = 1024
DEPTH = 2
CHUNK = 64
N_BRANCH = 4
BRANCH_W = D_MODEL // 4
N_HEADS = 4
D_HEAD = BRANCH_W // N_HEADS
DN_A = 64
DR_A = 32
DQK_A = DN_A + DR_A
Q_RANK = (3 * D_MODEL) // 16
KV_RANK = D_MODEL // 8
PREV_CHUNKS = 8
REL_CLIP = 128
ROPE_BASE = 10000.0
N_EXPERTS = 16
N_GROUPS = 4
EXPERTS_PER_GROUP = N_EXPERTS // N_GROUPS
TOP_K = 2
D_EXPERT = D_MODEL // 4
ALPHA = (2.0 * DEPTH) ** 0.25
EPS = 1e-5
NEG_INF = -1e30
LOG2E = 1.4426950408889634
SB_DEAD_LOG2 = -150.0

F32 = jnp.float32
BF16 = jnp.bfloat16

V7X_VMEM_LIMIT = 56 * 1024 * 1024
LANE = 128
TOKEN_TILE = 512
SEQS_PER_STEP = 2
CACHE_TILE = 2048
SAMPLE_KEY_TILE = 512

_IN_NAMES = ("c_q", "c_kv", "k_pe", "rq", "rk", "rv", "rg", "sq", "sk", "sv", "bq", "bk", "bv")
_IN_WIDTH = (Q_RANK, KV_RANK, DR_A) + (BRANCH_W,) * 10
QA_W = N_HEADS * LANE


def _round_up(n, m):
    return (n + m - 1) // m * m


_IN_OFF = {}
_off = 0
for _n, _w in zip(_IN_NAMES, _IN_WIDTH):
    _IN_OFF[_n] = (_off, _round_up(_w, LANE))
    _off += _round_up(_w, LANE)
IN_PACKED = _off
GATE_COL0 = sum(_IN_WIDTH)


def _params(sem):
    return pltpu.CompilerParams(dimension_semantics=sem, vmem_limit_bytes=V7X_VMEM_LIMIT)


def _nt_dot(a, b):
    return lax.dot_general(a, b, (((1,), (1,)), ((), ())), preferred_element_type=F32)


def _tn_dot(a, b):
    return lax.dot_general(a, b, (((0,), (0,)), ((), ())), preferred_element_type=F32)


def _dot(a, b):
    return jnp.dot(a, b, preferred_element_type=F32)


def _layer_norm(v, g, b):
    mu = jnp.mean(v, axis=-1, keepdims=True)
    d = v - mu
    var = jnp.mean(d * d, axis=-1, keepdims=True)
    return d * lax.rsqrt(var + EPS) * g + b


def _sigmoid(v):
    return 0.5 * jnp.tanh(0.5 * v) + 0.5


def _head_slice(h):
    return slice(h * D_HEAD, (h + 1) * D_HEAD)


def _stack_heads(q, lane_sets):
    lane = lax.broadcasted_iota(jnp.int32, q.shape, 1)
    zero = jnp.zeros_like(q)
    parts = []
    for h in range(N_HEADS):
        keep = None
        for lo, hi in lane_sets(h):
            m = jnp.logical_and(lane >= lo, lane < hi)
            keep = m if keep is None else jnp.logical_or(keep, m)
        parts.append(jnp.where(keep, q, zero))
    return jnp.concatenate(parts, axis=0)


def _own_lanes(h):
    return ((h * D_HEAD, (h + 1) * D_HEAD),)


def _mla_lanes(h):
    return ((h * LANE, (h + 1) * LANE),)


def _store_heads(o_ref, parts):
    for h, p in enumerate(parts):
        o_ref[:, _head_slice(h)] = p.astype(o_ref.dtype)


def _expand_kernel(ckv_ref, kpe_ref, w_ref, kf_ref, v_ref):
    kvx = _dot(ckv_ref[...].astype(BF16), w_ref[...])
    kp = kpe_ref[...].astype(BF16)
    tail = jnp.zeros((kp.shape[0], LANE - DN_A - DR_A), BF16)
    for h in range(N_HEADS):
        kf_ref[:, h * LANE:(h + 1) * LANE] = jnp.concatenate(
            [kvx[:, h * LANE:h * LANE + DN_A].astype(BF16), kp, tail], axis=1)
    v_ref[...] = kvx[:, QA_W:].astype(BF16)


def _expand_latent(ckv, kpe, w_ukv, layer, tm):
    m = ckv.shape[1]
    assert m % tm == 0
    return pl.pallas_call(
        _expand_kernel,
        grid=(m // tm,),
        in_specs=[pl.BlockSpec((None, tm, KV_RANK), lambda i: (layer, i, 0)),
                  pl.BlockSpec((None, tm, DR_A), lambda i: (layer, i, 0)),
                  pl.BlockSpec(w_ukv.shape, lambda i: (0, 0))],
        out_specs=[pl.BlockSpec((tm, QA_W), lambda i: (i, 0)),
                   pl.BlockSpec((tm, BRANCH_W), lambda i: (i, 0))],
        out_shape=[jax.ShapeDtypeStruct((m, QA_W), BF16),
                   jax.ShapeDtypeStruct((m, BRANCH_W), BF16)],
        compiler_params=_params(("parallel",)),
        name="expand_latent",
    )(ckv, kpe, w_ukv)


def _rope_block(x, cos, sin_a, sin_b, half):
    return x * cos + pltpu.roll(x, LANE - half, 1) * sin_a + pltpu.roll(x, half, 1) * sin_b


def _inproj_kernel(x_ref, w_ref, qn_ref, wuq_ref, kvn_ref, wukv_ref,
                   c32_ref, a32_ref, b32_ref, c64_ref, a64_ref, b64_ref, kdec_ref, cqa_ref, aqa_ref, bqa_ref,
                   qa_ref, ckv_ref, kpe_ref, kf_ref, va_ref,
                   rq_ref, rk_ref, rkd_ref, rv_ref, rg_ref,
                   sq_ref, sk_ref, sv_ref, bq_ref, bk_ref, bv_ref, t_refs=(), keep_refs=()):
    qat_ref, vat_ref, sqt_ref, svt_ref, bqt_ref, bvt_ref = t_refs if t_refs else (None,) * 6
    z = _dot(x_ref[...].astype(BF16), w_ref[...])

    def seg(name):
        o, w = _IN_OFF[name]
        return z[:, o:o + w]

    cq = seg("c_q")
    cqn = cq * lax.rsqrt(jnp.sum(cq * cq, axis=1, keepdims=True) * (1.0 / Q_RANK) + EPS) * qn_ref[...]
    qa = _dot(cqn.astype(BF16), wuq_ref[...])
    scale_a = DQK_A ** -0.5 * LOG2E
    qa_s = jnp.concatenate(
        [_rope_block(qa[:, h * LANE:(h + 1) * LANE], cqa_ref[...], aqa_ref[...], bqa_ref[...], DR_A // 2)
         for h in range(N_HEADS)], axis=1) * scale_a
    qa_ref[...] = qa_s.astype(BF16)
    if qat_ref is not None:
        qat_ref[...] = qa_s.T.astype(BF16)

    ckv_raw = seg("c_kv")
    ckv = ckv_raw * lax.rsqrt(jnp.mean(ckv_raw * ckv_raw, axis=1, keepdims=True) + EPS) * kvn_ref[...]
    ckv_ref[...] = ckv
    kvx = _dot(ckv.astype(BF16), wukv_ref[...])
    kp = _rope_block(seg("k_pe"), c32_ref[...], a32_ref[...], b32_ref[...], DR_A // 2)
    kpe_ref[...] = kp[:, :DR_A]
    kp_at = pltpu.roll(kp, DN_A, 1)
    for h in range(N_HEADS):
        kf_ref[:, h * LANE:(h + 1) * LANE] = (kvx[:, h * LANE:(h + 1) * LANE] + kp_at).astype(BF16)
    va_ref[...] = kvx[:, QA_W:].astype(BF16)
    if vat_ref is not None:
        vat_ref[...] = kvx[:, QA_W:].T.astype(BF16)

    rq, rk = seg("rq"), seg("rk")
    kdec = kdec_ref[...]
    for blk in range(BRANCH_W // LANE):
        cols = slice(blk * LANE, (blk + 1) * LANE)
        rq_ref[:, cols] = _rope_block(rq[:, cols], c64_ref[...], a64_ref[...], b64_ref[...],
                                      D_HEAD // 2).astype(BF16)
        rkb = _rope_block(rk[:, cols], c64_ref[...], a64_ref[...], b64_ref[...], D_HEAD // 2) * (D_HEAD ** -0.5)
        rk_ref[:, cols] = rkb.astype(BF16)
        rkd_ref[:, cols] = (rkb * kdec[:, cols]).astype(BF16)
    rv_ref[...] = seg("rv").astype(BF16)
    rg_ref[...] = seg("rg")

    scale_h = D_HEAD ** -0.5 * LOG2E
    sq = seg("sq") * scale_h
    sq_ref[...] = sq.astype(BF16)
    sk_ref[...] = seg("sk")
    sv_ref[...] = seg("sv")
    if sqt_ref is not None:
        sqt_ref[...] = sq.T.astype(BF16)
        svt_ref[...] = seg("sv").T.astype(BF16)
    bq = seg("bq") * scale_h
    bq_ref[...] = bq.astype(BF16)
    bk_ref[...] = seg("bk")
    bv_ref[...] = seg("bv")
    if bqt_ref is not None:
        bqt_ref[...] = bq.T.astype(BF16)
        bvt_ref[...] = seg("bv").T.astype(BF16)
    if keep_refs:
        keep_refs[0][...] = seg("bk")
        keep_refs[1][...] = seg("bv")


_INPROJ_OUT = (
    (QA_W, BF16), (KV_RANK, F32), (DR_A, F32), (QA_W, BF16), (BRANCH_W, BF16),
    (BRANCH_W, BF16), (BRANCH_W, BF16), (BRANCH_W, BF16), (BRANCH_W, BF16), (BRANCH_W, F32),
    (BRANCH_W, BF16), (BRANCH_W, F32), (BRANCH_W, F32), (BRANCH_W, BF16), (BRANCH_W, F32), (BRANCH_W, F32))


_STATE_OUTS = (1, 2, 11, 12, 14, 15)


def _inproj_entry(*refs, n_in, n_alias, n_t, n_keep):
    outs = refs[n_in + n_alias:]
    n_main = len(_INPROJ_OUT)
    _inproj_kernel(*refs[:n_in], *outs[:n_main], t_refs=outs[n_main:n_main + n_t],
                   keep_refs=outs[n_main + n_t:n_main + n_t + n_keep])


def _inproj(x, w_in_p, qn, wuq, kvn, wukv, tables, *, tm, layer, stacked, seq_len=None, keep_rows=None):
    m = x.shape[0]
    tm = min(tm, m)
    assert m % tm == 0
    n_pos_tiles = tables[0].shape[0] // tm
    assert tables[0].shape[0] % tm == 0
    row = lambda i: (i, 0)
    const = lambda i: (0, 0)
    pos = lambda i: (i % n_pos_tiles, 0)
    in_specs = [pl.BlockSpec((tm, D_MODEL), row),
                pl.BlockSpec(w_in_p.shape, const),
                pl.BlockSpec(qn.shape, const),
                pl.BlockSpec(wuq.shape, const),
                pl.BlockSpec(kvn.shape, const),
                pl.BlockSpec(wukv.shape, const)]
    in_specs += [pl.BlockSpec((tm, t.shape[1]), pos) for t in tables]
    n_in = len(in_specs)
    out_specs, out_shape = [], []
    for k, (w, dt) in enumerate(_INPROJ_OUT):
        if k in _STATE_OUTS:
            out_specs.append(pl.BlockSpec((None, tm, w), lambda i: (layer, i, 0)))
            out_shape.append(jax.ShapeDtypeStruct((DEPTH, m, w), dt))
        else:
            out_specs.append(pl.BlockSpec((tm, w), row))
            out_shape.append(jax.ShapeDtypeStruct((m, w), dt))
    if seq_len is not None:
        assert seq_len % tm == 0 and m % seq_len == 0
        tps = seq_len // tm
        for w in (QA_W,) + (BRANCH_W,) * 5:
            out_specs.append(pl.BlockSpec((None, w, tm), lambda i: (i // tps, 0, i % tps)))
            out_shape.append(jax.ShapeDtypeStruct((m // seq_len, w, seq_len), BF16))
    n_t = len(out_specs) - len(_INPROJ_OUT)
    state_outs = list(_STATE_OUTS)
    if keep_rows is not None:
        assert seq_len is not None and keep_rows % tm == 0 and keep_rows <= seq_len
        tps, ktiles = seq_len // tm, keep_rows // tm
        kept = lambda i: (layer, (i // tps) * ktiles + jnp.maximum(i % tps - (tps - ktiles), 0), 0)
        for _ in range(2):
            state_outs.append(len(out_specs))
            out_specs.append(pl.BlockSpec((None, tm, BRANCH_W), kept))
            out_shape.append(jax.ShapeDtypeStruct((DEPTH, m // seq_len * keep_rows, BRANCH_W), F32))
    aliases = {}
    extra = ()
    if stacked is not None:
        extra = tuple(stacked)
        assert len(extra) == len(state_outs)
        in_specs += [pl.BlockSpec(memory_space=pl.ANY)] * len(extra)
        aliases = {n_in + k: o for k, o in enumerate(state_outs)}
    outs = pl.pallas_call(
        functools.partial(_inproj_entry, n_in=n_in, n_alias=len(extra), n_t=n_t,
                          n_keep=len(state_outs) - len(_STATE_OUTS)),
        grid=(m // tm,),
        in_specs=in_specs,
        out_specs=out_specs,
        out_shape=out_shape,
        input_output_aliases=aliases,
        compiler_params=_params(("arbitrary",)),
        name="inproj",
    )(x, w_in_p, qn, wuq, kvn, wukv, *tables, *extra)
    return outs, tuple(outs[k] for k in state_outs)


def _tile_spec(src, rows):
    arr, layer = src
    return pl.BlockSpec((None, None, rows, arr.shape[3]), lambda i, t: (layer, i, t, 0))


def _full_spec(src):
    arr, layer = src
    return pl.BlockSpec((None, None, arr.shape[2], arr.shape[3]), lambda i, t: (layer, i, 0, 0))


def _mla_kernel(q_ref, kfo_ref, vo_ref, kfp_ref, vp_ref, o_ref, *, tq, tk, n_own, n_past):
    qi = pl.program_id(1)
    qm = _stack_heads(q_ref[...], _mla_lanes)
    rows = N_HEADS * tq
    tko = kfo_ref.shape[0]
    n_loop = qi if n_past is None else n_past

    def softmax_pv(s, v, carry):
        m, accs = carry
        m_new = jnp.maximum(m, jnp.max(s, axis=1, keepdims=True))
        pb = jnp.exp2(s - m_new).astype(BF16)
        a = jnp.exp2(m - m_new)
        ones = jnp.ones((v.shape[0], D_HEAD), BF16)
        accs = tuple(a[h * tq:(h + 1) * tq] * accs[h]
                     + _dot(pb[h * tq:(h + 1) * tq], jnp.concatenate([v[:, _head_slice(h)], ones], axis=1))
                     for h in range(N_HEADS))
        return m_new, accs

    def past_scores(j):
        start = pl.multiple_of(jnp.minimum(j, jnp.maximum(n_loop - 1, 0)) * tk, tk)
        return _nt_dot(qm, kfp_ref[pl.ds(start, tk), :])

    row_q = lax.broadcasted_iota(jnp.int32, (rows, tko), 0) % tq
    col = lax.broadcasted_iota(jnp.int32, (rows, tko), 1)
    own_mask = jnp.logical_and(col < n_own, col // CHUNK <= row_q // CHUNK)
    carry = (jnp.full((rows, 1), NEG_INF, F32),
             tuple(jnp.zeros((tq, 2 * D_HEAD), F32) for _ in range(N_HEADS)))
    s_next = past_scores(0)
    carry = softmax_pv(jnp.where(own_mask, _nt_dot(qm, kfo_ref[...]), NEG_INF), vo_ref[...], carry)

    def body(j, c):
        s_cur, rest = c
        s_after = past_scores(j + 1)
        start = pl.multiple_of(j * tk, tk)
        return s_after, softmax_pv(s_cur, vp_ref[pl.ds(start, tk), :], rest)

    _, (_, accs) = lax.fori_loop(0, n_loop, body, (s_next, carry))
    _store_heads(o_ref, [accs[h][:, :D_HEAD] / accs[h][:, D_HEAD:D_HEAD + 1] for h in range(N_HEADS)])


def _mla_attn(q, kf_own, v_own, kf_past, v_past, *, tq, tk, n_own, causal_tiles):
    b, lq, _ = q.shape
    nqt = lq // tq
    tko = kf_own[0].shape[2] // nqt
    lp = kf_past[0].shape[2]
    assert lq % tq == 0 and lp % tk == 0
    return pl.pallas_call(
        functools.partial(_mla_kernel, tq=tq, tk=tk, n_own=n_own,
                          n_past=None if causal_tiles else lp // tk),
        grid=(b, nqt),
        in_specs=[pl.BlockSpec((None, tq, QA_W), lambda i, t: (i, t, 0)),
                  _tile_spec(kf_own, tko), _tile_spec(v_own, tko),
                  _full_spec(kf_past), _full_spec(v_past)],
        out_specs=pl.BlockSpec((None, tq, BRANCH_W), lambda i, t: (i, t, 0)),
        out_shape=jax.ShapeDtypeStruct((b, lq, BRANCH_W), BF16),
        compiler_params=_params(("parallel", "arbitrary")),
        name="mla_attn",
    )(q, kf_own[0], v_own[0], kf_past[0], v_past[0])


def _mla_kernel_t(qt_ref, kfo_ref, vto_ref, kfp_ref, vtp_ref, o_ref, *, tq, tk, nseq):
    qi = pl.program_id(1)
    cols = N_HEADS * tq

    def scores(s, kf):
        qt = qt_ref[s]
        return jnp.concatenate([_dot(kf[:, h * LANE:(h + 1) * LANE], qt[h * LANE:(h + 1) * LANE, :])
                                for h in range(N_HEADS)], axis=1)

    def softmax_pv(st, vt, carry):
        m, l, accs = carry
        m_new = jnp.maximum(m, jnp.max(st, axis=0, keepdims=True))
        pt = jnp.exp2(st - m_new)
        a = jnp.exp2(m - m_new)
        l = a * l + jnp.sum(pt, axis=0, keepdims=True)
        pb = pt.astype(BF16)
        accs = tuple(a[:, h * tq:(h + 1) * tq] * accs[h]
                     + _dot(vt[_head_slice(h), :], pb[:, h * tq:(h + 1) * tq]) for h in range(N_HEADS))
        return m_new, l, accs

    def past_scores(s, j):
        start = pl.multiple_of(jnp.minimum(j, jnp.maximum(qi - 1, 0)) * tk, tk)
        return scores(s, kfp_ref[s, pl.ds(start, tk), :])

    key = lax.broadcasted_iota(jnp.int32, (tq, cols), 0)
    qry = lax.broadcasted_iota(jnp.int32, (tq, cols), 1) % tq
    own_mask = key // CHUNK <= qry // CHUNK
    s_next, carries = [], []
    for s in range(nseq):
        empty = (jnp.full((1, cols), NEG_INF, F32), jnp.zeros((1, cols), F32),
                 tuple(jnp.zeros((D_HEAD, tq), F32) for _ in range(N_HEADS)))
        s_next.append(past_scores(s, 0))
        carries.append(softmax_pv(jnp.where(own_mask, scores(s, kfo_ref[s]), NEG_INF), vto_ref[s], empty))

    def body(j, c):
        s_cur, rest = c
        start = pl.multiple_of(j * tk, tk)
        s_after = tuple(past_scores(s, j + 1) for s in range(nseq))
        return s_after, tuple(softmax_pv(s_cur[s], vtp_ref[s, :, pl.ds(start, tk)], rest[s])
                              for s in range(nseq))

    _, done = lax.fori_loop(0, qi, body, (tuple(s_next), tuple(carries)))
    for s in range(nseq):
        _, l, accs = done[s]
        out_t = jnp.concatenate([accs[h] / l[:, h * tq:(h + 1) * tq] for h in range(N_HEADS)], axis=0)
        o_ref[s] = out_t.T.astype(o_ref.dtype)


def _mla_attn_t(qt, kf, vt, *, tq, nseq):
    b, _, length = qt.shape
    assert length % tq == 0 and b % nseq == 0
    return pl.pallas_call(
        functools.partial(_mla_kernel_t, tq=tq, tk=tq, nseq=nseq),
        grid=(b // nseq, length // tq),
        in_specs=[pl.BlockSpec((nseq, QA_W, tq), lambda i, t: (i, 0, t)),
                  pl.BlockSpec((nseq, tq, QA_W), lambda i, t: (i, t, 0)),
                  pl.BlockSpec((nseq, BRANCH_W, tq), lambda i, t: (i, 0, t)),
                  pl.BlockSpec((nseq, length, QA_W), lambda i, t: (i, 0, 0)),
                  pl.BlockSpec((nseq, BRANCH_W, length), lambda i, t: (i, 0, 0))],
        out_specs=pl.BlockSpec((nseq, tq, BRANCH_W), lambda i, t: (i, t, 0)),
        out_shape=jax.ShapeDtypeStruct((b, length, BRANCH_W), BF16),
        compiler_params=_params(("parallel", "arbitrary")),
        name="mla_attn_t",
    )(qt, kf, vt, kf, vt)


def _sb_kernel(q_ref, ko_ref, vo_ref, kp_ref, vp_ref, o_ref, *, tq, tk, n_own, n_past):
    qi = pl.program_id(1)
    qm = _stack_heads(q_ref[...], _own_lanes)
    rows = N_HEADS * tq
    tko = ko_ref.shape[0]
    n_loop = qi if n_past is None else n_past

    def tri2(n):
        r = lax.broadcasted_iota(jnp.int32, (2 * n, n), 0) % n
        c = lax.broadcasted_iota(jnp.int32, (2 * n, n), 1)
        return jnp.where(r > c, 1.0, 0.0).astype(BF16)

    def weigh(z, v, carry, mask, tri):
        run, accs = carry
        neg_abs = lax.bitcast_convert_type(
            lax.bitcast_convert_type(z, jnp.uint32) | jnp.uint32(0x80000000), F32)
        t = jnp.log2(1.0 + jnp.exp2(neg_abs))
        log_beta = jnp.minimum(z, 0.0) - t
        log_stay = log_beta - z
        if mask is not None:
            log_stay = jnp.where(mask, log_stay, 0.0)
        hi = log_stay.astype(BF16)
        lo = (log_stay - hi.astype(F32)).astype(BF16)
        later = _dot(jnp.concatenate([hi, lo], axis=1), tri) + run
        w = jnp.exp2(log_beta + later)
        if mask is not None:
            w = jnp.where(mask, w, 0.0)
        wb = w.astype(BF16)
        vb = v.astype(BF16)
        accs = tuple(accs[h] + _dot(wb[h * tq:(h + 1) * tq], vb[:, _head_slice(h)])
                     for h in range(N_HEADS))
        return run + jnp.sum(log_stay, axis=1, keepdims=True), accs

    row_q = lax.broadcasted_iota(jnp.int32, (rows, tko), 0) % tq
    col = lax.broadcasted_iota(jnp.int32, (rows, tko), 1)
    own_mask = jnp.logical_and(col < n_own, col < row_q)

    def past_start(jj):
        return pl.multiple_of(jnp.clip(n_loop - 1 - jj, 0, kp_ref.shape[0] // tk - 1) * tk, tk)

    def past_scores(jj):
        return _nt_dot(qm, kp_ref[pl.ds(past_start(jj), tk), :].astype(BF16))

    def alive(run):
        return (jnp.max(run) > SB_DEAD_LOG2).astype(jnp.int32)

    z_next = past_scores(0)
    carry = (jnp.zeros((rows, 1), F32), tuple(jnp.zeros((tq, D_HEAD), F32) for _ in range(N_HEADS)))
    run, accs = weigh(_nt_dot(qm, ko_ref[...].astype(BF16)), vo_ref[...], carry, own_mask, tri2(tko))
    tri_past = tri2(tk)

    def cond(c):
        return jnp.logical_and(c[0] < n_loop, c[1] > 0)

    def body(c):
        jj, _, z_cur, run, accs = c
        z_after = past_scores(jj + 1)
        run, accs = weigh(z_cur, vp_ref[pl.ds(past_start(jj), tk), :], (run, accs), None, tri_past)
        return jj + 1, alive(run), z_after, run, accs

    out = lax.while_loop(cond, body, (jnp.int32(0), alive(run), z_next, run, accs))
    _store_heads(o_ref, out[4])


def _sb_attn(q, k_own, v_own, k_past, v_past, *, tq, tk, n_own, causal_tiles):
    b, lq, _ = q.shape
    nqt = lq // tq
    tko = k_own[0].shape[2] // nqt
    lp = k_past[0].shape[2]
    assert lq % tq == 0 and lp % tk == 0
    return pl.pallas_call(
        functools.partial(_sb_kernel, tq=tq, tk=tk, n_own=n_own,
                          n_past=None if causal_tiles else lp // tk),
        grid=(b, nqt),
        in_specs=[pl.BlockSpec((None, tq, BRANCH_W), lambda i, t: (i, t, 0)),
                  _tile_spec(k_own, tko), _tile_spec(v_own, tko),
                  _full_spec(k_past), _full_spec(v_past)],
        out_specs=pl.BlockSpec((None, tq, BRANCH_W), lambda i, t: (i, t, 0)),
        out_shape=jax.ShapeDtypeStruct((b, lq, BRANCH_W), BF16),
        compiler_params=_params(("parallel", "arbitrary")),
        name="sb_attn",
    )(q, k_own[0], v_own[0], k_past[0], v_past[0])


def _sb_kernel_t(qt_ref, ko_ref, vto_ref, kp_ref, vtp_ref, o_ref, *, tq, tk, nseq):
    qi = pl.program_id(1)
    qmts = []
    for s in range(nseq):
        qt = qt_ref[s]
        feat = lax.broadcasted_iota(jnp.int32, qt.shape, 0)
        qmts.append(jnp.concatenate(
            [jnp.where(jnp.logical_and(feat >= h * D_HEAD, feat < (h + 1) * D_HEAD), qt, jnp.zeros_like(qt))
             for h in range(N_HEADS)], axis=1))
    cols = N_HEADS * tq

    def tri2(n):
        r = lax.broadcasted_iota(jnp.int32, (n, 2 * n), 0)
        c = lax.broadcasted_iota(jnp.int32, (n, 2 * n), 1) % n
        return jnp.where(c > r, 1.0, 0.0).astype(BF16)

    def weigh(zt, vt, carry, mask, tri):
        run, accs = carry
        neg_abs = lax.bitcast_convert_type(
            lax.bitcast_convert_type(zt, jnp.uint32) | jnp.uint32(0x80000000), F32)
        t = jnp.log2(1.0 + jnp.exp2(neg_abs))
        log_beta = jnp.minimum(zt, 0.0) - t
        log_stay = log_beta - zt
        if mask is not None:
            log_stay = jnp.where(mask, log_stay, 0.0)
        hi = log_stay.astype(BF16)
        lo = (log_stay - hi.astype(F32)).astype(BF16)
        later = _dot(tri, jnp.concatenate([hi, lo], axis=0)) + run
        w = jnp.exp2(log_beta + later)
        if mask is not None:
            w = jnp.where(mask, w, 0.0)
        wb = w.astype(BF16)
        accs = tuple(accs[h] + _dot(vt[_head_slice(h), :], wb[:, h * tq:(h + 1) * tq])
                     for h in range(N_HEADS))
        return run + jnp.sum(log_stay, axis=0, keepdims=True), accs

    def past_start(jj):
        return pl.multiple_of(jnp.clip(qi - 1 - jj, 0, kp_ref.shape[1] // tk - 1) * tk, tk)

    def past_scores(s, jj):
        return _dot(kp_ref[s, pl.ds(past_start(jj), tk), :].astype(BF16), qmts[s])

    def alive(runs):
        top = jnp.max(runs[0])
        for r in runs[1:]:
            top = jnp.maximum(top, jnp.max(r))
        return (top > SB_DEAD_LOG2).astype(jnp.int32)

    key = lax.broadcasted_iota(jnp.int32, (tq, cols), 0)
    qry = lax.broadcasted_iota(jnp.int32, (tq, cols), 1) % tq
    own_mask = key < qry
    tri_own = tri2(tq)
    z_next, runs, accss = [], [], []
    for s in range(nseq):
        z_next.append(past_scores(s, 0))
        empty = (jnp.zeros((1, cols), F32), tuple(jnp.zeros((D_HEAD, tq), F32) for _ in range(N_HEADS)))
        run, accs = weigh(_dot(ko_ref[s].astype(BF16), qmts[s]), vto_ref[s], empty, own_mask, tri_own)
        runs.append(run)
        accss.append(accs)
    tri_past = tri_own if tk == tq else tri2(tk)

    def cond(c):
        return jnp.logical_and(c[0] < qi, c[1] > 0)

    def body(c):
        jj, _, z_cur, runs, accss = c
        z_after = tuple(past_scores(s, jj + 1) for s in range(nseq))
        new = [weigh(z_cur[s], vtp_ref[s, :, pl.ds(past_start(jj), tk)], (runs[s], accss[s]), None, tri_past)
               for s in range(nseq)]
        runs = tuple(n[0] for n in new)
        return jj + 1, alive(runs), z_after, runs, tuple(n[1] for n in new)

    out = lax.while_loop(cond, body, (jnp.int32(0), alive(runs), tuple(z_next), tuple(runs), tuple(accss)))
    for s in range(nseq):
        o_ref[s] = jnp.concatenate(out[4][s], axis=0).T.astype(o_ref.dtype)


def _sb_attn_t(qt, k, vt, *, tq, nseq):
    b, _, length = qt.shape
    k_arr, layer = k
    assert length % tq == 0 and b % nseq == 0
    return pl.pallas_call(
        functools.partial(_sb_kernel_t, tq=tq, tk=tq, nseq=nseq),
        grid=(b // nseq, length // tq),
        in_specs=[pl.BlockSpec((nseq, BRANCH_W, tq), lambda i, t: (i, 0, t)),
                  pl.BlockSpec((None, nseq, tq, BRANCH_W), lambda i, t: (layer, i, t, 0)),
                  pl.BlockSpec((nseq, BRANCH_W, tq), lambda i, t: (i, 0, t)),
                  pl.BlockSpec((None, nseq, length, BRANCH_W), lambda i, t: (layer, i, 0, 0)),
                  pl.BlockSpec((nseq, BRANCH_W, length), lambda i, t: (i, 0, 0))],
        out_specs=pl.BlockSpec((nseq, tq, BRANCH_W), lambda i, t: (i, t, 0)),
        out_shape=jax.ShapeDtypeStruct((b, length, BRANCH_W), BF16),
        compiler_params=_params(("parallel", "arbitrary")),
        name="sb_attn_t",
    )(qt, k_arr, vt, k_arr, vt)


def _ret_kernel(q_ref, k_ref, kd_ref, v_ref, rg_ref, s0_ref, dec_ref, qdec_ref, gl_ref,
                gng_ref, gnb_ref, o_ref, sout_ref, state_ref, *, lc, nseq):
    c = pl.program_id(1)

    @pl.when(c == 0)
    def _():
        state_ref[...] = s0_ref[...]

    for s in range(nseq):
        qm = _stack_heads(q_ref[s], _own_lanes)
        v = v_ref[s]
        state = state_ref[s]
        scores = (_nt_dot(qm, k_ref[s]) * dec_ref[...]).astype(BF16)
        cross = _dot(qm, state.astype(BF16)) * qdec_ref[...]
        kv_full = _tn_dot(kd_ref[s], v)
        new_state = gl_ref[...] * state + jnp.concatenate(
            [kv_full[_head_slice(h), _head_slice(h)] for h in range(N_HEADS)], axis=0)
        state_ref[s] = new_state

        rg = rg_ref[s]
        for h in range(N_HEADS):
            o = _dot(scores[h * lc:(h + 1) * lc], v[:, _head_slice(h)]) + cross[h * lc:(h + 1) * lc]
            mu = jnp.mean(o, axis=-1, keepdims=True)
            d = o - mu
            var = jnp.mean(d * d, axis=-1, keepdims=True)
            y = d * lax.rsqrt(var + EPS) * gng_ref[h] + gnb_ref[h]
            g = rg[:, _head_slice(h)]
            o_ref[s, :, _head_slice(h)] = (y * (g * _sigmoid(g))).astype(o_ref.dtype)

    @pl.when(c == pl.num_programs(1) - 1)
    def _():
        sout_ref[...] = state_ref[...]


def _retention(q, k, kd, v, rg, s0, dec, qdec, gl, gng, gnb, *, lc, nseq):
    b, length, _ = q.shape
    assert length % lc == 0 and b % nseq == 0
    seq = lambda i, t: (i, t, 0)
    st = lambda i, t: (i, 0, 0)
    c2 = lambda i, t: (0, 0)
    c3 = lambda i, t: (0, 0, 0)
    return pl.pallas_call(
        functools.partial(_ret_kernel, lc=lc, nseq=nseq),
        grid=(b // nseq, length // lc),
        in_specs=[pl.BlockSpec((nseq, lc, BRANCH_W), seq)] * 5
        + [pl.BlockSpec((nseq, BRANCH_W, D_HEAD), st),
           pl.BlockSpec(dec.shape, c2), pl.BlockSpec(qdec.shape, c2), pl.BlockSpec(gl.shape, c2),
           pl.BlockSpec(gng.shape, c3), pl.BlockSpec(gnb.shape, c3)],
        out_specs=[pl.BlockSpec((nseq, lc, BRANCH_W), seq),
                   pl.BlockSpec((nseq, BRANCH_W, D_HEAD), st)],
        out_shape=[jax.ShapeDtypeStruct((b, length, BRANCH_W), BF16),
                   jax.ShapeDtypeStruct((b, BRANCH_W, D_HEAD), F32)],
        scratch_shapes=[pltpu.VMEM((nseq, BRANCH_W, D_HEAD), F32)],
        compiler_params=_params(("parallel", "arbitrary")),
        name="retention",
    )(q, k, kd, v, rg, s0, dec, qdec, gl, gng, gnb)


def _band_kernel(q_ref, k_ref, v_ref, bias_ref, o_ref, *, tq, win, back):
    qi = pl.program_id(1)
    start = pl.multiple_of(jnp.maximum(qi - back, 0) * tq, tq)
    k = k_ref[pl.ds(start, win), :].astype(BF16)
    v = v_ref[pl.ds(start, win), :].astype(BF16)
    qm = _stack_heads(q_ref[...], _own_lanes)
    s = _nt_dot(qm, k) + bias_ref[...]
    pb = jnp.exp2(s - jnp.max(s, axis=1, keepdims=True)).astype(BF16)
    ones = jnp.ones((win, D_HEAD), BF16)
    outs = []
    for h in range(N_HEADS):
        o = _dot(pb[h * tq:(h + 1) * tq], jnp.concatenate([v[:, _head_slice(h)], ones], axis=1))
        outs.append(o[:, :D_HEAD] / o[:, D_HEAD:D_HEAD + 1])
    _store_heads(o_ref, outs)


def _band_attn(q, k, v, bias, *, tq, win, back):
    b, lq, _ = q.shape
    nvar = bias.shape[0]
    assert lq % tq == 0
    return pl.pallas_call(
        functools.partial(_band_kernel, tq=tq, win=win, back=back),
        grid=(b, lq // tq),
        in_specs=[pl.BlockSpec((None, tq, BRANCH_W), lambda i, t: (i, t, 0)),
                  _full_spec(k), _full_spec(v),
                  pl.BlockSpec((None, N_HEADS * tq, win), lambda i, t: (jnp.minimum(t, nvar - 1), 0, 0))],
        out_specs=pl.BlockSpec((None, tq, BRANCH_W), lambda i, t: (i, t, 0)),
        out_shape=jax.ShapeDtypeStruct((b, lq, BRANCH_W), BF16),
        compiler_params=_params(("parallel", "arbitrary")),
        name="band_attn",
    )(q, k[0], v[0], bias)


def _band_kernel_t(qt_ref, k_ref, vt_ref, bias_ref, o_ref, *, tq, win, back, nseq):
    qi = pl.program_id(1)
    start = pl.multiple_of(jnp.maximum(qi - back, 0) * tq, tq)
    for s in range(nseq):
        k = k_ref[s, pl.ds(start, win), :].astype(BF16)
        vt = vt_ref[s, :, pl.ds(start, win)]
        qt = qt_ref[s]
        feat = lax.broadcasted_iota(jnp.int32, qt.shape, 0)
        qmt = jnp.concatenate(
            [jnp.where(jnp.logical_and(feat >= h * D_HEAD, feat < (h + 1) * D_HEAD), qt, jnp.zeros_like(qt))
             for h in range(N_HEADS)], axis=1)
        st = _dot(k, qmt) + bias_ref[...]
        pt = jnp.exp2(st - jnp.max(st, axis=0, keepdims=True))
        l = jnp.sum(pt, axis=0, keepdims=True)
        pb = pt.astype(BF16)
        out_t = jnp.concatenate(
            [_dot(vt[_head_slice(h), :], pb[:, h * tq:(h + 1) * tq]) / l[:, h * tq:(h + 1) * tq]
             for h in range(N_HEADS)], axis=0)
        o_ref[s] = out_t.T.astype(o_ref.dtype)


def _band_attn_t(qt, k, vt, bias_t, *, tq, win, back, nseq):
    b, _, length = qt.shape
    nvar = bias_t.shape[0]
    k_arr, layer = k
    assert length % tq == 0 and b % nseq == 0
    return pl.pallas_call(
        functools.partial(_band_kernel_t, tq=tq, win=win, back=back, nseq=nseq),
        grid=(b // nseq, length // tq),
        in_specs=[pl.BlockSpec((nseq, BRANCH_W, tq), lambda i, t: (i, 0, t)),
                  pl.BlockSpec((None, nseq, k_arr.shape[2], BRANCH_W), lambda i, t: (layer, i, 0, 0)),
                  pl.BlockSpec((nseq, BRANCH_W, length), lambda i, t: (i, 0, 0)),
                  pl.BlockSpec((None, win, N_HEADS * tq), lambda i, t: (jnp.minimum(t, nvar - 1), 0, 0))],
        out_specs=pl.BlockSpec((nseq, tq, BRANCH_W), lambda i, t: (i, t, 0)),
        out_shape=jax.ShapeDtypeStruct((b, length, BRANCH_W), BF16),
        compiler_params=_params(("parallel", "arbitrary")),
        name="band_attn_t",
    )(qt, k_arr, vt, bias_t)


def _merge_kernel(x_ref, ba_ref, bb_ref, bc_ref, bd_ref, wg_ref, wb_ref, wo_ref, g_ref, b_ref, o_ref):
    x = x_ref[...]
    xb = x.astype(BF16)
    merged = None
    for n, br_ref in enumerate((ba_ref, bb_ref, bc_ref, bd_ref)):
        logits = _dot(xb, wg_ref[:, n * D_MODEL:(n + 1) * D_MODEL])
        term = _dot(br_ref[...], wb_ref[n]) * _sigmoid(logits)
        merged = term if merged is None else merged + term
    mix = _dot(merged.astype(BF16), wo_ref[...])
    o_ref[...] = _layer_norm(ALPHA * x + mix, g_ref[...], b_ref[...])


def _merge(x, branches, wg, wb, wo, g, b, *, tm):
    m = x.shape[0]
    tm = min(tm, m)
    assert m % tm == 0
    const2 = lambda i: (0, 0)
    row = lambda i: (i, 0)
    return pl.pallas_call(
        _merge_kernel,
        grid=(m // tm,),
        in_specs=[pl.BlockSpec((tm, D_MODEL), row)]
        + [pl.BlockSpec((tm, BRANCH_W), row)] * N_BRANCH
        + [pl.BlockSpec((D_MODEL, N_BRANCH * D_MODEL), const2),
           pl.BlockSpec((N_BRANCH, BRANCH_W, D_MODEL), lambda i: (0, 0, 0)),
           pl.BlockSpec((D_MODEL, D_MODEL), const2),
           pl.BlockSpec((1, D_MODEL), const2),
           pl.BlockSpec((1, D_MODEL), const2)],
        out_specs=pl.BlockSpec((tm, D_MODEL), row),
        out_shape=jax.ShapeDtypeStruct((m, D_MODEL), F32),
        compiler_params=_params(("parallel",)),
        name="merge",
    )(x, *branches, wg, wb, wo, g, b)


def _route(aff_t, sel_t):
    def top2_sum(a, b, c, d):
        hi1, lo1 = jnp.maximum(a, b), jnp.minimum(a, b)
        hi2, lo2 = jnp.maximum(c, d), jnp.minimum(c, d)
        return jnp.maximum(hi1, hi2) + jnp.maximum(jnp.minimum(hi1, hi2), jnp.maximum(lo1, lo2))

    score = [top2_sum(*sel_t[g * EXPERTS_PER_GROUP:(g + 1) * EXPERTS_PER_GROUP])
             for g in range(N_GROUPS)]
    best_here = []
    for g in range(N_GROUPS):
        ok = None
        for o in range(N_GROUPS):
            if o == g:
                continue
            c = (score[g] > score[o]) if o < g else (score[g] >= score[o])
            ok = c if ok is None else jnp.logical_and(ok, c)
        best_here.append(ok)
    picked = []
    for e in range(N_EXPERTS):
        g = e // EXPERTS_PER_GROUP
        rank = jnp.zeros_like(sel_t[e])
        for o in range(g * EXPERTS_PER_GROUP, (g + 1) * EXPERTS_PER_GROUP):
            if o == e:
                continue
            ahead = (sel_t[o] >= sel_t[e]) if o < e else (sel_t[o] > sel_t[e])
            rank = rank + jnp.where(ahead, 1.0, 0.0)
        picked.append(jnp.where(jnp.logical_and(best_here[g], rank < TOP_K), aff_t[e], 0.0))
    total = picked[0]
    for e in range(1, N_EXPERTS):
        total = total + picked[e]
    return [p / total for p in picked]


def _moe_kernel(x_ref, wr_ref, br_ref, wg_ref, wu_ref, wd_ref, g_ref, b_ref, o_ref, acc_ref):
    x = x_ref[...]
    xh = x.astype(BF16)
    xl = (x - xh.astype(F32)).astype(BF16)
    both = _dot(xh, wr_ref[...])
    logits = both[:, :LANE] + both[:, LANE:] + _dot(xl, wr_ref[:, :LANE])
    aff = _sigmoid(logits).T
    bias = br_ref[...]
    aff_t = [aff[e:e + 1, :] for e in range(N_EXPERTS)]
    sel_t = [aff_t[e] + bias[e:e + 1, :] for e in range(N_EXPERTS)]
    gate_rows = _route(aff_t, sel_t)
    tm = x.shape[0]
    gate_t = jnp.concatenate(gate_rows + [jnp.zeros((LANE - N_EXPERTS, tm), F32)], axis=0)
    gate = gate_t.T

    for e in range(N_EXPERTS):
        gt = _dot(xh, wg_ref[e])
        hmid = gt * _sigmoid(gt) * _dot(xh, wu_ref[e])
        y = _dot(hmid.astype(BF16), wd_ref[e]) * gate[:, e:e + 1]
        if e == 0:
            acc_ref[...] = y
        else:
            acc_ref[...] += y
    o_ref[...] = _layer_norm(ALPHA * x + acc_ref[...], g_ref[...], b_ref[...])


def _moe(x, wr, br, wg, wu, wd, g, b, *, tm, layer):
    m = x.shape[0]
    tm = min(tm, m)
    assert m % tm == 0
    const2 = lambda i: (0, 0)
    this_layer = lambda i: (layer, 0, 0, 0)
    return pl.pallas_call(
        _moe_kernel,
        grid=(m // tm,),
        in_specs=[pl.BlockSpec((tm, D_MODEL), lambda i: (i, 0)),
                  pl.BlockSpec((D_MODEL, 2 * LANE), const2),
                  pl.BlockSpec((LANE, 1), const2),
                  pl.BlockSpec((None, N_EXPERTS, D_MODEL, D_EXPERT), this_layer, pipeline_mode=pl.Buffered(1)),
                  pl.BlockSpec((None, N_EXPERTS, D_MODEL, D_EXPERT), this_layer, pipeline_mode=pl.Buffered(1)),
                  pl.BlockSpec((None, N_EXPERTS, D_EXPERT, D_MODEL), this_layer, pipeline_mode=pl.Buffered(1)),
                  pl.BlockSpec((1, D_MODEL), const2),
                  pl.BlockSpec((1, D_MODEL), const2)],
        out_specs=pl.BlockSpec((tm, D_MODEL), lambda i: (i, 0)),
        out_shape=jax.ShapeDtypeStruct((m, D_MODEL), F32),
        scratch_shapes=[pltpu.VMEM((tm, D_MODEL), F32)],
        compiler_params=_params(("parallel",)),
        name="moe",
    )(x, wr, br, wg, wu, wd, g, b)


def _rope_tables(pos, d):
    half = d // 2
    inv = jnp.power(ROPE_BASE, -jnp.arange(half, dtype=F32) / half)
    ang = pos.astype(F32)[:, None] * inv[None, :]
    cos, sin = jnp.cos(ang), jnp.sin(ang)
    zero = jnp.zeros_like(sin)
    rep = LANE // d
    cos_t = jnp.tile(jnp.concatenate([cos, cos], axis=1), (1, rep))
    sin_a = jnp.tile(jnp.concatenate([-sin, zero], axis=1), (1, rep))
    sin_b = jnp.tile(jnp.concatenate([zero, sin], axis=1), (1, rep))
    return cos_t, sin_a, sin_b


def _retention_tables(lc):
    log_g = jnp.log1p(-jnp.exp2(-5.0 - jnp.arange(N_HEADS, dtype=F32)))
    i = jnp.arange(lc, dtype=F32)
    diff = i[:, None] - i[None, :]
    dec = jnp.where(diff >= 0, jnp.exp(jnp.maximum(diff, 0.0)[None] * log_g[:, None, None]), 0.0)
    qdec = jnp.exp((i[None, :] + 1.0) * log_g[:, None])
    kdec = jnp.exp((lc - 1.0 - i)[None, :] * log_g[:, None])
    gl = jnp.exp(lc * log_g)
    dec = dec.reshape(N_HEADS * lc, lc)
    qdec = jnp.broadcast_to(qdec[:, :, None], (N_HEADS, lc, D_HEAD)).reshape(N_HEADS * lc, D_HEAD)
    gl = jnp.broadcast_to(gl[:, None, None], (N_HEADS, D_HEAD, D_HEAD)).reshape(BRANCH_W, D_HEAD)
    kdec = jnp.repeat(kdec.T, D_HEAD, axis=1)
    return dec, qdec, kdec, gl


def _band_bias(rel_bias, tq, win, q_minus_k0, valid):
    length = tq + win - 1
    d = np.arange(length) - (tq - 1) - q_minus_k0
    idx = np.clip(d, -REL_CLIP, REL_CLIP) + REL_CLIP
    g = rel_bias[:, idx].astype(F32) * LOG2E
    gp = jnp.concatenate([g, jnp.zeros((N_HEADS, 1), F32)], axis=1)
    m = jnp.tile(gp, (1, tq))[:, :tq * length].reshape(N_HEADS, tq, length)
    tile = m[:, :, tq - 1:tq - 1 + win]
    return jnp.where(valid[None], tile, NEG_INF).reshape(N_HEADS * tq, win)


def _pack_w_in(w_in_l):
    cols = []
    src = 0
    for w in _IN_WIDTH:
        seg = w_in_l[:, src:src + w]
        cols.append(jnp.pad(seg, ((0, 0), (0, _round_up(w, LANE) - w))))
        src += w
    return jnp.concatenate(cols, axis=1).astype(BF16)


def _pad_rows(t, n):
    return jnp.pad(t, ((0, 0), (0, n - t.shape[1]), (0, 0)))


def _token_mixers(x, pos0, past, lw, *, prompt, layer, stacked):
    (w_in_p, qn, wuq, kvn, wukv, gn_g, gn_b, rel_bias, b, length) = lw
    assert pos0 % CHUNK == 0
    pos = pos0 + jnp.arange(length)
    lc = 256 if prompt else length
    dec, qdec, kdec, gl = _retention_tables(lc)
    rope32 = _rope_tables(pos, DR_A)
    lane = np.arange(LANE)
    rotary = (lane >= DN_A) & (lane < DQK_A)
    rope_q = [jnp.where(rotary[None, :], t, fill) for t, fill in zip(rope32, (1.0, 0.0, 0.0))]
    tables = (list(rope32) + list(_rope_tables(pos, D_HEAD)) + [jnp.tile(kdec, (length // lc, 1))] + rope_q)
    if not prompt:
        tables = [jnp.tile(t, (b, 1)) for t in tables]
    keep = min(PREV_CHUNKS * CHUNK, length)
    outs, stacked = _inproj(x, w_in_p, qn, wuq, kvn, wukv, tables, tm=TOKEN_TILE if prompt else b * length,
                            layer=layer, stacked=stacked, seq_len=length if prompt else None,
                            keep_rows=keep if prompt and keep < length else None)
    per_batch = lambda o: o.reshape(o.shape[:-2] + (b, length, o.shape[-1]))
    (q_a, ckv, kpe, kf, v_a, rq, rk, rkd, rv, rg, sq, sk, sv, bq, bk, bv) = [
        per_batch(o) for o in outs[:len(_INPROJ_OUT)]]
    here = lambda t: (t, layer)
    only = lambda t: (t[None], 0)

    if prompt:
        tq = 256
        qa_t, va_t, sq_t, sv_t, bq_t, bv_t = outs[len(_INPROJ_OUT):len(_INPROJ_OUT) + 6]
        nseq = SEQS_PER_STEP if b % SEQS_PER_STEP == 0 else 1
        o_a = _mla_attn_t(qa_t, kf, va_t, tq=tq, nseq=nseq)
        o_c = _sb_attn_t(sq_t, here(sk), sv_t, tq=tq, nseq=nseq)
        s0 = jnp.zeros((b, BRANCH_W, D_HEAD), F32)
        win = 3 * tq
        i = np.arange(tq)[:, None]
        c = np.arange(win)[None, :]
        variants = []
        for t in range(3):
            qc, kc = i // CHUNK + t * (tq // CHUNK), c // CHUNK
            variants.append(_band_bias(rel_bias, tq, win, t * tq, (kc <= qc) & (kc >= qc - PREV_CHUNKS)))
        bias_t = jnp.stack(variants).transpose(0, 2, 1)
        o_d = _band_attn_t(bq_t, here(bk), bv_t, bias_t, tq=tq, win=win, back=2,
                           nseq=2 * nseq if b % (2 * nseq) == 0 else nseq)
    else:
        c_ckv, c_kpe, s_prev, c_sk, c_sv, c_bk, c_bv = past
        n_past = c_ckv.shape[2]
        tko = LANE
        kf_c, v_c = _expand_latent(c_ckv.reshape(DEPTH, b * n_past, KV_RANK),
                                   c_kpe.reshape(DEPTH, b * n_past, DR_A), wukv, layer,
                                   math.gcd(b * n_past, CACHE_TILE))
        o_a = _mla_attn(q_a, only(_pad_rows(kf, tko)), only(_pad_rows(v_a, tko)),
                        only(kf_c.reshape(b, n_past, QA_W)), only(v_c.reshape(b, n_past, BRANCH_W)),
                        tq=length, tk=math.gcd(n_past, SAMPLE_KEY_TILE), n_own=length, causal_tiles=False)
        o_c = _sb_attn(sq, only(_pad_rows(sk[layer], tko)), only(_pad_rows(sv[layer], tko)),
                       (c_sk.reshape(DEPTH, b, n_past, BRANCH_W), layer),
                       (c_sv.reshape(DEPTH, b, n_past, BRANCH_W), layer),
                       tq=length, tk=256, n_own=length, causal_tiles=False)
        s0 = s_prev[layer].reshape(b, BRANCH_W, D_HEAD)
        n_band = c_bk.shape[2]
        n_keys = n_band + length
        win = _round_up(n_keys, LANE)
        bk_all = _pad_rows(jnp.concatenate([c_bk[layer].reshape(b, n_band, BRANCH_W), bk[layer]], axis=1), win)
        bv_all = _pad_rows(jnp.concatenate([c_bv[layer].reshape(b, n_band, BRANCH_W), bv[layer]], axis=1), win)
        k_pos = pos0 - n_band + np.arange(win)
        q_pos = pos0 + np.arange(length)
        qc, kc = q_pos[:, None] // CHUNK, k_pos[None, :] // CHUNK
        valid = (np.arange(win)[None, :] < n_keys) & (k_pos[None, :] >= 0) & (kc <= qc) & (kc >= qc - PREV_CHUNKS)
        bias = _band_bias(rel_bias, length, win, n_band, valid)[None]
        o_d = _band_attn(bq, only(bk_all), only(bv_all), bias, tq=length, win=win, back=0)

    o_r, s_ret = _retention(rq, rk, rkd, rv, rg, s0, dec, qdec, gl,
                            gn_g.reshape(N_HEADS, 1, D_HEAD), gn_b.reshape(N_HEADS, 1, D_HEAD), lc=lc,
                            nseq=SEQS_PER_STEP if b % SEQS_PER_STEP == 0 else 1)
    s_ret = s_ret.reshape(b, N_HEADS, D_HEAD, D_HEAD)
    flat = lambda t: t.reshape(b * length, BRANCH_W)
    return (flat(o_a), flat(o_r), flat(o_c), flat(o_d)), s_ret, stacked


def _state_outputs(stacked, s_ret, b, length):
    per_seq = lambda t: t.reshape(DEPTH, b, t.shape[1] // b, t.shape[-1])
    heads4 = lambda t: per_seq(t).reshape(DEPTH, b, t.shape[1] // b, N_HEADS, D_HEAD)
    ckv, kpe, sk, sv, bk, bv = stacked[:6]
    if len(stacked) > 6:
        bk, bv = stacked[6:]
    return (per_seq(ckv), per_seq(kpe), jnp.stack(s_ret, axis=0), heads4(sk), heads4(sv), heads4(bk), heads4(bv))


def kernel(x_prompt, x_sample, cache_mla_ckv, cache_mla_kpe, state_ret, cache_sb_k, cache_sb_v, cache_band_k, cache_band_v, w_in, mla_q_norm, mla_w_uq, mla_kv_norm, mla_w_ukv, ret_gn_g, ret_gn_b, band_rel_bias, w_branch, w_o, ln1_g, ln1_b, w_router, b_router, w_exp_gate, w_exp_up, w_exp_down, ln2_g, ln2_b):
    bp, lp, _ = x_prompt.shape
    bs, ls, _ = x_sample.shape
    past_len = cache_mla_ckv.shape[2]
    xp = x_prompt.reshape(bp * lp, D_MODEL)
    xs = x_sample.reshape(bs * ls, D_MODEL)

    wr = jnp.pad(w_router, ((0, 0), (0, LANE - N_EXPERTS)))
    wrh = wr.astype(BF16)
    wr2 = jnp.concatenate([wrh, (wr - wrh.astype(F32)).astype(BF16)], axis=1)
    br = jnp.pad(b_router, (0, LANE - N_EXPERTS)).reshape(LANE, 1)

    we_gate, we_up, we_down = w_exp_gate.astype(BF16), w_exp_up.astype(BF16), w_exp_down.astype(BF16)
    past = (cache_mla_ckv, cache_mla_kpe, state_ret, cache_sb_k, cache_sb_v, cache_band_k, cache_band_v)
    ret_p, ret_s = [], []
    stacked_p = stacked_s = None
    for l in range(DEPTH):
        wuq = mla_w_uq[l]
        wuq = jnp.pad(wuq, ((0, _round_up(Q_RANK, LANE) - Q_RANK), (0, 0), (0, LANE - DQK_A)))
        wuq = wuq.reshape(-1, QA_W).astype(BF16)
        wukv = mla_w_ukv[l]
        wukv = jnp.concatenate(
            [jnp.pad(wukv[:, :, :DN_A], ((0, 0), (0, 0), (0, LANE - DN_A))).reshape(KV_RANK, QA_W),
             wukv[:, :, DN_A:].reshape(KV_RANK, -1)], axis=1).astype(BF16)
        qn = jnp.pad(mla_q_norm[l], (0, _round_up(Q_RANK, LANE) - Q_RANK)).reshape(1, -1)
        kvn = mla_kv_norm[l].reshape(1, KV_RANK)
        w_in_p = _pack_w_in(w_in[l])
        wg = w_in[l][:, GATE_COL0:].astype(BF16)
        wb = w_branch[l].astype(BF16)
        wo = w_o[l].astype(BF16)
        g1, b1 = ln1_g[l].reshape(1, D_MODEL), ln1_b[l].reshape(1, D_MODEL)
        g2, b2 = ln2_g[l].reshape(1, D_MODEL), ln2_b[l].reshape(1, D_MODEL)
        lw =(w_in_p, qn, wuq, kvn, wukv, ret_gn_g[l], ret_gn_b[l], band_rel_bias[l])

        br_p, s_ret_p, stacked_p = _token_mixers(xp, 0, None, lw + (bp, lp), prompt=True,
                                                 layer=l, stacked=stacked_p)
        br_s, s_ret_s, stacked_s = _token_mixers(xs, past_len, past, lw + (bs, ls), prompt=False,
                                                 layer=l, stacked=stacked_s)
        xp = _merge(xp, br_p, wg, wb, wo, g1, b1, tm=TOKEN_TILE)
        xs = _merge(xs, br_s, wg, wb, wo, g1, b1, tm=TOKEN_TILE)
        xp = _moe(xp, wr2, br, we_gate, we_up, we_down, g2, b2, tm=TOKEN_TILE, layer=l)
        xs = _moe(xs, wr2, br, we_gate, we_up, we_down, g2, b2, tm=TOKEN_TILE, layer=l)
        ret_p.append(s_ret_p)
        ret_s.append(s_ret_s)

    return ((xp.reshape(bp, lp, D_MODEL), xs.reshape(bs, ls, D_MODEL))
            + _state_outputs(stacked_p, ret_p, bp, lp)
            + _state_outputs(stacked_s, ret_s, bs, ls))
```

```python
import functools
import math

import jax
import jax.numpy as jnp
import numpy as np
from jax import lax
from jax.experimental import pallas as pl
from jax.experimental.pallas import tpu as pltpu

D_MODEL = 1024
DEPTH = 2
CHUNK = 64
N_BRANCH = 4
BRANCH_W = D_MODEL // 4
N_HEADS = 4
D_HEAD = BRANCH_W // N_HEADS
DN_A = 64
DR_A = 32
DQK_A = DN_A + DR_A
Q_RANK = (3 * D_MODEL) // 16
KV_RANK = D_MODEL // 8
PREV_CHUNKS = 8
REL_CLIP = 128
ROPE_BASE = 10000.0
N_EXPERTS = 16
N_GROUPS = 4
EXPERTS_PER_GROUP = N_EXPERTS // N_GROUPS
TOP_K = 2
D_EXPERT = D_MODEL // 4
ALPHA = (2.0 * DEPTH) ** 0.25
EPS = 1e-5
NEG_INF = -1e30
LOG2E = 1.4426950408889634
SB_DEAD_LOG2 = -150.0

F32 = jnp.float32
BF16 = jnp.bfloat16

V7X_VMEM_LIMIT = 56 * 1024 * 1024
LANE = 128
TOKEN_TILE = 512
SEQS_PER_STEP = 2
CACHE_TILE = 2048
SAMPLE_KEY_TILE = 512

_IN_NAMES = ("c_q", "c_kv", "k_pe", "rq", "rk", "rv", "rg", "sq", "sk", "sv", "bq", "bk", "bv")
_IN_WIDTH = (Q_RANK, KV_RANK, DR_A) + (BRANCH_W,) * 10
QA_W = N_HEADS * LANE


def _round_up(n, m):
    return (n + m - 1) // m * m


_IN_OFF = {}
_off = 0
for _n, _w in zip(_IN_NAMES, _IN_WIDTH):
    _IN_OFF[_n] = (_off, _round_up(_w, LANE))
    _off += _round_up(_w, LANE)
IN_PACKED = _off
GATE_COL0 = sum(_IN_WIDTH)


def _params(sem):
    return pltpu.CompilerParams(dimension_semantics=sem, vmem_limit_bytes=V7X_VMEM_LIMIT)


def _nt_dot(a, b):
    return lax.dot_general(a, b, (((1,), (1,)), ((), ())), preferred_element_type=F32)


def _tn_dot(a, b):
    return lax.dot_general(a, b, (((0,), (0,)), ((), ())), preferred_element_type=F32)


def _dot(a, b):
    return jnp.dot(a, b, preferred_element_type=F32)


def _layer_norm(v, g, b):
    mu = jnp.mean(v, axis=-1, keepdims=True)
    d = v - mu
    var = jnp.mean(d * d, axis=-1, keepdims=True)
    return d * lax.rsqrt(var + EPS) * g + b


def _sigmoid(v):
    return 0.5 * jnp.tanh(0.5 * v) + 0.5


def _head_slice(h):
    return slice(h * D_HEAD, (h + 1) * D_HEAD)


def _stack_heads(q, lane_sets):
    lane = lax.broadcasted_iota(jnp.int32, q.shape, 1)
    zero = jnp.zeros_like(q)
    parts = []
    for h in range(N_HEADS):
        keep = None
        for lo, hi in lane_sets(h):
            m = jnp.logical_and(lane >= lo, lane < hi)
            keep = m if keep is None else jnp.logical_or(keep, m)
        parts.append(jnp.where(keep, q, zero))
    return jnp.concatenate(parts, axis=0)


def _own_lanes(h):
    return ((h * D_HEAD, (h + 1) * D_HEAD),)


def _mla_lanes(h):
    return ((h * LANE, (h + 1) * LANE),)


def _store_heads(o_ref, parts):
    for h, p in enumerate(parts):
        o_ref[:, _head_slice(h)] = p.astype(o_ref.dtype)


def _expand_kernel(ckv_ref, kpe_ref, w_ref, kf_ref, v_ref):
    kvx = _dot(ckv_ref[...].astype(BF16), w_ref[...])
    kp = kpe_ref[...].astype(BF16)
    tail = jnp.zeros((kp.shape[0], LANE - DN_A - DR_A), BF16)
    for h in range(N_HEADS):
        kf_ref[:, h * LANE:(h + 1) * LANE] = jnp.concatenate(
            [kvx[:, h * LANE:h * LANE + DN_A].astype(BF16), kp, tail], axis=1)
    v_ref[...] = kvx[:, QA_W:].astype(BF16)


def _expand_latent(ckv, kpe, w_ukv, layer, tm):
    m = ckv.shape[1]
    assert m % tm == 0
    return pl.pallas_call(
        _expand_kernel,
        grid=(m // tm,),
        in_specs=[pl.BlockSpec((None, tm, KV_RANK), lambda i: (layer, i, 0)),
                  pl.BlockSpec((None, tm, DR_A), lambda i: (layer, i, 0)),
                  pl.BlockSpec(w_ukv.shape, lambda i: (0, 0))],
        out_specs=[pl.BlockSpec((tm, QA_W), lambda i: (i, 0)),
                   pl.BlockSpec((tm, BRANCH_W), lambda i: (i, 0))],
        out_shape=[jax.ShapeDtypeStruct((m, QA_W), BF16),
                   jax.ShapeDtypeStruct((m, BRANCH_W), BF16)],
        compiler_params=_params(("parallel",)),
        name="expand_latent",
    )(ckv, kpe, w_ukv)


def _rope_block(x, cos, sin_a, sin_b, half):
    return x * cos + pltpu.roll(x, LANE - half, 1) * sin_a + pltpu.roll(x, half, 1) * sin_b


def _put_state(ref, val):
    if len(ref.shape) == 3:
        for l in range(ref.shape[0]):
            ref[l] = val
    else:
        ref[...] = val


def _inproj_kernel(x_ref, w_ref, qn_ref, wuq_ref, kvn_ref, wukv_ref,
                   c32_ref, a32_ref, b32_ref, c64_ref, a64_ref, b64_ref, kdec_ref, cqa_ref, aqa_ref, bqa_ref,
                   qa_ref, ckv_ref, kpe_ref, kf_ref, va_ref,
                   rq_ref, rk_ref, rkd_ref, rv_ref, rg_ref,
                   sq_ref, sk_ref, sv_ref, bq_ref, bk_ref, bv_ref, t_refs=(), keep_refs=()):
    qat_ref, vat_ref, sqt_ref, svt_ref, bqt_ref, bvt_ref = t_refs if t_refs else (None,) * 6
    z = _dot(x_ref[...].astype(BF16), w_ref[...])

    def seg(name):
        o, w = _IN_OFF[name]
        return z[:, o:o + w]

    cq = seg("c_q")
    cqn = cq * lax.rsqrt(jnp.sum(cq * cq, axis=1, keepdims=True) * (1.0 / Q_RANK) + EPS) * qn_ref[...]
    qa = _dot(cqn.astype(BF16), wuq_ref[...])
    scale_a = DQK_A ** -0.5 * LOG2E
    qa_s = jnp.concatenate(
        [_rope_block(qa[:, h * LANE:(h + 1) * LANE], cqa_ref[...], aqa_ref[...], bqa_ref[...], DR_A // 2)
         for h in range(N_HEADS)], axis=1) * scale_a
    qa_ref[...] = qa_s.astype(BF16)
    if qat_ref is not None:
        qat_ref[...] = qa_s.T.astype(BF16)

    ckv_raw = seg("c_kv")
    ckv = ckv_raw * lax.rsqrt(jnp.mean(ckv_raw * ckv_raw, axis=1, keepdims=True) + EPS) * kvn_ref[...]
    _put_state(ckv_ref, ckv)
    kvx = _dot(ckv.astype(BF16), wukv_ref[...])
    kp = _rope_block(seg("k_pe"), c32_ref[...], a32_ref[...], b32_ref[...], DR_A // 2)
    _put_state(kpe_ref, kp[:, :DR_A])
    kp_at = pltpu.roll(kp, DN_A, 1)
    for h in range(N_HEADS):
        kf_ref[:, h * LANE:(h + 1) * LANE] = (kvx[:, h * LANE:(h + 1) * LANE] + kp_at).astype(BF16)
    va_ref[...] = kvx[:, QA_W:].astype(BF16)
    if vat_ref is not None:
        vat_ref[...] = kvx[:, QA_W:].T.astype(BF16)

    rq, rk = seg("rq"), seg("rk")
    kdec = kdec_ref[...]
    for blk in range(BRANCH_W // LANE):
        cols = slice(blk * LANE, (blk + 1) * LANE)
        rq_ref[:, cols] = _rope_block(rq[:, cols], c64_ref[...], a64_ref[...], b64_ref[...],
                                      D_HEAD // 2).astype(BF16)
        rkb = _rope_block(rk[:, cols], c64_ref[...], a64_ref[...], b64_ref[...], D_HEAD // 2) * (D_HEAD ** -0.5)
        rk_ref[:, cols] = rkb.astype(BF16)
        rkd_ref[:, cols] = (rkb * kdec[:, cols]).astype(BF16)
    rv_ref[...] = seg("rv").astype(BF16)
    rg_ref[...] = seg("rg")

    scale_h = D_HEAD ** -0.5 * LOG2E
    sq = seg("sq") * scale_h
    sq_ref[...] = sq.astype(BF16)
    _put_state(sk_ref, seg("sk"))
    _put_state(sv_ref, seg("sv"))
    if sqt_ref is not None:
        sqt_ref[...] = sq.T.astype(BF16)
        svt_ref[...] = seg("sv").T.astype(BF16)
    bq = seg("bq") * scale_h
    bq_ref[...] = bq.astype(BF16)
    _put_state(bk_ref, seg("bk"))
    _put_state(bv_ref, seg("bv"))
    if bqt_ref is not None:
        bqt_ref[...] = bq.T.astype(BF16)
        bvt_ref[...] = seg("bv").T.astype(BF16)
    if keep_refs:
        _put_state(keep_refs[0], seg("bk"))
        _put_state(keep_refs[1], seg("bv"))


_INPROJ_OUT = (
    (QA_W, BF16), (KV_RANK, F32), (DR_A, F32), (QA_W, BF16), (BRANCH_W, BF16),
    (BRANCH_W, BF16), (BRANCH_W, BF16), (BRANCH_W, BF16), (BRANCH_W, BF16), (BRANCH_W, F32),
    (BRANCH_W, BF16), (BRANCH_W, F32), (BRANCH_W, F32), (BRANCH_W, BF16), (BRANCH_W, F32), (BRANCH_W, F32))


_STATE_OUTS = (1, 2, 11, 12, 14, 15)


def _inproj_entry(*refs, n_in, n_alias, n_t, n_keep):
    outs = refs[n_in + n_alias:]
    n_main = len(_INPROJ_OUT)
    _inproj_kernel(*refs[:n_in], *outs[:n_main], t_refs=outs[n_main:n_main + n_t],
                   keep_refs=outs[n_main + n_t:n_main + n_t + n_keep])


def _inproj(x, w_in_p, qn, wuq, kvn, wukv, tables, *, tm, layer, stacked, seq_len=None, keep_rows=None):
    m = x.shape[0]
    tm = min(tm, m)
    assert m % tm == 0
    n_pos_tiles = tables[0].shape[0] // tm
    assert tables[0].shape[0] % tm == 0
    row = lambda i: (i, 0)
    const = lambda i: (0, 0)
    pos = lambda i: (i % n_pos_tiles, 0)
    in_specs = [pl.BlockSpec((tm, D_MODEL), row),
                pl.BlockSpec(w_in_p.shape, const),
                pl.BlockSpec(qn.shape, const),
                pl.BlockSpec(wuq.shape, const),
                pl.BlockSpec(kvn.shape, const),
                pl.BlockSpec(wukv.shape, const)]
    in_specs += [pl.BlockSpec((tm, t.shape[1]), pos) for t in tables]
    n_in = len(in_specs)
    out_specs, out_shape = [], []
    assert stacked is not None or layer == 0

    def state_spec(w, rows_of):
        if stacked is None:
            return pl.BlockSpec((DEPTH, tm, w), lambda i: (0, rows_of(i), 0))
        return pl.BlockSpec((None, tm, w), lambda i: (layer, rows_of(i), 0))

    for k, (w, dt) in enumerate(_INPROJ_OUT):
        if k in _STATE_OUTS:
            out_specs.append(state_spec(w, lambda i: i))
            out_shape.append(jax.ShapeDtypeStruct((DEPTH, m, w), dt))
        else:
            out_specs.append(pl.BlockSpec((tm, w), row))
            out_shape.append(jax.ShapeDtypeStruct((m, w), dt))
    if seq_len is not None:
        assert seq_len % tm == 0 and m % seq_len == 0
        tps = seq_len // tm
        for w in (QA_W,) + (BRANCH_W,) * 5:
            out_specs.append(pl.BlockSpec((None, w, tm), lambda i: (i // tps, 0, i % tps)))
            out_shape.append(jax.ShapeDtypeStruct((m // seq_len, w, seq_len), BF16))
    n_t = len(out_specs) - len(_INPROJ_OUT)
    state_outs = list(_STATE_OUTS)
    if keep_rows is not None:
        assert seq_len is not None and keep_rows % tm == 0 and keep_rows <= seq_len
        tps, ktiles = seq_len // tm, keep_rows // tm
        kept = lambda i: (i // tps) * ktiles + jnp.maximum(i % tps - (tps - ktiles), 0)
        for _ in range(2):
            state_outs.append(len(out_specs))
            out_specs.append(state_spec(BRANCH_W, kept))
            out_shape.append(jax.ShapeDtypeStruct((DEPTH, m // seq_len * keep_rows, BRANCH_W), F32))
    aliases = {}
    extra = ()
    if stacked is not None:
        extra = tuple(stacked)
        assert len(extra) == len(state_outs)
        in_specs += [pl.BlockSpec(memory_space=pl.ANY)] * len(extra)
        aliases = {n_in + k: o for k, o in enumerate(state_outs)}
    outs = pl.pallas_call(
        functools.partial(_inproj_entry, n_in=n_in, n_alias=len(extra), n_t=n_t,
                          n_keep=len(state_outs) - len(_STATE_OUTS)),
        grid=(m // tm,),
        in_specs=in_specs,
        out_specs=out_specs,
        out_shape=out_shape,
        input_output_aliases=aliases,
        compiler_params=_params(("arbitrary",)),
        name="inproj",
    )(x, w_in_p, qn, wuq, kvn, wukv, *tables, *extra)
    return outs, tuple(outs[k] for k in state_outs)


def _tile_spec(src, rows):
    arr, layer = src
    return pl.BlockSpec((None, None, rows, arr.shape[3]), lambda i, t: (layer, i, t, 0))


def _full_spec(src):
    arr, layer = src
    return pl.BlockSpec((None, None, arr.shape[2], arr.shape[3]), lambda i, t: (layer, i, 0, 0))


def _mla_kernel(q_ref, kfo_ref, vo_ref, kfp_ref, vp_ref, o_ref, *, tq, tk, n_own, n_past):
    qi = pl.program_id(1)
    qm = _stack_heads(q_ref[...], _mla_lanes)
    rows = N_HEADS * tq
    tko = kfo_ref.shape[0]
    n_loop = qi if n_past is None else n_past

    def softmax_pv(s, v, carry):
        m, accs = carry
        m_new = jnp.maximum(m, jnp.max(s, axis=1, keepdims=True))
        pb = jnp.exp2(s - m_new).astype(BF16)
        a = jnp.exp2(m - m_new)
        ones = jnp.ones((v.shape[0], D_HEAD), BF16)
        accs = tuple(a[h * tq:(h + 1) * tq] * accs[h]
                     + _dot(pb[h * tq:(h + 1) * tq], jnp.concatenate([v[:, _head_slice(h)], ones], axis=1))
                     for h in range(N_HEADS))
        return m_new, accs

    def past_scores(j):
        start = pl.multiple_of(jnp.minimum(j, jnp.maximum(n_loop - 1, 0)) * tk, tk)
        return _nt_dot(qm, kfp_ref[pl.ds(start, tk), :])

    row_q = lax.broadcasted_iota(jnp.int32, (rows, tko), 0) % tq
    col = lax.broadcasted_iota(jnp.int32, (rows, tko), 1)
    own_mask = jnp.logical_and(col < n_own, col // CHUNK <= row_q // CHUNK)
    carry = (jnp.full((rows, 1), NEG_INF, F32),
             tuple(jnp.zeros((tq, 2 * D_HEAD), F32) for _ in range(N_HEADS)))
    s_next = past_scores(0)
    carry = softmax_pv(jnp.where(own_mask, _nt_dot(qm, kfo_ref[...]), NEG_INF), vo_ref[...], carry)

    def body(j, c):
        s_cur, rest = c
        s_after = past_scores(j + 1)
        start = pl.multiple_of(j * tk, tk)
        return s_after, softmax_pv(s_cur, vp_ref[pl.ds(start, tk), :], rest)

    _, (_, accs) = lax.fori_loop(0, n_loop, body, (s_next, carry))
    _store_heads(o_ref, [accs[h][:, :D_HEAD] / accs[h][:, D_HEAD:D_HEAD + 1] for h in range(N_HEADS)])


def _mla_attn(q, kf_own, v_own, kf_past, v_past, *, tq, tk, n_own, causal_tiles):
    b, lq, _ = q.shape
    nqt = lq // tq
    tko = kf_own[0].shape[2] // nqt
    lp = kf_past[0].shape[2]
    assert lq % tq == 0 and lp % tk == 0
    return pl.pallas_call(
        functools.partial(_mla_kernel, tq=tq, tk=tk, n_own=n_own,
                          n_past=None if causal_tiles else lp // tk),
        grid=(b, nqt),
        in_specs=[pl.BlockSpec((None, tq, QA_W), lambda i, t: (i, t, 0)),
                  _tile_spec(kf_own, tko), _tile_spec(v_own, tko),
                  _full_spec(kf_past), _full_spec(v_past)],
        out_specs=pl.BlockSpec((None, tq, BRANCH_W), lambda i, t: (i, t, 0)),
        out_shape=jax.ShapeDtypeStruct((b, lq, BRANCH_W), BF16),
        compiler_params=_params(("parallel", "arbitrary")),
        name="mla_attn",
    )(q, kf_own[0], v_own[0], kf_past[0], v_past[0])


def _mla_kernel_t(qt_ref, kfo_ref, vto_ref, kfp_ref, vtp_ref, o_ref, *, tq, tk, nseq):
    qi = pl.program_id(1)
    cols = N_HEADS * tq

    def scores(s, kf):
        qt = qt_ref[s]
        return jnp.concatenate([_dot(kf[:, h * LANE:(h + 1) * LANE], qt[h * LANE:(h + 1) * LANE, :])
                                for h in range(N_HEADS)], axis=1)

    def softmax_pv(st, vt, carry):
        m, l, accs = carry
        m_new = jnp.maximum(m, jnp.max(st, axis=0, keepdims=True))
        pt = jnp.exp2(st - m_new)
        a = jnp.exp2(m - m_new)
        l = a * l + jnp.sum(pt, axis=0, keepdims=True)
        pb = pt.astype(BF16)
        accs = tuple(a[:, h * tq:(h + 1) * tq] * accs[h]
                     + _dot(vt[_head_slice(h), :], pb[:, h * tq:(h + 1) * tq]) for h in range(N_HEADS))
        return m_new, l, accs

    def past_scores(s, j):
        start = pl.multiple_of(jnp.minimum(j, jnp.maximum(qi - 1, 0)) * tk, tk)
        return scores(s, kfp_ref[s, pl.ds(start, tk), :])

    key = lax.broadcasted_iota(jnp.int32, (tq, cols), 0)
    qry = lax.broadcasted_iota(jnp.int32, (tq, cols), 1) % tq
    own_mask = key // CHUNK <= qry // CHUNK
    s_next, carries = [], []
    for s in range(nseq):
        empty = (jnp.full((1, cols), NEG_INF, F32), jnp.zeros((1, cols), F32),
                 tuple(jnp.zeros((D_HEAD, tq), F32) for _ in range(N_HEADS)))
        s_next.append(past_scores(s, 0))
        carries.append(softmax_pv(jnp.where(own_mask, scores(s, kfo_ref[s]), NEG_INF), vto_ref[s], empty))

    def body(j, c):
        s_cur, rest = c
        start = pl.multiple_of(j * tk, tk)
        s_after = tuple(past_scores(s, j + 1) for s in range(nseq))
        return s_after, tuple(softmax_pv(s_cur[s], vtp_ref[s, :, pl.ds(start, tk)], rest[s])
                              for s in range(nseq))

    _, done = lax.fori_loop(0, qi, body, (tuple(s_next), tuple(carries)))
    for s in range(nseq):
        _, l, accs = done[s]
        out_t = jnp.concatenate([accs[h] / l[:, h * tq:(h + 1) * tq] for h in range(N_HEADS)], axis=0)
        o_ref[s] = out_t.T.astype(o_ref.dtype)


def _mla_attn_t(qt, kf, vt, *, tq, nseq):
    b, _, length = qt.shape
    assert length % tq == 0 and b % nseq == 0
    return pl.pallas_call(
        functools.partial(_mla_kernel_t, tq=tq, tk=tq, nseq=nseq),
        grid=(b // nseq, length // tq),
        in_specs=[pl.BlockSpec((nseq, QA_W, tq), lambda i, t: (i, 0, t)),
                  pl.BlockSpec((nseq, tq, QA_W), lambda i, t: (i, t, 0)),
                  pl.BlockSpec((nseq, BRANCH_W, tq), lambda i, t: (i, 0, t)),
                  pl.BlockSpec((nseq, length, QA_W), lambda i, t: (i, 0, 0)),
                  pl.BlockSpec((nseq, BRANCH_W, length), lambda i, t: (i, 0, 0))],
        out_specs=pl.BlockSpec((nseq, tq, BRANCH_W), lambda i, t: (i, t, 0)),
        out_shape=jax.ShapeDtypeStruct((b, length, BRANCH_W), BF16),
        compiler_params=_params(("parallel", "arbitrary")),
        name="mla_attn_t",
    )(qt, kf, vt, kf, vt)


def _sb_kernel(q_ref, ko_ref, vo_ref, kp_ref, vp_ref, o_ref, *, tq, tk, n_own, n_past):
    qi = pl.program_id(1)
    qm = _stack_heads(q_ref[...], _own_lanes)
    rows = N_HEADS * tq
    tko = ko_ref.shape[0]
    n_loop = qi if n_past is None else n_past

    def tri2(n):
        r = lax.broadcasted_iota(jnp.int32, (2 * n, n), 0) % n
        c = lax.broadcasted_iota(jnp.int32, (2 * n, n), 1)
        return jnp.where(r > c, 1.0, 0.0).astype(BF16)

    def weigh(z, v, carry, mask, tri):
        run, accs = carry
        neg_abs = lax.bitcast_convert_type(
            lax.bitcast_convert_type(z, jnp.uint32) | jnp.uint32(0x80000000), F32)
        t = jnp.log2(1.0 + jnp.exp2(neg_abs))
        log_beta = jnp.minimum(z, 0.0) - t
        log_stay = log_beta - z
        if mask is not None:
            log_stay = jnp.where(mask, log_stay, 0.0)
        hi = log_stay.astype(BF16)
        lo = (log_stay - hi.astype(F32)).astype(BF16)
        later = _dot(jnp.concatenate([hi, lo], axis=1), tri) + run
        w = jnp.exp2(log_beta + later)
        if mask is not None:
            w = jnp.where(mask, w, 0.0)
        wb = w.astype(BF16)
        vb = v.astype(BF16)
        accs = tuple(accs[h] + _dot(wb[h * tq:(h + 1) * tq], vb[:, _head_slice(h)])
                     for h in range(N_HEADS))
        return run + jnp.sum(log_stay, axis=1, keepdims=True), accs

    row_q = lax.broadcasted_iota(jnp.int32, (rows, tko), 0) % tq
    col = lax.broadcasted_iota(jnp.int32, (rows, tko), 1)
    own_mask = jnp.logical_and(col < n_own, col < row_q)

    def past_start(jj):
        return pl.multiple_of(jnp.clip(n_loop - 1 - jj, 0, kp_ref.shape[0] // tk - 1) * tk, tk)

    def past_scores(jj):
        return _nt_dot(qm, kp_ref[pl.ds(past_start(jj), tk), :].astype(BF16))

    def alive(run):
        return (jnp.max(run) > SB_DEAD_LOG2).astype(jnp.int32)

    z_next = past_scores(0)
    carry = (jnp.zeros((rows, 1), F32), tuple(jnp.zeros((tq, D_HEAD), F32) for _ in range(N_HEADS)))
    run, accs = weigh(_nt_dot(qm, ko_ref[...].astype(BF16)), vo_ref[...], carry, own_mask, tri2(tko))
    tri_past = tri2(tk)

    def cond(c):
        return jnp.logical_and(c[0] < n_loop, c[1] > 0)

    def body(c):
        jj, _, z_cur, run, accs = c
        z_after = past_scores(jj + 1)
        run, accs = weigh(z_cur, vp_ref[pl.ds(past_start(jj), tk), :], (run, accs), None, tri_past)
        return jj + 1, alive(run), z_after, run, accs

    out = lax.while_loop(cond, body, (jnp.int32(0), alive(run), z_next, run, accs))
    _store_heads(o_ref, out[4])


def _sb_attn(q, k_own, v_own, k_past, v_past, *, tq, tk, n_own, causal_tiles):
    b, lq, _ = q.shape
    nqt = lq // tq
    tko = k_own[0].shape[2] // nqt
    lp = k_past[0].shape[2]
    assert lq % tq == 0 and lp % tk == 0
    return pl.pallas_call(
        functools.partial(_sb_kernel, tq=tq, tk=tk, n_own=n_own,
                          n_past=None if causal_tiles else lp // tk),
        grid=(b, nqt),
        in_specs=[pl.BlockSpec((None, tq, BRANCH_W), lambda i, t: (i, t, 0)),
                  _tile_spec(k_own, tko), _tile_spec(v_own, tko),
                  _full_spec(k_past), _full_spec(v_past)],
        out_specs=pl.BlockSpec((None, tq, BRANCH_W), lambda i, t: (i, t, 0)),
        out_shape=jax.ShapeDtypeStruct((b, lq, BRANCH_W), BF16),
        compiler_params=_params(("parallel", "arbitrary")),
        name="sb_attn",
    )(q, k_own[0], v_own[0], k_past[0], v_past[0])


def _sb_kernel_t(qt_ref, ko_ref, vto_ref, kp_ref, vtp_ref, o_ref, *, tq, tk, nseq):
    qi = pl.program_id(1)
    qmts = []
    for s in range(nseq):
        qt = qt_ref[s]
        feat = lax.broadcasted_iota(jnp.int32, qt.shape, 0)
        qmts.append(jnp.concatenate(
            [jnp.where(jnp.logical_and(feat >= h * D_HEAD, feat < (h + 1) * D_HEAD), qt, jnp.zeros_like(qt))
             for h in range(N_HEADS)], axis=1))
    cols = N_HEADS * tq

    def tri2(n):
        r = lax.broadcasted_iota(jnp.int32, (n, 2 * n), 0)
        c = lax.broadcasted_iota(jnp.int32, (n, 2 * n), 1) % n
        return jnp.where(c > r, 1.0, 0.0).astype(BF16)

    def weigh(zt, vt, carry, mask, tri):
        run, accs = carry
        neg_abs = lax.bitcast_convert_type(
            lax.bitcast_convert_type(zt, jnp.uint32) | jnp.uint32(0x80000000), F32)
        t = jnp.log2(1.0 + jnp.exp2(neg_abs))
        log_beta = jnp.minimum(zt, 0.0) - t
        log_stay = log_beta - zt
        if mask is not None:
            log_stay = jnp.where(mask, log_stay, 0.0)
        hi = log_stay.astype(BF16)
        lo = (log_stay - hi.astype(F32)).astype(BF16)
        later = _dot(tri, jnp.concatenate([hi, lo], axis=0)) + run
        w = jnp.exp2(log_beta + later)
        if mask is not None:
            w = jnp.where(mask, w, 0.0)
        wb = w.astype(BF16)
        accs = tuple(accs[h] + _dot(vt[_head_slice(h), :], wb[:, h * tq:(h + 1) * tq])
                     for h in range(N_HEADS))
        return run + jnp.sum(log_stay, axis=0, keepdims=True), accs

    def past_start(jj):
        return pl.multiple_of(jnp.clip(qi - 1 - jj, 0, kp_ref.shape[1] // tk - 1) * tk, tk)

    def past_scores(s, jj):
        return _dot(kp_ref[s, pl.ds(past_start(jj), tk), :].astype(BF16), qmts[s])

    def alive(runs):
        top = jnp.max(runs[0])
        for r in runs[1:]:
            top = jnp.maximum(top, jnp.max(r))
        return (top > SB_DEAD_LOG2).astype(jnp.int32)

    key = lax.broadcasted_iota(jnp.int32, (tq, cols), 0)
    qry = lax.broadcasted_iota(jnp.int32, (tq, cols), 1) % tq
    own_mask = key < qry
    tri_own = tri2(tq)
    z_next, runs, accss = [], [], []
    for s in range(nseq):
        z_next.append(past_scores(s, 0))
        empty = (jnp.zeros((1, cols), F32), tuple(jnp.zeros((D_HEAD, tq), F32) for _ in range(N_HEADS)))
        run, accs = weigh(_dot(ko_ref[s].astype(BF16), qmts[s]), vto_ref[s], empty, own_mask, tri_own)
        runs.append(run)
        accss.append(accs)
    tri_past = tri_own if tk == tq else tri2(tk)

    def cond(c):
        return jnp.logical_and(c[0] < qi, c[1] > 0)

    def body(c):
        jj, _, z_cur, runs, accss = c
        z_after = tuple(past_scores(s, jj + 1) for s in range(nseq))
        new = [weigh(z_cur[s], vtp_ref[s, :, pl.ds(past_start(jj), tk)], (runs[s], accss[s]), None, tri_past)
               for s in range(nseq)]
        runs = tuple(n[0] for n in new)
        return jj + 1, alive(runs), z_after, runs, tuple(n[1] for n in new)

    out = lax.while_loop(cond, body, (jnp.int32(0), alive(runs), tuple(z_next), tuple(runs), tuple(accss)))
    for s in range(nseq):
        o_ref[s] = jnp.concatenate(out[4][s], axis=0).T.astype(o_ref.dtype)


def _sb_attn_t(qt, k, vt, *, tq, nseq):
    b, _, length = qt.shape
    k_arr, layer = k
    assert length % tq == 0 and b % nseq == 0
    return pl.pallas_call(
        functools.partial(_sb_kernel_t, tq=tq, tk=tq, nseq=nseq),
        grid=(b // nseq, length // tq),
        in_specs=[pl.BlockSpec((nseq, BRANCH_W, tq), lambda i, t: (i, 0, t)),
                  pl.BlockSpec((None, nseq, tq, BRANCH_W), lambda i, t: (layer, i, t, 0)),
                  pl.BlockSpec((nseq, BRANCH_W, tq), lambda i, t: (i, 0, t)),
                  pl.BlockSpec((None, nseq, length, BRANCH_W), lambda i, t: (layer, i, 0, 0)),
                  pl.BlockSpec((nseq, BRANCH_W, length), lambda i, t: (i, 0, 0))],
        out_specs=pl.BlockSpec((nseq, tq, BRANCH_W), lambda i, t: (i, t, 0)),
        out_shape=jax.ShapeDtypeStruct((b, length, BRANCH_W), BF16),
        compiler_params=_params(("parallel", "arbitrary")),
        name="sb_attn_t",
    )(qt, k_arr, vt, k_arr, vt)


def _ret_kernel(q_ref, k_ref, kd_ref, v_ref, rg_ref, s0_ref, dec_ref, qdec_ref, gl_ref,
                gng_ref, gnb_ref, o_ref, sout_ref, state_ref, *, lc, nseq):
    c = pl.program_id(1)

    @pl.when(c == 0)
    def _():
        state_ref[...] = s0_ref[...]

    for s in range(nseq):
        qm = _stack_heads(q_ref[s], _own_lanes)
        v = v_ref[s]
        state = state_ref[s]
        scores = (_nt_dot(qm, k_ref[s]) * dec_ref[...]).astype(BF16)
        cross = _dot(qm, state.astype(BF16)) * qdec_ref[...]
        kv_full = _tn_dot(kd_ref[s], v)
        new_state = gl_ref[...] * state + jnp.concatenate(
            [kv_full[_head_slice(h), _head_slice(h)] for h in range(N_HEADS)], axis=0)
        state_ref[s] = new_state

        rg = rg_ref[s]
        for h in range(N_HEADS):
            o = _dot(scores[h * lc:(h + 1) * lc], v[:, _head_slice(h)]) + cross[h * lc:(h + 1) * lc]
            mu = jnp.mean(o, axis=-1, keepdims=True)
            d = o - mu
            var = jnp.mean(d * d, axis=-1, keepdims=True)
            y = d * lax.rsqrt(var + EPS) * gng_ref[h] + gnb_ref[h]
            g = rg[:, _head_slice(h)]
            o_ref[s, :, _head_slice(h)] = (y * (g * _sigmoid(g))).astype(o_ref.dtype)

    @pl.when(c == pl.num_programs(1) - 1)
    def _():
        sout_ref[...] = state_ref[...]


def _retention(q, k, kd, v, rg, s0, dec, qdec, gl, gng, gnb, *, lc, nseq):
    b, length, _ = q.shape
    assert length % lc == 0 and b % nseq == 0
    seq = lambda i, t: (i, t, 0)
    st = lambda i, t: (i, 0, 0)
    c2 = lambda i, t: (0, 0)
    c3 = lambda i, t: (0, 0, 0)
    return pl.pallas_call(
        functools.partial(_ret_kernel, lc=lc, nseq=nseq),
        grid=(b // nseq, length // lc),
        in_specs=[pl.BlockSpec((nseq, lc, BRANCH_W), seq)] * 5
        + [pl.BlockSpec((nseq, BRANCH_W, D_HEAD), st),
           pl.BlockSpec(dec.shape, c2), pl.BlockSpec(qdec.shape, c2), pl.BlockSpec(gl.shape, c2),
           pl.BlockSpec(gng.shape, c3), pl.BlockSpec(gnb.shape, c3)],
        out_specs=[pl.BlockSpec((nseq, lc, BRANCH_W), seq),
                   pl.BlockSpec((nseq, BRANCH_W, D_HEAD), st)],
        out_shape=[jax.ShapeDtypeStruct((b, length, BRANCH_W), BF16),
                   jax.ShapeDtypeStruct((b, BRANCH_W, D_HEAD), F32)],
        scratch_shapes=[pltpu.VMEM((nseq, BRANCH_W, D_HEAD), F32)],
        compiler_params=_params(("parallel", "arbitrary")),
        name="retention",
    )(q, k, kd, v, rg, s0, dec, qdec, gl, gng, gnb)


def _band_kernel(q_ref, k_ref, v_ref, bias_ref, o_ref, *, tq, win, back):
    qi = pl.program_id(1)
    start = pl.multiple_of(jnp.maximum(qi - back, 0) * tq, tq)
    k = k_ref[pl.ds(start, win), :].astype(BF16)
    v = v_ref[pl.ds(start, win), :].astype(BF16)
    qm = _stack_heads(q_ref[...], _own_lanes)
    s = _nt_dot(qm, k) + bias_ref[...]
    pb = jnp.exp2(s - jnp.max(s, axis=1, keepdims=True)).astype(BF16)
    ones = jnp.ones((win, D_HEAD), BF16)
    outs = []
    for h in range(N_HEADS):
        o = _dot(pb[h * tq:(h + 1) * tq], jnp.concatenate([v[:, _head_slice(h)], ones], axis=1))
        outs.append(o[:, :D_HEAD] / o[:, D_HEAD:D_HEAD + 1])
    _store_heads(o_ref, outs)


def _band_attn(q, k, v, bias, *, tq, win, back):
    b, lq, _ = q.shape
    nvar = bias.shape[0]
    assert lq % tq == 0
    return pl.pallas_call(
        functools.partial(_band_kernel, tq=tq, win=win, back=back),
        grid=(b, lq // tq),
        in_specs=[pl.BlockSpec((None, tq, BRANCH_W), lambda i, t: (i, t, 0)),
                  _full_spec(k), _full_spec(v),
                  pl.BlockSpec((None, N_HEADS * tq, win), lambda i, t: (jnp.minimum(t, nvar - 1), 0, 0))],
        out_specs=pl.BlockSpec((None, tq, BRANCH_W), lambda i, t: (i, t, 0)),
        out_shape=jax.ShapeDtypeStruct((b, lq, BRANCH_W), BF16),
        compiler_params=_params(("parallel", "arbitrary")),
        name="band_attn",
    )(q, k[0], v[0], bias)


def _band_kernel_t(qt_ref, k_ref, vt_ref, bias_ref, o_ref, *, tq, win, back, nseq):
    qi = pl.program_id(1)
    start = pl.multiple_of(jnp.maximum(qi - back, 0) * tq, tq)
    for s in range(nseq):
        k = k_ref[s, pl.ds(start, win), :].astype(BF16)
        vt = vt_ref[s, :, pl.ds(start, win)]
        qt = qt_ref[s]
        feat = lax.broadcasted_iota(jnp.int32, qt.shape, 0)
        qmt = jnp.concatenate(
            [jnp.where(jnp.logical_and(feat >= h * D_HEAD, feat < (h + 1) * D_HEAD), qt, jnp.zeros_like(qt))
             for h in range(N_HEADS)], axis=1)
        st = _dot(k, qmt) + bias_ref[...]
        pt = jnp.exp2(st - jnp.max(st, axis=0, keepdims=True))
        l = jnp.sum(pt, axis=0, keepdims=True)
        pb = pt.astype(BF16)
        out_t = jnp.concatenate(
            [_dot(vt[_head_slice(h), :], pb[:, h * tq:(h + 1) * tq]) / l[:, h * tq:(h + 1) * tq]
             for h in range(N_HEADS)], axis=0)
        o_ref[s] = out_t.T.astype(o_ref.dtype)


def _band_attn_t(qt, k, vt, bias_t, *, tq, win, back, nseq):
    b, _, length = qt.shape
    nvar = bias_t.shape[0]
    k_arr, layer = k
    assert length % tq == 0 and b % nseq == 0
    return pl.pallas_call(
        functools.partial(_band_kernel_t, tq=tq, win=win, back=back, nseq=nseq),
        grid=(b // nseq, length // tq),
        in_specs=[pl.BlockSpec((nseq, BRANCH_W, tq), lambda i, t: (i, 0, t)),
                  pl.BlockSpec((None, nseq, k_arr.shape[2], BRANCH_W), lambda i, t: (layer, i, 0, 0)),
                  pl.BlockSpec((nseq, BRANCH_W, length), lambda i, t: (i, 0, 0)),
                  pl.BlockSpec((None, win, N_HEADS * tq), lambda i, t: (jnp.minimum(t, nvar - 1), 0, 0))],
        out_specs=pl.BlockSpec((nseq, tq, BRANCH_W), lambda i, t: (i, t, 0)),
        out_shape=jax.ShapeDtypeStruct((b, length, BRANCH_W), BF16),
        compiler_params=_params(("parallel", "arbitrary")),
        name="band_attn_t",
    )(qt, k_arr, vt, bias_t)


def _merge_kernel(x_ref, ba_ref, bb_ref, bc_ref, bd_ref, wg_ref, wb_ref, wo_ref, g_ref, b_ref, o_ref):
    x = x_ref[...]
    xb = x.astype(BF16)
    merged = None
    for n, br_ref in enumerate((ba_ref, bb_ref, bc_ref, bd_ref)):
        logits = _dot(xb, wg_ref[:, n * D_MODEL:(n + 1) * D_MODEL])
        term = _dot(br_ref[...], wb_ref[n]) * _sigmoid(logits)
        merged = term if merged is None else merged + term
    mix = _dot(merged.astype(BF16), wo_ref[...])
    o_ref[...] = _layer_norm(ALPHA * x + mix, g_ref[...], b_ref[...])


def _merge(x, branches, wg, wb, wo, g, b, *, tm):
    m = x.shape[0]
    tm = min(tm, m)
    assert m % tm == 0
    const2 = lambda i: (0, 0)
    row = lambda i: (i, 0)
    return pl.pallas_call(
        _merge_kernel,
        grid=(m // tm,),
        in_specs=[pl.BlockSpec((tm, D_MODEL), row)]
        + [pl.BlockSpec((tm, BRANCH_W), row)] * N_BRANCH
        + [pl.BlockSpec((D_MODEL, N_BRANCH * D_MODEL), const2),
           pl.BlockSpec((N_BRANCH, BRANCH_W, D_MODEL), lambda i: (0, 0, 0)),
           pl.BlockSpec((D_MODEL, D_MODEL), const2),
           pl.BlockSpec((1, D_MODEL), const2),
           pl.BlockSpec((1, D_MODEL), const2)],
        out_specs=pl.BlockSpec((tm, D_MODEL), row),
        out_shape=jax.ShapeDtypeStruct((m, D_MODEL), F32),
        compiler_params=_params(("parallel",)),
        name="merge",
    )(x, *branches, wg, wb, wo, g, b)


def _route(aff_t, sel_t):
    def top2_sum(a, b, c, d):
        hi1, lo1 = jnp.maximum(a, b), jnp.minimum(a, b)
        hi2, lo2 = jnp.maximum(c, d), jnp.minimum(c, d)
        return jnp.maximum(hi1, hi2) + jnp.maximum(jnp.minimum(hi1, hi2), jnp.maximum(lo1, lo2))

    score = [top2_sum(*sel_t[g * EXPERTS_PER_GROUP:(g + 1) * EXPERTS_PER_GROUP])
             for g in range(N_GROUPS)]
    best_here = []
    for g in range(N_GROUPS):
        ok = None
        for o in range(N_GROUPS):
            if o == g:
                continue
            c = (score[g] > score[o]) if o < g else (score[g] >= score[o])
            ok = c if ok is None else jnp.logical_and(ok, c)
        best_here.append(ok)
    picked = []
    for e in range(N_EXPERTS):
        g = e // EXPERTS_PER_GROUP
        rank = jnp.zeros_like(sel_t[e])
        for o in range(g * EXPERTS_PER_GROUP, (g + 1) * EXPERTS_PER_GROUP):
            if o == e:
                continue
            ahead = (sel_t[o] >= sel_t[e]) if o < e else (sel_t[o] > sel_t[e])
            rank = rank + jnp.where(ahead, 1.0, 0.0)
        picked.append(jnp.where(jnp.logical_and(best_here[g], rank < TOP_K), aff_t[e], 0.0))
    total = picked[0]
    for e in range(1, N_EXPERTS):
        total = total + picked[e]
    return [p / total for p in picked]


def _moe_kernel(x_ref, wr_ref, br_ref, wg_ref, wu_ref, wd_ref, g_ref, b_ref, o_ref, acc_ref):
    x = x_ref[...]
    xh = x.astype(BF16)
    xl = (x - xh.astype(F32)).astype(BF16)
    both = _dot(xh, wr_ref[...])
    logits = both[:, :LANE] + both[:, LANE:] + _dot(xl, wr_ref[:, :LANE])
    aff = _sigmoid(logits).T
    bias = br_ref[...]
    aff_t = [aff[e:e + 1, :] for e in range(N_EXPERTS)]
    sel_t = [aff_t[e] + bias[e:e + 1, :] for e in range(N_EXPERTS)]
    gate_rows = _route(aff_t, sel_t)
    tm = x.shape[0]
    gate_t = jnp.concatenate(gate_rows + [jnp.zeros((LANE - N_EXPERTS, tm), F32)], axis=0)
    gate = gate_t.T

    for e in range(N_EXPERTS):
        gt = _dot(xh, wg_ref[e])
        hmid = gt * _sigmoid(gt) * _dot(xh, wu_ref[e])
        y = _dot(hmid.astype(BF16), wd_ref[e]) * gate[:, e:e + 1]
        if e == 0:
            acc_ref[...] = y
        else:
            acc_ref[...] += y
    o_ref[...] = _layer_norm(ALPHA * x + acc_ref[...], g_ref[...], b_ref[...])


def _moe(x, wr, br, wg, wu, wd, g, b, *, tm, layer):
    m = x.shape[0]
    tm = min(tm, m)
    assert m % tm == 0
    const2 = lambda i: (0, 0)
    this_layer = lambda i: (layer, 0, 0, 0)
    return pl.pallas_call(
        _moe_kernel,
        grid=(m // tm,),
        in_specs=[pl.BlockSpec((tm, D_MODEL), lambda i: (i, 0)),
                  pl.BlockSpec((D_MODEL, 2 * LANE), const2),
                  pl.BlockSpec((LANE, 1), const2),
                  pl.BlockSpec((None, N_EXPERTS, D_MODEL, D_EXPERT), this_layer, pipeline_mode=pl.Buffered(1)),
                  pl.BlockSpec((None, N_EXPERTS, D_MODEL, D_EXPERT), this_layer, pipeline_mode=pl.Buffered(1)),
                  pl.BlockSpec((None, N_EXPERTS, D_EXPERT, D_MODEL), this_layer, pipeline_mode=pl.Buffered(1)),
                  pl.BlockSpec((1, D_MODEL), const2),
                  pl.BlockSpec((1, D_MODEL), const2)],
        out_specs=pl.BlockSpec((tm, D_MODEL), lambda i: (i, 0)),
        out_shape=jax.ShapeDtypeStruct((m, D_MODEL), F32),
        scratch_shapes=[pltpu.VMEM((tm, D_MODEL), F32)],
        compiler_params=_params(("parallel",)),
        name="moe",
    )(x, wr, br, wg, wu, wd, g, b)


def _rope_tables(pos, d):
    half = d // 2
    inv = jnp.power(ROPE_BASE, -jnp.arange(half, dtype=F32) / half)
    ang = pos.astype(F32)[:, None] * inv[None, :]
    cos, sin = jnp.cos(ang), jnp.sin(ang)
    zero = jnp.zeros_like(sin)
    rep = LANE // d
    cos_t = jnp.tile(jnp.concatenate([cos, cos], axis=1), (1, rep))
    sin_a = jnp.tile(jnp.concatenate([-sin, zero], axis=1), (1, rep))
    sin_b = jnp.tile(jnp.concatenate([zero, sin], axis=1), (1, rep))
    return cos_t, sin_a, sin_b


def _retention_tables(lc):
    log_g = jnp.log1p(-jnp.exp2(-5.0 - jnp.arange(N_HEADS, dtype=F32)))
    i = jnp.arange(lc, dtype=F32)
    diff = i[:, None] - i[None, :]
    dec = jnp.where(diff >= 0, jnp.exp(jnp.maximum(diff, 0.0)[None] * log_g[:, None, None]), 0.0)
    qdec = jnp.exp((i[None, :] + 1.0) * log_g[:, None])
    kdec = jnp.exp((lc - 1.0 - i)[None, :] * log_g[:, None])
    gl = jnp.exp(lc * log_g)
    dec = dec.reshape(N_HEADS * lc, lc)
    qdec = jnp.broadcast_to(qdec[:, :, None], (N_HEADS, lc, D_HEAD)).reshape(N_HEADS * lc, D_HEAD)
    gl = jnp.broadcast_to(gl[:, None, None], (N_HEADS, D_HEAD, D_HEAD)).reshape(BRANCH_W, D_HEAD)
    kdec = jnp.repeat(kdec.T, D_HEAD, axis=1)
    return dec, qdec, kdec, gl


def _band_bias(rel_bias, tq, win, q_minus_k0, valid):
    length = tq + win - 1
    d = np.arange(length) - (tq - 1) - q_minus_k0
    idx = np.clip(d, -REL_CLIP, REL_CLIP) + REL_CLIP
    g = rel_bias[:, idx].astype(F32) * LOG2E
    gp = jnp.concatenate([g, jnp.zeros((N_HEADS, 1), F32)], axis=1)
    m = jnp.tile(gp, (1, tq))[:, :tq * length].reshape(N_HEADS, tq, length)
    tile = m[:, :, tq - 1:tq - 1 + win]
    return jnp.where(valid[None], tile, NEG_INF).reshape(N_HEADS * tq, win)


def _pack_w_in(w_in_l):
    cols = []
    src = 0
    for w in _IN_WIDTH:
        seg = w_in_l[:, src:src + w]
        cols.append(jnp.pad(seg, ((0, 0), (0, _round_up(w, LANE) - w))))
        src += w
    return jnp.concatenate(cols, axis=1).astype(BF16)


def _pad_rows(t, n):
    return jnp.pad(t, ((0, 0), (0, n - t.shape[1]), (0, 0)))


def _token_mixers(x, pos0, past, lw, *, prompt, layer, stacked):
    (w_in_p, qn, wuq, kvn, wukv, gn_g, gn_b, rel_bias, b, length) = lw
    assert pos0 % CHUNK == 0
    pos = pos0 + jnp.arange(length)
    lc = 256 if prompt else length
    dec, qdec, kdec, gl = _retention_tables(lc)
    rope32 = _rope_tables(pos, DR_A)
    lane = np.arange(LANE)
    rotary = (lane >= DN_A) & (lane < DQK_A)
    rope_q = [jnp.where(rotary[None, :], t, fill) for t, fill in zip(rope32, (1.0, 0.0, 0.0))]
    tables = (list(rope32) + list(_rope_tables(pos, D_HEAD)) + [jnp.tile(kdec, (length // lc, 1))] + rope_q)
    if not prompt:
        tables = [jnp.tile(t, (b, 1)) for t in tables]
    keep = min(PREV_CHUNKS * CHUNK, length)
    outs, stacked = _inproj(x, w_in_p, qn, wuq, kvn, wukv, tables, tm=TOKEN_TILE if prompt else b * length,
                            layer=layer, stacked=stacked, seq_len=length if prompt else None,
                            keep_rows=keep if prompt and keep < length else None)
    per_batch = lambda o: o.reshape(o.shape[:-2] + (b, length, o.shape[-1]))
    (q_a, ckv, kpe, kf, v_a, rq, rk, rkd, rv, rg, sq, sk, sv, bq, bk, bv) = [
        per_batch(o) for o in outs[:len(_INPROJ_OUT)]]
    here = lambda t: (t, layer)
    only = lambda t: (t[None], 0)

    if prompt:
        tq = 256
        qa_t, va_t, sq_t, sv_t, bq_t, bv_t = outs[len(_INPROJ_OUT):len(_INPROJ_OUT) + 6]
        nseq = SEQS_PER_STEP if b % SEQS_PER_STEP == 0 else 1
        o_a = _mla_attn_t(qa_t, kf, va_t, tq=tq, nseq=nseq)
        o_c = _sb_attn_t(sq_t, here(sk), sv_t, tq=tq, nseq=nseq)
        s0 = jnp.zeros((b, BRANCH_W, D_HEAD), F32)
        win = 3 * tq
        i = np.arange(tq)[:, None]
        c = np.arange(win)[None, :]
        variants = []
        for t in range(3):
            qc, kc = i // CHUNK + t * (tq // CHUNK), c // CHUNK
            variants.append(_band_bias(rel_bias, tq, win, t * tq, (kc <= qc) & (kc >= qc - PREV_CHUNKS)))
        bias_t = jnp.stack(variants).transpose(0, 2, 1)
        o_d = _band_attn_t(bq_t, here(bk), bv_t, bias_t, tq=tq, win=win, back=2,
                           nseq=2 * nseq if b % (2 * nseq) == 0 else nseq)
    else:
        c_ckv, c_kpe, s_prev, c_sk, c_sv, c_bk, c_bv = past
        n_past = c_ckv.shape[2]
        tko = LANE
        kf_c, v_c = _expand_latent(c_ckv.reshape(DEPTH, b * n_past, KV_RANK),
                                   c_kpe.reshape(DEPTH, b * n_past, DR_A), wukv, layer,
                                   math.gcd(b * n_past, CACHE_TILE))
        o_a = _mla_attn(q_a, only(_pad_rows(kf, tko)), only(_pad_rows(v_a, tko)),
                        only(kf_c.reshape(b, n_past, QA_W)), only(v_c.reshape(b, n_past, BRANCH_W)),
                        tq=length, tk=math.gcd(n_past, SAMPLE_KEY_TILE), n_own=length, causal_tiles=False)
        o_c = _sb_attn(sq, only(_pad_rows(sk[layer], tko)), only(_pad_rows(sv[layer], tko)),
                       (c_sk.reshape(DEPTH, b, n_past, BRANCH_W), layer),
                       (c_sv.reshape(DEPTH, b, n_past, BRANCH_W), layer),
                       tq=length, tk=256, n_own=length, causal_tiles=False)
        s0 = s_prev[layer].reshape(b, BRANCH_W, D_HEAD)
        n_band = c_bk.shape[2]
        n_keys = n_band + length
        win = _round_up(n_keys, LANE)
        bk_all = _pad_rows(jnp.concatenate([c_bk[layer].reshape(b, n_band, BRANCH_W), bk[layer]], axis=1), win)
        bv_all = _pad_rows(jnp.concatenate([c_bv[layer].reshape(b, n_band, BRANCH_W), bv[layer]], axis=1), win)
        k_pos = pos0 - n_band + np.arange(win)
        q_pos = pos0 + np.arange(length)
        qc, kc = q_pos[:, None] // CHUNK, k_pos[None, :] // CHUNK
        valid = (np.arange(win)[None, :] < n_keys) & (k_pos[None, :] >= 0) & (kc <= qc) & (kc >= qc - PREV_CHUNKS)
        bias = _band_bias(rel_bias, length, win, n_band, valid)[None]
        o_d = _band_attn(bq, only(bk_all), only(bv_all), bias, tq=length, win=win, back=0)

    o_r, s_ret = _retention(rq, rk, rkd, rv, rg, s0, dec, qdec, gl,
                            gn_g.reshape(N_HEADS, 1, D_HEAD), gn_b.reshape(N_HEADS, 1, D_HEAD), lc=lc,
                            nseq=SEQS_PER_STEP if b % SEQS_PER_STEP == 0 else 1)
    s_ret = s_ret.reshape(b, N_HEADS, D_HEAD, D_HEAD)
    flat = lambda t: t.reshape(b * length, BRANCH_W)
    return (flat(o_a), flat(o_r), flat(o_c), flat(o_d)), s_ret, stacked


def _state_outputs(stacked, s_ret, b, length):
    per_seq = lambda t: t.reshape(DEPTH, b, t.shape[1] // b, t.shape[-1])
    heads4 = lambda t: per_seq(t).reshape(DEPTH, b, t.shape[1] // b, N_HEADS, D_HEAD)
    ckv, kpe, sk, sv, bk, bv = stacked[:6]
    if len(stacked) > 6:
        bk, bv = stacked[6:]
    return (per_seq(ckv), per_seq(kpe), jnp.stack(s_ret, axis=0), heads4(sk), heads4(sv), heads4(bk), heads4(bv))


def kernel(x_prompt, x_sample, cache_mla_ckv, cache_mla_kpe, state_ret, cache_sb_k, cache_sb_v, cache_band_k, cache_band_v, w_in, mla_q_norm, mla_w_uq, mla_kv_norm, mla_w_ukv, ret_gn_g, ret_gn_b, band_rel_bias, w_branch, w_o, ln1_g, ln1_b, w_router, b_router, w_exp_gate, w_exp_up, w_exp_down, ln2_g, ln2_b):
    bp, lp, _ = x_prompt.shape
    bs, ls, _ = x_sample.shape
    past_len = cache_mla_ckv.shape[2]
    xp = x_prompt.reshape(bp * lp, D_MODEL)
    xs = x_sample.reshape(bs * ls, D_MODEL)

    wr = jnp.pad(w_router, ((0, 0), (0, LANE - N_EXPERTS)))
    wrh = wr.astype(BF16)
    wr2 = jnp.concatenate([wrh, (wr - wrh.astype(F32)).astype(BF16)], axis=1)
    br = jnp.pad(b_router, (0, LANE - N_EXPERTS)).reshape(LANE, 1)

    we_gate, we_up, we_down = w_exp_gate.astype(BF16), w_exp_up.astype(BF16), w_exp_down.astype(BF16)
    past = (cache_mla_ckv, cache_mla_kpe, state_ret, cache_sb_k, cache_sb_v, cache_band_k, cache_band_v)
    ret_p, ret_s = [], []
    stacked_p = stacked_s = None
    for l in range(DEPTH):
        wuq = mla_w_uq[l]
        wuq = jnp.pad(wuq, ((0, _round_up(Q_RANK, LANE) - Q_RANK), (0, 0), (0, LANE - DQK_A)))
        wuq = wuq.reshape(-1, QA_W).astype(BF16)
        wukv = mla_w_ukv[l]
        wukv = jnp.concatenate(
            [jnp.pad(wukv[:, :, :DN_A], ((0, 0), (0, 0), (0, LANE - DN_A))).reshape(KV_RANK, QA_W),
             wukv[:, :, DN_A:].reshape(KV_RANK, -1)], axis=1).astype(BF16)
        qn = jnp.pad(mla_q_norm[l], (0, _round_up(Q_RANK, LANE) - Q_RANK)).reshape(1, -1)
        kvn = mla_kv_norm[l].reshape(1, KV_RANK)
        w_in_p = _pack_w_in(w_in[l])
        wg = w_in[l][:, GATE_COL0:].astype(BF16)
        wb = w_branch[l].astype(BF16)
        wo = w_o[l].astype(BF16)
        g1, b1 = ln1_g[l].reshape(1, D_MODEL), ln1_b[l].reshape(1, D_MODEL)
        g2, b2 = ln2_g[l].reshape(1, D_MODEL), ln2_b[l].reshape(1, D_MODEL)
        lw =(w_in_p, qn, wuq, kvn, wukv, ret_gn_g[l], ret_gn_b[l], band_rel_bias[l])

        br_p, s_ret_p, stacked_p = _token_mixers(xp, 0, None, lw + (bp, lp), prompt=True,
                                                 layer=l, stacked=stacked_p)
        br_s, s_ret_s, stacked_s = _token_mixers(xs, past_len, past, lw + (bs, ls), prompt=False,
                                                 layer=l, stacked=stacked_s)
        xp = _merge(xp, br_p, wg, wb, wo, g1, b1, tm=TOKEN_TILE)
        xs = _merge(xs, br_s, wg, wb, wo, g1, b1, tm=TOKEN_TILE)
        xp = _moe(xp, wr2, br, we_gate, we_up, we_down, g2, b2, tm=TOKEN_TILE, layer=l)
        xs = _moe(xs, wr2, br, we_gate, we_up, we_down, g2, b2, tm=TOKEN_TILE, layer=l)
        ret_p.append(s_ret_p)
        ret_s.append(s_ret_s)

    return ((xp.reshape(bp, lp, D_MODEL), xs.reshape(bs, ls, D_MODEL))
            + _state_outputs(stacked_p, ret_p, bp, lp)
            + _state_outputs(stacked_s, ret_s, bs, ls))
```

```python
import functools
import math

import jax
import jax.numpy as jnp
import numpy as np
from jax import lax
from jax.experimental import pallas as pl
from jax.experimental.pallas import tpu as pltpu

D_MODEL = 1024
DEPTH = 2
CHUNK = 64
N_BRANCH = 4
BRANCH_W = D_MODEL // 4
N_HEADS = 4
D_HEAD = BRANCH_W // N_HEADS
DN_A = 64
DR_A = 32
DQK_A = DN_A + DR_A
Q_RANK = (3 * D_MODEL) // 16
KV_RANK = D_MODEL // 8
PREV_CHUNKS = 8
REL_CLIP = 128
ROPE_BASE = 10000.0
N_EXPERTS = 16
N_GROUPS = 4
EXPERTS_PER_GROUP = N_EXPERTS // N_GROUPS
TOP_K = 2
D_EXPERT = D_MODEL // 4
ALPHA = (2.0 * DEPTH) ** 0.25
EPS = 1e-5
NEG_INF = -1e30
LOG2E = 1.4426950408889634
SB_DEAD_LOG2 = -150.0

F32 = jnp.float32
BF16 = jnp.bfloat16

V7X_VMEM_LIMIT = 56 * 1024 * 1024
LANE = 128
TOKEN_TILE = 512
INPROJ_ROW_BLOCKS = 2
SEQS_PER_STEP = 2
CACHE_TILE = 2048
SAMPLE_KEY_TILE = 512

_IN_NAMES = ("c_q", "c_kv", "k_pe", "rq", "rk", "rv", "rg", "sq", "sk", "sv", "bq", "bk", "bv")
_IN_WIDTH = (Q_RANK, KV_RANK, DR_A) + (BRANCH_W,) * 10
QA_W = N_HEADS * LANE


def _round_up(n, m):
    return (n + m - 1) // m * m


_IN_OFF = {}
_off = 0
for _n, _w in zip(_IN_NAMES, _IN_WIDTH):
    _IN_OFF[_n] = (_off, _round_up(_w, LANE))
    _off += _round_up(_w, LANE)
IN_PACKED = _off
GATE_COL0 = sum(_IN_WIDTH)


def _params(sem):
    return pltpu.CompilerParams(dimension_semantics=sem, vmem_limit_bytes=V7X_VMEM_LIMIT)


def _nt_dot(a, b):
    return lax.dot_general(a, b, (((1,), (1,)), ((), ())), preferred_element_type=F32)


def _tn_dot(a, b):
    return lax.dot_general(a, b, (((0,), (0,)), ((), ())), preferred_element_type=F32)


def _dot(a, b):
    return jnp.dot(a, b, preferred_element_type=F32)


def _layer_norm(v, g, b):
    mu = jnp.mean(v, axis=-1, keepdims=True)
    d = v - mu
    var = jnp.mean(d * d, axis=-1, keepdims=True)
    return d * lax.rsqrt(var + EPS) * g + b


def _sigmoid(v):
    return 0.5 * jnp.tanh(0.5 * v) + 0.5


def _head_slice(h):
    return slice(h * D_HEAD, (h + 1) * D_HEAD)


def _stack_heads(q, lane_sets):
    lane = lax.broadcasted_iota(jnp.int32, q.shape, 1)
    zero = jnp.zeros_like(q)
    parts = []
    for h in range(N_HEADS):
        keep = None
        for lo, hi in lane_sets(h):
            m = jnp.logical_and(lane >= lo, lane < hi)
            keep = m if keep is None else jnp.logical_or(keep, m)
        parts.append(jnp.where(keep, q, zero))
    return jnp.concatenate(parts, axis=0)


def _own_lanes(h):
    return ((h * D_HEAD, (h + 1) * D_HEAD),)


def _mla_lanes(h):
    return ((h * LANE, (h + 1) * LANE),)


def _store_heads(o_ref, parts):
    for h, p in enumerate(parts):
        o_ref[:, _head_slice(h)] = p.astype(o_ref.dtype)


def _expand_kernel(ckv_ref, kpe_ref, w_ref, kf_ref, v_ref):
    kvx = _dot(ckv_ref[...].astype(BF16), w_ref[...])
    kp = kpe_ref[...].astype(BF16)
    tail = jnp.zeros((kp.shape[0], LANE - DN_A - DR_A), BF16)
    for h in range(N_HEADS):
        kf_ref[:, h * LANE:(h + 1) * LANE] = jnp.concatenate(
            [kvx[:, h * LANE:h * LANE + DN_A].astype(BF16), kp, tail], axis=1)
    v_ref[...] = kvx[:, QA_W:].astype(BF16)


def _expand_latent(ckv, kpe, w_ukv, layer, tm):
    m = ckv.shape[1]
    assert m % tm == 0
    return pl.pallas_call(
        _expand_kernel,
        grid=(m // tm,),
        in_specs=[pl.BlockSpec((None, tm, KV_RANK), lambda i: (layer, i, 0)),
                  pl.BlockSpec((None, tm, DR_A), lambda i: (layer, i, 0)),
                  pl.BlockSpec(w_ukv.shape, lambda i: (0, 0))],
        out_specs=[pl.BlockSpec((tm, QA_W), lambda i: (i, 0)),
                   pl.BlockSpec((tm, BRANCH_W), lambda i: (i, 0))],
        out_shape=[jax.ShapeDtypeStruct((m, QA_W), BF16),
                   jax.ShapeDtypeStruct((m, BRANCH_W), BF16)],
        compiler_params=_params(("parallel",)),
        name="expand_latent",
    )(ckv, kpe, w_ukv)


def _rope_block(x, cos, sin_a, sin_b, half):
    return x * cos + pltpu.roll(x, LANE - half, 1) * sin_a + pltpu.roll(x, half, 1) * sin_b


def _put_state(ref, val):
    if len(ref.shape) == 3:
        for l in range(ref.shape[0]):
            ref[l] = val
    else:
        ref[...] = val


def _inproj_kernel(x_ref, w_ref, qn_ref, wuq_ref, kvn_ref, wukv_ref,
                   c32_ref, a32_ref, b32_ref, c64_ref, a64_ref, b64_ref, kdec_ref, cqa_ref, aqa_ref, bqa_ref,
                   qa_ref, ckv_ref, kpe_ref, kf_ref, va_ref,
                   rq_ref, rk_ref, rkd_ref, rv_ref, rg_ref,
                   sq_ref, sk_ref, sv_ref, bq_ref, bk_ref, bv_ref, t_refs=(), keep_refs=()):
    qat_ref, vat_ref, sqt_ref, svt_ref, bqt_ref, bvt_ref = t_refs if t_refs else (None,) * 6
    z = _dot(x_ref[...].astype(BF16), w_ref[...])

    def seg(name):
        o, w = _IN_OFF[name]
        return z[:, o:o + w]

    cq = seg("c_q")
    cqn = cq * lax.rsqrt(jnp.sum(cq * cq, axis=1, keepdims=True) * (1.0 / Q_RANK) + EPS) * qn_ref[...]
    qa = _dot(cqn.astype(BF16), wuq_ref[...])
    scale_a = DQK_A ** -0.5 * LOG2E
    qa_s = jnp.concatenate(
        [_rope_block(qa[:, h * LANE:(h + 1) * LANE], cqa_ref[...], aqa_ref[...], bqa_ref[...], DR_A // 2)
         for h in range(N_HEADS)], axis=1) * scale_a
    qa_ref[...] = qa_s.astype(BF16)
    if qat_ref is not None:
        qat_ref[...] = qa_s.T.astype(BF16)

    ckv_raw = seg("c_kv")
    ckv = ckv_raw * lax.rsqrt(jnp.mean(ckv_raw * ckv_raw, axis=1, keepdims=True) + EPS) * kvn_ref[...]
    _put_state(ckv_ref, ckv)
    kvx = _dot(ckv.astype(BF16), wukv_ref[...])
    kp = _rope_block(seg("k_pe"), c32_ref[...], a32_ref[...], b32_ref[...], DR_A // 2)
    _put_state(kpe_ref, kp[:, :DR_A])
    kp_at = pltpu.roll(kp, DN_A, 1)
    for h in range(N_HEADS):
        kf_ref[:, h * LANE:(h + 1) * LANE] = (kvx[:, h * LANE:(h + 1) * LANE] + kp_at).astype(BF16)
    va_ref[...] = kvx[:, QA_W:].astype(BF16)
    if vat_ref is not None:
        vat_ref[...] = kvx[:, QA_W:].T.astype(BF16)

    rq, rk = seg("rq"), seg("rk")
    kdec = kdec_ref[...]
    for blk in range(BRANCH_W // LANE):
        cols = slice(blk * LANE, (blk + 1) * LANE)
        rq_ref[:, cols] = _rope_block(rq[:, cols], c64_ref[...], a64_ref[...], b64_ref[...],
                                      D_HEAD // 2).astype(BF16)
        rkb = _rope_block(rk[:, cols], c64_ref[...], a64_ref[...], b64_ref[...], D_HEAD // 2) * (D_HEAD ** -0.5)
        rk_ref[:, cols] = rkb.astype(BF16)
        rkd_ref[:, cols] = (rkb * kdec[:, cols]).astype(BF16)
    rv_ref[...] = seg("rv").astype(BF16)
    rg_ref[...] = seg("rg")

    scale_h = D_HEAD ** -0.5 * LOG2E
    sq = seg("sq") * scale_h
    sq_ref[...] = sq.astype(BF16)
    _put_state(sk_ref, seg("sk"))
    _put_state(sv_ref, seg("sv"))
    if sqt_ref is not None:
        sqt_ref[...] = sq.T.astype(BF16)
        svt_ref[...] = seg("sv").T.astype(BF16)
    bq = seg("bq") * scale_h
    bq_ref[...] = bq.astype(BF16)
    _put_state(bk_ref, seg("bk"))
    _put_state(bv_ref, seg("bv"))
    if bqt_ref is not None:
        bqt_ref[...] = bq.T.astype(BF16)
        bvt_ref[...] = seg("bv").T.astype(BF16)
    if keep_refs:
        _put_state(keep_refs[0], seg("bk"))
        _put_state(keep_refs[1], seg("bv"))


_INPROJ_OUT = (
    (QA_W, BF16), (KV_RANK, F32), (DR_A, F32), (QA_W, BF16), (BRANCH_W, BF16),
    (BRANCH_W, BF16), (BRANCH_W, BF16), (BRANCH_W, BF16), (BRANCH_W, BF16), (BRANCH_W, F32),
    (BRANCH_W, BF16), (BRANCH_W, F32), (BRANCH_W, F32), (BRANCH_W, BF16), (BRANCH_W, F32), (BRANCH_W, F32))


_STATE_OUTS = (1, 2, 11, 12, 14, 15)


def _inproj_entry(*refs, n_in, n_alias, n_t, n_keep, n_sub):
    ins, outs = refs[:n_in], refs[n_in + n_alias:]
    n_main = len(_INPROJ_OUT)
    sub = ins[0].shape[0] // n_sub
    n_w = 6
    for r in range(n_sub):
        rows = pl.ds(r * sub, sub)
        row_view = lambda ref: ref.at[:, rows] if len(ref.shape) == 3 else ref.at[rows]
        _inproj_kernel(ins[0].at[rows], *ins[1:n_w], *[t.at[rows] for t in ins[n_w:]],
                       *[row_view(o) for o in outs[:n_main]],
                       t_refs=[o.at[:, rows] for o in outs[n_main:n_main + n_t]],
                       keep_refs=[row_view(o) for o in outs[n_main + n_t:n_main + n_t + n_keep]])


def _inproj(x, w_in_p, qn, wuq, kvn, wukv, tables, *, tm, layer, stacked, seq_len=None, keep_rows=None):
    m = x.shape[0]
    tm = min(tm, m)
    assert m % tm == 0
    n_pos_tiles = tables[0].shape[0] // tm
    assert tables[0].shape[0] % tm == 0
    row = lambda i: (i, 0)
    const = lambda i: (0, 0)
    pos = lambda i: (i % n_pos_tiles, 0)
    in_specs = [pl.BlockSpec((tm, D_MODEL), row),
                pl.BlockSpec(w_in_p.shape, const),
                pl.BlockSpec(qn.shape, const),
                pl.BlockSpec(wuq.shape, const),
                pl.BlockSpec(kvn.shape, const),
                pl.BlockSpec(wukv.shape, const)]
    in_specs += [pl.BlockSpec((tm, t.shape[1]), pos) for t in tables]
    n_in = len(in_specs)
    out_specs, out_shape = [], []
    assert stacked is not None or layer == 0

    def state_spec(w, rows_of):
        if stacked is None:
            return pl.BlockSpec((DEPTH, tm, w), lambda i: (0, rows_of(i), 0))
        return pl.BlockSpec((None, tm, w), lambda i: (layer, rows_of(i), 0))

    for k, (w, dt) in enumerate(_INPROJ_OUT):
        if k in _STATE_OUTS:
            out_specs.append(state_spec(w, lambda i: i))
            out_shape.append(jax.ShapeDtypeStruct((DEPTH, m, w), dt))
        else:
            out_specs.append(pl.BlockSpec((tm, w), row))
            out_shape.append(jax.ShapeDtypeStruct((m, w), dt))
    if seq_len is not None:
        assert seq_len % tm == 0 and m % seq_len == 0
        tps = seq_len // tm
        for w in (QA_W,) + (BRANCH_W,) * 5:
            out_specs.append(pl.BlockSpec((None, w, tm), lambda i: (i // tps, 0, i % tps)))
            out_shape.append(jax.ShapeDtypeStruct((m // seq_len, w, seq_len), BF16))
    n_t = len(out_specs) - len(_INPROJ_OUT)
    state_outs = list(_STATE_OUTS)
    if keep_rows is not None:
        assert seq_len is not None and keep_rows % tm == 0 and keep_rows <= seq_len
        tps, ktiles = seq_len // tm, keep_rows // tm
        kept = lambda i: (i // tps) * ktiles + jnp.maximum(i % tps - (tps - ktiles), 0)
        for _ in range(2):
            state_outs.append(len(out_specs))
            out_specs.append(state_spec(BRANCH_W, kept))
            out_shape.append(jax.ShapeDtypeStruct((DEPTH, m // seq_len * keep_rows, BRANCH_W), F32))
    aliases = {}
    extra = ()
    if stacked is not None:
        extra = tuple(stacked)
        assert len(extra) == len(state_outs)
        in_specs += [pl.BlockSpec(memory_space=pl.ANY)] * len(extra)
        aliases = {n_in + k: o for k, o in enumerate(state_outs)}
    outs = pl.pallas_call(
        functools.partial(_inproj_entry, n_in=n_in, n_alias=len(extra), n_t=n_t,
                          n_keep=len(state_outs) - len(_STATE_OUTS),
                          n_sub=INPROJ_ROW_BLOCKS if tm % (INPROJ_ROW_BLOCKS * 2 * LANE) == 0 else 1),
        grid=(m // tm,),
        in_specs=in_specs,
        out_specs=out_specs,
        out_shape=out_shape,
        input_output_aliases=aliases,
        compiler_params=_params(("arbitrary",)),
        name="inproj",
    )(x, w_in_p, qn, wuq, kvn, wukv, *tables, *extra)
    return outs, tuple(outs[k] for k in state_outs)


def _tile_spec(src, rows):
    arr, layer = src
    return pl.BlockSpec((None, None, rows, arr.shape[3]), lambda i, t: (layer, i, t, 0))


def _full_spec(src):
    arr, layer = src
    return pl.BlockSpec((None, None, arr.shape[2], arr.shape[3]), lambda i, t: (layer, i, 0, 0))


def _mla_kernel(q_ref, kfo_ref, vo_ref, kfp_ref, vp_ref, o_ref, *, tq, tk, n_own, n_past):
    qi = pl.program_id(1)
    qm = _stack_heads(q_ref[...], _mla_lanes)
    rows = N_HEADS * tq
    tko = kfo_ref.shape[0]
    n_loop = qi if n_past is None else n_past

    def softmax_pv(s, v, carry):
        m, accs = carry
        m_new = jnp.maximum(m, jnp.max(s, axis=1, keepdims=True))
        pb = jnp.exp2(s - m_new).astype(BF16)
        a = jnp.exp2(m - m_new)
        ones = jnp.ones((v.shape[0], D_HEAD), BF16)
        accs = tuple(a[h * tq:(h + 1) * tq] * accs[h]
                     + _dot(pb[h * tq:(h + 1) * tq], jnp.concatenate([v[:, _head_slice(h)], ones], axis=1))
                     for h in range(N_HEADS))
        return m_new, accs

    def past_scores(j):
        start = pl.multiple_of(jnp.minimum(j, jnp.maximum(n_loop - 1, 0)) * tk, tk)
        return _nt_dot(qm, kfp_ref[pl.ds(start, tk), :])

    row_q = lax.broadcasted_iota(jnp.int32, (rows, tko), 0) % tq
    col = lax.broadcasted_iota(jnp.int32, (rows, tko), 1)
    own_mask = jnp.logical_and(col < n_own, col // CHUNK <= row_q // CHUNK)
    carry = (jnp.full((rows, 1), NEG_INF, F32),
             tuple(jnp.zeros((tq, 2 * D_HEAD), F32) for _ in range(N_HEADS)))
    s_next = past_scores(0)
    carry = softmax_pv(jnp.where(own_mask, _nt_dot(qm, kfo_ref[...]), NEG_INF), vo_ref[...], carry)

    def body(j, c):
        s_cur, rest = c
        s_after = past_scores(j + 1)
        start = pl.multiple_of(j * tk, tk)
        return s_after, softmax_pv(s_cur, vp_ref[pl.ds(start, tk), :], rest)

    _, (_, accs) = lax.fori_loop(0, n_loop, body, (s_next, carry))
    _store_heads(o_ref, [accs[h][:, :D_HEAD] / accs[h][:, D_HEAD:D_HEAD + 1] for h in range(N_HEADS)])


def _mla_attn(q, kf_own, v_own, kf_past, v_past, *, tq, tk, n_own, causal_tiles):
    b, lq, _ = q.shape
    nqt = lq // tq
    tko = kf_own[0].shape[2] // nqt
    lp = kf_past[0].shape[2]
    assert lq % tq == 0 and lp % tk == 0
    return pl.pallas_call(
        functools.partial(_mla_kernel, tq=tq, tk=tk, n_own=n_own,
                          n_past=None if causal_tiles else lp // tk),
        grid=(b, nqt),
        in_specs=[pl.BlockSpec((None, tq, QA_W), lambda i, t: (i, t, 0)),
                  _tile_spec(kf_own, tko), _tile_spec(v_own, tko),
                  _full_spec(kf_past), _full_spec(v_past)],
        out_specs=pl.BlockSpec((None, tq, BRANCH_W), lambda i, t: (i, t, 0)),
        out_shape=jax.ShapeDtypeStruct((b, lq, BRANCH_W), BF16),
        compiler_params=_params(("parallel", "arbitrary")),
        name="mla_attn",
    )(q, kf_own[0], v_own[0], kf_past[0], v_past[0])


def _mla_kernel_t(qt_ref, kfo_ref, vto_ref, kfp_ref, vtp_ref, o_ref, *, tq, tk, nseq):
    qi = pl.program_id(1)
    cols = N_HEADS * tq

    def scores(s, kf):
        qt = qt_ref[s]
        return jnp.concatenate([_dot(kf[:, h * LANE:(h + 1) * LANE], qt[h * LANE:(h + 1) * LANE, :])
                                for h in range(N_HEADS)], axis=1)

    def softmax_pv(st, vt, carry):
        m, l, accs = carry
        m_new = jnp.maximum(m, jnp.max(st, axis=0, keepdims=True))
        pt = jnp.exp2(st - m_new)
        a = jnp.exp2(m - m_new)
        l = a * l + jnp.sum(pt, axis=0, keepdims=True)
        pb = pt.astype(BF16)
        accs = tuple(a[:, h * tq:(h + 1) * tq] * accs[h]
                     + _dot(vt[_head_slice(h), :], pb[:, h * tq:(h + 1) * tq]) for h in range(N_HEADS))
        return m_new, l, accs

    def past_scores(s, j):
        start = pl.multiple_of(jnp.minimum(j, jnp.maximum(qi - 1, 0)) * tk, tk)
        return scores(s, kfp_ref[s, pl.ds(start, tk), :])

    key = lax.broadcasted_iota(jnp.int32, (tq, cols), 0)
    qry = lax.broadcasted_iota(jnp.int32, (tq, cols), 1) % tq
    own_mask = key // CHUNK <= qry // CHUNK
    s_next, carries = [], []
    for s in range(nseq):
        empty = (jnp.full((1, cols), NEG_INF, F32), jnp.zeros((1, cols), F32),
                 tuple(jnp.zeros((D_HEAD, tq), F32) for _ in range(N_HEADS)))
        s_next.append(past_scores(s, 0))
        carries.append(softmax_pv(jnp.where(own_mask, scores(s, kfo_ref[s]), NEG_INF), vto_ref[s], empty))

    def body(j, c):
        s_cur, rest = c
        start = pl.multiple_of(j * tk, tk)
        s_after = tuple(past_scores(s, j + 1) for s in range(nseq))
        return s_after, tuple(softmax_pv(s_cur[s], vtp_ref[s, :, pl.ds(start, tk)], rest[s])
                              for s in range(nseq))

    _, done = lax.fori_loop(0, qi, body, (tuple(s_next), tuple(carries)))
    for s in range(nseq):
        _, l, accs = done[s]
        out_t = jnp.concatenate([accs[h] / l[:, h * tq:(h + 1) * tq] for h in range(N_HEADS)], axis=0)
        o_ref[s] = out_t.T.astype(o_ref.dtype)


def _mla_attn_t(qt, kf, vt, *, tq, nseq):
    b, _, length = qt.shape
    assert length % tq == 0 and b % nseq == 0
    return pl.pallas_call(
        functools.partial(_mla_kernel_t, tq=tq, tk=tq, nseq=nseq),
        grid=(b // nseq, length // tq),
        in_specs=[pl.BlockSpec((nseq, QA_W, tq), lambda i, t: (i, 0, t)),
                  pl.BlockSpec((nseq, tq, QA_W), lambda i, t: (i, t, 0)),
                  pl.BlockSpec((nseq, BRANCH_W, tq), lambda i, t: (i, 0, t)),
                  pl.BlockSpec((nseq, length, QA_W), lambda i, t: (i, 0, 0)),
                  pl.BlockSpec((nseq, BRANCH_W, length), lambda i, t: (i, 0, 0))],
        out_specs=pl.BlockSpec((nseq, tq, BRANCH_W), lambda i, t: (i, t, 0)),
        out_shape=jax.ShapeDtypeStruct((b, length, BRANCH_W), BF16),
        compiler_params=_params(("parallel", "arbitrary")),
        name="mla_attn_t",
    )(qt, kf, vt, kf, vt)


def _sb_kernel(q_ref, ko_ref, vo_ref, kp_ref, vp_ref, o_ref, *, tq, tk, n_own, n_past):
    qi = pl.program_id(1)
    qm = _stack_heads(q_ref[...], _own_lanes)
    rows = N_HEADS * tq
    tko = ko_ref.shape[0]
    n_loop = qi if n_past is None else n_past

    def tri2(n):
        r = lax.broadcasted_iota(jnp.int32, (2 * n, n), 0) % n
        c = lax.broadcasted_iota(jnp.int32, (2 * n, n), 1)
        return jnp.where(r > c, 1.0, 0.0).astype(BF16)

    def weigh(z, v, carry, mask, tri):
        run, accs = carry
        neg_abs = lax.bitcast_convert_type(
            lax.bitcast_convert_type(z, jnp.uint32) | jnp.uint32(0x80000000), F32)
        t = jnp.log2(1.0 + jnp.exp2(neg_abs))
        log_beta = jnp.minimum(z, 0.0) - t
        log_stay = log_beta - z
        if mask is not None:
            log_stay = jnp.where(mask, log_stay, 0.0)
        hi = log_stay.astype(BF16)
        lo = (log_stay - hi.astype(F32)).astype(BF16)
        later = _dot(jnp.concatenate([hi, lo], axis=1), tri) + run
        w = jnp.exp2(log_beta + later)
        if mask is not None:
            w = jnp.where(mask, w, 0.0)
        wb = w.astype(BF16)
        vb = v.astype(BF16)
        accs = tuple(accs[h] + _dot(wb[h * tq:(h + 1) * tq], vb[:, _head_slice(h)])
                     for h in range(N_HEADS))
        return run + jnp.sum(log_stay, axis=1, keepdims=True), accs

    row_q = lax.broadcasted_iota(jnp.int32, (rows, tko), 0) % tq
    col = lax.broadcasted_iota(jnp.int32, (rows, tko), 1)
    own_mask = jnp.logical_and(col < n_own, col < row_q)

    def past_start(jj):
        return pl.multiple_of(jnp.clip(n_loop - 1 - jj, 0, kp_ref.shape[0] // tk - 1) * tk, tk)

    def past_scores(jj):
        return _nt_dot(qm, kp_ref[pl.ds(past_start(jj), tk), :].astype(BF16))

    def alive(run):
        return (jnp.max(run) > SB_DEAD_LOG2).astype(jnp.int32)

    z_next = past_scores(0)
    carry = (jnp.zeros((rows, 1), F32), tuple(jnp.zeros((tq, D_HEAD), F32) for _ in range(N_HEADS)))
    run, accs = weigh(_nt_dot(qm, ko_ref[...].astype(BF16)), vo_ref[...], carry, own_mask, tri2(tko))
    tri_past = tri2(tk)

    def cond(c):
        return jnp.logical_and(c[0] < n_loop, c[1] > 0)

    def body(c):
        jj, _, z_cur, run, accs = c
        z_after = past_scores(jj + 1)
        run, accs = weigh(z_cur, vp_ref[pl.ds(past_start(jj), tk), :], (run, accs), None, tri_past)
        return jj + 1, alive(run), z_after, run, accs

    out = lax.while_loop(cond, body, (jnp.int32(0), alive(run), z_next, run, accs))
    _store_heads(o_ref, out[4])


def _sb_attn(q, k_own, v_own, k_past, v_past, *, tq, tk, n_own, causal_tiles):
    b, lq, _ = q.shape
    nqt = lq // tq
    tko = k_own[0].shape[2] // nqt
    lp = k_past[0].shape[2]
    assert lq % tq == 0 and lp % tk == 0
    return pl.pallas_call(
        functools.partial(_sb_kernel, tq=tq, tk=tk, n_own=n_own,
                          n_past=None if causal_tiles else lp // tk),
        grid=(b, nqt),
        in_specs=[pl.BlockSpec((None, tq, BRANCH_W), lambda i, t: (i, t, 0)),
                  _tile_spec(k_own, tko), _tile_spec(v_own, tko),
                  _full_spec(k_past), _full_spec(v_past)],
        out_specs=pl.BlockSpec((None, tq, BRANCH_W), lambda i, t: (i, t, 0)),
        out_shape=jax.ShapeDtypeStruct((b, lq, BRANCH_W), BF16),
        compiler_params=_params(("parallel", "arbitrary")),
        name="sb_attn",
    )(q, k_own[0], v_own[0], k_past[0], v_past[0])


def _sb_kernel_t(qt_ref, ko_ref, vto_ref, kp_ref, vtp_ref, o_ref, *, tq, tk, nseq):
    qi = pl.program_id(1)
    qmts = []
    for s in range(nseq):
        qt = qt_ref[s]
        feat = lax.broadcasted_iota(jnp.int32, qt.shape, 0)
        qmts.append(jnp.concatenate(
            [jnp.where(jnp.logical_and(feat >= h * D_HEAD, feat < (h + 1) * D_HEAD), qt, jnp.zeros_like(qt))
             for h in range(N_HEADS)], axis=1))
    cols = N_HEADS * tq

    def tri2(n):
        r = lax.broadcasted_iota(jnp.int32, (n, 2 * n), 0)
        c = lax.broadcasted_iota(jnp.int32, (n, 2 * n), 1) % n
        return jnp.where(c > r, 1.0, 0.0).astype(BF16)

    def weigh(zt, vt, carry, mask, tri):
        run, accs = carry
        neg_abs = lax.bitcast_convert_type(
            lax.bitcast_convert_type(zt, jnp.uint32) | jnp.uint32(0x80000000), F32)
        t = jnp.log2(1.0 + jnp.exp2(neg_abs))
        log_beta = jnp.minimum(zt, 0.0) - t
        log_stay = log_beta - zt
        if mask is not None:
            log_stay = jnp.where(mask, log_stay, 0.0)
        hi = log_stay.astype(BF16)
        lo = (log_stay - hi.astype(F32)).astype(BF16)
        later = _dot(tri, jnp.concatenate([hi, lo], axis=0)) + run
        w = jnp.exp2(log_beta + later)
        if mask is not None:
            w = jnp.where(mask, w, 0.0)
        wb = w.astype(BF16)
        accs = tuple(accs[h] + _dot(vt[_head_slice(h), :], wb[:, h * tq:(h + 1) * tq])
                     for h in range(N_HEADS))
        return run + jnp.sum(log_stay, axis=0, keepdims=True), accs

    def past_start(jj):
        return pl.multiple_of(jnp.clip(qi - 1 - jj, 0, kp_ref.shape[1] // tk - 1) * tk, tk)

    def past_scores(s, jj):
        return _dot(kp_ref[s, pl.ds(past_start(jj), tk), :].astype(BF16), qmts[s])

    def alive(runs):
        top = jnp.max(runs[0])
        for r in runs[1:]:
            top = jnp.maximum(top, jnp.max(r))
        return (top > SB_DEAD_LOG2).astype(jnp.int32)

    key = lax.broadcasted_iota(jnp.int32, (tq, cols), 0)
    qry = lax.broadcasted_iota(jnp.int32, (tq, cols), 1) % tq
    own_mask = key < qry
    tri_own = tri2(tq)
    z_next, runs, accss = [], [], []
    for s in range(nseq):
        z_next.append(past_scores(s, 0))
        empty = (jnp.zeros((1, cols), F32), tuple(jnp.zeros((D_HEAD, tq), F32) for _ in range(N_HEADS)))
        run, accs = weigh(_dot(ko_ref[s].astype(BF16), qmts[s]), vto_ref[s], empty, own_mask, tri_own)
        runs.append(run)
        accss.append(accs)
    tri_past = tri_own if tk == tq else tri2(tk)

    def cond(c):
        return jnp.logical_and(c[0] < qi, c[1] > 0)

    def body(c):
        jj, _, z_cur, runs, accss = c
        z_after = tuple(past_scores(s, jj + 1) for s in range(nseq))
        new = [weigh(z_cur[s], vtp_ref[s, :, pl.ds(past_start(jj), tk)], (runs[s], accss[s]), None, tri_past)
               for s in range(nseq)]
        runs = tuple(n[0] for n in new)
        return jj + 1, alive(runs), z_after, runs, tuple(n[1] for n in new)

    out = lax.while_loop(cond, body, (jnp.int32(0), alive(runs), tuple(z_next), tuple(runs), tuple(accss)))
    for s in range(nseq):
        o_ref[s] = jnp.concatenate(out[4][s], axis=0).T.astype(o_ref.dtype)


def _sb_attn_t(qt, k, vt, *, tq, nseq):
    b, _, length = qt.shape
    k_arr, layer = k
    assert length % tq == 0 and b % nseq == 0
    return pl.pallas_call(
        functools.partial(_sb_kernel_t, tq=tq, tk=tq, nseq=nseq),
        grid=(b // nseq, length // tq),
        in_specs=[pl.BlockSpec((nseq, BRANCH_W, tq), lambda i, t: (i, 0, t)),
                  pl.BlockSpec((None, nseq, tq, BRANCH_W), lambda i, t: (layer, i, t, 0)),
                  pl.BlockSpec((nseq, BRANCH_W, tq), lambda i, t: (i, 0, t)),
                  pl.BlockSpec((None, nseq, length, BRANCH_W), lambda i, t: (layer, i, 0, 0)),
                  pl.BlockSpec((nseq, BRANCH_W, length), lambda i, t: (i, 0, 0))],
        out_specs=pl.BlockSpec((nseq, tq, BRANCH_W), lambda i, t: (i, t, 0)),
        out_shape=jax.ShapeDtypeStruct((b, length, BRANCH_W), BF16),
        compiler_params=_params(("parallel", "arbitrary")),
        name="sb_attn_t",
    )(qt, k_arr, vt, k_arr, vt)


def _ret_kernel(q_ref, k_ref, kd_ref, v_ref, rg_ref, s0_ref, dec_ref, qdec_ref, gl_ref,
                gng_ref, gnb_ref, o_ref, sout_ref, state_ref, *, lc, nseq):
    c = pl.program_id(1)

    @pl.when(c == 0)
    def _():
        state_ref[...] = s0_ref[...]

    for s in range(nseq):
        qm = _stack_heads(q_ref[s], _own_lanes)
        v = v_ref[s]
        state = state_ref[s]
        scores = (_nt_dot(qm, k_ref[s]) * dec_ref[...]).astype(BF16)
        cross = _dot(qm, state.astype(BF16)) * qdec_ref[...]
        kv_full = _tn_dot(kd_ref[s], v)
        new_state = gl_ref[...] * state + jnp.concatenate(
            [kv_full[_head_slice(h), _head_slice(h)] for h in range(N_HEADS)], axis=0)
        state_ref[s] = new_state

        rg = rg_ref[s]
        for h in range(N_HEADS):
            o = _dot(scores[h * lc:(h + 1) * lc], v[:, _head_slice(h)]) + cross[h * lc:(h + 1) * lc]
            mu = jnp.mean(o, axis=-1, keepdims=True)
            d = o - mu
            var = jnp.mean(d * d, axis=-1, keepdims=True)
            y = d * lax.rsqrt(var + EPS) * gng_ref[h] + gnb_ref[h]
            g = rg[:, _head_slice(h)]
            o_ref[s, :, _head_slice(h)] = (y * (g * _sigmoid(g))).astype(o_ref.dtype)

    @pl.when(c == pl.num_programs(1) - 1)
    def _():
        sout_ref[...] = state_ref[...]


def _retention(q, k, kd, v, rg, s0, dec, qdec, gl, gng, gnb, *, lc, nseq):
    b, length, _ = q.shape
    assert length % lc == 0 and b % nseq == 0
    seq = lambda i, t: (i, t, 0)
    st = lambda i, t: (i, 0, 0)
    c2 = lambda i, t: (0, 0)
    c3 = lambda i, t: (0, 0, 0)
    return pl.pallas_call(
        functools.partial(_ret_kernel, lc=lc, nseq=nseq),
        grid=(b // nseq, length // lc),
        in_specs=[pl.BlockSpec((nseq, lc, BRANCH_W), seq)] * 5
        + [pl.BlockSpec((nseq, BRANCH_W, D_HEAD), st),
           pl.BlockSpec(dec.shape, c2), pl.BlockSpec(qdec.shape, c2), pl.BlockSpec(gl.shape, c2),
           pl.BlockSpec(gng.shape, c3), pl.BlockSpec(gnb.shape, c3)],
        out_specs=[pl.BlockSpec((nseq, lc, BRANCH_W), seq),
                   pl.BlockSpec((nseq, BRANCH_W, D_HEAD), st)],
        out_shape=[jax.ShapeDtypeStruct((b, length, BRANCH_W), BF16),
                   jax.ShapeDtypeStruct((b, BRANCH_W, D_HEAD), F32)],
        scratch_shapes=[pltpu.VMEM((nseq, BRANCH_W, D_HEAD), F32)],
        compiler_params=_params(("parallel", "arbitrary")),
        name="retention",
    )(q, k, kd, v, rg, s0, dec, qdec, gl, gng, gnb)


def _band_kernel(q_ref, k_ref, v_ref, bias_ref, o_ref, *, tq, win, back):
    qi = pl.program_id(1)
    start = pl.multiple_of(jnp.maximum(qi - back, 0) * tq, tq)
    k = k_ref[pl.ds(start, win), :].astype(BF16)
    v = v_ref[pl.ds(start, win), :].astype(BF16)
    qm = _stack_heads(q_ref[...], _own_lanes)
    s = _nt_dot(qm, k) + bias_ref[...]
    pb = jnp.exp2(s - jnp.max(s, axis=1, keepdims=True)).astype(BF16)
    ones = jnp.ones((win, D_HEAD), BF16)
    outs = []
    for h in range(N_HEADS):
        o = _dot(pb[h * tq:(h + 1) * tq], jnp.concatenate([v[:, _head_slice(h)], ones], axis=1))
        outs.append(o[:, :D_HEAD] / o[:, D_HEAD:D_HEAD + 1])
    _store_heads(o_ref, outs)


def _band_attn(q, k, v, bias, *, tq, win, back):
    b, lq, _ = q.shape
    nvar = bias.shape[0]
    assert lq % tq == 0
    return pl.pallas_call(
        functools.partial(_band_kernel, tq=tq, win=win, back=back),
        grid=(b, lq // tq),
        in_specs=[pl.BlockSpec((None, tq, BRANCH_W), lambda i, t: (i, t, 0)),
                  _full_spec(k), _full_spec(v),
                  pl.BlockSpec((None, N_HEADS * tq, win), lambda i, t: (jnp.minimum(t, nvar - 1), 0, 0))],
        out_specs=pl.BlockSpec((None, tq, BRANCH_W), lambda i, t: (i, t, 0)),
        out_shape=jax.ShapeDtypeStruct((b, lq, BRANCH_W), BF16),
        compiler_params=_params(("parallel", "arbitrary")),
        name="band_attn",
    )(q, k[0], v[0], bias)


def _band_kernel_t(qt_ref, k_ref, vt_ref, bias_ref, o_ref, *, tq, win, back, nseq):
    qi = pl.program_id(1)
    start = pl.multiple_of(jnp.maximum(qi - back, 0) * tq, tq)
    for s in range(nseq):
        k = k_ref[s, pl.ds(start, win), :].astype(BF16)
        vt = vt_ref[s, :, pl.ds(start, win)]
        qt = qt_ref[s]
        feat = lax.broadcasted_iota(jnp.int32, qt.shape, 0)
        qmt = jnp.concatenate(
            [jnp.where(jnp.logical_and(feat >= h * D_HEAD, feat < (h + 1) * D_HEAD), qt, jnp.zeros_like(qt))
             for h in range(N_HEADS)], axis=1)
        st = _dot(k, qmt) + bias_ref[...]
        pt = jnp.exp2(st - jnp.max(st, axis=0, keepdims=True))
        l = jnp.sum(pt, axis=0, keepdims=True)
        pb = pt.astype(BF16)
        out_t = jnp.concatenate(
            [_dot(vt[_head_slice(h), :], pb[:, h * tq:(h + 1) * tq]) / l[:, h * tq:(h + 1) * tq]
             for h in range(N_HEADS)], axis=0)
        o_ref[s] = out_t.T.astype(o_ref.dtype)


def _band_attn_t(qt, k, vt, bias_t, *, tq, win, back, nseq):
    b, _, length = qt.shape
    nvar = bias_t.shape[0]
    k_arr, layer = k
    assert length % tq == 0 and b % nseq == 0
    return pl.pallas_call(
        functools.partial(_band_kernel_t, tq=tq, win=win, back=back, nseq=nseq),
        grid=(b // nseq, length // tq),
        in_specs=[pl.BlockSpec((nseq, BRANCH_W, tq), lambda i, t: (i, 0, t)),
                  pl.BlockSpec((None, nseq, k_arr.shape[2], BRANCH_W), lambda i, t: (layer, i, 0, 0)),
                  pl.BlockSpec((nseq, BRANCH_W, length), lambda i, t: (i, 0, 0)),
                  pl.BlockSpec((None, win, N_HEADS * tq), lambda i, t: (jnp.minimum(t, nvar - 1), 0, 0))],
        out_specs=pl.BlockSpec((nseq, tq, BRANCH_W), lambda i, t: (i, t, 0)),
        out_shape=jax.ShapeDtypeStruct((b, length, BRANCH_W), BF16),
        compiler_params=_params(("parallel", "arbitrary")),
        name="band_attn_t",
    )(qt, k_arr, vt, bias_t)


def _merge_kernel(x_ref, ba_ref, bb_ref, bc_ref, bd_ref, wg_ref, wb_ref, wo_ref, g_ref, b_ref, o_ref):
    x = x_ref[...]
    xb = x.astype(BF16)
    merged = None
    for n, br_ref in enumerate((ba_ref, bb_ref, bc_ref, bd_ref)):
        logits = _dot(xb, wg_ref[:, n * D_MODEL:(n + 1) * D_MODEL])
        term = _dot(br_ref[...], wb_ref[n]) * _sigmoid(logits)
        merged = term if merged is None else merged + term
    mix = _dot(merged.astype(BF16), wo_ref[...])
    o_ref[...] = _layer_norm(ALPHA * x + mix, g_ref[...], b_ref[...])


def _merge(x, branches, wg, wb, wo, g, b, *, tm):
    m = x.shape[0]
    tm = min(tm, m)
    assert m % tm == 0
    const2 = lambda i: (0, 0)
    row = lambda i: (i, 0)
    return pl.pallas_call(
        _merge_kernel,
        grid=(m // tm,),
        in_specs=[pl.BlockSpec((tm, D_MODEL), row)]
        + [pl.BlockSpec((tm, BRANCH_W), row)] * N_BRANCH
        + [pl.BlockSpec((D_MODEL, N_BRANCH * D_MODEL), const2),
           pl.BlockSpec((N_BRANCH, BRANCH_W, D_MODEL), lambda i: (0, 0, 0)),
           pl.BlockSpec((D_MODEL, D_MODEL), const2),
           pl.BlockSpec((1, D_MODEL), const2),
           pl.BlockSpec((1, D_MODEL), const2)],
        out_specs=pl.BlockSpec((tm, D_MODEL), row),
        out_shape=jax.ShapeDtypeStruct((m, D_MODEL), F32),
        compiler_params=_params(("parallel",)),
        name="merge",
    )(x, *branches, wg, wb, wo, g, b)


def _route(aff_t, sel_t):
    def top2_sum(a, b, c, d):
        hi1, lo1 = jnp.maximum(a, b), jnp.minimum(a, b)
        hi2, lo2 = jnp.maximum(c, d), jnp.minimum(c, d)
        return jnp.maximum(hi1, hi2) + jnp.maximum(jnp.minimum(hi1, hi2), jnp.maximum(lo1, lo2))

    score = [top2_sum(*sel_t[g * EXPERTS_PER_GROUP:(g + 1) * EXPERTS_PER_GROUP])
             for g in range(N_GROUPS)]
    best_here = []
    for g in range(N_GROUPS):
        ok = None
        for o in range(N_GROUPS):
            if o == g:
                continue
            c = (score[g] > score[o]) if o < g else (score[g] >= score[o])
            ok = c if ok is None else jnp.logical_and(ok, c)
        best_here.append(ok)
    picked = []
    for e in range(N_EXPERTS):
        g = e // EXPERTS_PER_GROUP
        rank = jnp.zeros_like(sel_t[e])
        for o in range(g * EXPERTS_PER_GROUP, (g + 1) * EXPERTS_PER_GROUP):
            if o == e:
                continue
            ahead = (sel_t[o] >= sel_t[e]) if o < e else (sel_t[o] > sel_t[e])
            rank = rank + jnp.where(ahead, 1.0, 0.0)
        picked.append(jnp.where(jnp.logical_and(best_here[g], rank < TOP_K), aff_t[e], 0.0))
    total = picked[0]
    for e in range(1, N_EXPERTS):
        total = total + picked[e]
    return [p / total for p in picked]


def _moe_kernel(x_ref, wr_ref, br_ref, wg_ref, wu_ref, wd_ref, g_ref, b_ref, o_ref, acc_ref):
    x = x_ref[...]
    xh = x.astype(BF16)
    xl = (x - xh.astype(F32)).astype(BF16)
    both = _dot(xh, wr_ref[...])
    logits = both[:, :LANE] + both[:, LANE:] + _dot(xl, wr_ref[:, :LANE])
    aff = _sigmoid(logits).T
    bias = br_ref[...]
    aff_t = [aff[e:e + 1, :] for e in range(N_EXPERTS)]
    sel_t = [aff_t[e] + bias[e:e + 1, :] for e in range(N_EXPERTS)]
    gate_rows = _route(aff_t, sel_t)
    tm = x.shape[0]
    gate_t = jnp.concatenate(gate_rows + [jnp.zeros((LANE - N_EXPERTS, tm), F32)], axis=0)
    gate = gate_t.T

    for e in range(N_EXPERTS):
        gt = _dot(xh, wg_ref[e])
        hmid = gt * _sigmoid(gt) * _dot(xh, wu_ref[e])
        y = _dot(hmid.astype(BF16), wd_ref[e]) * gate[:, e:e + 1]
        if e == 0:
            acc_ref[...] = y
        else:
            acc_ref[...] += y
    o_ref[...] = _layer_norm(ALPHA * x + acc_ref[...], g_ref[...], b_ref[...])


def _moe(x, wr, br, wg, wu, wd, g, b, *, tm, layer):
    m = x.shape[0]
    tm = min(tm, m)
    assert m % tm == 0
    const2 = lambda i: (0, 0)
    this_layer = lambda i: (layer, 0, 0, 0)
    return pl.pallas_call(
        _moe_kernel,
        grid=(m // tm,),
        in_specs=[pl.BlockSpec((tm, D_MODEL), lambda i: (i, 0)),
                  pl.BlockSpec((D_MODEL, 2 * LANE), const2),
                  pl.BlockSpec((LANE, 1), const2),
                  pl.BlockSpec((None, N_EXPERTS, D_MODEL, D_EXPERT), this_layer, pipeline_mode=pl.Buffered(1)),
                  pl.BlockSpec((None, N_EXPERTS, D_MODEL, D_EXPERT), this_layer, pipeline_mode=pl.Buffered(1)),
                  pl.BlockSpec((None, N_EXPERTS, D_EXPERT, D_MODEL), this_layer, pipeline_mode=pl.Buffered(1)),
                  pl.BlockSpec((1, D_MODEL), const2),
                  pl.BlockSpec((1, D_MODEL), const2)],
        out_specs=pl.BlockSpec((tm, D_MODEL), lambda i: (i, 0)),
        out_shape=jax.ShapeDtypeStruct((m, D_MODEL), F32),
        scratch_shapes=[pltpu.VMEM((tm, D_MODEL), F32)],
        compiler_params=_params(("parallel",)),
        name="moe",
    )(x, wr, br, wg, wu, wd, g, b)


def _rope_tables(pos, d):
    half = d // 2
    inv = jnp.power(ROPE_BASE, -jnp.arange(half, dtype=F32) / half)
    ang = pos.astype(F32)[:, None] * inv[None, :]
    cos, sin = jnp.cos(ang), jnp.sin(ang)
    zero = jnp.zeros_like(sin)
    rep = LANE // d
    cos_t = jnp.tile(jnp.concatenate([cos, cos], axis=1), (1, rep))
    sin_a = jnp.tile(jnp.concatenate([-sin, zero], axis=1), (1, rep))
    sin_b = jnp.tile(jnp.concatenate([zero, sin], axis=1), (1, rep))
    return cos_t, sin_a, sin_b


def _retention_tables(lc):
    log_g = jnp.log1p(-jnp.exp2(-5.0 - jnp.arange(N_HEADS, dtype=F32)))
    i = jnp.arange(lc, dtype=F32)
    diff = i[:, None] - i[None, :]
    dec = jnp.where(diff >= 0, jnp.exp(jnp.maximum(diff, 0.0)[None] * log_g[:, None, None]), 0.0)
    qdec = jnp.exp((i[None, :] + 1.0) * log_g[:, None])
    kdec = jnp.exp((lc - 1.0 - i)[None, :] * log_g[:, None])
    gl = jnp.exp(lc * log_g)
    dec = dec.reshape(N_HEADS * lc, lc)
    qdec = jnp.broadcast_to(qdec[:, :, None], (N_HEADS, lc, D_HEAD)).reshape(N_HEADS * lc, D_HEAD)
    gl = jnp.broadcast_to(gl[:, None, None], (N_HEADS, D_HEAD, D_HEAD)).reshape(BRANCH_W, D_HEAD)
    kdec = jnp.repeat(kdec.T, D_HEAD, axis=1)
    return dec, qdec, kdec, gl


def _band_bias(rel_bias, tq, win, q_minus_k0, valid):
    length = tq + win - 1
    d = np.arange(length) - (tq - 1) - q_minus_k0
    idx = np.clip(d, -REL_CLIP, REL_CLIP) + REL_CLIP
    g = rel_bias[:, idx].astype(F32) * LOG2E
    gp = jnp.concatenate([g, jnp.zeros((N_HEADS, 1), F32)], axis=1)
    m = jnp.tile(gp, (1, tq))[:, :tq * length].reshape(N_HEADS, tq, length)
    tile = m[:, :, tq - 1:tq - 1 + win]
    return jnp.where(valid[None], tile, NEG_INF).reshape(N_HEADS * tq, win)


def _pack_w_in(w_in_l):
    cols = []
    src = 0
    for w in _IN_WIDTH:
        seg = w_in_l[:, src:src + w]
        cols.append(jnp.pad(seg, ((0, 0), (0, _round_up(w, LANE) - w))))
        src += w
    return jnp.concatenate(cols, axis=1).astype(BF16)


def _pad_rows(t, n):
    return jnp.pad(t, ((0, 0), (0, n - t.shape[1]), (0, 0)))


def _token_mixers(x, pos0, past, lw, *, prompt, layer, stacked):
    (w_in_p, qn, wuq, kvn, wukv, gn_g, gn_b, rel_bias, b, length) = lw
    assert pos0 % CHUNK == 0
    pos = pos0 + jnp.arange(length)
    lc = 256 if prompt else length
    dec, qdec, kdec, gl = _retention_tables(lc)
    rope32 = _rope_tables(pos, DR_A)
    lane = np.arange(LANE)
    rotary = (lane >= DN_A) & (lane < DQK_A)
    rope_q = [jnp.where(rotary[None, :], t, fill) for t, fill in zip(rope32, (1.0, 0.0, 0.0))]
    tables = (list(rope32) + list(_rope_tables(pos, D_HEAD)) + [jnp.tile(kdec, (length // lc, 1))] + rope_q)
    if not prompt:
        tables = [jnp.tile(t, (b, 1)) for t in tables]
    keep = min(PREV_CHUNKS * CHUNK, length)
    outs, stacked = _inproj(x, w_in_p, qn, wuq, kvn, wukv, tables, tm=TOKEN_TILE if prompt else b * length,
                            layer=layer, stacked=stacked, seq_len=length if prompt else None,
                            keep_rows=keep if prompt and keep < length else None)
    per_batch = lambda o: o.reshape(o.shape[:-2] + (b, length, o.shape[-1]))
    (q_a, ckv, kpe, kf, v_a, rq, rk, rkd, rv, rg, sq, sk, sv, bq, bk, bv) = [
        per_batch(o) for o in outs[:len(_INPROJ_OUT)]]
    here = lambda t: (t, layer)
    only = lambda t: (t[None], 0)

    if prompt:
        tq = 256
        qa_t, va_t, sq_t, sv_t, bq_t, bv_t = outs[len(_INPROJ_OUT):len(_INPROJ_OUT) + 6]
        nseq = SEQS_PER_STEP if b % SEQS_PER_STEP == 0 else 1
        o_a = _mla_attn_t(qa_t, kf, va_t, tq=tq, nseq=nseq)
        o_c = _sb_attn_t(sq_t, here(sk), sv_t, tq=tq, nseq=nseq)
        s0 = jnp.zeros((b, BRANCH_W, D_HEAD), F32)
        win = 3 * tq
        i = np.arange(tq)[:, None]
        c = np.arange(win)[None, :]
        variants = []
        for t in range(3):
            qc, kc = i // CHUNK + t * (tq // CHUNK), c // CHUNK
            variants.append(_band_bias(rel_bias, tq, win, t * tq, (kc <= qc) & (kc >= qc - PREV_CHUNKS)))
        bias_t = jnp.stack(variants).transpose(0, 2, 1)
        o_d = _band_attn_t(bq_t, here(bk), bv_t, bias_t, tq=tq, win=win, back=2,
                           nseq=2 * nseq if b % (2 * nseq) == 0 else nseq)
    else:
        c_ckv, c_kpe, s_prev, c_sk, c_sv, c_bk, c_bv = past
        n_past = c_ckv.shape[2]
        tko = LANE
        kf_c, v_c = _expand_latent(c_ckv.reshape(DEPTH, b * n_past, KV_RANK),
                                   c_kpe.reshape(DEPTH, b * n_past, DR_A), wukv, layer,
                                   math.gcd(b * n_past, CACHE_TILE))
        o_a = _mla_attn(q_a, only(_pad_rows(kf, tko)), only(_pad_rows(v_a, tko)),
                        only(kf_c.reshape(b, n_past, QA_W)), only(v_c.reshape(b, n_past, BRANCH_W)),
                        tq=length, tk=math.gcd(n_past, SAMPLE_KEY_TILE), n_own=length, causal_tiles=False)
        o_c = _sb_attn(sq, only(_pad_rows(sk[layer], tko)), only(_pad_rows(sv[layer], tko)),
                       (c_sk.reshape(DEPTH, b, n_past, BRANCH_W), layer),
                       (c_sv.reshape(DEPTH, b, n_past, BRANCH_W), layer),
                       tq=length, tk=256, n_own=length, causal_tiles=False)
        s0 = s_prev[layer].reshape(b, BRANCH_W, D_HEAD)
        n_band = c_bk.shape[2]
        n_keys = n_band + length
        win = _round_up(n_keys, LANE)
        bk_all = _pad_rows(jnp.concatenate([c_bk[layer].reshape(b, n_band, BRANCH_W), bk[layer]], axis=1), win)
        bv_all = _pad_rows(jnp.concatenate([c_bv[layer].reshape(b, n_band, BRANCH_W), bv[layer]], axis=1), win)
        k_pos = pos0 - n_band + np.arange(win)
        q_pos = pos0 + np.arange(length)
        qc, kc = q_pos[:, None] // CHUNK, k_pos[None, :] // CHUNK
        valid = (np.arange(win)[None, :] < n_keys) & (k_pos[None, :] >= 0) & (kc <= qc) & (kc >= qc - PREV_CHUNKS)
        bias = _band_bias(rel_bias, length, win, n_band, valid)[None]
        o_d = _band_attn(bq, only(bk_all), only(bv_all), bias, tq=length, win=win, back=0)

    o_r, s_ret = _retention(rq, rk, rkd, rv, rg, s0, dec, qdec, gl,
                            gn_g.reshape(N_HEADS, 1, D_HEAD), gn_b.reshape(N_HEADS, 1, D_HEAD), lc=lc,
                            nseq=SEQS_PER_STEP if b % SEQS_PER_STEP == 0 else 1)
    s_ret = s_ret.reshape(b, N_HEADS, D_HEAD, D_HEAD)
    flat = lambda t: t.reshape(b * length, BRANCH_W)
    return (flat(o_a), flat(o_r), flat(o_c), flat(o_d)), s_ret, stacked


def _state_outputs(stacked, s_ret, b, length):
    per_seq = lambda t: t.reshape(DEPTH, b, t.shape[1] // b, t.shape[-1])
    heads4 = lambda t: per_seq(t).reshape(DEPTH, b, t.shape[1] // b, N_HEADS, D_HEAD)
    ckv, kpe, sk, sv, bk, bv = stacked[:6]
    if len(stacked) > 6:
        bk, bv = stacked[6:]
    return (per_seq(ckv), per_seq(kpe), jnp.stack(s_ret, axis=0), heads4(sk), heads4(sv), heads4(bk), heads4(bv))


def kernel(x_prompt, x_sample, cache_mla_ckv, cache_mla_kpe, state_ret, cache_sb_k, cache_sb_v, cache_band_k, cache_band_v, w_in, mla_q_norm, mla_w_uq, mla_kv_norm, mla_w_ukv, ret_gn_g, ret_gn_b, band_rel_bias, w_branch, w_o, ln1_g, ln1_b, w_router, b_router, w_exp_gate, w_exp_up, w_exp_down, ln2_g, ln2_b):
    bp, lp, _ = x_prompt.shape
    bs, ls, _ = x_sample.shape
    past_len = cache_mla_ckv.shape[2]
    xp = x_prompt.reshape(bp * lp, D_MODEL)
    xs = x_sample.reshape(bs * ls, D_MODEL)

    wr = jnp.pad(w_router, ((0, 0), (0, LANE - N_EXPERTS)))
    wrh = wr.astype(BF16)
    wr2 = jnp.concatenate([wrh, (wr - wrh.astype(F32)).astype(BF16)], axis=1)
    br = jnp.pad(b_router, (0, LANE - N_EXPERTS)).reshape(LANE, 1)

    we_gate, we_up, we_down = w_exp_gate.astype(BF16), w_exp_up.astype(BF16), w_exp_down.astype(BF16)
    past = (cache_mla_ckv, cache_mla_kpe, state_ret, cache_sb_k, cache_sb_v, cache_band_k, cache_band_v)
    ret_p, ret_s = [], []
    stacked_p = stacked_s = None
    for l in range(DEPTH):
        wuq = mla_w_uq[l]
        wuq = jnp.pad(wuq, ((0, _round_up(Q_RANK, LANE) - Q_RANK), (0, 0), (0, LANE - DQK_A)))
        wuq = wuq.reshape(-1, QA_W).astype(BF16)
        wukv = mla_w_ukv[l]
        wukv = jnp.concatenate(
            [jnp.pad(wukv[:, :, :DN_A], ((0, 0), (0, 0), (0, LANE - DN_A))).reshape(KV_RANK, QA_W),
             wukv[:, :, DN_A:].reshape(KV_RANK, -1)], axis=1).astype(BF16)
        qn = jnp.pad(mla_q_norm[l], (0, _round_up(Q_RANK, LANE) - Q_RANK)).reshape(1, -1)
        kvn = mla_kv_norm[l].reshape(1, KV_RANK)
        w_in_p = _pack_w_in(w_in[l])
        wg = w_in[l][:, GATE_COL0:].astype(BF16)
        wb = w_branch[l].astype(BF16)
        wo = w_o[l].astype(BF16)
        g1, b1 = ln1_g[l].reshape(1, D_MODEL), ln1_b[l].reshape(1, D_MODEL)
        g2, b2 = ln2_g[l].reshape(1, D_MODEL), ln2_b[l].reshape(1, D_MODEL)
        lw =(w_in_p, qn, wuq, kvn, wukv, ret_gn_g[l], ret_gn_b[l], band_rel_bias[l])

        br_p, s_ret_p, stacked_p = _token_mixers(xp, 0, None, lw + (bp, lp), prompt=True,
                                                 layer=l, stacked=stacked_p)
        br_s, s_ret_s, stacked_s = _token_mixers(xs, past_len, past, lw + (bs, ls), prompt=False,
                                                 layer=l, stacked=stacked_s)
        xp = _merge(xp, br_p, wg, wb, wo, g1, b1, tm=TOKEN_TILE)
        xs = _merge(xs, br_s, wg, wb, wo, g1, b1, tm=TOKEN_TILE)
        xp = _moe(xp, wr2, br, we_gate, we_up, we_down, g2, b2, tm=TOKEN_TILE, layer=l)
        xs = _moe(xs, wr2, br, we_gate, we_up, we_down, g2, b2, tm=TOKEN_TILE, layer=l)
        ret_p.append(s_ret_p)
        ret_s.append(s_ret_s)

    return ((xp.reshape(bp, lp, D_MODEL), xs.reshape(bs, ls, D_MODEL))
            + _state_outputs(stacked_p, ret_p, bp, lp)
            + _state_outputs(stacked_s, ret_s, bs, ls))
```
